```python
import math
import jax, jax.numpy as jnp
from jax import lax
import numpy as np

D_MODEL = 2048
BATCH = 8
SEQ = 2048
DEPTH = 2
DEC_BATCH = 32
DEC_SEQ = 16
PAST_LEN = 1024

CHUNK = 64
A_HEADS = 16
A_KV_HEADS = 4
A_GROUP = A_HEADS // A_KV_HEADS
A_HEAD_DIM = 64
A_WIDTH = A_HEADS * A_HEAD_DIM
A_KV_WIDTH = A_KV_HEADS * A_HEAD_DIM
WINDOW = 128
WIN_CHUNKS = WINDOW // CHUNK
SWA_CACHE = min(WINDOW, PAST_LEN)
N_BUCKETS = 32
MAX_DISTANCE = 128
B_WIDTH = 512
B_CONV = 3
C_WIDTH = 512
C_CONV = 31
N_BRANCH = 3
N_MEM = 256
X_HEADS = 4
X_HEAD_DIM = 128
X_WIDTH = X_HEADS * X_HEAD_DIM
D_FF = 5632
N_EXPERTS = 8
TOP_K = 2
N_DENSE = (DEPTH + 1) // 2
N_MOE = DEPTH // 2
EPS = 1e-6

_COL_SIZES = (A_WIDTH, A_KV_WIDTH, A_KV_WIDTH, B_WIDTH, B_WIDTH, B_WIDTH, C_WIDTH, C_WIDTH, N_BRANCH * D_MODEL)
IN_COLS = sum(_COL_SIZES)

kernel_name = 'hybrid_streaming_encoder_step'


def _split_points():
    return [int(p) for p in np.cumsum(_COL_SIZES)[:-1]]


def rms_norm(x, g):
    xf = x.astype(jnp.float32)
    y = xf * lax.rsqrt(jnp.mean(xf * xf, axis=-1, keepdims=True) + EPS)
    return (y * g.astype(jnp.float32)).astype(x.dtype)


def layer_norm(x, g, b):
    xf = x.astype(jnp.float32)
    mu = jnp.mean(xf, axis=-1, keepdims=True)
    xc = xf - mu
    y = xc * lax.rsqrt(jnp.mean(xc * xc, axis=-1, keepdims=True) + EPS)
    return (y * g.astype(jnp.float32) + b.astype(jnp.float32)).astype(x.dtype)


def t5_bucket(rel):
    nb = N_BUCKETS // 2
    max_exact = nb // 2
    ret = jnp.where(rel > 0, nb, 0)
    n = jnp.abs(rel)
    nf = jnp.maximum(n, 1).astype(jnp.float32)
    large = max_exact + (jnp.log(nf / max_exact) / math.log(MAX_DISTANCE / max_exact) * (nb - max_exact)).astype(jnp.int32)
    large = jnp.minimum(large, nb - 1)
    return ret + jnp.where(n < max_exact, n, large)


def rel_bias(table, n_q, n_k, offset):
    rel = jnp.arange(n_k, dtype=jnp.int32)[None, :] - offset - jnp.arange(n_q, dtype=jnp.int32)[:, None]
    bias = jnp.take(table, t5_bucket(rel), axis=0)
    return jnp.transpose(bias, (2, 0, 1)).reshape(A_KV_HEADS, A_GROUP, n_q, n_k).astype(jnp.float32)


def sink_attention(q, k, v, bias, valid, sinks):
    s = jnp.einsum('bcqhgd,bckhd->bchgqk', q, k).astype(jnp.float32) * (A_HEAD_DIM ** -0.5)
    s = s + bias[None, None]
    s = jnp.where(valid[None, :, None, None, None, :], s, -jnp.inf)
    sink = sinks.astype(jnp.float32)[None, None, :, :, None, None]
    m = jnp.maximum(jnp.max(s, axis=-1, keepdims=True), sink)
    p = jnp.exp(s - m)
    p = p / (jnp.sum(p, axis=-1, keepdims=True) + jnp.exp(sink - m))
    return jnp.einsum('bchgqk,bckhd->bcqhgd', p.astype(v.dtype), v)


def band_blocks(xc):
    nc = xc.shape[1]
    xp = jnp.pad(xc, ((0, 0), (WIN_CHUNKS, 0), (0, 0), (0, 0), (0, 0)))
    return jnp.concatenate([xp[:, i:i + nc] for i in range(WIN_CHUNKS + 1)], axis=2)


def causal_dwconv(u, prefix, w):
    width, c = w.shape
    up = jnp.concatenate([prefix.astype(u.dtype), u], axis=1)
    y = lax.conv_general_dilated(up, w[:, None, :].astype(u.dtype), window_strides=(1,), padding='VALID',
                                 dimension_numbers=('NWC', 'WIO', 'NWC'), feature_group_count=c)
    return y, up[:, -(width - 1):]


def mixer(x, mw, rel_table, kv_prev, convb_prev, convc_prev):
    (g_mix, w_in, b_gate, q_g, k_g, sinks, w_cb, w_cc, b_cc, ln_g, ln_b, w_pa, w_pb, w_pc, w_o) = mw
    b, t, _ = x.shape
    h = rms_norm(x, g_mix)
    z = h @ w_in
    q, k, v, gate_b, gate_c, h_b, glu_a, glu_g, gate_logits = jnp.split(z, _split_points(), axis=-1)
    q = rms_norm(q.reshape(b, t, A_HEADS, A_HEAD_DIM), q_g)
    k = rms_norm(k.reshape(b, t, A_KV_HEADS, A_HEAD_DIM), k_g)
    v = v.reshape(b, t, A_KV_HEADS, A_HEAD_DIM)
    sk = sinks.reshape(A_KV_HEADS, A_GROUP)
    if kv_prev is None:
        nc = t // CHUNK
        n_k = (WIN_CHUNKS + 1) * CHUNK
        qc = q.reshape(b, nc, CHUNK, A_KV_HEADS, A_GROUP, A_HEAD_DIM)
        kb = band_blocks(k.reshape(b, nc, CHUNK, A_KV_HEADS, A_HEAD_DIM))
        vb = band_blocks(v.reshape(b, nc, CHUNK, A_KV_HEADS, A_HEAD_DIM))
        key_chunk = jnp.arange(nc)[:, None] + (jnp.arange(n_k) // CHUNK)[None, :] - WIN_CHUNKS
        o = sink_attention(qc, kb, vb, rel_bias(rel_table, CHUNK, n_k, WIN_CHUNKS * CHUNK), key_chunk >= 0, sk)
        new_k, new_v = k[:, -WINDOW:], v[:, -WINDOW:]
    else:
        k_prev, v_prev = kv_prev
        n_past = k_prev.shape[1]
        kf = jnp.concatenate([k_prev.astype(k.dtype), k], axis=1)
        vf = jnp.concatenate([v_prev.astype(v.dtype), v], axis=1)
        n_k = kf.shape[1]
        qs = q.reshape(b, 1, t, A_KV_HEADS, A_GROUP, A_HEAD_DIM)
        o = sink_attention(qs, kf[:, None], vf[:, None], rel_bias(rel_table, t, n_k, n_past),
                           jnp.ones((1, n_k), dtype=bool), sk)
        new_k, new_v = k, v
    o_a = o.reshape(b, t, A_WIDTH)
    if convb_prev is None:
        convb_prev = jnp.zeros((b, B_CONV - 1, B_WIDTH), x.dtype)
    y_b, new_cb = causal_dwconv(gate_c * h_b, convb_prev, w_cb)
    o_b = gate_b * y_b
    if convc_prev is None:
        convc_prev = jnp.zeros((b, C_CONV - 1, C_WIDTH), x.dtype)
    u = glu_a * jax.nn.sigmoid(glu_g)
    y_c, new_cc = causal_dwconv(u, convc_prev, w_cc)
    o_c = jax.nn.silu(layer_norm(y_c + b_cc, ln_g, ln_b))
    gates = jax.nn.sigmoid(gate_logits + b_gate).reshape(b, t, N_BRANCH, D_MODEL)
    merged = gates[:, :, 0] * (o_a @ w_pa) + gates[:, :, 1] * (o_b @ w_pb) + gates[:, :, 2] * (o_c @ w_pc)
    return merged @ w_o, (new_k, new_v, new_cb, new_cc)


def memory_kv(mem, g_mem, w_xkv, xk_g):
    b, n, _ = mem.shape
    kv = rms_norm(mem, g_mem) @ w_xkv
    k, v = jnp.split(kv, 2, axis=-1)
    k = rms_norm(k.reshape(b, n, X_HEADS, X_HEAD_DIM), xk_g)
    return k, v.reshape(b, n, X_HEADS, X_HEAD_DIM)


def cross_attention(x, g, w_xq, xq_g, mk, mv, w_xo):
    b, t, _ = x.shape
    q = rms_norm((rms_norm(x, g) @ w_xq).reshape(b, t, X_HEADS, X_HEAD_DIM), xq_g)
    s = jnp.einsum('bthd,bnhd->bhtn', q, mk.astype(q.dtype)).astype(jnp.float32) * (X_HEAD_DIM ** -0.5)
    p = jax.nn.softmax(s, axis=-1).astype(x.dtype)
    o = jnp.einsum('bhtn,bnhd->bthd', p, mv.astype(x.dtype)).reshape(b, t, X_WIDTH)
    return o @ w_xo


def swiglu(h, w_gu, w_d):
    g, u = jnp.split(h @ w_gu, 2, axis=-1)
    return (jax.nn.silu(g) * u) @ w_d


def moe_swiglu(h, w_router, b_router, w_gu, w_d):
    logits = (h @ w_router).astype(jnp.float32) + b_router.astype(jnp.float32)
    top_v, top_i = lax.top_k(logits, TOP_K)
    top_w = jax.nn.softmax(top_v, axis=-1)
    gates = jnp.sum(jax.nn.one_hot(top_i, N_EXPERTS, dtype=jnp.float32) * top_w[..., None], axis=-2).astype(h.dtype)
    y = jnp.zeros_like(h)
    for e in range(N_EXPERTS):
        y = y + gates[..., e:e + 1] * swiglu(h, w_gu[e], w_d[e])
    return y


def channel_mixer(x, l, g, w_ffn_gu, w_ffn_d, w_router, b_router, w_moe_gu, w_moe_d):
    h = rms_norm(x, g)
    if l % 2 == 0:
        return swiglu(h, w_ffn_gu[l // 2], w_ffn_d[l // 2])
    i = l // 2
    return moe_swiglu(h, w_router[i], b_router[i], w_moe_gu[i], w_moe_d[i])


def setup_inputs(seed: int = 0) -> dict:
    key = jax.random.key(seed)
    ks = iter(jax.random.split(key, 48))
    D = D_MODEL

    def nrm(shape, scale):
        return jax.random.normal(next(ks), shape, jnp.float32) * scale

    def gain(shape):
        return 1.0 + 0.05 * jax.random.normal(next(ks), shape, jnp.float32)

    return {
        'x_prompt': nrm((BATCH, SEQ, D), 1.0),
        'x_sample': nrm((DEC_BATCH, DEC_SEQ, D), 1.0),
        'mem_prompt': nrm((BATCH, N_MEM, D), 1.0),
        'cache_mem_k': nrm((DEPTH, DEC_BATCH, N_MEM, X_HEADS, X_HEAD_DIM), 1.0),
        'cache_mem_v': nrm((DEPTH, DEC_BATCH, N_MEM, X_HEADS, X_HEAD_DIM), 1.0),
        'cache_swa_k': nrm((DEPTH, DEC_BATCH, SWA_CACHE, A_KV_HEADS, A_HEAD_DIM), 1.0),
        'cache_swa_v': nrm((DEPTH, DEC_BATCH, SWA_CACHE, A_KV_HEADS, A_HEAD_DIM), 1.0),
        'state_conv_b': nrm((DEPTH, DEC_BATCH, B_CONV - 1, B_WIDTH), 0.5),
        'state_conv_c': nrm((DEPTH, DEC_BATCH, C_CONV - 1, C_WIDTH), 0.5),
        'rel_table': nrm((N_BUCKETS, A_HEADS), 0.5),
        'g_mix': gain((DEPTH, D)),
        'w_in': nrm((DEPTH, D, IN_COLS), D ** -0.5),
        'b_gate': nrm((DEPTH, N_BRANCH * D), 0.1),
        'q_norm_g': gain((DEPTH, A_HEAD_DIM)),
        'k_norm_g': gain((DEPTH, A_HEAD_DIM)),
        'sinks': nrm((DEPTH, A_HEADS), 0.5),
        'w_conv_b': nrm((DEPTH, B_CONV, B_WIDTH), B_CONV ** -0.5),
        'w_conv_c': nrm((DEPTH, C_CONV, C_WIDTH), C_CONV ** -0.5),
        'b_conv_c': nrm((DEPTH, C_WIDTH), 0.02),
        'ln_c_g': gain((DEPTH, C_WIDTH)),
        'ln_c_b': nrm((DEPTH, C_WIDTH), 0.02),
        'w_proj_a': nrm((DEPTH, A_WIDTH, D), A_WIDTH ** -0.5),
        'w_proj_b': nrm((DEPTH, B_WIDTH, D), B_WIDTH ** -0.5),
        'w_proj_c': nrm((DEPTH, C_WIDTH, D), C_WIDTH ** -0.5),
        'w_out': nrm((DEPTH, D, D), D ** -0.5),
        'g_xattn': gain((DEPTH, D)),
        'g_mem': gain((DEPTH, D)),
        'w_xq': nrm((DEPTH, D, X_WIDTH), D ** -0.5),
        'w_xkv': nrm((DEPTH, D, 2 * X_WIDTH), D ** -0.5),
        'xq_norm_g': gain((DEPTH, X_HEAD_DIM)),
        'xk_norm_g': gain((DEPTH, X_HEAD_DIM)),
        'w_xo': nrm((DEPTH, X_WIDTH, D), X_WIDTH ** -0.5),
        'g_ffn': gain((DEPTH, D)),
        'w_ffn_gu': nrm((N_DENSE, D, 2 * D_FF), D ** -0.5),
        'w_ffn_d': nrm((N_DENSE, D_FF, D), D_FF ** -0.5),
        'w_router': nrm((N_MOE, D, N_EXPERTS), D ** -0.5),
        'b_router': nrm((N_MOE, N_EXPERTS), 0.01),
        'w_moe_gu': nrm((N_MOE, N_EXPERTS, D, 2 * D_FF), D ** -0.5),
        'w_moe_d': nrm((N_MOE, N_EXPERTS, D_FF, D), D_FF ** -0.5),
    }


def reference(x_prompt, x_sample, mem_prompt, cache_mem_k, cache_mem_v, cache_swa_k, cache_swa_v,
              state_conv_b, state_conv_c, rel_table, g_mix, w_in, b_gate, q_norm_g, k_norm_g, sinks,
              w_conv_b, w_conv_c, b_conv_c, ln_c_g, ln_c_b, w_proj_a, w_proj_b, w_proj_c, w_out,
              g_xattn, g_mem, w_xq, w_xkv, xq_norm_g, xk_norm_g, w_xo,
              g_ffn, w_ffn_gu, w_ffn_d, w_router, b_router, w_moe_gu, w_moe_d):
    yp, ys = x_prompt, x_sample
    mk_p, mv_p, kp_l, vp_l, cbp_l, ccp_l = [], [], [], [], [], []
    ks_l, vs_l, cbs_l, ccs_l = [], [], [], []
    for l in range(DEPTH):
        mw = (g_mix[l], w_in[l], b_gate[l], q_norm_g[l], k_norm_g[l], sinks[l],
              w_conv_b[l], w_conv_c[l], b_conv_c[l], ln_c_g[l], ln_c_b[l],
              w_proj_a[l], w_proj_b[l], w_proj_c[l], w_out[l])
        a, (k_new, v_new, cb_new, cc_new) = mixer(yp, mw, rel_table, None, None, None)
        yp = yp + a
        mk, mv = memory_kv(mem_prompt, g_mem[l], w_xkv[l], xk_norm_g[l])
        yp = yp + cross_attention(yp, g_xattn[l], w_xq[l], xq_norm_g[l], mk, mv, w_xo[l])
        yp = yp + channel_mixer(yp, l, g_ffn[l], w_ffn_gu, w_ffn_d, w_router, b_router, w_moe_gu, w_moe_d)
        mk_p.append(mk); mv_p.append(mv); kp_l.append(k_new); vp_l.append(v_new)
        cbp_l.append(cb_new); ccp_l.append(cc_new)
        a, (k_new, v_new, cb_new, cc_new) = mixer(ys, mw, rel_table, (cache_swa_k[l], cache_swa_v[l]),
                                                  state_conv_b[l], state_conv_c[l])
        ys = ys + a
        ys = ys + cross_attention(ys, g_xattn[l], w_xq[l], xq_norm_g[l], cache_mem_k[l], cache_mem_v[l], w_xo[l])
        ys = ys + channel_mixer(ys, l, g_ffn[l], w_ffn_gu, w_ffn_d, w_router, b_router, w_moe_gu, w_moe_d)
        ks_l.append(k_new); vs_l.append(v_new); cbs_l.append(cb_new); ccs_l.append(cc_new)
    return (yp, ys,
            jnp.stack(mk_p), jnp.stack(mv_p), jnp.stack(kp_l), jnp.stack(vp_l),
            jnp.stack(cbp_l), jnp.stack(ccp_l),
            jnp.stack(ks_l), jnp.stack(vs_l), jnp.stack(cbs_l), jnp.stack(ccs_l))
```

```python
import functools
import math

import numpy as np
import jax
import jax.numpy as jnp
from jax import lax
from jax.experimental import pallas as pl
from jax.experimental.pallas import tpu as pltpu

F32 = jnp.float32
BF16 = jnp.bfloat16

D_MODEL = 2048
CHUNK = 64
A_HEADS = 16
A_KV_HEADS = 4
A_HEAD_DIM = 64
A_WIDTH = A_HEADS * A_HEAD_DIM
A_KV_WIDTH = A_KV_HEADS * A_HEAD_DIM
WINDOW = 128
N_BUCKETS = 32
MAX_DISTANCE = 128
B_WIDTH = 512
B_CONV = 3
C_WIDTH = 512
C_CONV = 31
N_MEM = 256
X_HEADS = 4
X_HEAD_DIM = 128
X_WIDTH = X_HEADS * X_HEAD_DIM
D_FF = 5632
N_EXPERTS = 8
EPS = 1e-6

LANES = 128
KEY_TILE = 256
Q_TILE = 128
HALO = 32
NEG = -1e30
VMEM_LIMIT = 56 * 1024 * 1024

COL_Q = 0
COL_K, COL_V = 4, 5
COL_GB, COL_GC, COL_HB, COL_GA, COL_GG = 3, 4, 5, 6, 7
COL_GATE0 = 2
IN_COLS = 4096 + 3 * D_MODEL

HEAD_PERM = np.array([8 * n + (p % 2) * 4 + p // 2 for n in range(2) for p in range(8)])


def _cparams(sem):
    return pltpu.CompilerParams(dimension_semantics=sem, vmem_limit_bytes=VMEM_LIMIT)


def _rms(x, g):
    ms = jnp.mean(x * x, axis=-1, keepdims=True)
    return x * lax.rsqrt(ms + EPS) * g


def _resident(shape):
    nd = len(shape)
    return pl.BlockSpec(shape, lambda *_: (0,) * nd, pipeline_mode=pl.Buffered(1))


def _norm_matmul_kernel(x_ref, g_ref, w_ref, o_ref, hn_ref):
    @pl.when(pl.program_id(1) == 0)
    def _():
        hn_ref[...] = _rms(x_ref[...], g_ref[...]).astype(BF16)

    o_ref[...] = jnp.dot(hn_ref[...], w_ref[...], preferred_element_type=F32).astype(o_ref.dtype)


def norm_matmul(x, g, w, *, tm, tn, out_dtype):
    m, k = x.shape
    n = w.shape[1]
    return pl.pallas_call(
        _norm_matmul_kernel,
        grid=(m // tm, n // tn),
        in_specs=[pl.BlockSpec((tm, k), lambda i, j: (i, 0)),
                  pl.BlockSpec((1, k), lambda i, j: (0, 0)),
                  pl.BlockSpec((k, tn), lambda i, j: (0, j))],
        out_specs=pl.BlockSpec((tm, tn), lambda i, j: (i, j)),
        out_shape=jax.ShapeDtypeStruct((m, n), out_dtype),
        scratch_shapes=[pltpu.VMEM((tm, k), BF16)],
        compiler_params=_cparams(("parallel", "arbitrary")),
        name="norm_matmul",
    )(x, g.reshape(1, k), w)


def _mem_kv_kernel(x_ref, g_ref, w_ref, kg_ref, k_ref, v_ref):
    hn = _rms(x_ref[...], g_ref[...]).astype(BF16)
    kv = jnp.dot(hn, w_ref[...], preferred_element_type=F32)
    for h in range(X_HEADS):
        sl = slice(h * X_HEAD_DIM, (h + 1) * X_HEAD_DIM)
        k_ref[:, sl] = _rms(kv[:, sl], kg_ref[...])
    v_ref[...] = kv[:, X_WIDTH:]


def mem_kv(mem, g, w_bf, kg):
    m, k = mem.shape
    tm = 256
    return pl.pallas_call(
        _mem_kv_kernel,
        grid=(m // tm,),
        in_specs=[pl.BlockSpec((tm, k), lambda i: (i, 0)),
                  pl.BlockSpec((1, k), lambda i: (0, 0)),
                  _resident((k, 2 * X_WIDTH)),
                  pl.BlockSpec((1, X_HEAD_DIM), lambda i: (0, 0))],
        out_specs=[pl.BlockSpec((tm, X_WIDTH), lambda i: (i, 0)),
                   pl.BlockSpec((tm, X_WIDTH), lambda i: (i, 0))],
        out_shape=[jax.ShapeDtypeStruct((m, X_WIDTH), F32)] * 2,
        compiler_params=_cparams(("parallel",)),
        name="mem_kv",
    )(mem, g.reshape(1, k), w_bf, kg.reshape(1, X_HEAD_DIM))


def _half_norm(x, g, bd):
    x2 = x * x
    hi = x2.astype(BF16)
    lo = (x2 - hi.astype(F32)).astype(BF16)
    ms = (jnp.dot(hi, bd, preferred_element_type=F32)
          + jnp.dot(lo, bd, preferred_element_type=F32))
    return x * lax.rsqrt(ms + EPS) * g


def _swa_heads(q_ref, qg, bd, k2, v2, bias_ref, sink_ref, dyn_mask, o_ref):
    lo_lane = lax.broadcasted_iota(jnp.int32, (1, LANES), 1) < A_HEAD_DIM
    k_half = []
    for n in range(2):
        k_half.append((jnp.where(lo_lane, k2[n], 0.0).astype(BF16),
                       jnp.where(lo_lane, 0.0, k2[n]).astype(BF16)))
    for c in range(A_WIDTH // LANES):
        n = c // 4
        qc = _half_norm(q_ref[:, c * LANES:(c + 1) * LANES].astype(F32), qg, bd).astype(BF16)
        halves = []
        for half in range(2):
            j = 2 * c + half
            s = lax.dot_general(qc, k_half[n][half], (((1,), (1,)), ((), ())),
                                preferred_element_type=F32)
            s = s * (A_HEAD_DIM ** -0.5) + bias_ref[j]
            if dyn_mask is not None:
                s = jnp.where(dyn_mask, NEG, s)
            sink = sink_ref[j]
            m = jnp.maximum(jnp.max(s, axis=-1, keepdims=True), sink)
            p = jnp.exp(s - m)
            den = jnp.sum(p, axis=-1, keepdims=True) + jnp.exp(sink - m)
            o = jnp.dot(p.astype(BF16), v2[n], preferred_element_type=F32)
            halves.append(o / den)
        o_ref[:, c * LANES:(c + 1) * LANES] = jnp.where(lo_lane, halves[0], halves[1]).astype(o_ref.dtype)


def _swa_prompt_kernel(q_ref, kc_ref, kp_ref, vc_ref, vp_ref, qg_ref, kg_ref, bd_ref,
                       bias_ref, sink_ref, o_ref, nk_ref, nv_ref):
    t = pl.program_id(1)
    bd = bd_ref[...]
    k2, v2 = [], []
    for n in range(2):
        sl = slice(n * LANES, (n + 1) * LANES)
        kcat = jnp.concatenate([kp_ref[:, sl], kc_ref[:, sl]], axis=0).astype(F32)
        kn = _half_norm(kcat, kg_ref[...], bd)
        nk_ref[0, :, sl] = kn[Q_TILE:]
        k2.append(kn)
        v2.append(jnp.concatenate([vp_ref[:, sl], vc_ref[:, sl]], axis=0))
    nv_ref[0] = vc_ref[...].astype(F32)
    col = lax.broadcasted_iota(jnp.int32, (1, KEY_TILE), 1)
    dyn_mask = jnp.logical_and(col < Q_TILE, t == 0)
    _swa_heads(q_ref, qg_ref[...], bd, k2, v2, bias_ref, sink_ref, dyn_mask, o_ref)


def swa_prompt(z, batch, seq, qg2, kg2, bd, bias, sinks):
    nt = seq // Q_TILE
    row = lambda b, t: b * nt + t
    prev = lambda b, t: jnp.maximum(b * nt + t - 1, 0)
    return pl.pallas_call(
        _swa_prompt_kernel,
        grid=(batch, nt),
        in_specs=[pl.BlockSpec((Q_TILE, A_WIDTH), lambda b, t: (row(b, t), COL_Q)),
                  pl.BlockSpec((Q_TILE, A_KV_WIDTH), lambda b, t: (row(b, t), COL_K)),
                  pl.BlockSpec((Q_TILE, A_KV_WIDTH), lambda b, t: (prev(b, t), COL_K)),
                  pl.BlockSpec((Q_TILE, A_KV_WIDTH), lambda b, t: (row(b, t), COL_V)),
                  pl.BlockSpec((Q_TILE, A_KV_WIDTH), lambda b, t: (prev(b, t), COL_V)),
                  pl.BlockSpec((1, LANES), lambda b, t: (0, 0)),
                  pl.BlockSpec((1, LANES), lambda b, t: (0, 0)),
                  pl.BlockSpec((LANES, LANES), lambda b, t: (0, 0)),
                  _resident((A_HEADS, Q_TILE, KEY_TILE)),
                  pl.BlockSpec(memory_space=pltpu.SMEM)],
        out_specs=[pl.BlockSpec((Q_TILE, A_WIDTH), lambda b, t: (row(b, t), 0)),
                   pl.BlockSpec((1, WINDOW, A_KV_WIDTH), lambda b, t: (b, 0, 0)),
                   pl.BlockSpec((1, WINDOW, A_KV_WIDTH), lambda b, t: (b, 0, 0))],
        out_shape=[jax.ShapeDtypeStruct((batch * seq, A_WIDTH), BF16),
                   jax.ShapeDtypeStruct((batch, WINDOW, A_KV_WIDTH), F32),
                   jax.ShapeDtypeStruct((batch, WINDOW, A_KV_WIDTH), F32)],
        compiler_params=_cparams(("parallel", "arbitrary")),
        name="swa_prompt",
    )(z, z, z, z, z, qg2, kg2, bd, bias, sinks)


def _swa_sample_kernel(q_ref, kn_ref, vn_ref, ck_ref, cv_ref, qg_ref, kg_ref, bd_ref,
                       bias_ref, sink_ref, o_ref, nk_ref, nv_ref):
    bd = bd_ref[...]
    rows = q_ref.shape[0]
    pad = KEY_TILE - WINDOW - rows
    k2, v2 = [], []
    for n in range(2):
        sl = slice(n * LANES, (n + 1) * LANES)
        kn = _half_norm(kn_ref[:, sl].astype(F32), kg_ref[...], bd)
        nk_ref[0, :, sl] = kn
        k2.append(jnp.concatenate([ck_ref[0, :, sl], kn, jnp.zeros((pad, LANES), F32)], axis=0))
        v2.append(jnp.concatenate([cv_ref[0, :, sl].astype(BF16), vn_ref[:, sl],
                                   jnp.zeros((pad, LANES), BF16)], axis=0))
    nv_ref[0] = vn_ref[...].astype(F32)
    _swa_heads(q_ref, qg_ref[...], bd, k2, v2, bias_ref, sink_ref, None, o_ref)


def swa_sample(z, batch, rows, cache_k, cache_v, qg2, kg2, bd, bias, sinks):
    return pl.pallas_call(
        _swa_sample_kernel,
        grid=(batch,),
        in_specs=[pl.BlockSpec((rows, A_WIDTH), lambda b: (b, COL_Q)),
                  pl.BlockSpec((rows, A_KV_WIDTH), lambda b: (b, COL_K)),
                  pl.BlockSpec((rows, A_KV_WIDTH), lambda b: (b, COL_V)),
                  pl.BlockSpec((1, WINDOW, A_KV_WIDTH), lambda b: (b, 0, 0)),
                  pl.BlockSpec((1, WINDOW, A_KV_WIDTH), lambda b: (b, 0, 0)),
                  pl.BlockSpec((1, LANES), lambda b: (0, 0)),
                  pl.BlockSpec((1, LANES), lambda b: (0, 0)),
                  pl.BlockSpec((LANES, LANES), lambda b: (0, 0)),
                  _resident((A_HEADS, rows, KEY_TILE)),
                  pl.BlockSpec(memory_space=pltpu.SMEM)],
        out_specs=[pl.BlockSpec((rows, A_WIDTH), lambda b: (b, 0)),
                   pl.BlockSpec((1, rows, A_KV_WIDTH), lambda b: (b, 0, 0)),
                   pl.BlockSpec((1, rows, A_KV_WIDTH), lambda b: (b, 0, 0))],
        out_shape=[jax.ShapeDtypeStruct((batch * rows, A_WIDTH), BF16),
                   jax.ShapeDtypeStruct((batch, rows, A_KV_WIDTH), F32),
                   jax.ShapeDtypeStruct((batch, rows, A_KV_WIDTH), F32)],
        compiler_params=_cparams(("parallel",)),
        name="swa_sample",
    )(z, z, z, cache_k, cache_v, qg2, kg2, bd, bias, sinks)


def _conv_body(gb_ref, ub_main, uc_main, ub_halo, uc_halo, wb_ref, wc_ref, bc_ref, lg_ref, lb_ref,
               ob_ref, oc_ref, nb_ref, nc_ref, sb_ref, sc_ref, write_state):
    rows = ub_main.shape[0]
    sb_ref[0:HALO] = ub_halo
    sb_ref[HALO:HALO + rows] = ub_main
    sc_ref[0:HALO] = uc_halo
    sc_ref[HALO:HALO + rows] = uc_main
    sub = min(rows, 32)
    for r0 in range(0, rows, sub):
        yb = jnp.zeros((sub, B_WIDTH), F32)
        for k in range(B_CONV):
            yb = yb + wb_ref[k:k + 1, :] * sb_ref[pl.ds(r0 + HALO - (B_CONV - 1) + k, sub), :]
        ob_ref[r0:r0 + sub, :] = (gb_ref[r0:r0 + sub, :].astype(F32) * yb).astype(ob_ref.dtype)
        yc = jnp.zeros((sub, C_WIDTH), F32)
        for k in range(C_CONV):
            yc = yc + wc_ref[k:k + 1, :] * sc_ref[pl.ds(r0 + HALO - (C_CONV - 1) + k, sub), :]
        yc = yc + bc_ref[...]
        mu = jnp.mean(yc, axis=-1, keepdims=True)
        xc = yc - mu
        y = xc * lax.rsqrt(jnp.mean(xc * xc, axis=-1, keepdims=True) + EPS)
        y = y * lg_ref[...] + lb_ref[...]
        oc_ref[r0:r0 + sub, :] = (y * jax.nn.sigmoid(y)).astype(oc_ref.dtype)

    def _state():
        nb_ref[0] = sb_ref[rows + HALO - 8:rows + HALO]
        nc_ref[0] = sc_ref[rows:rows + HALO]

    write_state(_state)


def _conv_prompt_kernel(gb_ref, gc_ref, hb_ref, ga_ref, gg_ref, gch_ref, hbh_ref, gah_ref, ggh_ref,
                        wb_ref, wc_ref, bc_ref, lg_ref, lb_ref,
                        ob_ref, oc_ref, nb_ref, nc_ref, sb_ref, sc_ref):
    t = pl.program_id(1)
    hist = (t > 0).astype(F32)
    ub_main = gc_ref[...].astype(F32) * hb_ref[...].astype(F32)
    uc_main = ga_ref[...].astype(F32) * jax.nn.sigmoid(gg_ref[...].astype(F32))
    ub_halo = gch_ref[...].astype(F32) * hbh_ref[...].astype(F32) * hist
    uc_halo = gah_ref[...].astype(F32) * jax.nn.sigmoid(ggh_ref[...].astype(F32)) * hist
    last = pl.num_programs(1) - 1
    _conv_body(gb_ref, ub_main, uc_main, ub_halo, uc_halo, wb_ref, wc_ref, bc_ref, lg_ref, lb_ref,
               ob_ref, oc_ref, nb_ref, nc_ref, sb_ref, sc_ref,
               lambda f: pl.when(t == last)(f))


def conv_prompt(z, batch, seq, wb, wc, bc, lg, lb, *, tr=128):
    nt = seq // tr
    hp = tr // HALO
    main = lambda c: pl.BlockSpec((tr, B_WIDTH), lambda b, t: (b * nt + t, c))
    halo = lambda c: pl.BlockSpec((HALO, B_WIDTH), lambda b, t: (jnp.maximum((b * nt + t) * hp - 1, 0), c))
    vec = lambda r: pl.BlockSpec((r, B_WIDTH), lambda b, t: (0, 0))
    return pl.pallas_call(
        _conv_prompt_kernel,
        grid=(batch, nt),
        in_specs=[main(COL_GB), main(COL_GC), main(COL_HB), main(COL_GA), main(COL_GG),
                  halo(COL_GC), halo(COL_HB), halo(COL_GA), halo(COL_GG),
                  vec(B_CONV), vec(C_CONV), vec(1), vec(1), vec(1)],
        out_specs=[pl.BlockSpec((tr, B_WIDTH), lambda b, t: (b * nt + t, 0)),
                   pl.BlockSpec((tr, C_WIDTH), lambda b, t: (b * nt + t, 0)),
                   pl.BlockSpec((1, 8, B_WIDTH), lambda b, t: (b, 0, 0)),
                   pl.BlockSpec((1, HALO, C_WIDTH), lambda b, t: (b, 0, 0))],
        out_shape=[jax.ShapeDtypeStruct((batch * seq, B_WIDTH), BF16),
                   jax.ShapeDtypeStruct((batch * seq, C_WIDTH), BF16),
                   jax.ShapeDtypeStruct((batch, 8, B_WIDTH), F32),
                   jax.ShapeDtypeStruct((batch, HALO, C_WIDTH), F32)],
        scratch_shapes=[pltpu.VMEM((tr + HALO, B_WIDTH), F32), pltpu.VMEM((tr + HALO, C_WIDTH), F32)],
        compiler_params=_cparams(("parallel", "arbitrary")),
        name="conv_prompt",
    )(z, z, z, z, z, z, z, z, z, wb, wc, bc.reshape(1, -1), lg.reshape(1, -1), lb.reshape(1, -1))


def _conv_sample_kernel(gb_ref, gc_ref, hb_ref, ga_ref, gg_ref, stb_ref, stc_ref,
                        wb_ref, wc_ref, bc_ref, lg_ref, lb_ref,
                        ob_ref, oc_ref, nb_ref, nc_ref, sb_ref, sc_ref):
    ub_main = gc_ref[...].astype(F32) * hb_ref[...].astype(F32)
    uc_main = ga_ref[...].astype(F32) * jax.nn.sigmoid(gg_ref[...].astype(F32))
    _conv_body(gb_ref, ub_main, uc_main, stb_ref[0], stc_ref[0], wb_ref, wc_ref, bc_ref, lg_ref, lb_ref,
               ob_ref, oc_ref, nb_ref, nc_ref, sb_ref, sc_ref, lambda f: f())


def conv_sample(z, batch, rows, stb, stc, wb, wc, bc, lg, lb):
    main = lambda c: pl.BlockSpec((rows, B_WIDTH), lambda b: (b, c))
    vec = lambda r: pl.BlockSpec((r, B_WIDTH), lambda b: (0, 0))
    return pl.pallas_call(
        _conv_sample_kernel,
        grid=(batch,),
        in_specs=[main(COL_GB), main(COL_GC), main(COL_HB), main(COL_GA), main(COL_GG),
                  pl.BlockSpec((1, HALO, B_WIDTH), lambda b: (b, 0, 0)),
                  pl.BlockSpec((1, HALO, C_WIDTH), lambda b: (b, 0, 0)),
                  vec(B_CONV), vec(C_CONV), vec(1), vec(1), vec(1)],
        out_specs=[pl.BlockSpec((rows, B_WIDTH), lambda b: (b, 0)),
                   pl.BlockSpec((rows, C_WIDTH), lambda b: (b, 0)),
                   pl.BlockSpec((1, 8, B_WIDTH), lambda b: (b, 0, 0)),
                   pl.BlockSpec((1, HALO, C_WIDTH), lambda b: (b, 0, 0))],
        out_shape=[jax.ShapeDtypeStruct((batch * rows, B_WIDTH), BF16),
                   jax.ShapeDtypeStruct((batch * rows, C_WIDTH), BF16),
                   jax.ShapeDtypeStruct((batch, 8, B_WIDTH), F32),
                   jax.ShapeDtypeStruct((batch, HALO, C_WIDTH), F32)],
        scratch_shapes=[pltpu.VMEM((rows + HALO, B_WIDTH), F32), pltpu.VMEM((rows + HALO, C_WIDTH), F32)],
        compiler_params=_cparams(("parallel",)),
        name="conv_sample",
    )(z, z, z, z, z, stb, stc, wb, wc, bc.reshape(1, -1), lg.reshape(1, -1), lb.reshape(1, -1))


def _merge_kernel(oa_ref, ob_ref, oc_ref, l0_ref, l1_ref, l2_ref, bg_ref,
                  wpa_ref, wpb_ref, wpc_ref, wo_ref, x_ref, o_ref):
    def gated(l_ref, i, o_r, w_r):
        gate = jax.nn.sigmoid(l_ref[...].astype(F32) + bg_ref[i:i + 1, :])
        return gate * jnp.dot(o_r[...], w_r[...], preferred_element_type=F32)

    merged = gated(l0_ref, 0, oa_ref, wpa_ref)
    merged = merged + gated(l1_ref, 1, ob_ref, wpb_ref)
    merged = merged + gated(l2_ref, 2, oc_ref, wpc_ref)
    o_ref[...] = x_ref[...] + jnp.dot(merged.astype(BF16), wo_ref[...], preferred_element_type=F32)


def merge(oa, ob, oc, z, bg, wpa, wpb, wpc, wo, x, *, tm=256):
    m = x.shape[0]
    tm = min(tm, m)
    rows = lambda w: pl.BlockSpec((tm, w), lambda i: (i, 0))
    gate = lambda c: pl.BlockSpec((tm, D_MODEL), lambda i: (i, COL_GATE0 + c))
    return pl.pallas_call(
        _merge_kernel,
        grid=(m // tm,),
        in_specs=[rows(A_WIDTH), rows(B_WIDTH), rows(C_WIDTH), gate(0), gate(1), gate(2),
                  pl.BlockSpec((3, D_MODEL), lambda i: (0, 0)),
                  _resident((A_WIDTH, D_MODEL)), _resident((B_WIDTH, D_MODEL)),
                  _resident((C_WIDTH, D_MODEL)), _resident((D_MODEL, D_MODEL)),
                  rows(D_MODEL)],
        out_specs=rows(D_MODEL),
        out_shape=jax.ShapeDtypeStruct((m, D_MODEL), F32),
        compiler_params=_cparams(("parallel",)),
        name="merge",
    )(oa, ob, oc, z, z, z, bg.reshape(3, D_MODEL), wpa, wpb, wpc, wo, x)


def _xattn_kernel(y_ref, g_ref, wq_ref, qg_ref, mk_ref, mv_ref, wo_ref, o_ref, *, nb, rpb):
    y = y_ref[...]
    hn = _rms(y, g_ref[...]).astype(BF16)
    q = jnp.dot(hn, wq_ref[...], preferred_element_type=F32)
    heads = []
    for h in range(X_HEADS):
        sl = slice(h * X_HEAD_DIM, (h + 1) * X_HEAD_DIM)
        qh = _rms(q[:, sl], qg_ref[...])
        per_batch = []
        for b in range(nb):
            qb = qh[b * rpb:(b + 1) * rpb].astype(BF16)
            kh = mk_ref[b, :, sl].astype(BF16)
            vh = mv_ref[b, :, sl].astype(BF16)
            s = lax.dot_general(qb, kh, (((1,), (1,)), ((), ())),
                                preferred_element_type=F32) * (X_HEAD_DIM ** -0.5)
            m = jnp.max(s, axis=-1, keepdims=True)
            p = jnp.exp(s - m)
            den = jnp.sum(p, axis=-1, keepdims=True)
            per_batch.append(jnp.dot(p.astype(BF16), vh, preferred_element_type=F32) / den)
        heads.append(per_batch[0] if nb == 1 else jnp.concatenate(per_batch, axis=0))
    o = jnp.concatenate(heads, axis=1).astype(BF16)
    o_ref[...] = y + jnp.dot(o, wo_ref[...], preferred_element_type=F32)


def xattn(y, g, wq, qg, mk, mv, wo, *, nb, rpb, tiles_per_mem):
    m = y.shape[0]
    tm = nb * rpb
    mem_idx = (lambda i: (i // tiles_per_mem, 0, 0)) if nb == 1 else (lambda i: (i, 0, 0))
    return pl.pallas_call(
        functools.partial(_xattn_kernel, nb=nb, rpb=rpb),
        grid=(m // tm,),
        in_specs=[pl.BlockSpec((tm, D_MODEL), lambda i: (i, 0)),
                  pl.BlockSpec((1, D_MODEL), lambda i: (0, 0)),
                  _resident((D_MODEL, X_WIDTH)),
                  pl.BlockSpec((1, X_HEAD_DIM), lambda i: (0, 0)),
                  pl.BlockSpec((nb, N_MEM, X_WIDTH), mem_idx),
                  pl.BlockSpec((nb, N_MEM, X_WIDTH), mem_idx),
                  _resident((X_WIDTH, D_MODEL))],
        out_specs=pl.BlockSpec((tm, D_MODEL), lambda i: (i, 0)),
        out_shape=jax.ShapeDtypeStruct((m, D_MODEL), F32),
        compiler_params=_cparams(("parallel",)),
        name="xattn",
    )(y, g.reshape(1, -1), wq, qg.reshape(1, -1), mk, mv, wo)


def _ffn_kernel(*refs, expert):
    if expert is None:
        x_ref, g_ref, wg_ref, wu_ref, wd_ref, r_ref, o_ref, hn_ref, acc_ref = refs
    else:
        x_ref, g_ref, wg_ref, wu_ref, wd_ref, r_ref, gate_ref, o_ref, hn_ref, acc_ref = refs
    f = pl.program_id(1)

    @pl.when(f == 0)
    def _():
        hn_ref[...] = _rms(x_ref[...], g_ref[...]).astype(BF16)
        acc_ref[...] = jnp.zeros_like(acc_ref)

    hn = hn_ref[...]
    g = jnp.dot(hn, wg_ref[...].astype(BF16), preferred_element_type=F32)
    u = jnp.dot(hn, wu_ref[...].astype(BF16), preferred_element_type=F32)
    a = (g * jax.nn.sigmoid(g) * u).astype(BF16)
    acc_ref[...] += jnp.dot(a, wd_ref[...].astype(BF16), preferred_element_type=F32)

    @pl.when(f == pl.num_programs(1) - 1)
    def _():
        y = acc_ref[...]
        if expert is not None:
            lane = lax.broadcasted_iota(jnp.int32, (1, LANES), 1)
            y = y * jnp.sum(jnp.where(lane == expert, gate_ref[...], 0.0), axis=-1, keepdims=True)
        o_ref[...] = r_ref[...] + y


def ffn(x, g, w_gu, w_d, resid, *, e=0, gates=None, tm=512, tf=256):
    m = x.shape[0]
    tm = min(tm, m)
    nf = D_FF // tf
    in_specs = [pl.BlockSpec((tm, D_MODEL), lambda i, f: (i, 0)),
                pl.BlockSpec((1, D_MODEL), lambda i, f: (0, 0)),
                pl.BlockSpec((None, D_MODEL, tf), lambda i, f: (e, 0, f)),
                pl.BlockSpec((None, D_MODEL, tf), lambda i, f: (e, 0, nf + f)),
                pl.BlockSpec((None, tf, D_MODEL), lambda i, f: (e, f, 0)),
                pl.BlockSpec((tm, D_MODEL), lambda i, f: (i, 0))]
    args = [x, g.reshape(1, -1), w_gu, w_gu, w_d, resid]
    if gates is not None:
        in_specs.append(pl.BlockSpec((tm, LANES), lambda i, f: (i, 0)))
        args.append(gates)
    return pl.pallas_call(
        functools.partial(_ffn_kernel, expert=None if gates is None else e),
        grid=(m // tm, nf),
        in_specs=in_specs,
        out_specs=pl.BlockSpec((tm, D_MODEL), lambda i, f: (i, 0)),
        out_shape=jax.ShapeDtypeStruct((m, D_MODEL), F32),
        scratch_shapes=[pltpu.VMEM((tm, D_MODEL), BF16), pltpu.VMEM((tm, D_MODEL), F32)],
        compiler_params=_cparams(("parallel", "arbitrary")),
        name="ffn",
    )(*args)


def _split3(x):
    hi = x.astype(BF16)
    lo = (x - hi.astype(F32)).astype(BF16)
    return hi, lo


def _router_kernel(x_ref, g_ref, w_ref, b_ref, gate_ref):
    hn = _rms(x_ref[...], g_ref[...])
    h_hi, h_lo = _split3(hn)
    w_hi, w_lo = _split3(w_ref[...])
    dot = functools.partial(jnp.dot, preferred_element_type=F32)
    logits = dot(h_hi, w_hi) + dot(h_hi, w_lo) + dot(h_lo, w_hi) + b_ref[...]
    lane = lax.broadcasted_iota(jnp.int32, logits.shape, 1).astype(F32)
    logits = jnp.where(lane < N_EXPERTS, logits, -jnp.inf)
    v1 = jnp.max(logits, axis=-1, keepdims=True)
    i1 = jnp.min(jnp.where(logits == v1, lane, float(LANES)), axis=-1, keepdims=True)
    rest = jnp.where(lane == i1, -jnp.inf, logits)
    v2 = jnp.max(rest, axis=-1, keepdims=True)
    i2 = jnp.min(jnp.where(rest == v2, lane, float(LANES)), axis=-1, keepdims=True)
    e2 = jnp.exp(v2 - v1)
    den = 1.0 + e2
    gate_ref[...] = jnp.where(lane == i1, 1.0 / den, 0.0) + jnp.where(lane == i2, e2 / den, 0.0)


def router(x, g, w_pad, b_pad, *, tm=512):
    m = x.shape[0]
    tm = min(tm, m)
    return pl.pallas_call(
        _router_kernel,
        grid=(m // tm,),
        in_specs=[pl.BlockSpec((tm, D_MODEL), lambda i: (i, 0)),
                  pl.BlockSpec((1, D_MODEL), lambda i: (0, 0)),
                  pl.BlockSpec((D_MODEL, LANES), lambda i: (0, 0)),
                  pl.BlockSpec((1, LANES), lambda i: (0, 0))],
        out_specs=pl.BlockSpec((tm, LANES), lambda i: (i, 0)),
        out_shape=jax.ShapeDtypeStruct((m, LANES), F32),
        compiler_params=_cparams(("parallel",)),
        name="router",
    )(x, g.reshape(1, -1), w_pad, b_pad)


def _t5_bucket_np(rel):
    nb = N_BUCKETS // 2
    max_exact = nb // 2
    ret = np.where(rel > 0, nb, 0)
    n = np.abs(rel)
    nf = np.maximum(n, 1).astype(np.float32)
    large = max_exact + (np.log(nf / np.float32(max_exact)) / np.float32(math.log(MAX_DISTANCE / max_exact))
                         * np.float32(nb - max_exact)).astype(np.int32)
    large = np.minimum(large, nb - 1)
    return (ret + np.where(n < max_exact, n, large)).astype(np.int32)


def _bias_tensor(rel_table, n_q, valid):
    rel = np.arange(KEY_TILE, dtype=np.int32)[None, :] - WINDOW - np.arange(n_q, dtype=np.int32)[:, None]
    bias = jnp.take(rel_table, jnp.asarray(_t5_bucket_np(rel)), axis=0)
    bias = jnp.transpose(bias, (2, 0, 1))[HEAD_PERM]
    return jnp.where(jnp.asarray(valid)[None], bias, NEG).astype(F32)


def _prompt_valid():
    qc = np.arange(Q_TILE)[:, None] // CHUNK
    kc = np.arange(KEY_TILE)[None, :] // CHUNK
    return (kc >= qc) & (kc <= qc + WINDOW // CHUNK)


def _sample_valid(rows):
    return np.broadcast_to(np.arange(KEY_TILE)[None, :] < WINDOW + rows, (rows, KEY_TILE))


def _half_avg():
    blk = np.kron(np.eye(2), np.ones((A_HEAD_DIM, A_HEAD_DIM))) / A_HEAD_DIM
    return jnp.asarray(blk, BF16)


def _mixer_weights(l, g_mix, w_in, b_gate, q_norm_g, k_norm_g, sinks, w_conv_b, w_conv_c, b_conv_c,
                   ln_c_g, ln_c_b, w_proj_a, w_proj_b, w_proj_c, w_out):
    w = w_in[l]
    wq = w[:, :A_WIDTH].reshape(D_MODEL, A_HEADS, A_HEAD_DIM)[:, HEAD_PERM].reshape(D_MODEL, A_WIDTH)
    w_in_bf = jnp.concatenate([wq, w[:, A_WIDTH:]], axis=1).astype(BF16)
    wpa = w_proj_a[l].reshape(A_HEADS, A_HEAD_DIM, D_MODEL)[HEAD_PERM].reshape(A_WIDTH, D_MODEL)
    return dict(
        g_mix=g_mix[l], w_in=w_in_bf, b_gate=b_gate[l],
        qg2=jnp.tile(q_norm_g[l], 2).reshape(1, LANES), kg2=jnp.tile(k_norm_g[l], 2).reshape(1, LANES),
        sinks=sinks[l][HEAD_PERM],
        w_cb=w_conv_b[l], w_cc=w_conv_c[l], b_cc=b_conv_c[l], ln_g=ln_c_g[l], ln_b=ln_c_b[l],
        wpa=wpa.astype(BF16), wpb=w_proj_b[l].astype(BF16), wpc=w_proj_c[l].astype(BF16),
        wo=w_out[l].astype(BF16))


def _mixer_prompt(x, mw, bias, bd, batch, seq):
    z = norm_matmul(x, mw["g_mix"], mw["w_in"], tm=1024, tn=1024, out_dtype=BF16)
    oa, nk, nv = swa_prompt(z, batch, seq, mw["qg2"], mw["kg2"], bd, bias, mw["sinks"])
    ob, oc, ncb, ncc = conv_prompt(z, batch, seq, mw["w_cb"], mw["w_cc"], mw["b_cc"], mw["ln_g"], mw["ln_b"])
    y = merge(oa, ob, oc, z, mw["b_gate"], mw["wpa"], mw["wpb"], mw["wpc"], mw["wo"], x)
    return y, (nk, nv, ncb[:, 8 - (B_CONV - 1):], ncc[:, HALO - (C_CONV - 1):])


def _mixer_sample(x, mw, bias, bd, batch, rows, cache_k, cache_v, st_b, st_c):
    z = norm_matmul(x, mw["g_mix"], mw["w_in"], tm=x.shape[0], tn=1024, out_dtype=BF16)
    oa, nk, nv = swa_sample(z, batch, rows, cache_k, cache_v, mw["qg2"], mw["kg2"], bd, bias, mw["sinks"])
    stb = jnp.pad(st_b, ((0, 0), (HALO - (B_CONV - 1), 0), (0, 0)))
    stc = jnp.pad(st_c, ((0, 0), (HALO - (C_CONV - 1), 0), (0, 0)))
    ob, oc, ncb, ncc = conv_sample(z, batch, rows, stb, stc, mw["w_cb"], mw["w_cc"], mw["b_cc"],
                                   mw["ln_g"], mw["ln_b"])
    y = merge(oa, ob, oc, z, mw["b_gate"], mw["wpa"], mw["wpb"], mw["wpc"], mw["wo"], x)
    return y, (nk, nv, ncb[:, 8 - (B_CONV - 1):], ncc[:, HALO - (C_CONV - 1):])


def _channel_mixer(y, l, g_ffn, w_ffn_gu, w_ffn_d, w_router, b_router, w_moe_gu, w_moe_d):
    if l % 2 == 0:
        return ffn(y, g_ffn[l], w_ffn_gu, w_ffn_d, y, e=l // 2)
    i = l // 2
    w_pad = jnp.pad(w_router[i], ((0, 0), (0, LANES - N_EXPERTS)))
    b_pad = jnp.pad(b_router[i], (0, LANES - N_EXPERTS)).reshape(1, LANES)
    gates = router(y, g_ffn[l], w_pad, b_pad)
    out = y
    for e in range(N_EXPERTS):
        out = ffn(y, g_ffn[l], w_moe_gu[i], w_moe_d[i], out, e=e, gates=gates)
    return out


def kernel(x_prompt, x_sample, mem_prompt, cache_mem_k, cache_mem_v, cache_swa_k, cache_swa_v, state_conv_b, state_conv_c, rel_table, g_mix, w_in, b_gate, q_norm_g, k_norm_g, sinks, w_conv_b, w_conv_c, b_conv_c, ln_c_g, ln_c_b, w_proj_a, w_proj_b, w_proj_c, w_out, g_xattn, g_mem, w_xq, w_xkv, xq_norm_g, xk_norm_g, w_xo, g_ffn, w_ffn_gu, w_ffn_d, w_router, b_router, w_moe_gu, w_moe_d):
    batch, seq, d = x_prompt.shape
    dec_batch, dec_seq, _ = x_sample.shape
    depth = g_mix.shape[0]
    yp = x_prompt.reshape(batch * seq, d)
    ys = x_sample.reshape(dec_batch * dec_seq, d)
    mem = mem_prompt.reshape(batch * N_MEM, d)
    bd = _half_avg()
    bias_p = _bias_tensor(rel_table, Q_TILE, _prompt_valid())
    bias_s = _bias_tensor(rel_table, dec_seq, _sample_valid(dec_seq))
    outs = [[] for _ in range(10)]
    for l in range(depth):
        mw = _mixer_weights(l, g_mix, w_in, b_gate, q_norm_g, k_norm_g, sinks, w_conv_b, w_conv_c,
                            b_conv_c, ln_c_g, ln_c_b, w_proj_a, w_proj_b, w_proj_c, w_out)
        wq = w_xq[l].astype(BF16)
        wo = w_xo[l].astype(BF16)
        ffn_args = (g_ffn, w_ffn_gu, w_ffn_d, w_router, b_router, w_moe_gu, w_moe_d)
        yp, (nk, nv, ncb, ncc) = _mixer_prompt(yp, mw, bias_p, bd, batch, seq)
        mk, mv = mem_kv(mem, g_mem[l], w_xkv[l].astype(BF16), xk_norm_g[l])
        mk3 = mk.reshape(batch, N_MEM, X_WIDTH)
        mv3 = mv.reshape(batch, N_MEM, X_WIDTH)
        yp = xattn(yp, g_xattn[l], wq, xq_norm_g[l], mk3, mv3, wo, nb=1, rpb=512, tiles_per_mem=seq // 512)
        yp = _channel_mixer(yp, l, *ffn_args)
        for lst, v in zip(outs[:6], (mk3.reshape(batch, N_MEM, X_HEADS, X_HEAD_DIM),
                                     mv3.reshape(batch, N_MEM, X_HEADS, X_HEAD_DIM),
                                     nk.reshape(batch, WINDOW, A_KV_HEADS, A_HEAD_DIM),
                                     nv.reshape(batch, WINDOW, A_KV_HEADS, A_HEAD_DIM), ncb, ncc)):
            lst.append(v)
        ck = cache_swa_k[l].reshape(dec_batch, WINDOW, A_KV_WIDTH)
        cv = cache_swa_v[l].reshape(dec_batch, WINDOW, A_KV_WIDTH)
        ys, (nk, nv, ncb, ncc) = _mixer_sample(ys, mw, bias_s, bd, dec_batch, dec_seq, ck, cv,
                                               state_conv_b[l], state_conv_c[l])
        cmk = cache_mem_k[l].reshape(dec_batch, N_MEM, X_WIDTH)
        cmv = cache_mem_v[l].reshape(dec_batch, N_MEM, X_WIDTH)
        ys = xattn(ys, g_xattn[l], wq, xq_norm_g[l], cmk, cmv, wo, nb=8, rpb=dec_seq, tiles_per_mem=1)
        ys = _channel_mixer(ys, l, *ffn_args)
        for lst, v in zip(outs[6:], (nk.reshape(dec_batch, dec_seq, A_KV_HEADS, A_HEAD_DIM),
                                     nv.reshape(dec_batch, dec_seq, A_KV_HEADS, A_HEAD_DIM), ncb, ncc)):
            lst.append(v)
    return (yp.reshape(batch, seq, d), ys.reshape(dec_batch, dec_seq, d)) + tuple(jnp.stack(o) for o in outs)
```

```python
import functools
import math

import numpy as np
import jax
import jax.numpy as jnp
from jax import lax
from jax.experimental import pallas as pl
from jax.experimental.pallas import tpu as pltpu

F32 = jnp.float32
BF16 = jnp.bfloat16

D_MODEL = 2048
CHUNK = 64
A_HEADS = 16
A_KV_HEADS = 4
A_HEAD_DIM = 64
A_WIDTH = A_HEADS * A_HEAD_DIM
A_KV_WIDTH = A_KV_HEADS * A_HEAD_DIM
WINDOW = 128
N_BUCKETS = 32
MAX_DISTANCE = 128
B_WIDTH = 512
B_CONV = 3
C_WIDTH = 512
C_CONV = 31
N_MEM = 256
X_HEADS = 4
X_HEAD_DIM = 128
X_WIDTH = X_HEADS * X_HEAD_DIM
D_FF = 5632
N_EXPERTS = 8
EPS = 1e-6

LANES = 128
KEY_TILE = 256
Q_TILE = 128
HALO = 32
NEG = -1e30
VMEM_LIMIT = 56 * 1024 * 1024

COL_Q = 0
COL_K, COL_V = 4, 5
COL_GB, COL_GC, COL_HB, COL_GA, COL_GG = 3, 4, 5, 6, 7
COL_GATE0 = 2
IN_COLS = 4096 + 3 * D_MODEL

HEAD_PERM = np.array([8 * n + (p % 2) * 4 + p // 2 for n in range(2) for p in range(8)])


def _cparams(sem):
    return pltpu.CompilerParams(dimension_semantics=sem, vmem_limit_bytes=VMEM_LIMIT)


def _rms(x, g):
    ms = jnp.mean(x * x, axis=-1, keepdims=True)
    return x * lax.rsqrt(ms + EPS) * g


def _resident(shape):
    nd = len(shape)
    return pl.BlockSpec(shape, lambda *_: (0,) * nd, pipeline_mode=pl.Buffered(1))


def _norm_matmul_kernel(x_ref, g_ref, w_ref, o_ref, hn_ref):
    @pl.when(pl.program_id(1) == 0)
    def _():
        hn_ref[...] = _rms(x_ref[...], g_ref[...]).astype(BF16)

    o_ref[...] = jnp.dot(hn_ref[...], w_ref[...], preferred_element_type=F32).astype(o_ref.dtype)


def norm_matmul(x, g, w, *, tm, tn, out_dtype):
    m, k = x.shape
    n = w.shape[1]
    return pl.pallas_call(
        _norm_matmul_kernel,
        grid=(m // tm, n // tn),
        in_specs=[pl.BlockSpec((tm, k), lambda i, j: (i, 0)),
                  pl.BlockSpec((1, k), lambda i, j: (0, 0)),
                  pl.BlockSpec((k, tn), lambda i, j: (0, j))],
        out_specs=pl.BlockSpec((tm, tn), lambda i, j: (i, j)),
        out_shape=jax.ShapeDtypeStruct((m, n), out_dtype),
        scratch_shapes=[pltpu.VMEM((tm, k), BF16)],
        compiler_params=_cparams(("parallel", "arbitrary")),
        name="norm_matmul",
    )(x, g.reshape(1, k), w)


def _mem_kv_kernel(x_ref, g_ref, w_ref, kg_ref, k_ref, v_ref):
    hn = _rms(x_ref[...], g_ref[...]).astype(BF16)
    kv = jnp.dot(hn, w_ref[...], preferred_element_type=F32)
    for h in range(X_HEADS):
        sl = slice(h * X_HEAD_DIM, (h + 1) * X_HEAD_DIM)
        k_ref[:, sl] = _rms(kv[:, sl], kg_ref[...])
    v_ref[...] = kv[:, X_WIDTH:]


def mem_kv(mem, g, w_bf, kg):
    m, k = mem.shape
    tm = 256
    return pl.pallas_call(
        _mem_kv_kernel,
        grid=(m // tm,),
        in_specs=[pl.BlockSpec((tm, k), lambda i: (i, 0)),
                  pl.BlockSpec((1, k), lambda i: (0, 0)),
                  _resident((k, 2 * X_WIDTH)),
                  pl.BlockSpec((1, X_HEAD_DIM), lambda i: (0, 0))],
        out_specs=[pl.BlockSpec((tm, X_WIDTH), lambda i: (i, 0)),
                   pl.BlockSpec((tm, X_WIDTH), lambda i: (i, 0))],
        out_shape=[jax.ShapeDtypeStruct((m, X_WIDTH), F32)] * 2,
        compiler_params=_cparams(("parallel",)),
        name="mem_kv",
    )(mem, g.reshape(1, k), w_bf, kg.reshape(1, X_HEAD_DIM))


def _half_norm(x, g, bd):
    x2 = x * x
    hi = x2.astype(BF16)
    lo = (x2 - hi.astype(F32)).astype(BF16)
    ms = (jnp.dot(hi, bd, preferred_element_type=F32)
          + jnp.dot(lo, bd, preferred_element_type=F32))
    return x * lax.rsqrt(ms + EPS) * g


def _swa_heads(q_ref, qg, bd, k2, v2, bias_ref, sink_ref, dyn_mask, o_ref):
    lo_lane = lax.broadcasted_iota(jnp.int32, (1, LANES), 1) < A_HEAD_DIM
    k_half = []
    for n in range(2):
        k_half.append((jnp.where(lo_lane, k2[n], 0.0).astype(BF16),
                       jnp.where(lo_lane, 0.0, k2[n]).astype(BF16)))
    for c in range(A_WIDTH // LANES):
        n = c // 4
        qc = _half_norm(q_ref[:, c * LANES:(c + 1) * LANES].astype(F32), qg, bd).astype(BF16)
        halves = []
        for half in range(2):
            j = 2 * c + half
            s = lax.dot_general(qc, k_half[n][half], (((1,), (1,)), ((), ())),
                                preferred_element_type=F32)
            s = s * (A_HEAD_DIM ** -0.5) + bias_ref[j]
            if dyn_mask is not None:
                s = jnp.where(dyn_mask, NEG, s)
            sink = sink_ref[j]
            m = jnp.maximum(jnp.max(s, axis=-1, keepdims=True), sink)
            p = jnp.exp(s - m)
            den = jnp.sum(p, axis=-1, keepdims=True) + jnp.exp(sink - m)
            o = jnp.dot(p.astype(BF16), v2[n], preferred_element_type=F32)
            halves.append(o / den)
        o_ref[:, c * LANES:(c + 1) * LANES] = jnp.where(lo_lane, halves[0], halves[1]).astype(o_ref.dtype)


def _swa_prompt_kernel(q_ref, kc_ref, kp_ref, vc_ref, vp_ref, qg_ref, kg_ref, bd_ref,
                       bias_ref, sink_ref, o_ref, nk_ref, nv_ref):
    t = pl.program_id(1)
    bd = bd_ref[...]
    k2, v2 = [], []
    for n in range(2):
        sl = slice(n * LANES, (n + 1) * LANES)
        kcat = jnp.concatenate([kp_ref[:, sl], kc_ref[:, sl]], axis=0).astype(F32)
        kn = _half_norm(kcat, kg_ref[...], bd)
        nk_ref[0, :, sl] = kn[Q_TILE:]
        k2.append(kn)
        v2.append(jnp.concatenate([vp_ref[:, sl], vc_ref[:, sl]], axis=0))
    nv_ref[0] = vc_ref[...].astype(F32)
    col = lax.broadcasted_iota(jnp.int32, (1, KEY_TILE), 1)
    dyn_mask = jnp.logical_and(col < Q_TILE, t == 0)
    _swa_heads(q_ref, qg_ref[...], bd, k2, v2, bias_ref, sink_ref, dyn_mask, o_ref)


def swa_prompt(z, batch, seq, qg2, kg2, bd, bias, sinks):
    nt = seq // Q_TILE
    row = lambda b, t: b * nt + t
    prev = lambda b, t: jnp.maximum(b * nt + t - 1, 0)
    return pl.pallas_call(
        _swa_prompt_kernel,
        grid=(batch, nt),
        in_specs=[pl.BlockSpec((Q_TILE, A_WIDTH), lambda b, t: (row(b, t), COL_Q)),
                  pl.BlockSpec((Q_TILE, A_KV_WIDTH), lambda b, t: (row(b, t), COL_K)),
                  pl.BlockSpec((Q_TILE, A_KV_WIDTH), lambda b, t: (prev(b, t), COL_K)),
                  pl.BlockSpec((Q_TILE, A_KV_WIDTH), lambda b, t: (row(b, t), COL_V)),
                  pl.BlockSpec((Q_TILE, A_KV_WIDTH), lambda b, t: (prev(b, t), COL_V)),
                  pl.BlockSpec((1, LANES), lambda b, t: (0, 0)),
                  pl.BlockSpec((1, LANES), lambda b, t: (0, 0)),
                  pl.BlockSpec((LANES, LANES), lambda b, t: (0, 0)),
                  _resident((A_HEADS, Q_TILE, KEY_TILE)),
                  pl.BlockSpec(memory_space=pltpu.SMEM)],
        out_specs=[pl.BlockSpec((Q_TILE, A_WIDTH), lambda b, t: (row(b, t), 0)),
                   pl.BlockSpec((1, WINDOW, A_KV_WIDTH), lambda b, t: (b, 0, 0)),
                   pl.BlockSpec((1, WINDOW, A_KV_WIDTH), lambda b, t: (b, 0, 0))],
        out_shape=[jax.ShapeDtypeStruct((batch * seq, A_WIDTH), BF16),
                   jax.ShapeDtypeStruct((batch, WINDOW, A_KV_WIDTH), F32),
                   jax.ShapeDtypeStruct((batch, WINDOW, A_KV_WIDTH), F32)],
        compiler_params=_cparams(("parallel", "arbitrary")),
        name="swa_prompt",
    )(z, z, z, z, z, qg2, kg2, bd, bias, sinks)


def _swa_sample_kernel(q_ref, kn_ref, vn_ref, ck_ref, cv_ref, qg_ref, kg_ref, bd_ref,
                       bias_ref, sink_ref, o_ref, nk_ref, nv_ref):
    bd = bd_ref[...]
    rows = q_ref.shape[0]
    pad = KEY_TILE - WINDOW - rows
    k2, v2 = [], []
    for n in range(2):
        sl = slice(n * LANES, (n + 1) * LANES)
        kn = _half_norm(kn_ref[:, sl].astype(F32), kg_ref[...], bd)
        nk_ref[0, :, sl] = kn
        k2.append(jnp.concatenate([ck_ref[0, :, sl], kn, jnp.zeros((pad, LANES), F32)], axis=0))
        v2.append(jnp.concatenate([cv_ref[0, :, sl].astype(BF16), vn_ref[:, sl],
                                   jnp.zeros((pad, LANES), BF16)], axis=0))
    nv_ref[0] = vn_ref[...].astype(F32)
    _swa_heads(q_ref, qg_ref[...], bd, k2, v2, bias_ref, sink_ref, None, o_ref)


def swa_sample(z, batch, rows, cache_k, cache_v, qg2, kg2, bd, bias, sinks):
    return pl.pallas_call(
        _swa_sample_kernel,
        grid=(batch,),
        in_specs=[pl.BlockSpec((rows, A_WIDTH), lambda b: (b, COL_Q)),
                  pl.BlockSpec((rows, A_KV_WIDTH), lambda b: (b, COL_K)),
                  pl.BlockSpec((rows, A_KV_WIDTH), lambda b: (b, COL_V)),
                  pl.BlockSpec((1, WINDOW, A_KV_WIDTH), lambda b: (b, 0, 0)),
                  pl.BlockSpec((1, WINDOW, A_KV_WIDTH), lambda b: (b, 0, 0)),
                  pl.BlockSpec((1, LANES), lambda b: (0, 0)),
                  pl.BlockSpec((1, LANES), lambda b: (0, 0)),
                  pl.BlockSpec((LANES, LANES), lambda b: (0, 0)),
                  _resident((A_HEADS, rows, KEY_TILE)),
                  pl.BlockSpec(memory_space=pltpu.SMEM)],
        out_specs=[pl.BlockSpec((rows, A_WIDTH), lambda b: (b, 0)),
                   pl.BlockSpec((1, rows, A_KV_WIDTH), lambda b: (b, 0, 0)),
                   pl.BlockSpec((1, rows, A_KV_WIDTH), lambda b: (b, 0, 0))],
        out_shape=[jax.ShapeDtypeStruct((batch * rows, A_WIDTH), BF16),
                   jax.ShapeDtypeStruct((batch, rows, A_KV_WIDTH), F32),
                   jax.ShapeDtypeStruct((batch, rows, A_KV_WIDTH), F32)],
        compiler_params=_cparams(("parallel",)),
        name="swa_sample",
    )(z, z, z, cache_k, cache_v, qg2, kg2, bd, bias, sinks)


def _conv_body(gb_ref, ub_main, uc_main, ub_halo, uc_halo, wb_ref, wc_ref, bc_ref, lg_ref, lb_ref,
               ob_ref, oc_ref, nb_ref, nc_ref, sb_ref, sc_ref, write_state):
    rows = ub_main.shape[0]
    sb_ref[0:HALO] = ub_halo
    sb_ref[HALO:HALO + rows] = ub_main
    sc_ref[0:HALO] = uc_halo
    sc_ref[HALO:HALO + rows] = uc_main
    sub = min(rows, 32)
    for r0 in range(0, rows, sub):
        yb = jnp.zeros((sub, B_WIDTH), F32)
        for k in range(B_CONV):
            yb = yb + wb_ref[k:k + 1, :] * sb_ref[pl.ds(r0 + HALO - (B_CONV - 1) + k, sub), :]
        ob_ref[r0:r0 + sub, :] = (gb_ref[r0:r0 + sub, :].astype(F32) * yb).astype(ob_ref.dtype)
        yc = jnp.zeros((sub, C_WIDTH), F32)
        for k in range(C_CONV):
            yc = yc + wc_ref[k:k + 1, :] * sc_ref[pl.ds(r0 + HALO - (C_CONV - 1) + k, sub), :]
        yc = yc + bc_ref[...]
        mu = jnp.mean(yc, axis=-1, keepdims=True)
        xc = yc - mu
        y = xc * lax.rsqrt(jnp.mean(xc * xc, axis=-1, keepdims=True) + EPS)
        y = y * lg_ref[...] + lb_ref[...]
        oc_ref[r0:r0 + sub, :] = (y * jax.nn.sigmoid(y)).astype(oc_ref.dtype)

    def _state():
        nb_ref[0] = sb_ref[rows + HALO - 8:rows + HALO]
        nc_ref[0] = sc_ref[rows:rows + HALO]

    write_state(_state)


def _conv_prompt_kernel(gb_ref, gc_ref, hb_ref, ga_ref, gg_ref, gch_ref, hbh_ref, gah_ref, ggh_ref,
                        wb_ref, wc_ref, bc_ref, lg_ref, lb_ref,
                        ob_ref, oc_ref, nb_ref, nc_ref, sb_ref, sc_ref):
    t = pl.program_id(1)
    hist = (t > 0).astype(F32)
    ub_main = gc_ref[...].astype(F32) * hb_ref[...].astype(F32)
    uc_main = ga_ref[...].astype(F32) * jax.nn.sigmoid(gg_ref[...].astype(F32))
    ub_halo = gch_ref[...].astype(F32) * hbh_ref[...].astype(F32) * hist
    uc_halo = gah_ref[...].astype(F32) * jax.nn.sigmoid(ggh_ref[...].astype(F32)) * hist
    last = pl.num_programs(1) - 1
    _conv_body(gb_ref, ub_main, uc_main, ub_halo, uc_halo, wb_ref, wc_ref, bc_ref, lg_ref, lb_ref,
               ob_ref, oc_ref, nb_ref, nc_ref, sb_ref, sc_ref,
               lambda f: pl.when(t == last)(f))


def conv_prompt(z, batch, seq, wb, wc, bc, lg, lb, *, tr=128):
    nt = seq // tr
    hp = tr // HALO
    main = lambda c: pl.BlockSpec((tr, B_WIDTH), lambda b, t: (b * nt + t, c))
    halo = lambda c: pl.BlockSpec((HALO, B_WIDTH), lambda b, t: (jnp.maximum((b * nt + t) * hp - 1, 0), c))
    vec = lambda r: pl.BlockSpec((r, B_WIDTH), lambda b, t: (0, 0))
    return pl.pallas_call(
        _conv_prompt_kernel,
        grid=(batch, nt),
        in_specs=[main(COL_GB), main(COL_GC), main(COL_HB), main(COL_GA), main(COL_GG),
                  halo(COL_GC), halo(COL_HB), halo(COL_GA), halo(COL_GG),
                  vec(B_CONV), vec(C_CONV), vec(1), vec(1), vec(1)],
        out_specs=[pl.BlockSpec((tr, B_WIDTH), lambda b, t: (b * nt + t, 0)),
                   pl.BlockSpec((tr, C_WIDTH), lambda b, t: (b * nt + t, 0)),
                   pl.BlockSpec((1, 8, B_WIDTH), lambda b, t: (b, 0, 0)),
                   pl.BlockSpec((1, HALO, C_WIDTH), lambda b, t: (b, 0, 0))],
        out_shape=[jax.ShapeDtypeStruct((batch * seq, B_WIDTH), BF16),
                   jax.ShapeDtypeStruct((batch * seq, C_WIDTH), BF16),
                   jax.ShapeDtypeStruct((batch, 8, B_WIDTH), F32),
                   jax.ShapeDtypeStruct((batch, HALO, C_WIDTH), F32)],
        scratch_shapes=[pltpu.VMEM((tr + HALO, B_WIDTH), F32), pltpu.VMEM((tr + HALO, C_WIDTH), F32)],
        compiler_params=_cparams(("parallel", "arbitrary")),
        name="conv_prompt",
    )(z, z, z, z, z, z, z, z, z, wb, wc, bc.reshape(1, -1), lg.reshape(1, -1), lb.reshape(1, -1))


def _conv_sample_kernel(gb_ref, gc_ref, hb_ref, ga_ref, gg_ref, stb_ref, stc_ref,
                        wb_ref, wc_ref, bc_ref, lg_ref, lb_ref,
                        ob_ref, oc_ref, nb_ref, nc_ref, sb_ref, sc_ref):
    ub_main = gc_ref[...].astype(F32) * hb_ref[...].astype(F32)
    uc_main = ga_ref[...].astype(F32) * jax.nn.sigmoid(gg_ref[...].astype(F32))
    _conv_body(gb_ref, ub_main, uc_main, stb_ref[0], stc_ref[0], wb_ref, wc_ref, bc_ref, lg_ref, lb_ref,
               ob_ref, oc_ref, nb_ref, nc_ref, sb_ref, sc_ref, lambda f: f())


def conv_sample(z, batch, rows, stb, stc, wb, wc, bc, lg, lb):
    main = lambda c: pl.BlockSpec((rows, B_WIDTH), lambda b: (b, c))
    vec = lambda r: pl.BlockSpec((r, B_WIDTH), lambda b: (0, 0))
    return pl.pallas_call(
        _conv_sample_kernel,
        grid=(batch,),
        in_specs=[main(COL_GB), main(COL_GC), main(COL_HB), main(COL_GA), main(COL_GG),
                  pl.BlockSpec((1, HALO, B_WIDTH), lambda b: (b, 0, 0)),
                  pl.BlockSpec((1, HALO, C_WIDTH), lambda b: (b, 0, 0)),
                  vec(B_CONV), vec(C_CONV), vec(1), vec(1), vec(1)],
        out_specs=[pl.BlockSpec((rows, B_WIDTH), lambda b: (b, 0)),
                   pl.BlockSpec((rows, C_WIDTH), lambda b: (b, 0)),
                   pl.BlockSpec((1, 8, B_WIDTH), lambda b: (b, 0, 0)),
                   pl.BlockSpec((1, HALO, C_WIDTH), lambda b: (b, 0, 0))],
        out_shape=[jax.ShapeDtypeStruct((batch * rows, B_WIDTH), BF16),
                   jax.ShapeDtypeStruct((batch * rows, C_WIDTH), BF16),
                   jax.ShapeDtypeStruct((batch, 8, B_WIDTH), F32),
                   jax.ShapeDtypeStruct((batch, HALO, C_WIDTH), F32)],
        scratch_shapes=[pltpu.VMEM((rows + HALO, B_WIDTH), F32), pltpu.VMEM((rows + HALO, C_WIDTH), F32)],
        compiler_params=_cparams(("parallel",)),
        name="conv_sample",
    )(z, z, z, z, z, stb, stc, wb, wc, bc.reshape(1, -1), lg.reshape(1, -1), lb.reshape(1, -1))


def _merge_kernel(oa_ref, ob_ref, oc_ref, l0_ref, l1_ref, l2_ref, bg_ref,
                  wpa_ref, wpb_ref, wpc_ref, wo_ref, x_ref, o_ref):
    def gated(l_ref, i, o_r, w_r):
        gate = jax.nn.sigmoid(l_ref[...].astype(F32) + bg_ref[i:i + 1, :])
        return gate * jnp.dot(o_r[...], w_r[...], preferred_element_type=F32)

    merged = gated(l0_ref, 0, oa_ref, wpa_ref)
    merged = merged + gated(l1_ref, 1, ob_ref, wpb_ref)
    merged = merged + gated(l2_ref, 2, oc_ref, wpc_ref)
    o_ref[...] = x_ref[...] + jnp.dot(merged.astype(BF16), wo_ref[...], preferred_element_type=F32)


def merge(oa, ob, oc, z, bg, wpa, wpb, wpc, wo, x, *, tm=256):
    m = x.shape[0]
    tm = min(tm, m)
    rows = lambda w: pl.BlockSpec((tm, w), lambda i: (i, 0))
    gate = lambda c: pl.BlockSpec((tm, D_MODEL), lambda i: (i, COL_GATE0 + c))
    return pl.pallas_call(
        _merge_kernel,
        grid=(m // tm,),
        in_specs=[rows(A_WIDTH), rows(B_WIDTH), rows(C_WIDTH), gate(0), gate(1), gate(2),
                  pl.BlockSpec((3, D_MODEL), lambda i: (0, 0)),
                  _resident((A_WIDTH, D_MODEL)), _resident((B_WIDTH, D_MODEL)),
                  _resident((C_WIDTH, D_MODEL)), _resident((D_MODEL, D_MODEL)),
                  rows(D_MODEL)],
        out_specs=rows(D_MODEL),
        out_shape=jax.ShapeDtypeStruct((m, D_MODEL), F32),
        compiler_params=_cparams(("parallel",)),
        name="merge",
    )(oa, ob, oc, z, z, z, bg.reshape(3, D_MODEL), wpa, wpb, wpc, wo, x)


def _xattn_kernel(y_ref, g_ref, wq_ref, qg_ref, mk_ref, mv_ref, wo_ref, o_ref, *, nb, rpb):
    y = y_ref[...]
    hn = _rms(y, g_ref[...]).astype(BF16)
    q = jnp.dot(hn, wq_ref[...], preferred_element_type=F32)
    heads = []
    for h in range(X_HEADS):
        sl = slice(h * X_HEAD_DIM, (h + 1) * X_HEAD_DIM)
        qh = _rms(q[:, sl], qg_ref[...])
        per_batch = []
        for b in range(nb):
            qb = qh[b * rpb:(b + 1) * rpb].astype(BF16)
            kh = mk_ref[b, :, sl].astype(BF16)
            vh = mv_ref[b, :, sl].astype(BF16)
            s = lax.dot_general(qb, kh, (((1,), (1,)), ((), ())),
                                preferred_element_type=F32) * (X_HEAD_DIM ** -0.5)
            m = jnp.max(s, axis=-1, keepdims=True)
            p = jnp.exp(s - m)
            den = jnp.sum(p, axis=-1, keepdims=True)
            per_batch.append(jnp.dot(p.astype(BF16), vh, preferred_element_type=F32) / den)
        heads.append(per_batch[0] if nb == 1 else jnp.concatenate(per_batch, axis=0))
    o = jnp.concatenate(heads, axis=1).astype(BF16)
    o_ref[...] = y + jnp.dot(o, wo_ref[...], preferred_element_type=F32)


def xattn(y, g, wq, qg, mk, mv, wo, *, nb, rpb, tiles_per_mem):
    m = y.shape[0]
    tm = nb * rpb
    mem_idx = (lambda i: (i // tiles_per_mem, 0, 0)) if nb == 1 else (lambda i: (i, 0, 0))
    return pl.pallas_call(
        functools.partial(_xattn_kernel, nb=nb, rpb=rpb),
        grid=(m // tm,),
        in_specs=[pl.BlockSpec((tm, D_MODEL), lambda i: (i, 0)),
                  pl.BlockSpec((1, D_MODEL), lambda i: (0, 0)),
                  _resident((D_MODEL, X_WIDTH)),
                  pl.BlockSpec((1, X_HEAD_DIM), lambda i: (0, 0)),
                  pl.BlockSpec((nb, N_MEM, X_WIDTH), mem_idx),
                  pl.BlockSpec((nb, N_MEM, X_WIDTH), mem_idx),
                  _resident((X_WIDTH, D_MODEL))],
        out_specs=pl.BlockSpec((tm, D_MODEL), lambda i: (i, 0)),
        out_shape=jax.ShapeDtypeStruct((m, D_MODEL), F32),
        compiler_params=_cparams(("parallel",)),
        name="xattn",
    )(y, g.reshape(1, -1), wq, qg.reshape(1, -1), mk, mv, wo)


def _ffn_kernel(*refs, expert):
    if expert is None:
        x_ref, g_ref, wg_ref, wu_ref, wd_ref, r_ref, o_ref, hn_ref, acc_ref = refs
    else:
        x_ref, g_ref, wg_ref, wu_ref, wd_ref, r_ref, gate_ref, o_ref, hn_ref, acc_ref = refs
    f = pl.program_id(1)

    @pl.when(f == 0)
    def _():
        hn_ref[...] = _rms(x_ref[...], g_ref[...]).astype(BF16)
        acc_ref[...] = jnp.zeros_like(acc_ref)

    hn = hn_ref[...]
    g = jnp.dot(hn, wg_ref[...].astype(BF16), preferred_element_type=F32)
    u = jnp.dot(hn, wu_ref[...].astype(BF16), preferred_element_type=F32)
    a = (g * jax.nn.sigmoid(g) * u).astype(BF16)
    acc_ref[...] += jnp.dot(a, wd_ref[...].astype(BF16), preferred_element_type=F32)

    @pl.when(f == pl.num_programs(1) - 1)
    def _():
        y = acc_ref[...]
        if expert is not None:
            lane = lax.broadcasted_iota(jnp.int32, (1, LANES), 1)
            y = y * jnp.sum(jnp.where(lane == expert, gate_ref[...], 0.0), axis=-1, keepdims=True)
        o_ref[...] = r_ref[...] + y


def ffn(x, g, w_gu, w_d, resid, *, e=0, gates=None, tm=512, tf=256):
    m = x.shape[0]
    tm = min(tm, m)
    nf = D_FF // tf
    in_specs = [pl.BlockSpec((tm, D_MODEL), lambda i, f: (i, 0)),
                pl.BlockSpec((1, D_MODEL), lambda i, f: (0, 0)),
                pl.BlockSpec((None, D_MODEL, tf), lambda i, f: (e, 0, f)),
                pl.BlockSpec((None, D_MODEL, tf), lambda i, f: (e, 0, nf + f)),
                pl.BlockSpec((None, tf, D_MODEL), lambda i, f: (e, f, 0)),
                pl.BlockSpec((tm, D_MODEL), lambda i, f: (i, 0))]
    args = [x, g.reshape(1, -1), w_gu, w_gu, w_d, resid]
    if gates is not None:
        in_specs.append(pl.BlockSpec((tm, LANES), lambda i, f: (i, 0)))
        args.append(gates)
    return pl.pallas_call(
        functools.partial(_ffn_kernel, expert=None if gates is None else e),
        grid=(m // tm, nf),
        in_specs=in_specs,
        out_specs=pl.BlockSpec((tm, D_MODEL), lambda i, f: (i, 0)),
        out_shape=jax.ShapeDtypeStruct((m, D_MODEL), F32),
        scratch_shapes=[pltpu.VMEM((tm, D_MODEL), BF16), pltpu.VMEM((tm, D_MODEL), F32)],
        compiler_params=_cparams(("parallel", "arbitrary")),
        name="ffn",
    )(*args)


def _split3(x):
    hi = x.astype(BF16)
    lo = (x - hi.astype(F32)).astype(BF16)
    return hi, lo


def _router_kernel(x_ref, g_ref, w_ref, b_ref, wts_ref, ids_ref, hnp_ref):
    hn = _rms(x_ref[...], g_ref[...])
    h_hi, h_lo = _split3(hn)
    w_hi, w_lo = _split3(w_ref[...])
    dot = functools.partial(jnp.dot, preferred_element_type=F32)
    logits = dot(h_hi, w_hi) + dot(h_hi, w_lo) + dot(h_lo, w_hi) + b_ref[...]
    lane = lax.broadcasted_iota(jnp.int32, logits.shape, 1).astype(F32)
    logits = jnp.where(lane < N_EXPERTS, logits, -jnp.inf)
    v1 = jnp.max(logits, axis=-1, keepdims=True)
    i1 = jnp.min(jnp.where(logits == v1, lane, float(LANES)), axis=-1, keepdims=True)
    rest = jnp.where(lane == i1, -jnp.inf, logits)
    v2 = jnp.max(rest, axis=-1, keepdims=True)
    i2 = jnp.min(jnp.where(rest == v2, lane, float(LANES)), axis=-1, keepdims=True)
    e2 = jnp.exp(v2 - v1)
    den = 1.0 + e2
    wts_ref[...] = jnp.where(lane == 0.0, 1.0 / den, 0.0) + jnp.where(lane == 1.0, e2 / den, 0.0)
    ids_ref[...] = (jnp.where(lane == 0.0, i1, 0.0) + jnp.where(lane == 1.0, i2, 0.0)).astype(jnp.int32)
    half = D_MODEL // 2
    lo_bits = lax.bitcast_convert_type(h_hi[:, :half].astype(F32), jnp.uint32)
    hi_bits = lax.bitcast_convert_type(h_hi[:, half:].astype(F32), jnp.uint32)
    hnp_ref[...] = (lo_bits >> 16) | (hi_bits & jnp.uint32(0xFFFF0000))


def router(x, g, w_pad, b_pad, *, tm=512):
    m = x.shape[0]
    tm = min(tm, m)
    return pl.pallas_call(
        _router_kernel,
        grid=(m // tm,),
        in_specs=[pl.BlockSpec((tm, D_MODEL), lambda i: (i, 0)),
                  pl.BlockSpec((1, D_MODEL), lambda i: (0, 0)),
                  pl.BlockSpec((D_MODEL, LANES), lambda i: (0, 0)),
                  pl.BlockSpec((1, LANES), lambda i: (0, 0))],
        out_specs=[pl.BlockSpec((tm, LANES), lambda i: (i, 0)),
                   pl.BlockSpec((tm, LANES), lambda i: (i, 0)),
                   pl.BlockSpec((tm, D_MODEL // 2), lambda i: (i, 0))],
        out_shape=[jax.ShapeDtypeStruct((m, LANES), F32),
                   jax.ShapeDtypeStruct((m, LANES), jnp.int32),
                   jax.ShapeDtypeStruct((m, D_MODEL // 2), jnp.uint32)],
        compiler_params=_cparams(("parallel",)),
        name="router",
    )(x, g.reshape(1, -1), w_pad, b_pad)


MOE_TM = 1024
MOE_TF = 256


def _row_copy(src_hbm, row, dst_vmem, r, sem):
    return pltpu.make_async_copy(src_hbm.at[pl.ds(row, 1)], dst_vmem.at[pl.ds(r, 1)], sem)


def _moe_ffn_kernel(te_ref, tv_ref, src_ref, hnp_hbm, wg_ref, wu_ref, wd_ref, o_ref, xbuf, hn_ref, sem):
    t = pl.program_id(0)
    f = pl.program_id(1)
    valid = tv_ref[t] != 0
    rows = xbuf.shape[0]
    half = D_MODEL // 2

    @pl.when(jnp.logical_and(valid, f == 0))
    def _():
        def issue(r, c):
            _row_copy(hnp_hbm, src_ref[0, 0, r], xbuf, r, sem).start()
            return c

        lax.fori_loop(0, rows, issue, 0)

        def wait(r, c):
            _row_copy(hnp_hbm, 0, xbuf, r, sem).wait()
            return c

        lax.fori_loop(0, rows, wait, 0)
        xu = xbuf[...]
        hn_ref[:, :half] = lax.bitcast_convert_type(xu << 16, F32).astype(BF16)
        hn_ref[:, half:] = lax.bitcast_convert_type(xu & jnp.uint32(0xFFFF0000), F32).astype(BF16)

    @pl.when(valid)
    def _():
        hn = hn_ref[...]
        g = jnp.dot(hn, wg_ref[...].astype(BF16), preferred_element_type=F32)
        u = jnp.dot(hn, wu_ref[...].astype(BF16), preferred_element_type=F32)
        a = (g * jax.nn.sigmoid(g) * u).astype(BF16)
        d = jnp.dot(a, wd_ref[...].astype(BF16), preferred_element_type=F32)

        @pl.when(f == 0)
        def _():
            o_ref[...] = d

        @pl.when(f > 0)
        def _():
            o_ref[...] += d

    @pl.when(jnp.logical_and(jnp.logical_not(valid), f == 0))
    def _():
        o_ref[...] = jnp.zeros_like(o_ref)


def moe_ffn(tile_expert, tile_valid, src, hnp, w_gu, w_d):
    nt, _, tm = src.shape
    nf = D_FF // MOE_TF
    last = nf - 1
    col = lambda f, tv, t: jnp.where(tv[t] != 0, f, last)
    grid_spec = pltpu.PrefetchScalarGridSpec(
        num_scalar_prefetch=2,
        grid=(nt, nf),
        in_specs=[pl.BlockSpec((1, 1, tm), lambda t, f, te, tv: (t, 0, 0), memory_space=pltpu.SMEM),
                  pl.BlockSpec(memory_space=pl.ANY),
                  pl.BlockSpec((None, D_MODEL, MOE_TF), lambda t, f, te, tv: (te[t], 0, col(f, tv, t))),
                  pl.BlockSpec((None, D_MODEL, MOE_TF), lambda t, f, te, tv: (te[t], 0, nf + col(f, tv, t))),
                  pl.BlockSpec((None, MOE_TF, D_MODEL), lambda t, f, te, tv: (te[t], col(f, tv, t), 0))],
        out_specs=pl.BlockSpec((tm, D_MODEL), lambda t, f, te, tv: (t, 0)),
        scratch_shapes=[pltpu.VMEM((tm, D_MODEL // 2), jnp.uint32),
                        pltpu.VMEM((tm, D_MODEL), BF16),
                        pltpu.SemaphoreType.DMA],
    )
    return pl.pallas_call(
        _moe_ffn_kernel,
        grid_spec=grid_spec,
        out_shape=jax.ShapeDtypeStruct((nt * tm, D_MODEL), F32),
        compiler_params=_cparams(("arbitrary", "arbitrary")),
        name="moe_ffn",
    )(tile_expert, tile_valid, src, hnp, w_gu, w_gu, w_d)


def _moe_combine_kernel(pos_ref, x_ref, w_ref, osort_hbm, o_ref, abuf, sem):
    tm = x_ref.shape[0]

    def issue(r, c):
        _row_copy(osort_hbm, pos_ref[0, 0, r], abuf, r, sem).start()
        return c

    lax.fori_loop(0, 2 * tm, issue, 0)

    def wait(r, c):
        _row_copy(osort_hbm, 0, abuf, r, sem).wait()
        return c

    lax.fori_loop(0, 2 * tm, wait, 0)
    w = w_ref[...]
    o_ref[...] = x_ref[...] + w[:, 0:1] * abuf[0:tm, :] + w[:, 1:2] * abuf[tm:2 * tm, :]


def moe_combine(pos, x, wts, osort, *, tm=256):
    m = x.shape[0]
    return pl.pallas_call(
        _moe_combine_kernel,
        grid=(m // tm,),
        in_specs=[pl.BlockSpec((1, 1, 2 * tm), lambda i: (i, 0, 0), memory_space=pltpu.SMEM),
                  pl.BlockSpec((tm, D_MODEL), lambda i: (i, 0)),
                  pl.BlockSpec((tm, LANES), lambda i: (i, 0)),
                  pl.BlockSpec(memory_space=pl.ANY)],
        out_specs=pl.BlockSpec((tm, D_MODEL), lambda i: (i, 0)),
        out_shape=jax.ShapeDtypeStruct((m, D_MODEL), F32),
        scratch_shapes=[pltpu.VMEM((2 * tm, D_MODEL), F32), pltpu.SemaphoreType.DMA],
        compiler_params=_cparams(("arbitrary",)),
        name="moe_combine",
    )(pos, x, wts, osort)


def _dispatch_plan(ids, tm):
    m = ids.shape[0]
    nt = (2 * m) // tm + N_EXPERTS
    flat = ids.reshape(-1)
    onehot = (flat[:, None] == jnp.arange(N_EXPERTS, dtype=jnp.int32)[None, :]).astype(jnp.int32)
    csum = jnp.cumsum(onehot, axis=0)
    rank = jnp.sum((csum - onehot) * onehot, axis=1)
    counts = csum[-1]
    padded = ((counts + tm - 1) // tm) * tm
    ends = jnp.cumsum(padded)
    pos = (ends - padded)[flat] + rank
    src = jnp.zeros((nt * tm,), jnp.int32).at[pos].set(jnp.arange(2 * m, dtype=jnp.int32) // 2)
    starts = jnp.arange(nt, dtype=jnp.int32) * tm
    tile_valid = (starts < ends[-1]).astype(jnp.int32)
    tile_expert = jnp.minimum(jnp.sum((starts[:, None] >= ends[None, :]).astype(jnp.int32), axis=1),
                              N_EXPERTS - 1)
    last_valid = jnp.maximum(jnp.sum(tile_valid) - 1, 0)
    tile_expert = jnp.where(tile_valid != 0, tile_expert, tile_expert[last_valid])
    return tile_expert, tile_valid, src.reshape(nt, 1, tm), pos.reshape(m, 2)


def _combine_pos(pos, tm):
    m = pos.shape[0]
    return jnp.transpose(pos.reshape(m // tm, tm, 2), (0, 2, 1)).reshape(m // tm, 1, 2 * tm)


def _t5_bucket_np(rel):
    nb = N_BUCKETS // 2
    max_exact = nb // 2
    ret = np.where(rel > 0, nb, 0)
    n = np.abs(rel)
    nf = np.maximum(n, 1).astype(np.float32)
    large = max_exact + (np.log(nf / np.float32(max_exact)) / np.float32(math.log(MAX_DISTANCE / max_exact))
                         * np.float32(nb - max_exact)).astype(np.int32)
    large = np.minimum(large, nb - 1)
    return (ret + np.where(n < max_exact, n, large)).astype(np.int32)


def _bias_tensor(rel_table, n_q, valid):
    rel = np.arange(KEY_TILE, dtype=np.int32)[None, :] - WINDOW - np.arange(n_q, dtype=np.int32)[:, None]
    bias = jnp.take(rel_table, jnp.asarray(_t5_bucket_np(rel)), axis=0)
    bias = jnp.transpose(bias, (2, 0, 1))[HEAD_PERM]
    return jnp.where(jnp.asarray(valid)[None], bias, NEG).astype(F32)


def _prompt_valid():
    qc = np.arange(Q_TILE)[:, None] // CHUNK
    kc = np.arange(KEY_TILE)[None, :] // CHUNK
    return (kc >= qc) & (kc <= qc + WINDOW // CHUNK)


def _sample_valid(rows):
    return np.broadcast_to(np.arange(KEY_TILE)[None, :] < WINDOW + rows, (rows, KEY_TILE))


def _half_avg():
    blk = np.kron(np.eye(2), np.ones((A_HEAD_DIM, A_HEAD_DIM))) / A_HEAD_DIM
    return jnp.asarray(blk, BF16)


def _mixer_weights(l, g_mix, w_in, b_gate, q_norm_g, k_norm_g, sinks, w_conv_b, w_conv_c, b_conv_c,
                   ln_c_g, ln_c_b, w_proj_a, w_proj_b, w_proj_c, w_out):
    w = w_in[l]
    wq = w[:, :A_WIDTH].reshape(D_MODEL, A_HEADS, A_HEAD_DIM)[:, HEAD_PERM].reshape(D_MODEL, A_WIDTH)
    w_in_bf = jnp.concatenate([wq, w[:, A_WIDTH:]], axis=1).astype(BF16)
    wpa = w_proj_a[l].reshape(A_HEADS, A_HEAD_DIM, D_MODEL)[HEAD_PERM].reshape(A_WIDTH, D_MODEL)
    return dict(
        g_mix=g_mix[l], w_in=w_in_bf, b_gate=b_gate[l],
        qg2=jnp.tile(q_norm_g[l], 2).reshape(1, LANES), kg2=jnp.tile(k_norm_g[l], 2).reshape(1, LANES),
        sinks=sinks[l][HEAD_PERM],
        w_cb=w_conv_b[l], w_cc=w_conv_c[l], b_cc=b_conv_c[l], ln_g=ln_c_g[l], ln_b=ln_c_b[l],
        wpa=wpa.astype(BF16), wpb=w_proj_b[l].astype(BF16), wpc=w_proj_c[l].astype(BF16),
        wo=w_out[l].astype(BF16))


def _mixer_prompt(x, mw, bias, bd, batch, seq):
    z = norm_matmul(x, mw["g_mix"], mw["w_in"], tm=1024, tn=1024, out_dtype=BF16)
    oa, nk, nv = swa_prompt(z, batch, seq, mw["qg2"], mw["kg2"], bd, bias, mw["sinks"])
    ob, oc, ncb, ncc = conv_prompt(z, batch, seq, mw["w_cb"], mw["w_cc"], mw["b_cc"], mw["ln_g"], mw["ln_b"])
    y = merge(oa, ob, oc, z, mw["b_gate"], mw["wpa"], mw["wpb"], mw["wpc"], mw["wo"], x)
    return y, (nk, nv, ncb[:, 8 - (B_CONV - 1):], ncc[:, HALO - (C_CONV - 1):])


def _mixer_sample(x, mw, bias, bd, batch, rows, cache_k, cache_v, st_b, st_c):
    z = norm_matmul(x, mw["g_mix"], mw["w_in"], tm=x.shape[0], tn=1024, out_dtype=BF16)
    oa, nk, nv = swa_sample(z, batch, rows, cache_k, cache_v, mw["qg2"], mw["kg2"], bd, bias, mw["sinks"])
    stb = jnp.pad(st_b, ((0, 0), (HALO - (B_CONV - 1), 0), (0, 0)))
    stc = jnp.pad(st_c, ((0, 0), (HALO - (C_CONV - 1), 0), (0, 0)))
    ob, oc, ncb, ncc = conv_sample(z, batch, rows, stb, stc, mw["w_cb"], mw["w_cc"], mw["b_cc"],
                                   mw["ln_g"], mw["ln_b"])
    y = merge(oa, ob, oc, z, mw["b_gate"], mw["wpa"], mw["wpb"], mw["wpc"], mw["wo"], x)
    return y, (nk, nv, ncb[:, 8 - (B_CONV - 1):], ncc[:, HALO - (C_CONV - 1):])


def _channel_mixer(yp, ys, l, g_ffn, w_ffn_gu, w_ffn_d, w_router, b_router, w_moe_gu, w_moe_d):
    if l % 2 == 0:
        return (ffn(yp, g_ffn[l], w_ffn_gu, w_ffn_d, yp, e=l // 2),
                ffn(ys, g_ffn[l], w_ffn_gu, w_ffn_d, ys, e=l // 2))
    i = l // 2
    w_pad = jnp.pad(w_router[i], ((0, 0), (0, LANES - N_EXPERTS)))
    b_pad = jnp.pad(b_router[i], (0, LANES - N_EXPERTS)).reshape(1, LANES)
    wts_p, ids_p, hnp_p = router(yp, g_ffn[l], w_pad, b_pad)
    wts_s, ids_s, hnp_s = router(ys, g_ffn[l], w_pad, b_pad)
    ids = jnp.concatenate([ids_p[:, :2], ids_s[:, :2]], axis=0)
    hnp = jnp.concatenate([hnp_p, hnp_s], axis=0)
    tile_expert, tile_valid, src, pos = _dispatch_plan(ids, MOE_TM)
    osort = moe_ffn(tile_expert, tile_valid, src, hnp, w_moe_gu[i], w_moe_d[i])
    mp = yp.shape[0]
    tm_p, tm_s = 256, min(256, ys.shape[0])
    return (moe_combine(_combine_pos(pos[:mp], tm_p), yp, wts_p, osort, tm=tm_p),
            moe_combine(_combine_pos(pos[mp:], tm_s), ys, wts_s, osort, tm=tm_s))


def kernel(x_prompt, x_sample, mem_prompt, cache_mem_k, cache_mem_v, cache_swa_k, cache_swa_v, state_conv_b, state_conv_c, rel_table, g_mix, w_in, b_gate, q_norm_g, k_norm_g, sinks, w_conv_b, w_conv_c, b_conv_c, ln_c_g, ln_c_b, w_proj_a, w_proj_b, w_proj_c, w_out, g_xattn, g_mem, w_xq, w_xkv, xq_norm_g, xk_norm_g, w_xo, g_ffn, w_ffn_gu, w_ffn_d, w_router, b_router, w_moe_gu, w_moe_d):
    batch, seq, d = x_prompt.shape
    dec_batch, dec_seq, _ = x_sample.shape
    depth = g_mix.shape[0]
    yp = x_prompt.reshape(batch * seq, d)
    ys = x_sample.reshape(dec_batch * dec_seq, d)
    mem = mem_prompt.reshape(batch * N_MEM, d)
    bd = _half_avg()
    bias_p = _bias_tensor(rel_table, Q_TILE, _prompt_valid())
    bias_s = _bias_tensor(rel_table, dec_seq, _sample_valid(dec_seq))
    outs = [[] for _ in range(10)]
    for l in range(depth):
        mw = _mixer_weights(l, g_mix, w_in, b_gate, q_norm_g, k_norm_g, sinks, w_conv_b, w_conv_c,
                            b_conv_c, ln_c_g, ln_c_b, w_proj_a, w_proj_b, w_proj_c, w_out)
        wq = w_xq[l].astype(BF16)
        wo = w_xo[l].astype(BF16)
        ffn_args = (g_ffn, w_ffn_gu, w_ffn_d, w_router, b_router, w_moe_gu, w_moe_d)
        yp, (nk, nv, ncb, ncc) = _mixer_prompt(yp, mw, bias_p, bd, batch, seq)
        mk, mv = mem_kv(mem, g_mem[l], w_xkv[l].astype(BF16), xk_norm_g[l])
        mk3 = mk.reshape(batch, N_MEM, X_WIDTH)
        mv3 = mv.reshape(batch, N_MEM, X_WIDTH)
        yp = xattn(yp, g_xattn[l], wq, xq_norm_g[l], mk3, mv3, wo, nb=1, rpb=512, tiles_per_mem=seq // 512)
        for lst, v in zip(outs[:6], (mk3.reshape(batch, N_MEM, X_HEADS, X_HEAD_DIM),
                                     mv3.reshape(batch, N_MEM, X_HEADS, X_HEAD_DIM),
                                     nk.reshape(batch, WINDOW, A_KV_HEADS, A_HEAD_DIM),
                                     nv.reshape(batch, WINDOW, A_KV_HEADS, A_HEAD_DIM), ncb, ncc)):
            lst.append(v)
        ck = cache_swa_k[l].reshape(dec_batch, WINDOW, A_KV_WIDTH)
        cv = cache_swa_v[l].reshape(dec_batch, WINDOW, A_KV_WIDTH)
        ys, (nk, nv, ncb, ncc) = _mixer_sample(ys, mw, bias_s, bd, dec_batch, dec_seq, ck, cv,
                                               state_conv_b[l], state_conv_c[l])
        cmk = cache_mem_k[l].reshape(dec_batch, N_MEM, X_WIDTH)
        cmv = cache_mem_v[l].reshape(dec_batch, N_MEM, X_WIDTH)
        ys = xattn(ys, g_xattn[l], wq, xq_norm_g[l], cmk, cmv, wo, nb=8, rpb=dec_seq, tiles_per_mem=1)
        yp, ys = _channel_mixer(yp, ys, l, *ffn_args)
        for lst, v in zip(outs[6:], (nk.reshape(dec_batch, dec_seq, A_KV_HEADS, A_HEAD_DIM),
                                     nv.reshape(dec_batch, dec_seq, A_KV_HEADS, A_HEAD_DIM), ncb, ncc)):
            lst.append(v)
    return (yp.reshape(batch, seq, d), ys.reshape(dec_batch, dec_seq, d)) + tuple(jnp.stack(o) for o in outs)
```

```python
import functools
import math

import numpy as np
import jax
import jax.numpy as jnp
from jax import lax
from jax.experimental import pallas as pl
from jax.experimental.pallas import tpu as pltpu

F32 = jnp.float32
BF16 = jnp.bfloat16

D_MODEL = 2048
CHUNK = 64
A_HEADS = 16
A_KV_HEADS = 4
A_HEAD_DIM = 64
A_WIDTH = A_HEADS * A_HEAD_DIM
A_KV_WIDTH = A_KV_HEADS * A_HEAD_DIM
WINDOW = 128
N_BUCKETS = 32
MAX_DISTANCE = 128
B_WIDTH = 512
B_CONV = 3
C_WIDTH = 512
C_CONV = 31
N_MEM = 256
X_HEADS = 4
X_HEAD_DIM = 128
X_WIDTH = X_HEADS * X_HEAD_DIM
D_FF = 5632
N_EXPERTS = 8
EPS = 1e-6

LANES = 128
KEY_TILE = 256
Q_TILE = 128
HALO = 32
NEG = -1e30
VMEM_LIMIT = 56 * 1024 * 1024

COL_Q = 0
COL_K, COL_V = 4, 5
COL_GB, COL_GC, COL_HB, COL_GA, COL_GG = 3, 4, 5, 6, 7
COL_GATE0 = 2
IN_COLS = 4096 + 3 * D_MODEL

HEAD_PERM = np.array([8 * n + (p % 2) * 4 + p // 2 for n in range(2) for p in range(8)])


def _cparams(sem):
    return pltpu.CompilerParams(dimension_semantics=sem, vmem_limit_bytes=VMEM_LIMIT)


def _rms(x, g):
    ms = jnp.mean(x * x, axis=-1, keepdims=True)
    return x * lax.rsqrt(ms + EPS) * g


def _resident(shape):
    nd = len(shape)
    return pl.BlockSpec(shape, lambda *_: (0,) * nd, pipeline_mode=pl.Buffered(1))


def _norm_matmul_kernel(x_ref, g_ref, w_ref, o_ref, hn_ref):
    @pl.when(pl.program_id(1) == 0)
    def _():
        hn_ref[...] = _rms(x_ref[...], g_ref[...]).astype(BF16)

    o_ref[...] = jnp.dot(hn_ref[...], w_ref[...], preferred_element_type=F32).astype(o_ref.dtype)


def norm_matmul(x, g, w, *, tm, tn, out_dtype):
    m, k = x.shape
    n = w.shape[1]
    return pl.pallas_call(
        _norm_matmul_kernel,
        grid=(m // tm, n // tn),
        in_specs=[pl.BlockSpec((tm, k), lambda i, j: (i, 0)),
                  pl.BlockSpec((1, k), lambda i, j: (0, 0)),
                  pl.BlockSpec((k, tn), lambda i, j: (0, j))],
        out_specs=pl.BlockSpec((tm, tn), lambda i, j: (i, j)),
        out_shape=jax.ShapeDtypeStruct((m, n), out_dtype),
        scratch_shapes=[pltpu.VMEM((tm, k), BF16)],
        compiler_params=_cparams(("parallel", "arbitrary")),
        name="norm_matmul",
    )(x, g.reshape(1, k), w)


def _mem_kv_kernel(x_ref, g_ref, w_ref, kg_ref, k_ref, v_ref):
    hn = _rms(x_ref[...], g_ref[...]).astype(BF16)
    kv = jnp.dot(hn, w_ref[...], preferred_element_type=F32)
    for h in range(X_HEADS):
        sl = slice(h * X_HEAD_DIM, (h + 1) * X_HEAD_DIM)
        k_ref[:, sl] = _rms(kv[:, sl], kg_ref[...])
    v_ref[...] = kv[:, X_WIDTH:]


def mem_kv(mem, g, w_bf, kg):
    m, k = mem.shape
    tm = 256
    return pl.pallas_call(
        _mem_kv_kernel,
        grid=(m // tm,),
        in_specs=[pl.BlockSpec((tm, k), lambda i: (i, 0)),
                  pl.BlockSpec((1, k), lambda i: (0, 0)),
                  _resident((k, 2 * X_WIDTH)),
                  pl.BlockSpec((1, X_HEAD_DIM), lambda i: (0, 0))],
        out_specs=[pl.BlockSpec((tm, X_WIDTH), lambda i: (i, 0)),
                   pl.BlockSpec((tm, X_WIDTH), lambda i: (i, 0))],
        out_shape=[jax.ShapeDtypeStruct((m, X_WIDTH), F32)] * 2,
        compiler_params=_cparams(("parallel",)),
        name="mem_kv",
    )(mem, g.reshape(1, k), w_bf, kg.reshape(1, X_HEAD_DIM))


def _half_norm(x, g, bd):
    x2 = x * x
    hi = x2.astype(BF16)
    lo = (x2 - hi.astype(F32)).astype(BF16)
    ms = (jnp.dot(hi, bd, preferred_element_type=F32)
          + jnp.dot(lo, bd, preferred_element_type=F32))
    return x * lax.rsqrt(ms + EPS) * g


def _swa_heads(q_ref, qg, bd, k2, v2, bias_ref, sink_ref, dyn_mask, o_ref):
    lo_lane = lax.broadcasted_iota(jnp.int32, (1, LANES), 1) < A_HEAD_DIM
    k_half = []
    for n in range(2):
        k_half.append((jnp.where(lo_lane, k2[n], 0.0).astype(BF16),
                       jnp.where(lo_lane, 0.0, k2[n]).astype(BF16)))
    for c in range(A_WIDTH // LANES):
        n = c // 4
        qc = _half_norm(q_ref[:, c * LANES:(c + 1) * LANES].astype(F32), qg, bd).astype(BF16)
        halves = []
        for half in range(2):
            j = 2 * c + half
            s = lax.dot_general(qc, k_half[n][half], (((1,), (1,)), ((), ())),
                                preferred_element_type=F32)
            s = s * (A_HEAD_DIM ** -0.5) + bias_ref[j]
            if dyn_mask is not None:
                s = jnp.where(dyn_mask, NEG, s)
            sink = sink_ref[j]
            m = jnp.maximum(jnp.max(s, axis=-1, keepdims=True), sink)
            p = jnp.exp(s - m)
            den = jnp.sum(p, axis=-1, keepdims=True) + jnp.exp(sink - m)
            o = jnp.dot(p.astype(BF16), v2[n], preferred_element_type=F32)
            halves.append(o / den)
        o_ref[:, c * LANES:(c + 1) * LANES] = jnp.where(lo_lane, halves[0], halves[1]).astype(o_ref.dtype)


def _swa_prompt_kernel(q_ref, kc_ref, kp_ref, vc_ref, vp_ref, qg_ref, kg_ref, bd_ref,
                       bias_ref, sink_ref, o_ref, nk_ref, nv_ref):
    t = pl.program_id(1)
    bd = bd_ref[...]
    k2, v2 = [], []
    for n in range(2):
        sl = slice(n * LANES, (n + 1) * LANES)
        kcat = jnp.concatenate([kp_ref[:, sl], kc_ref[:, sl]], axis=0).astype(F32)
        kn = _half_norm(kcat, kg_ref[...], bd)
        nk_ref[0, :, sl] = kn[Q_TILE:]
        k2.append(kn)
        v2.append(jnp.concatenate([vp_ref[:, sl], vc_ref[:, sl]], axis=0))
    nv_ref[0] = vc_ref[...].astype(F32)
    col = lax.broadcasted_iota(jnp.int32, (1, KEY_TILE), 1)
    dyn_mask = jnp.logical_and(col < Q_TILE, t == 0)
    _swa_heads(q_ref, qg_ref[...], bd, k2, v2, bias_ref, sink_ref, dyn_mask, o_ref)


def swa_prompt(z, batch, seq, qg2, kg2, bd, bias, sinks):
    nt = seq // Q_TILE
    row = lambda b, t: b * nt + t
    prev = lambda b, t: jnp.maximum(b * nt + t - 1, 0)
    return pl.pallas_call(
        _swa_prompt_kernel,
        grid=(batch, nt),
        in_specs=[pl.BlockSpec((Q_TILE, A_WIDTH), lambda b, t: (row(b, t), COL_Q)),
                  pl.BlockSpec((Q_TILE, A_KV_WIDTH), lambda b, t: (row(b, t), COL_K)),
                  pl.BlockSpec((Q_TILE, A_KV_WIDTH), lambda b, t: (prev(b, t), COL_K)),
                  pl.BlockSpec((Q_TILE, A_KV_WIDTH), lambda b, t: (row(b, t), COL_V)),
                  pl.BlockSpec((Q_TILE, A_KV_WIDTH), lambda b, t: (prev(b, t), COL_V)),
                  pl.BlockSpec((1, LANES), lambda b, t: (0, 0)),
                  pl.BlockSpec((1, LANES), lambda b, t: (0, 0)),
                  pl.BlockSpec((LANES, LANES), lambda b, t: (0, 0)),
                  _resident((A_HEADS, Q_TILE, KEY_TILE)),
                  pl.BlockSpec(memory_space=pltpu.SMEM)],
        out_specs=[pl.BlockSpec((Q_TILE, A_WIDTH), lambda b, t: (row(b, t), 0)),
                   pl.BlockSpec((1, WINDOW, A_KV_WIDTH), lambda b, t: (b, 0, 0)),
                   pl.BlockSpec((1, WINDOW, A_KV_WIDTH), lambda b, t: (b, 0, 0))],
        out_shape=[jax.ShapeDtypeStruct((batch * seq, A_WIDTH), BF16),
                   jax.ShapeDtypeStruct((batch, WINDOW, A_KV_WIDTH), F32),
                   jax.ShapeDtypeStruct((batch, WINDOW, A_KV_WIDTH), F32)],
        compiler_params=_cparams(("parallel", "arbitrary")),
        name="swa_prompt",
    )(z, z, z, z, z, qg2, kg2, bd, bias, sinks)


def _swa_sample_kernel(q_ref, kn_ref, vn_ref, ck_ref, cv_ref, qg_ref, kg_ref, bd_ref,
                       bias_ref, sink_ref, o_ref, nk_ref, nv_ref):
    bd = bd_ref[...]
    rows = q_ref.shape[0]
    pad = KEY_TILE - WINDOW - rows
    k2, v2 = [], []
    for n in range(2):
        sl = slice(n * LANES, (n + 1) * LANES)
        kn = _half_norm(kn_ref[:, sl].astype(F32), kg_ref[...], bd)
        nk_ref[0, :, sl] = kn
        k2.append(jnp.concatenate([ck_ref[0, :, sl], kn, jnp.zeros((pad, LANES), F32)], axis=0))
        v2.append(jnp.concatenate([cv_ref[0, :, sl].astype(BF16), vn_ref[:, sl],
                                   jnp.zeros((pad, LANES), BF16)], axis=0))
    nv_ref[0] = vn_ref[...].astype(F32)
    _swa_heads(q_ref, qg_ref[...], bd, k2, v2, bias_ref, sink_ref, None, o_ref)


def swa_sample(z, batch, rows, cache_k, cache_v, qg2, kg2, bd, bias, sinks):
    return pl.pallas_call(
        _swa_sample_kernel,
        grid=(batch,),
        in_specs=[pl.BlockSpec((rows, A_WIDTH), lambda b: (b, COL_Q)),
                  pl.BlockSpec((rows, A_KV_WIDTH), lambda b: (b, COL_K)),
                  pl.BlockSpec((rows, A_KV_WIDTH), lambda b: (b, COL_V)),
                  pl.BlockSpec((1, WINDOW, A_KV_WIDTH), lambda b: (b, 0, 0)),
                  pl.BlockSpec((1, WINDOW, A_KV_WIDTH), lambda b: (b, 0, 0)),
                  pl.BlockSpec((1, LANES), lambda b: (0, 0)),
                  pl.BlockSpec((1, LANES), lambda b: (0, 0)),
                  pl.BlockSpec((LANES, LANES), lambda b: (0, 0)),
                  _resident((A_HEADS, rows, KEY_TILE)),
                  pl.BlockSpec(memory_space=pltpu.SMEM)],
        out_specs=[pl.BlockSpec((rows, A_WIDTH), lambda b: (b, 0)),
                   pl.BlockSpec((1, rows, A_KV_WIDTH), lambda b: (b, 0, 0)),
                   pl.BlockSpec((1, rows, A_KV_WIDTH), lambda b: (b, 0, 0))],
        out_shape=[jax.ShapeDtypeStruct((batch * rows, A_WIDTH), BF16),
                   jax.ShapeDtypeStruct((batch, rows, A_KV_WIDTH), F32),
                   jax.ShapeDtypeStruct((batch, rows, A_KV_WIDTH), F32)],
        compiler_params=_cparams(("parallel",)),
        name="swa_sample",
    )(z, z, z, cache_k, cache_v, qg2, kg2, bd, bias, sinks)


def _conv_body(gb_ref, ub_main, uc_main, ub_halo, uc_halo, wb_ref, wc_ref, bc_ref, lg_ref, lb_ref,
               ob_ref, oc_ref, nb_ref, nc_ref, sb_ref, sc_ref, write_state):
    rows = ub_main.shape[0]
    sb_ref[0:HALO] = ub_halo
    sb_ref[HALO:HALO + rows] = ub_main
    sc_ref[0:HALO] = uc_halo
    sc_ref[HALO:HALO + rows] = uc_main
    sub = min(rows, 32)
    for r0 in range(0, rows, sub):
        yb = jnp.zeros((sub, B_WIDTH), F32)
        for k in range(B_CONV):
            yb = yb + wb_ref[k:k + 1, :] * sb_ref[pl.ds(r0 + HALO - (B_CONV - 1) + k, sub), :]
        ob_ref[r0:r0 + sub, :] = (gb_ref[r0:r0 + sub, :].astype(F32) * yb).astype(ob_ref.dtype)
        yc = jnp.zeros((sub, C_WIDTH), F32)
        for k in range(C_CONV):
            yc = yc + wc_ref[k:k + 1, :] * sc_ref[pl.ds(r0 + HALO - (C_CONV - 1) + k, sub), :]
        yc = yc + bc_ref[...]
        mu = jnp.mean(yc, axis=-1, keepdims=True)
        xc = yc - mu
        y = xc * lax.rsqrt(jnp.mean(xc * xc, axis=-1, keepdims=True) + EPS)
        y = y * lg_ref[...] + lb_ref[...]
        oc_ref[r0:r0 + sub, :] = (y * jax.nn.sigmoid(y)).astype(oc_ref.dtype)

    def _state():
        nb_ref[0] = sb_ref[rows + HALO - 8:rows + HALO]
        nc_ref[0] = sc_ref[rows:rows + HALO]

    write_state(_state)


def _conv_prompt_kernel(gb_ref, gc_ref, hb_ref, ga_ref, gg_ref, gch_ref, hbh_ref, gah_ref, ggh_ref,
                        wb_ref, wc_ref, bc_ref, lg_ref, lb_ref,
                        ob_ref, oc_ref, nb_ref, nc_ref, sb_ref, sc_ref):
    t = pl.program_id(1)
    hist = (t > 0).astype(F32)
    ub_main = gc_ref[...].astype(F32) * hb_ref[...].astype(F32)
    uc_main = ga_ref[...].astype(F32) * jax.nn.sigmoid(gg_ref[...].astype(F32))
    ub_halo = gch_ref[...].astype(F32) * hbh_ref[...].astype(F32) * hist
    uc_halo = gah_ref[...].astype(F32) * jax.nn.sigmoid(ggh_ref[...].astype(F32)) * hist
    last = pl.num_programs(1) - 1
    _conv_body(gb_ref, ub_main, uc_main, ub_halo, uc_halo, wb_ref, wc_ref, bc_ref, lg_ref, lb_ref,
               ob_ref, oc_ref, nb_ref, nc_ref, sb_ref, sc_ref,
               lambda f: pl.when(t == last)(f))


def conv_prompt(z, batch, seq, wb, wc, bc, lg, lb, *, tr=128):
    nt = seq // tr
    hp = tr // HALO
    main = lambda c: pl.BlockSpec((tr, B_WIDTH), lambda b, t: (b * nt + t, c))
    halo = lambda c: pl.BlockSpec((HALO, B_WIDTH), lambda b, t: (jnp.maximum((b * nt + t) * hp - 1, 0), c))
    vec = lambda r: pl.BlockSpec((r, B_WIDTH), lambda b, t: (0, 0))
    return pl.pallas_call(
        _conv_prompt_kernel,
        grid=(batch, nt),
        in_specs=[main(COL_GB), main(COL_GC), main(COL_HB), main(COL_GA), main(COL_GG),
                  halo(COL_GC), halo(COL_HB), halo(COL_GA), halo(COL_GG),
                  vec(B_CONV), vec(C_CONV), vec(1), vec(1), vec(1)],
        out_specs=[pl.BlockSpec((tr, B_WIDTH), lambda b, t: (b * nt + t, 0)),
                   pl.BlockSpec((tr, C_WIDTH), lambda b, t: (b * nt + t, 0)),
                   pl.BlockSpec((1, 8, B_WIDTH), lambda b, t: (b, 0, 0)),
                   pl.BlockSpec((1, HALO, C_WIDTH), lambda b, t: (b, 0, 0))],
        out_shape=[jax.ShapeDtypeStruct((batch * seq, B_WIDTH), BF16),
                   jax.ShapeDtypeStruct((batch * seq, C_WIDTH), BF16),
                   jax.ShapeDtypeStruct((batch, 8, B_WIDTH), F32),
                   jax.ShapeDtypeStruct((batch, HALO, C_WIDTH), F32)],
        scratch_shapes=[pltpu.VMEM((tr + HALO, B_WIDTH), F32), pltpu.VMEM((tr + HALO, C_WIDTH), F32)],
        compiler_params=_cparams(("parallel", "arbitrary")),
        name="conv_prompt",
    )(z, z, z, z, z, z, z, z, z, wb, wc, bc.reshape(1, -1), lg.reshape(1, -1), lb.reshape(1, -1))


def _conv_sample_kernel(gb_ref, gc_ref, hb_ref, ga_ref, gg_ref, stb_ref, stc_ref,
                        wb_ref, wc_ref, bc_ref, lg_ref, lb_ref,
                        ob_ref, oc_ref, nb_ref, nc_ref, sb_ref, sc_ref):
    ub_main = gc_ref[...].astype(F32) * hb_ref[...].astype(F32)
    uc_main = ga_ref[...].astype(F32) * jax.nn.sigmoid(gg_ref[...].astype(F32))
    _conv_body(gb_ref, ub_main, uc_main, stb_ref[0], stc_ref[0], wb_ref, wc_ref, bc_ref, lg_ref, lb_ref,
               ob_ref, oc_ref, nb_ref, nc_ref, sb_ref, sc_ref, lambda f: f())


def conv_sample(z, batch, rows, stb, stc, wb, wc, bc, lg, lb):
    main = lambda c: pl.BlockSpec((rows, B_WIDTH), lambda b: (b, c))
    vec = lambda r: pl.BlockSpec((r, B_WIDTH), lambda b: (0, 0))
    return pl.pallas_call(
        _conv_sample_kernel,
        grid=(batch,),
        in_specs=[main(COL_GB), main(COL_GC), main(COL_HB), main(COL_GA), main(COL_GG),
                  pl.BlockSpec((1, HALO, B_WIDTH), lambda b: (b, 0, 0)),
                  pl.BlockSpec((1, HALO, C_WIDTH), lambda b: (b, 0, 0)),
                  vec(B_CONV), vec(C_CONV), vec(1), vec(1), vec(1)],
        out_specs=[pl.BlockSpec((rows, B_WIDTH), lambda b: (b, 0)),
                   pl.BlockSpec((rows, C_WIDTH), lambda b: (b, 0)),
                   pl.BlockSpec((1, 8, B_WIDTH), lambda b: (b, 0, 0)),
                   pl.BlockSpec((1, HALO, C_WIDTH), lambda b: (b, 0, 0))],
        out_shape=[jax.ShapeDtypeStruct((batch * rows, B_WIDTH), BF16),
                   jax.ShapeDtypeStruct((batch * rows, C_WIDTH), BF16),
                   jax.ShapeDtypeStruct((batch, 8, B_WIDTH), F32),
                   jax.ShapeDtypeStruct((batch, HALO, C_WIDTH), F32)],
        scratch_shapes=[pltpu.VMEM((rows + HALO, B_WIDTH), F32), pltpu.VMEM((rows + HALO, C_WIDTH), F32)],
        compiler_params=_cparams(("parallel",)),
        name="conv_sample",
    )(z, z, z, z, z, stb, stc, wb, wc, bc.reshape(1, -1), lg.reshape(1, -1), lb.reshape(1, -1))


def _merge_kernel(oa_ref, ob_ref, oc_ref, l0_ref, l1_ref, l2_ref, bg_ref,
                  wpa_ref, wpb_ref, wpc_ref, wo_ref, x_ref, o_ref):
    def gated(l_ref, i, o_r, w_r):
        gate = jax.nn.sigmoid(l_ref[...].astype(F32) + bg_ref[i:i + 1, :])
        return gate * jnp.dot(o_r[...], w_r[...], preferred_element_type=F32)

    merged = gated(l0_ref, 0, oa_ref, wpa_ref)
    merged = merged + gated(l1_ref, 1, ob_ref, wpb_ref)
    merged = merged + gated(l2_ref, 2, oc_ref, wpc_ref)
    o_ref[...] = x_ref[...] + jnp.dot(merged.astype(BF16), wo_ref[...], preferred_element_type=F32)


def merge(oa, ob, oc, z, bg, wpa, wpb, wpc, wo, x, *, tm=256):
    m = x.shape[0]
    tm = min(tm, m)
    rows = lambda w: pl.BlockSpec((tm, w), lambda i: (i, 0))
    gate = lambda c: pl.BlockSpec((tm, D_MODEL), lambda i: (i, COL_GATE0 + c))
    return pl.pallas_call(
        _merge_kernel,
        grid=(m // tm,),
        in_specs=[rows(A_WIDTH), rows(B_WIDTH), rows(C_WIDTH), gate(0), gate(1), gate(2),
                  pl.BlockSpec((3, D_MODEL), lambda i: (0, 0)),
                  _resident((A_WIDTH, D_MODEL)), _resident((B_WIDTH, D_MODEL)),
                  _resident((C_WIDTH, D_MODEL)), _resident((D_MODEL, D_MODEL)),
                  rows(D_MODEL)],
        out_specs=rows(D_MODEL),
        out_shape=jax.ShapeDtypeStruct((m, D_MODEL), F32),
        compiler_params=_cparams(("parallel",)),
        name="merge",
    )(oa, ob, oc, z, z, z, bg.reshape(3, D_MODEL), wpa, wpb, wpc, wo, x)


def _xattn_kernel(y_ref, g_ref, wq_ref, qg_ref, mk_ref, mv_ref, wo_ref, o_ref, *, nb, rpb):
    y = y_ref[...]
    hn = _rms(y, g_ref[...]).astype(BF16)
    q = jnp.dot(hn, wq_ref[...], preferred_element_type=F32)
    heads = []
    for h in range(X_HEADS):
        sl = slice(h * X_HEAD_DIM, (h + 1) * X_HEAD_DIM)
        qh = _rms(q[:, sl], qg_ref[...])
        per_batch = []
        for b in range(nb):
            qb = qh[b * rpb:(b + 1) * rpb].astype(BF16)
            kh = mk_ref[b, :, sl].astype(BF16)
            vh = mv_ref[b, :, sl].astype(BF16)
            s = lax.dot_general(qb, kh, (((1,), (1,)), ((), ())),
                                preferred_element_type=F32) * (X_HEAD_DIM ** -0.5)
            m = jnp.max(s, axis=-1, keepdims=True)
            p = jnp.exp(s - m)
            den = jnp.sum(p, axis=-1, keepdims=True)
            per_batch.append(jnp.dot(p.astype(BF16), vh, preferred_element_type=F32) / den)
        heads.append(per_batch[0] if nb == 1 else jnp.concatenate(per_batch, axis=0))
    o = jnp.concatenate(heads, axis=1).astype(BF16)
    o_ref[...] = y + jnp.dot(o, wo_ref[...], preferred_element_type=F32)


def xattn(y, g, wq, qg, mk, mv, wo, *, nb, rpb, tiles_per_mem):
    m = y.shape[0]
    tm = nb * rpb
    mem_idx = (lambda i: (i // tiles_per_mem, 0, 0)) if nb == 1 else (lambda i: (i, 0, 0))
    return pl.pallas_call(
        functools.partial(_xattn_kernel, nb=nb, rpb=rpb),
        grid=(m // tm,),
        in_specs=[pl.BlockSpec((tm, D_MODEL), lambda i: (i, 0)),
                  pl.BlockSpec((1, D_MODEL), lambda i: (0, 0)),
                  _resident((D_MODEL, X_WIDTH)),
                  pl.BlockSpec((1, X_HEAD_DIM), lambda i: (0, 0)),
                  pl.BlockSpec((nb, N_MEM, X_WIDTH), mem_idx),
                  pl.BlockSpec((nb, N_MEM, X_WIDTH), mem_idx),
                  _resident((X_WIDTH, D_MODEL))],
        out_specs=pl.BlockSpec((tm, D_MODEL), lambda i: (i, 0)),
        out_shape=jax.ShapeDtypeStruct((m, D_MODEL), F32),
        compiler_params=_cparams(("parallel",)),
        name="xattn",
    )(y, g.reshape(1, -1), wq, qg.reshape(1, -1), mk, mv, wo)


def _ffn_kernel(x_ref, g_ref, wg_ref, wu_ref, wd_ref, o_ref, hn_ref):
    @pl.when(pl.program_id(1) == 0)
    def _():
        x = x_ref[...]
        hn_ref[...] = _rms(x, g_ref[...]).astype(BF16)
        o_ref[...] = x

    hn = hn_ref[...]
    g = jnp.dot(hn, wg_ref[...].astype(BF16), preferred_element_type=F32)
    u = jnp.dot(hn, wu_ref[...].astype(BF16), preferred_element_type=F32)
    a = (g * jax.nn.sigmoid(g) * u).astype(BF16)
    o_ref[...] += jnp.dot(a, wd_ref[...].astype(BF16), preferred_element_type=F32)


def ffn(x, g, w_gu, w_d, *, tm=1024, tf=256):
    m = x.shape[0]
    tm = min(tm, m)
    nf = D_FF // tf
    return pl.pallas_call(
        _ffn_kernel,
        grid=(m // tm, nf),
        in_specs=[pl.BlockSpec((tm, D_MODEL), lambda i, f: (i, 0), pipeline_mode=pl.Buffered(1)),
                  pl.BlockSpec((1, D_MODEL), lambda i, f: (0, 0)),
                  pl.BlockSpec((D_MODEL, tf), lambda i, f: (0, f)),
                  pl.BlockSpec((D_MODEL, tf), lambda i, f: (0, nf + f)),
                  pl.BlockSpec((tf, D_MODEL), lambda i, f: (f, 0))],
        out_specs=pl.BlockSpec((tm, D_MODEL), lambda i, f: (i, 0)),
        out_shape=jax.ShapeDtypeStruct((m, D_MODEL), F32),
        scratch_shapes=[pltpu.VMEM((tm, D_MODEL), BF16)],
        compiler_params=_cparams(("parallel", "arbitrary")),
        name="ffn",
    )(x, g.reshape(1, -1), w_gu, w_gu, w_d)


def _split3(x):
    hi = x.astype(BF16)
    lo = (x - hi.astype(F32)).astype(BF16)
    return hi, lo


def _router_kernel(x_ref, g_ref, w_ref, b_ref, wts_ref, ids_ref, hnp_ref):
    hn = _rms(x_ref[...], g_ref[...])
    h_hi, h_lo = _split3(hn)
    w_hi, w_lo = _split3(w_ref[...])
    dot = functools.partial(jnp.dot, preferred_element_type=F32)
    logits = dot(h_hi, w_hi) + dot(h_hi, w_lo) + dot(h_lo, w_hi) + b_ref[...]
    lane = lax.broadcasted_iota(jnp.int32, logits.shape, 1).astype(F32)
    logits = jnp.where(lane < N_EXPERTS, logits, -jnp.inf)
    v1 = jnp.max(logits, axis=-1, keepdims=True)
    i1 = jnp.min(jnp.where(logits == v1, lane, float(LANES)), axis=-1, keepdims=True)
    rest = jnp.where(lane == i1, -jnp.inf, logits)
    v2 = jnp.max(rest, axis=-1, keepdims=True)
    i2 = jnp.min(jnp.where(rest == v2, lane, float(LANES)), axis=-1, keepdims=True)
    e2 = jnp.exp(v2 - v1)
    den = 1.0 + e2
    wts_ref[...] = jnp.where(lane == 0.0, 1.0 / den, 0.0) + jnp.where(lane == 1.0, e2 / den, 0.0)
    ids_ref[...] = (jnp.where(lane == 0.0, i1, 0.0) + jnp.where(lane == 1.0, i2, 0.0)).astype(jnp.int32)
    half = D_MODEL // 2
    lo_bits = lax.bitcast_convert_type(h_hi[:, :half].astype(F32), jnp.uint32)
    hi_bits = lax.bitcast_convert_type(h_hi[:, half:].astype(F32), jnp.uint32)
    hnp_ref[...] = (lo_bits >> 16) | (hi_bits & jnp.uint32(0xFFFF0000))


def router(x, g, w_pad, b_pad, *, tm=512):
    m = x.shape[0]
    tm = min(tm, m)
    return pl.pallas_call(
        _router_kernel,
        grid=(m // tm,),
        in_specs=[pl.BlockSpec((tm, D_MODEL), lambda i: (i, 0)),
                  pl.BlockSpec((1, D_MODEL), lambda i: (0, 0)),
                  pl.BlockSpec((D_MODEL, LANES), lambda i: (0, 0)),
                  pl.BlockSpec((1, LANES), lambda i: (0, 0))],
        out_specs=[pl.BlockSpec((tm, LANES), lambda i: (i, 0)),
                   pl.BlockSpec((tm, LANES), lambda i: (i, 0)),
                   pl.BlockSpec((tm, D_MODEL // 2), lambda i: (i, 0))],
        out_shape=[jax.ShapeDtypeStruct((m, LANES), F32),
                   jax.ShapeDtypeStruct((m, LANES), jnp.int32),
                   jax.ShapeDtypeStruct((m, D_MODEL // 2), jnp.uint32)],
        compiler_params=_cparams(("parallel",)),
        name="router",
    )(x, g.reshape(1, -1), w_pad, b_pad)


MOE_TM = 1024
MOE_TF = 256


def _row_copy(src_hbm, row, dst_vmem, r, sem):
    return pltpu.make_async_copy(src_hbm.at[pl.ds(row, 1)], dst_vmem.at[pl.ds(r, 1)], sem)


def _moe_ffn_kernel(te_ref, tv_ref, src_ref, hnp_hbm, wg_ref, wu_ref, wd_ref, o_ref, xbuf, hn_ref, sem):
    t = pl.program_id(0)
    f = pl.program_id(1)
    valid = tv_ref[t] != 0
    rows = xbuf.shape[0]
    half = D_MODEL // 2

    @pl.when(f == 0)
    def _():
        o_ref[...] = jnp.zeros_like(o_ref)

    @pl.when(jnp.logical_and(valid, f == 0))
    def _():
        def issue(r, c):
            _row_copy(hnp_hbm, src_ref[0, 0, r], xbuf, r, sem).start()
            return c

        lax.fori_loop(0, rows, issue, 0, unroll=8)

        def wait(r, c):
            _row_copy(hnp_hbm, 0, xbuf, r, sem).wait()
            return c

        lax.fori_loop(0, rows, wait, 0, unroll=8)
        xu = xbuf[...]
        hn_ref[:, :half] = lax.bitcast_convert_type(xu << 16, F32).astype(BF16)
        hn_ref[:, half:] = lax.bitcast_convert_type(xu & jnp.uint32(0xFFFF0000), F32).astype(BF16)

    @pl.when(valid)
    def _():
        hn = hn_ref[...]
        g = jnp.dot(hn, wg_ref[...].astype(BF16), preferred_element_type=F32)
        u = jnp.dot(hn, wu_ref[...].astype(BF16), preferred_element_type=F32)
        a = (g * jax.nn.sigmoid(g) * u).astype(BF16)
        o_ref[...] += jnp.dot(a, wd_ref[...].astype(BF16), preferred_element_type=F32)


def moe_ffn(tile_expert, tile_valid, src, hnp, w_gu, w_d):
    nt, _, tm = src.shape
    nf = D_FF // MOE_TF
    last = nf - 1
    col = lambda f, tv, t: jnp.where(tv[t] != 0, f, last)
    grid_spec = pltpu.PrefetchScalarGridSpec(
        num_scalar_prefetch=2,
        grid=(nt, nf),
        in_specs=[pl.BlockSpec((1, 1, tm), lambda t, f, te, tv: (t, 0, 0), memory_space=pltpu.SMEM),
                  pl.BlockSpec(memory_space=pl.ANY),
                  pl.BlockSpec((None, D_MODEL, MOE_TF), lambda t, f, te, tv: (te[t], 0, col(f, tv, t))),
                  pl.BlockSpec((None, D_MODEL, MOE_TF), lambda t, f, te, tv: (te[t], 0, nf + col(f, tv, t))),
                  pl.BlockSpec((None, MOE_TF, D_MODEL), lambda t, f, te, tv: (te[t], col(f, tv, t), 0))],
        out_specs=pl.BlockSpec((tm, D_MODEL), lambda t, f, te, tv: (t, 0)),
        scratch_shapes=[pltpu.VMEM((tm, D_MODEL // 2), jnp.uint32),
                        pltpu.VMEM((tm, D_MODEL), BF16),
                        pltpu.SemaphoreType.DMA],
    )
    return pl.pallas_call(
        _moe_ffn_kernel,
        grid_spec=grid_spec,
        out_shape=jax.ShapeDtypeStruct((nt * tm, D_MODEL), F32),
        compiler_params=_cparams(("arbitrary", "arbitrary")),
        name="moe_ffn",
    )(tile_expert, tile_valid, src, hnp, w_gu, w_gu, w_d)


def _moe_combine_kernel(pos_ref, x_ref, w_ref, osort_hbm, o_ref, abuf, sem):
    tm = x_ref.shape[0]

    def issue(r, c):
        _row_copy(osort_hbm, pos_ref[0, 0, r], abuf, r, sem).start()
        return c

    lax.fori_loop(0, 2 * tm, issue, 0, unroll=8)

    def wait(r, c):
        _row_copy(osort_hbm, 0, abuf, r, sem).wait()
        return c

    lax.fori_loop(0, 2 * tm, wait, 0, unroll=8)
    w = w_ref[...]
    o_ref[...] = x_ref[...] + w[:, 0:1] * abuf[0:tm, :] + w[:, 1:2] * abuf[tm:2 * tm, :]


def moe_combine(pos, x, wts, osort, *, tm=256):
    m = x.shape[0]
    return pl.pallas_call(
        _moe_combine_kernel,
        grid=(m // tm,),
        in_specs=[pl.BlockSpec((1, 1, 2 * tm), lambda i: (i, 0, 0), memory_space=pltpu.SMEM),
                  pl.BlockSpec((tm, D_MODEL), lambda i: (i, 0)),
                  pl.BlockSpec((tm, LANES), lambda i: (i, 0)),
                  pl.BlockSpec(memory_space=pl.ANY)],
        out_specs=pl.BlockSpec((tm, D_MODEL), lambda i: (i, 0)),
        out_shape=jax.ShapeDtypeStruct((m, D_MODEL), F32),
        scratch_shapes=[pltpu.VMEM((2 * tm, D_MODEL), F32), pltpu.SemaphoreType.DMA],
        compiler_params=_cparams(("arbitrary",)),
        name="moe_combine",
    )(pos, x, wts, osort)


def _dispatch_plan(ids, tm):
    m = ids.shape[0]
    nt = (2 * m) // tm + N_EXPERTS
    flat = ids.reshape(-1)
    onehot = (flat[:, None] == jnp.arange(N_EXPERTS, dtype=jnp.int32)[None, :]).astype(jnp.int32)
    csum = jnp.cumsum(onehot, axis=0)
    rank = jnp.sum((csum - onehot) * onehot, axis=1)
    counts = csum[-1]
    padded = ((counts + tm - 1) // tm) * tm
    ends = jnp.cumsum(padded)
    pos = (ends - padded)[flat] + rank
    src = jnp.zeros((nt * tm,), jnp.int32).at[pos].set(jnp.arange(2 * m, dtype=jnp.int32) // 2)
    starts = jnp.arange(nt, dtype=jnp.int32) * tm
    tile_valid = (starts < ends[-1]).astype(jnp.int32)
    tile_expert = jnp.minimum(jnp.sum((starts[:, None] >= ends[None, :]).astype(jnp.int32), axis=1),
                              N_EXPERTS - 1)
    last_valid = jnp.maximum(jnp.sum(tile_valid) - 1, 0)
    tile_expert = jnp.where(tile_valid != 0, tile_expert, tile_expert[last_valid])
    return tile_expert, tile_valid, src.reshape(nt, 1, tm), pos.reshape(m, 2)


def _combine_pos(pos, tm):
    m = pos.shape[0]
    return jnp.transpose(pos.reshape(m // tm, tm, 2), (0, 2, 1)).reshape(m // tm, 1, 2 * tm)


def _t5_bucket_np(rel):
    nb = N_BUCKETS // 2
    max_exact = nb // 2
    ret = np.where(rel > 0, nb, 0)
    n = np.abs(rel)
    nf = np.maximum(n, 1).astype(np.float32)
    large = max_exact + (np.log(nf / np.float32(max_exact)) / np.float32(math.log(MAX_DISTANCE / max_exact))
                         * np.float32(nb - max_exact)).astype(np.int32)
    large = np.minimum(large, nb - 1)
    return (ret + np.where(n < max_exact, n, large)).astype(np.int32)


def _bias_tensor(rel_table, n_q, valid):
    rel = np.arange(KEY_TILE, dtype=np.int32)[None, :] - WINDOW - np.arange(n_q, dtype=np.int32)[:, None]
    onehot = np.eye(N_BUCKETS, dtype=np.float32)[:, _t5_bucket_np(rel).reshape(-1)]
    bias = jnp.dot(rel_table.T[HEAD_PERM], jnp.asarray(onehot), precision=lax.Precision.HIGHEST)
    return jnp.where(jnp.asarray(valid)[None], bias.reshape(A_HEADS, n_q, KEY_TILE), NEG).astype(F32)


def _prompt_valid():
    qc = np.arange(Q_TILE)[:, None] // CHUNK
    kc = np.arange(KEY_TILE)[None, :] // CHUNK
    return (kc >= qc) & (kc <= qc + WINDOW // CHUNK)


def _sample_valid(rows):
    return np.broadcast_to(np.arange(KEY_TILE)[None, :] < WINDOW + rows, (rows, KEY_TILE))


def _half_avg():
    blk = np.kron(np.eye(2), np.ones((A_HEAD_DIM, A_HEAD_DIM))) / A_HEAD_DIM
    return jnp.asarray(blk, BF16)


def _mixer_weights(l, g_mix, w_in, b_gate, q_norm_g, k_norm_g, sinks, w_conv_b, w_conv_c, b_conv_c,
                   ln_c_g, ln_c_b, w_proj_a, w_proj_b, w_proj_c, w_out):
    w = w_in[l]
    wq = w[:, :A_WIDTH].reshape(D_MODEL, A_HEADS, A_HEAD_DIM)[:, HEAD_PERM].reshape(D_MODEL, A_WIDTH)
    w_in_bf = jnp.concatenate([wq, w[:, A_WIDTH:]], axis=1).astype(BF16)
    wpa = w_proj_a[l].reshape(A_HEADS, A_HEAD_DIM, D_MODEL)[HEAD_PERM].reshape(A_WIDTH, D_MODEL)
    return dict(
        g_mix=g_mix[l], w_in=w_in_bf, b_gate=b_gate[l],
        qg2=jnp.tile(q_norm_g[l], 2).reshape(1, LANES), kg2=jnp.tile(k_norm_g[l], 2).reshape(1, LANES),
        sinks=sinks[l][HEAD_PERM],
        w_cb=w_conv_b[l], w_cc=w_conv_c[l], b_cc=b_conv_c[l], ln_g=ln_c_g[l], ln_b=ln_c_b[l],
        wpa=wpa.astype(BF16), wpb=w_proj_b[l].astype(BF16), wpc=w_proj_c[l].astype(BF16),
        wo=w_out[l].astype(BF16))


def _mixer_prompt(x, mw, bias, bd, batch, seq):
    z = norm_matmul(x, mw["g_mix"], mw["w_in"], tm=1024, tn=1024, out_dtype=BF16)
    oa, nk, nv = swa_prompt(z, batch, seq, mw["qg2"], mw["kg2"], bd, bias, mw["sinks"])
    ob, oc, ncb, ncc = conv_prompt(z, batch, seq, mw["w_cb"], mw["w_cc"], mw["b_cc"], mw["ln_g"], mw["ln_b"])
    y = merge(oa, ob, oc, z, mw["b_gate"], mw["wpa"], mw["wpb"], mw["wpc"], mw["wo"], x)
    return y, (nk, nv, ncb[:, 8 - (B_CONV - 1):], ncc[:, HALO - (C_CONV - 1):])


def _mixer_sample(x, mw, bias, bd, batch, rows, cache_k, cache_v, st_b, st_c):
    z = norm_matmul(x, mw["g_mix"], mw["w_in"], tm=x.shape[0], tn=1024, out_dtype=BF16)
    oa, nk, nv = swa_sample(z, batch, rows, cache_k, cache_v, mw["qg2"], mw["kg2"], bd, bias, mw["sinks"])
    stb = jnp.pad(st_b, ((0, 0), (HALO - (B_CONV - 1), 0), (0, 0)))
    stc = jnp.pad(st_c, ((0, 0), (HALO - (C_CONV - 1), 0), (0, 0)))
    ob, oc, ncb, ncc = conv_sample(z, batch, rows, stb, stc, mw["w_cb"], mw["w_cc"], mw["b_cc"],
                                   mw["ln_g"], mw["ln_b"])
    y = merge(oa, ob, oc, z, mw["b_gate"], mw["wpa"], mw["wpb"], mw["wpc"], mw["wo"], x)
    return y, (nk, nv, ncb[:, 8 - (B_CONV - 1):], ncc[:, HALO - (C_CONV - 1):])


def _channel_mixer(yp, ys, l, g_ffn, w_ffn_gu, w_ffn_d, w_router, b_router, w_moe_gu, w_moe_d):
    if l % 2 == 0:
        return (ffn(yp, g_ffn[l], w_ffn_gu[l // 2], w_ffn_d[l // 2]),
                ffn(ys, g_ffn[l], w_ffn_gu[l // 2], w_ffn_d[l // 2]))
    i = l // 2
    w_pad = jnp.pad(w_router[i], ((0, 0), (0, LANES - N_EXPERTS)))
    b_pad = jnp.pad(b_router[i], (0, LANES - N_EXPERTS)).reshape(1, LANES)
    wts_p, ids_p, hnp_p = router(yp, g_ffn[l], w_pad, b_pad)
    wts_s, ids_s, hnp_s = router(ys, g_ffn[l], w_pad, b_pad)
    ids = jnp.concatenate([ids_p[:, :2], ids_s[:, :2]], axis=0)
    hnp = jnp.concatenate([hnp_p, hnp_s], axis=0)
    tile_expert, tile_valid, src, pos = _dispatch_plan(ids, MOE_TM)
    osort = moe_ffn(tile_expert, tile_valid, src, hnp, w_moe_gu[i], w_moe_d[i])
    mp = yp.shape[0]
    tm_p, tm_s = 512, min(512, ys.shape[0])
    return (moe_combine(_combine_pos(pos[:mp], tm_p), yp, wts_p, osort, tm=tm_p),
            moe_combine(_combine_pos(pos[mp:], tm_s), ys, wts_s, osort, tm=tm_s))


def kernel(x_prompt, x_sample, mem_prompt, cache_mem_k, cache_mem_v, cache_swa_k, cache_swa_v, state_conv_b, state_conv_c, rel_table, g_mix, w_in, b_gate, q_norm_g, k_norm_g, sinks, w_conv_b, w_conv_c, b_conv_c, ln_c_g, ln_c_b, w_proj_a, w_proj_b, w_proj_c, w_out, g_xattn, g_mem, w_xq, w_xkv, xq_norm_g, xk_norm_g, w_xo, g_ffn, w_ffn_gu, w_ffn_d, w_router, b_router, w_moe_gu, w_moe_d):
    batch, seq, d = x_prompt.shape
    dec_batch, dec_seq, _ = x_sample.shape
    depth = g_mix.shape[0]
    yp = x_prompt.reshape(batch * seq, d)
    ys = x_sample.reshape(dec_batch * dec_seq, d)
    mem = mem_prompt.reshape(batch * N_MEM, d)
    bd = _half_avg()
    bias_p = _bias_tensor(rel_table, Q_TILE, _prompt_valid())
    bias_s = _bias_tensor(rel_table, dec_seq, _sample_valid(dec_seq))
    outs = [[] for _ in range(10)]
    for l in range(depth):
        mw = _mixer_weights(l, g_mix, w_in, b_gate, q_norm_g, k_norm_g, sinks, w_conv_b, w_conv_c,
                            b_conv_c, ln_c_g, ln_c_b, w_proj_a, w_proj_b, w_proj_c, w_out)
        wq = w_xq[l].astype(BF16)
        wo = w_xo[l].astype(BF16)
        ffn_args = (g_ffn, w_ffn_gu, w_ffn_d, w_router, b_router, w_moe_gu, w_moe_d)
        yp, (nk, nv, ncb, ncc) = _mixer_prompt(yp, mw, bias_p, bd, batch, seq)
        mk, mv = mem_kv(mem, g_mem[l], w_xkv[l].astype(BF16), xk_norm_g[l])
        mk3 = mk.reshape(batch, N_MEM, X_WIDTH)
        mv3 = mv.reshape(batch, N_MEM, X_WIDTH)
        yp = xattn(yp, g_xattn[l], wq, xq_norm_g[l], mk3, mv3, wo, nb=1, rpb=512, tiles_per_mem=seq // 512)
        for lst, v in zip(outs[:6], (mk3.reshape(batch, N_MEM, X_HEADS, X_HEAD_DIM),
                                     mv3.reshape(batch, N_MEM, X_HEADS, X_HEAD_DIM),
                                     nk.reshape(batch, WINDOW, A_KV_HEADS, A_HEAD_DIM),
                                     nv.reshape(batch, WINDOW, A_KV_HEADS, A_HEAD_DIM), ncb, ncc)):
            lst.append(v)
        ck = cache_swa_k[l].reshape(dec_batch, WINDOW, A_KV_WIDTH)
        cv = cache_swa_v[l].reshape(dec_batch, WINDOW, A_KV_WIDTH)
        ys, (nk, nv, ncb, ncc) = _mixer_sample(ys, mw, bias_s, bd, dec_batch, dec_seq, ck, cv,
                                               state_conv_b[l], state_conv_c[l])
        cmk = cache_mem_k[l].reshape(dec_batch, N_MEM, X_WIDTH)
        cmv = cache_mem_v[l].reshape(dec_batch, N_MEM, X_WIDTH)
        ys = xattn(ys, g_xattn[l], wq, xq_norm_g[l], cmk, cmv, wo, nb=8, rpb=dec_seq, tiles_per_mem=1)
        yp, ys = _channel_mixer(yp, ys, l, *ffn_args)
        for lst, v in zip(outs[6:], (nk.reshape(dec_batch, dec_seq, A_KV_HEADS, A_HEAD_DIM),
                                     nv.reshape(dec_batch, dec_seq, A_KV_HEADS, A_HEAD_DIM), ncb, ncc)):
            lst.append(v)
    return (yp.reshape(batch, seq, d), ys.reshape(dec_batch, dec_seq, d)) + tuple(jnp.stack(o) for o in outs)
```

```python
import functools
import math

import numpy as np
import jax
import jax.numpy as jnp
from jax import lax
from jax.experimental import pallas as pl
from jax.experimental.pallas import tpu as pltpu

F32 = jnp.float32
BF16 = jnp.bfloat16

D_MODEL = 2048
CHUNK = 64
A_HEADS = 16
A_KV_HEADS = 4
A_HEAD_DIM = 64
A_WIDTH = A_HEADS * A_HEAD_DIM
A_KV_WIDTH = A_KV_HEADS * A_HEAD_DIM
WINDOW = 128
N_BUCKETS = 32
MAX_DISTANCE = 128
B_WIDTH = 512
B_CONV = 3
C_WIDTH = 512
C_CONV = 31
N_MEM = 256
X_HEADS = 4
X_HEAD_DIM = 128
X_WIDTH = X_HEADS * X_HEAD_DIM
D_FF = 5632
N_EXPERTS = 8
EPS = 1e-6

LANES = 128
KEY_TILE = 256
Q_TILE = 128
HALO = 32
NEG = -1e30
VMEM_LIMIT = 56 * 1024 * 1024

COL_Q = 0
COL_K, COL_V = 4, 5
COL_GB, COL_GC, COL_HB, COL_GA, COL_GG = 3, 4, 5, 6, 7
COL_GATE0 = 2
IN_COLS = 4096 + 3 * D_MODEL

HEAD_PERM = np.array([8 * n + (p % 2) * 4 + p // 2 for n in range(2) for p in range(8)])


def _cparams(sem):
    return pltpu.CompilerParams(dimension_semantics=sem, vmem_limit_bytes=VMEM_LIMIT)


def _rms(x, g):
    ms = jnp.mean(x * x, axis=-1, keepdims=True)
    return x * lax.rsqrt(ms + EPS) * g


def _resident(shape):
    nd = len(shape)
    return pl.BlockSpec(shape, lambda *_: (0,) * nd, pipeline_mode=pl.Buffered(1))


def _norm_matmul_kernel(x_ref, g_ref, w_ref, o_ref, hn_ref):
    @pl.when(pl.program_id(1) == 0)
    def _():
        hn_ref[...] = _rms(x_ref[...], g_ref[...]).astype(BF16)

    o_ref[...] = jnp.dot(hn_ref[...], w_ref[...], preferred_element_type=F32).astype(o_ref.dtype)


def norm_matmul(x, g, w, *, tm, tn, out_dtype):
    m, k = x.shape
    n = w.shape[1]
    return pl.pallas_call(
        _norm_matmul_kernel,
        grid=(m // tm, n // tn),
        in_specs=[pl.BlockSpec((tm, k), lambda i, j: (i, 0)),
                  pl.BlockSpec((1, k), lambda i, j: (0, 0)),
                  pl.BlockSpec((k, tn), lambda i, j: (0, j))],
        out_specs=pl.BlockSpec((tm, tn), lambda i, j: (i, j)),
        out_shape=jax.ShapeDtypeStruct((m, n), out_dtype),
        scratch_shapes=[pltpu.VMEM((tm, k), BF16)],
        compiler_params=_cparams(("parallel", "arbitrary")),
        name="norm_matmul",
    )(x, g.reshape(1, k), w)


def _mem_kv_kernel(x_ref, g_ref, w_ref, kg_ref, k_ref, v_ref):
    hn = _rms(x_ref[...], g_ref[...]).astype(BF16)
    kv = jnp.dot(hn, w_ref[...], preferred_element_type=F32)
    for h in range(X_HEADS):
        sl = slice(h * X_HEAD_DIM, (h + 1) * X_HEAD_DIM)
        k_ref[:, sl] = _rms(kv[:, sl], kg_ref[...])
    v_ref[...] = kv[:, X_WIDTH:]


def mem_kv(mem, g, w_bf, kg):
    m, k = mem.shape
    tm = 256
    return pl.pallas_call(
        _mem_kv_kernel,
        grid=(m // tm,),
        in_specs=[pl.BlockSpec((tm, k), lambda i: (i, 0)),
                  pl.BlockSpec((1, k), lambda i: (0, 0)),
                  _resident((k, 2 * X_WIDTH)),
                  pl.BlockSpec((1, X_HEAD_DIM), lambda i: (0, 0))],
        out_specs=[pl.BlockSpec((tm, X_WIDTH), lambda i: (i, 0)),
                   pl.BlockSpec((tm, X_WIDTH), lambda i: (i, 0))],
        out_shape=[jax.ShapeDtypeStruct((m, X_WIDTH), F32)] * 2,
        compiler_params=_cparams(("parallel",)),
        name="mem_kv",
    )(mem, g.reshape(1, k), w_bf, kg.reshape(1, X_HEAD_DIM))


def _half_norm(x, g, bd):
    x2 = x * x
    hi = x2.astype(BF16)
    lo = (x2 - hi.astype(F32)).astype(BF16)
    ms = (jnp.dot(hi, bd, preferred_element_type=F32)
          + jnp.dot(lo, bd, preferred_element_type=F32))
    return x * lax.rsqrt(ms + EPS) * g


def _swa_heads(q_ref, qg, bd, k2, v2, bias_ref, sink_ref, o_ref):
    lo_lane = lax.broadcasted_iota(jnp.int32, (1, LANES), 1) < A_HEAD_DIM
    k_half = []
    for n in range(2):
        k_half.append((jnp.where(lo_lane, k2[n], 0.0).astype(BF16),
                       jnp.where(lo_lane, 0.0, k2[n]).astype(BF16)))
    for c in range(A_WIDTH // LANES):
        n = c // 4
        qc = _half_norm(q_ref[:, c * LANES:(c + 1) * LANES].astype(F32), qg, bd).astype(BF16)
        halves = []
        for half in range(2):
            j = 2 * c + half
            s = lax.dot_general(qc, k_half[n][half], (((1,), (1,)), ((), ())),
                                preferred_element_type=F32)
            s = s + bias_ref[j]
            sink = sink_ref[j]
            m = jnp.maximum(jnp.max(s, axis=-1, keepdims=True), sink)
            p = jnp.exp(s - m)
            den = jnp.sum(p, axis=-1, keepdims=True) + jnp.exp(sink - m)
            o = jnp.dot(p.astype(BF16), v2[n], preferred_element_type=F32)
            halves.append(o / den)
        o_ref[:, c * LANES:(c + 1) * LANES] = jnp.where(lo_lane, halves[0], halves[1]).astype(o_ref.dtype)


def _swa_prompt_kernel(q_ref, kc_ref, kp_ref, vc_ref, vp_ref, qg_ref, kg_ref, bd_ref,
                       bias_ref, sink_ref, o_ref, nk_ref, nv_ref):
    bd = bd_ref[...]
    k2, v2 = [], []
    for n in range(2):
        sl = slice(n * LANES, (n + 1) * LANES)
        kcat = jnp.concatenate([kp_ref[:, sl], kc_ref[:, sl]], axis=0).astype(F32)
        kn = _half_norm(kcat, kg_ref[...], bd)
        nk_ref[0, :, sl] = kn[Q_TILE:]
        k2.append(kn)
        v2.append(jnp.concatenate([vp_ref[:, sl], vc_ref[:, sl]], axis=0))
    nv_ref[0] = vc_ref[...].astype(F32)
    _swa_heads(q_ref, qg_ref[...], bd, k2, v2, bias_ref, sink_ref, o_ref)


def swa_prompt(z, batch, seq, qg2, kg2, bd, bias, sinks):
    nt = seq // Q_TILE
    row = lambda b, t: b * nt + t
    prev = lambda b, t: jnp.maximum(b * nt + t - 1, 0)
    return pl.pallas_call(
        _swa_prompt_kernel,
        grid=(batch, nt),
        in_specs=[pl.BlockSpec((Q_TILE, A_WIDTH), lambda b, t: (row(b, t), COL_Q)),
                  pl.BlockSpec((Q_TILE, A_KV_WIDTH), lambda b, t: (row(b, t), COL_K)),
                  pl.BlockSpec((Q_TILE, A_KV_WIDTH), lambda b, t: (prev(b, t), COL_K)),
                  pl.BlockSpec((Q_TILE, A_KV_WIDTH), lambda b, t: (row(b, t), COL_V)),
                  pl.BlockSpec((Q_TILE, A_KV_WIDTH), lambda b, t: (prev(b, t), COL_V)),
                  pl.BlockSpec((1, LANES), lambda b, t: (0, 0)),
                  pl.BlockSpec((1, LANES), lambda b, t: (0, 0)),
                  pl.BlockSpec((LANES, LANES), lambda b, t: (0, 0)),
                  pl.BlockSpec((None, A_HEADS, Q_TILE, KEY_TILE), lambda b, t: (jnp.minimum(t, 1), 0, 0, 0)),
                  pl.BlockSpec(memory_space=pltpu.SMEM)],
        out_specs=[pl.BlockSpec((Q_TILE, A_WIDTH), lambda b, t: (row(b, t), 0)),
                   pl.BlockSpec((1, WINDOW, A_KV_WIDTH), lambda b, t: (b, 0, 0)),
                   pl.BlockSpec((1, WINDOW, A_KV_WIDTH), lambda b, t: (b, 0, 0))],
        out_shape=[jax.ShapeDtypeStruct((batch * seq, A_WIDTH), BF16),
                   jax.ShapeDtypeStruct((batch, WINDOW, A_KV_WIDTH), F32),
                   jax.ShapeDtypeStruct((batch, WINDOW, A_KV_WIDTH), F32)],
        compiler_params=_cparams(("parallel", "arbitrary")),
        name="swa_prompt",
    )(z, z, z, z, z, qg2, kg2, bd, bias, sinks)


def _swa_sample_kernel(q_ref, kn_ref, vn_ref, ck_ref, cv_ref, qg_ref, kg_ref, bd_ref,
                       bias_ref, sink_ref, o_ref, nk_ref, nv_ref):
    bd = bd_ref[...]
    rows = q_ref.shape[0]
    pad = KEY_TILE - WINDOW - rows
    k2, v2 = [], []
    for n in range(2):
        sl = slice(n * LANES, (n + 1) * LANES)
        kn = _half_norm(kn_ref[:, sl].astype(F32), kg_ref[...], bd)
        nk_ref[0, :, sl] = kn
        k2.append(jnp.concatenate([ck_ref[0, :, sl], kn, jnp.zeros((pad, LANES), F32)], axis=0))
        v2.append(jnp.concatenate([cv_ref[0, :, sl].astype(BF16), vn_ref[:, sl],
                                   jnp.zeros((pad, LANES), BF16)], axis=0))
    nv_ref[0] = vn_ref[...].astype(F32)
    _swa_heads(q_ref, qg_ref[...], bd, k2, v2, bias_ref, sink_ref, o_ref)


def swa_sample(z, batch, rows, cache_k, cache_v, layer, qg2, kg2, bd, bias, sinks):
    return pl.pallas_call(
        _swa_sample_kernel,
        grid=(batch,),
        in_specs=[pl.BlockSpec((rows, A_WIDTH), lambda b: (b, COL_Q)),
                  pl.BlockSpec((rows, A_KV_WIDTH), lambda b: (b, COL_K)),
                  pl.BlockSpec((rows, A_KV_WIDTH), lambda b: (b, COL_V)),
                  pl.BlockSpec((None, 1, WINDOW, A_KV_WIDTH), lambda b: (layer, b, 0, 0)),
                  pl.BlockSpec((None, 1, WINDOW, A_KV_WIDTH), lambda b: (layer, b, 0, 0)),
                  pl.BlockSpec((1, LANES), lambda b: (0, 0)),
                  pl.BlockSpec((1, LANES), lambda b: (0, 0)),
                  pl.BlockSpec((LANES, LANES), lambda b: (0, 0)),
                  _resident((A_HEADS, rows, KEY_TILE)),
                  pl.BlockSpec(memory_space=pltpu.SMEM)],
        out_specs=[pl.BlockSpec((rows, A_WIDTH), lambda b: (b, 0)),
                   pl.BlockSpec((1, rows, A_KV_WIDTH), lambda b: (b, 0, 0)),
                   pl.BlockSpec((1, rows, A_KV_WIDTH), lambda b: (b, 0, 0))],
        out_shape=[jax.ShapeDtypeStruct((batch * rows, A_WIDTH), BF16),
                   jax.ShapeDtypeStruct((batch, rows, A_KV_WIDTH), F32),
                   jax.ShapeDtypeStruct((batch, rows, A_KV_WIDTH), F32)],
        compiler_params=_cparams(("parallel",)),
        name="swa_sample",
    )(z, z, z, cache_k, cache_v, qg2, kg2, bd, bias, sinks)


def _conv_body(gb_ref, ub_main, uc_main, ub_halo, uc_halo, wb_ref, wc_ref, bc_ref, lg_ref, lb_ref,
               ob_ref, oc_ref, nb_ref, nc_ref, sb_ref, sc_ref, write_state):
    rows = ub_main.shape[0]
    sb_ref[0:HALO] = ub_halo
    sb_ref[HALO:HALO + rows] = ub_main
    sc_ref[0:HALO] = uc_halo
    sc_ref[HALO:HALO + rows] = uc_main
    sub = min(rows, 32)
    for r0 in range(0, rows, sub):
        yb = jnp.zeros((sub, B_WIDTH), F32)
        for k in range(B_CONV):
            yb = yb + wb_ref[k:k + 1, :] * sb_ref[pl.ds(r0 + HALO - (B_CONV - 1) + k, sub), :]
        ob_ref[r0:r0 + sub, :] = (gb_ref[r0:r0 + sub, :].astype(F32) * yb).astype(ob_ref.dtype)
        yc = jnp.zeros((sub, C_WIDTH), F32)
        for k in range(C_CONV):
            yc = yc + wc_ref[k:k + 1, :] * sc_ref[pl.ds(r0 + HALO - (C_CONV - 1) + k, sub), :]
        yc = yc + bc_ref[...]
        mu = jnp.mean(yc, axis=-1, keepdims=True)
        xc = yc - mu
        y = xc * lax.rsqrt(jnp.mean(xc * xc, axis=-1, keepdims=True) + EPS)
        y = y * lg_ref[...] + lb_ref[...]
        oc_ref[r0:r0 + sub, :] = (y * jax.nn.sigmoid(y)).astype(oc_ref.dtype)

    def _state():
        nb_ref[0] = sb_ref[rows + HALO - 8:rows + HALO]
        nc_ref[0] = sc_ref[rows:rows + HALO]

    write_state(_state)


def _conv_prompt_kernel(gb_ref, gc_ref, hb_ref, ga_ref, gg_ref, gch_ref, hbh_ref, gah_ref, ggh_ref,
                        wb_ref, wc_ref, bc_ref, lg_ref, lb_ref,
                        ob_ref, oc_ref, nb_ref, nc_ref, sb_ref, sc_ref):
    t = pl.program_id(1)
    hist = (t > 0).astype(F32)
    ub_main = gc_ref[...].astype(F32) * hb_ref[...].astype(F32)
    uc_main = ga_ref[...].astype(F32) * jax.nn.sigmoid(gg_ref[...].astype(F32))
    ub_halo = gch_ref[...].astype(F32) * hbh_ref[...].astype(F32) * hist
    uc_halo = gah_ref[...].astype(F32) * jax.nn.sigmoid(ggh_ref[...].astype(F32)) * hist
    last = pl.num_programs(1) - 1
    _conv_body(gb_ref, ub_main, uc_main, ub_halo, uc_halo, wb_ref, wc_ref, bc_ref, lg_ref, lb_ref,
               ob_ref, oc_ref, nb_ref, nc_ref, sb_ref, sc_ref,
               lambda f: pl.when(t == last)(f))


def conv_prompt(z, batch, seq, wb, wc, bc, lg, lb, *, tr=128):
    nt = seq // tr
    hp = tr // HALO
    main = lambda c: pl.BlockSpec((tr, B_WIDTH), lambda b, t: (b * nt + t, c))
    halo = lambda c: pl.BlockSpec((HALO, B_WIDTH), lambda b, t: (jnp.maximum((b * nt + t) * hp - 1, 0), c))
    vec = lambda r: pl.BlockSpec((r, B_WIDTH), lambda b, t: (0, 0))
    return pl.pallas_call(
        _conv_prompt_kernel,
        grid=(batch, nt),
        in_specs=[main(COL_GB), main(COL_GC), main(COL_HB), main(COL_GA), main(COL_GG),
                  halo(COL_GC), halo(COL_HB), halo(COL_GA), halo(COL_GG),
                  vec(B_CONV), vec(C_CONV), vec(1), vec(1), vec(1)],
        out_specs=[pl.BlockSpec((tr, B_WIDTH), lambda b, t: (b * nt + t, 0)),
                   pl.BlockSpec((tr, C_WIDTH), lambda b, t: (b * nt + t, 0)),
                   pl.BlockSpec((1, 8, B_WIDTH), lambda b, t: (b, 0, 0)),
                   pl.BlockSpec((1, HALO, C_WIDTH), lambda b, t: (b, 0, 0))],
        out_shape=[jax.ShapeDtypeStruct((batch * seq, B_WIDTH), BF16),
                   jax.ShapeDtypeStruct((batch * seq, C_WIDTH), BF16),
                   jax.ShapeDtypeStruct((batch, 8, B_WIDTH), F32),
                   jax.ShapeDtypeStruct((batch, HALO, C_WIDTH), F32)],
        scratch_shapes=[pltpu.VMEM((tr + HALO, B_WIDTH), F32), pltpu.VMEM((tr + HALO, C_WIDTH), F32)],
        compiler_params=_cparams(("parallel", "arbitrary")),
        name="conv_prompt",
    )(z, z, z, z, z, z, z, z, z, wb, wc, bc.reshape(1, -1), lg.reshape(1, -1), lb.reshape(1, -1))


def _conv_sample_kernel(gb_ref, gc_ref, hb_ref, ga_ref, gg_ref, stb_ref, stc_ref,
                        wb_ref, wc_ref, bc_ref, lg_ref, lb_ref,
                        ob_ref, oc_ref, nb_ref, nc_ref, sb_ref, sc_ref):
    ub_main = gc_ref[...].astype(F32) * hb_ref[...].astype(F32)
    uc_main = ga_ref[...].astype(F32) * jax.nn.sigmoid(gg_ref[...].astype(F32))
    _conv_body(gb_ref, ub_main, uc_main, stb_ref[0], stc_ref[0], wb_ref, wc_ref, bc_ref, lg_ref, lb_ref,
               ob_ref, oc_ref, nb_ref, nc_ref, sb_ref, sc_ref, lambda f: f())


def conv_sample(z, batch, rows, stb, stc, wb, wc, bc, lg, lb):
    main = lambda c: pl.BlockSpec((rows, B_WIDTH), lambda b: (b, c))
    vec = lambda r: pl.BlockSpec((r, B_WIDTH), lambda b: (0, 0))
    return pl.pallas_call(
        _conv_sample_kernel,
        grid=(batch,),
        in_specs=[main(COL_GB), main(COL_GC), main(COL_HB), main(COL_GA), main(COL_GG),
                  pl.BlockSpec((1, HALO, B_WIDTH), lambda b: (b, 0, 0)),
                  pl.BlockSpec((1, HALO, C_WIDTH), lambda b: (b, 0, 0)),
                  vec(B_CONV), vec(C_CONV), vec(1), vec(1), vec(1)],
        out_specs=[pl.BlockSpec((rows, B_WIDTH), lambda b: (b, 0)),
                   pl.BlockSpec((rows, C_WIDTH), lambda b: (b, 0)),
                   pl.BlockSpec((1, 8, B_WIDTH), lambda b: (b, 0, 0)),
                   pl.BlockSpec((1, HALO, C_WIDTH), lambda b: (b, 0, 0))],
        out_shape=[jax.ShapeDtypeStruct((batch * rows, B_WIDTH), BF16),
                   jax.ShapeDtypeStruct((batch * rows, C_WIDTH), BF16),
                   jax.ShapeDtypeStruct((batch, 8, B_WIDTH), F32),
                   jax.ShapeDtypeStruct((batch, HALO, C_WIDTH), F32)],
        scratch_shapes=[pltpu.VMEM((rows + HALO, B_WIDTH), F32), pltpu.VMEM((rows + HALO, C_WIDTH), F32)],
        compiler_params=_cparams(("parallel",)),
        name="conv_sample",
    )(z, z, z, z, z, stb, stc, wb, wc, bc.reshape(1, -1), lg.reshape(1, -1), lb.reshape(1, -1))


def _merge_kernel(oa_ref, ob_ref, oc_ref, l0_ref, l1_ref, l2_ref, bg_ref,
                  wpa_ref, wpb_ref, wpc_ref, wo_ref, x_ref, o_ref):
    def gated(l_ref, i, o_r, w_r):
        gate = jax.nn.sigmoid(l_ref[...].astype(F32) + bg_ref[i:i + 1, :])
        return gate * jnp.dot(o_r[...], w_r[...], preferred_element_type=F32)

    merged = gated(l0_ref, 0, oa_ref, wpa_ref)
    merged = merged + gated(l1_ref, 1, ob_ref, wpb_ref)
    merged = merged + gated(l2_ref, 2, oc_ref, wpc_ref)
    o_ref[...] = x_ref[...] + jnp.dot(merged.astype(BF16), wo_ref[...], preferred_element_type=F32)


def merge(oa, ob, oc, z, bg, wpa, wpb, wpc, wo, x, *, tm=256):
    m = x.shape[0]
    tm = min(tm, m)
    rows = lambda w: pl.BlockSpec((tm, w), lambda i: (i, 0))
    gate = lambda c: pl.BlockSpec((tm, D_MODEL), lambda i: (i, COL_GATE0 + c))
    return pl.pallas_call(
        _merge_kernel,
        grid=(m // tm,),
        in_specs=[rows(A_WIDTH), rows(B_WIDTH), rows(C_WIDTH), gate(0), gate(1), gate(2),
                  pl.BlockSpec((3, D_MODEL), lambda i: (0, 0)),
                  _resident((A_WIDTH, D_MODEL)), _resident((B_WIDTH, D_MODEL)),
                  _resident((C_WIDTH, D_MODEL)), _resident((D_MODEL, D_MODEL)),
                  rows(D_MODEL)],
        out_specs=rows(D_MODEL),
        out_shape=jax.ShapeDtypeStruct((m, D_MODEL), F32),
        compiler_params=_cparams(("parallel",)),
        name="merge",
    )(oa, ob, oc, z, z, z, bg.reshape(3, D_MODEL), wpa, wpb, wpc, wo, x)


def _xattn_kernel(y_ref, g_ref, wq_ref, qg_ref, mk_ref, mv_ref, wo_ref, o_ref, *, nb, rpb):
    y = y_ref[...]
    hn = _rms(y, g_ref[...]).astype(BF16)
    q = jnp.dot(hn, wq_ref[...], preferred_element_type=F32)
    heads = []
    for h in range(X_HEADS):
        sl = slice(h * X_HEAD_DIM, (h + 1) * X_HEAD_DIM)
        qh = _rms(q[:, sl], qg_ref[...])
        per_batch = []
        for b in range(nb):
            qb = qh[b * rpb:(b + 1) * rpb].astype(BF16)
            kh = mk_ref[b, :, sl].astype(BF16)
            vh = mv_ref[b, :, sl].astype(BF16)
            s = lax.dot_general(qb, kh, (((1,), (1,)), ((), ())),
                                preferred_element_type=F32) * (X_HEAD_DIM ** -0.5)
            m = jnp.max(s, axis=-1, keepdims=True)
            p = jnp.exp(s - m)
            den = jnp.sum(p, axis=-1, keepdims=True)
            per_batch.append(jnp.dot(p.astype(BF16), vh, preferred_element_type=F32) / den)
        heads.append(per_batch[0] if nb == 1 else jnp.concatenate(per_batch, axis=0))
    o = jnp.concatenate(heads, axis=1).astype(BF16)
    o_ref[...] = y + jnp.dot(o, wo_ref[...], preferred_element_type=F32)


def xattn(y, g, wq, qg, mk, mv, wo, *, layer, nb, rpb, tiles_per_mem):
    m = y.shape[0]
    tm = nb * rpb
    mem_idx = ((lambda i: (layer, i // tiles_per_mem, 0, 0)) if nb == 1
               else (lambda i: (layer, i, 0, 0)))
    return pl.pallas_call(
        functools.partial(_xattn_kernel, nb=nb, rpb=rpb),
        grid=(m // tm,),
        in_specs=[pl.BlockSpec((tm, D_MODEL), lambda i: (i, 0)),
                  pl.BlockSpec((1, D_MODEL), lambda i: (0, 0)),
                  _resident((D_MODEL, X_WIDTH)),
                  pl.BlockSpec((1, X_HEAD_DIM), lambda i: (0, 0)),
                  pl.BlockSpec((None, nb, N_MEM, X_WIDTH), mem_idx),
                  pl.BlockSpec((None, nb, N_MEM, X_WIDTH), mem_idx),
                  _resident((X_WIDTH, D_MODEL))],
        out_specs=pl.BlockSpec((tm, D_MODEL), lambda i: (i, 0)),
        out_shape=jax.ShapeDtypeStruct((m, D_MODEL), F32),
        compiler_params=_cparams(("parallel",)),
        name="xattn",
    )(y, g.reshape(1, -1), wq, qg.reshape(1, -1), mk, mv, wo)


def _ffn_kernel(x_ref, g_ref, wg_ref, wu_ref, wd_ref, o_ref, hn_ref):
    @pl.when(pl.program_id(1) == 0)
    def _():
        x = x_ref[...]
        hn_ref[...] = _rms(x, g_ref[...]).astype(BF16)
        o_ref[...] = x

    hn = hn_ref[...]
    g = jnp.dot(hn, wg_ref[...].astype(BF16), preferred_element_type=F32)
    u = jnp.dot(hn, wu_ref[...].astype(BF16), preferred_element_type=F32)
    a = (g * jax.nn.sigmoid(g) * u).astype(BF16)
    o_ref[...] += jnp.dot(a, wd_ref[...].astype(BF16), preferred_element_type=F32)


def ffn(x, g, w_gu, w_d, *, tm=1024, tf=512):
    m = x.shape[0]
    tm = min(tm, m)
    nf = D_FF // tf
    return pl.pallas_call(
        _ffn_kernel,
        grid=(m // tm, nf),
        in_specs=[pl.BlockSpec((tm, D_MODEL), lambda i, f: (i, 0), pipeline_mode=pl.Buffered(1)),
                  pl.BlockSpec((1, D_MODEL), lambda i, f: (0, 0)),
                  pl.BlockSpec((D_MODEL, tf), lambda i, f: (0, f)),
                  pl.BlockSpec((D_MODEL, tf), lambda i, f: (0, nf + f)),
                  pl.BlockSpec((tf, D_MODEL), lambda i, f: (f, 0))],
        out_specs=pl.BlockSpec((tm, D_MODEL), lambda i, f: (i, 0), pipeline_mode=pl.Buffered(1)),
        out_shape=jax.ShapeDtypeStruct((m, D_MODEL), F32),
        scratch_shapes=[pltpu.VMEM((tm, D_MODEL), BF16)],
        compiler_params=_cparams(("parallel", "arbitrary")),
        name="ffn",
    )(x, g.reshape(1, -1), w_gu, w_gu, w_d)


def _split3(x):
    hi = x.astype(BF16)
    lo = (x - hi.astype(F32)).astype(BF16)
    return hi, lo


def _router_kernel(x_ref, g_ref, w_ref, b_ref, wts_ref, ids_ref, hnp_ref):
    hn = _rms(x_ref[...], g_ref[...])
    h_hi, h_lo = _split3(hn)
    w_hi, w_lo = _split3(w_ref[...])
    dot = functools.partial(jnp.dot, preferred_element_type=F32)
    logits = dot(h_hi, w_hi) + dot(h_hi, w_lo) + dot(h_lo, w_hi) + b_ref[...]
    lane = lax.broadcasted_iota(jnp.int32, logits.shape, 1).astype(F32)
    logits = jnp.where(lane < N_EXPERTS, logits, -jnp.inf)
    v1 = jnp.max(logits, axis=-1, keepdims=True)
    i1 = jnp.min(jnp.where(logits == v1, lane, float(LANES)), axis=-1, keepdims=True)
    rest = jnp.where(lane == i1, -jnp.inf, logits)
    v2 = jnp.max(rest, axis=-1, keepdims=True)
    i2 = jnp.min(jnp.where(rest == v2, lane, float(LANES)), axis=-1, keepdims=True)
    e2 = jnp.exp(v2 - v1)
    den = 1.0 + e2
    wts_ref[...] = jnp.where(lane == 0.0, 1.0 / den, 0.0) + jnp.where(lane == 1.0, e2 / den, 0.0)
    ids_ref[...] = (jnp.where(lane == 0.0, i1, 0.0) + jnp.where(lane == 1.0, i2, 0.0)).astype(jnp.int32)
    half = D_MODEL // 2
    lo_bits = lax.bitcast_convert_type(h_hi[:, :half].astype(F32), jnp.uint32)
    hi_bits = lax.bitcast_convert_type(h_hi[:, half:].astype(F32), jnp.uint32)
    hnp_ref[...] = (lo_bits >> 16) | (hi_bits & jnp.uint32(0xFFFF0000))


def router(x, g, w_pad, b_pad, *, tm=512):
    m = x.shape[0]
    tm = min(tm, m)
    return pl.pallas_call(
        _router_kernel,
        grid=(m // tm,),
        in_specs=[pl.BlockSpec((tm, D_MODEL), lambda i: (i, 0)),
                  pl.BlockSpec((1, D_MODEL), lambda i: (0, 0)),
                  pl.BlockSpec((D_MODEL, LANES), lambda i: (0, 0)),
                  pl.BlockSpec((1, LANES), lambda i: (0, 0))],
        out_specs=[pl.BlockSpec((tm, LANES), lambda i: (i, 0)),
                   pl.BlockSpec((tm, LANES), lambda i: (i, 0)),
                   pl.BlockSpec((tm, D_MODEL // 2), lambda i: (i, 0))],
        out_shape=[jax.ShapeDtypeStruct((m, LANES), F32),
                   jax.ShapeDtypeStruct((m, LANES), jnp.int32),
                   jax.ShapeDtypeStruct((m, D_MODEL // 2), jnp.uint32)],
        compiler_params=_cparams(("parallel",)),
        name="router",
    )(x, g.reshape(1, -1), w_pad, b_pad)


MOE_TM = 1024
MOE_TF = 512


def _row_copy(src_hbm, row, dst_vmem, r, sem):
    return pltpu.make_async_copy(src_hbm.at[pl.ds(row, 1)], dst_vmem.at[pl.ds(r, 1)], sem)


def _moe_ffn_kernel(te_ref, tv_ref, src_ref, hnp_hbm, wg_ref, wu_ref, wd_ref, o_ref, xbuf, hn_ref, sem):
    t = pl.program_id(0)
    f = pl.program_id(1)
    valid = tv_ref[t] != 0
    rows = xbuf.shape[0]
    half = D_MODEL // 2

    @pl.when(f == 0)
    def _():
        o_ref[...] = jnp.zeros_like(o_ref)

    @pl.when(jnp.logical_and(valid, f == 0))
    def _():
        def issue(r, c):
            _row_copy(hnp_hbm, src_ref[0, 0, r], xbuf, r, sem).start()
            return c

        lax.fori_loop(0, rows, issue, 0, unroll=8)

        def wait(r, c):
            _row_copy(hnp_hbm, 0, xbuf, r, sem).wait()
            return c

        lax.fori_loop(0, rows, wait, 0, unroll=8)
        xu = xbuf[...]
        hn_ref[:, :half] = lax.bitcast_convert_type(xu << 16, F32).astype(BF16)
        hn_ref[:, half:] = lax.bitcast_convert_type(xu & jnp.uint32(0xFFFF0000), F32).astype(BF16)

    def swiglu(r):
        hn = hn_ref[0:r]
        g = jnp.dot(hn, wg_ref[...].astype(BF16), preferred_element_type=F32)
        u = jnp.dot(hn, wu_ref[...].astype(BF16), preferred_element_type=F32)
        a = (g * jax.nn.sigmoid(g) * u).astype(BF16)
        o_ref[0:r] += jnp.dot(a, wd_ref[...].astype(BF16), preferred_element_type=F32)

    nv = tv_ref[t]

    @pl.when(nv > rows // 2)
    def _():
        swiglu(rows)

    @pl.when(jnp.logical_and(valid, nv <= rows // 2))
    def _():
        swiglu(rows // 2)


def moe_ffn(tile_expert, tile_valid, src, hnp, w_gu, w_d):
    nt, _, tm = src.shape
    nf = D_FF // MOE_TF
    last = nf - 1
    col = lambda f, tv, t: jnp.where(tv[t] != 0, f, last)
    grid_spec = pltpu.PrefetchScalarGridSpec(
        num_scalar_prefetch=2,
        grid=(nt, nf),
        in_specs=[pl.BlockSpec((1, 1, tm), lambda t, f, te, tv: (t, 0, 0), memory_space=pltpu.SMEM),
                  pl.BlockSpec(memory_space=pl.ANY),
                  pl.BlockSpec((None, D_MODEL, MOE_TF), lambda t, f, te, tv: (te[t], 0, col(f, tv, t))),
                  pl.BlockSpec((None, D_MODEL, MOE_TF), lambda t, f, te, tv: (te[t], 0, nf + col(f, tv, t))),
                  pl.BlockSpec((None, MOE_TF, D_MODEL), lambda t, f, te, tv: (te[t], col(f, tv, t), 0))],
        out_specs=pl.BlockSpec((tm, D_MODEL), lambda t, f, te, tv: (t, 0), pipeline_mode=pl.Buffered(1)),
        scratch_shapes=[pltpu.VMEM((tm, D_MODEL // 2), jnp.uint32),
                        pltpu.VMEM((tm, D_MODEL), BF16),
                        pltpu.SemaphoreType.DMA],
    )
    return pl.pallas_call(
        _moe_ffn_kernel,
        grid_spec=grid_spec,
        out_shape=jax.ShapeDtypeStruct((nt * tm, D_MODEL), F32),
        compiler_params=_cparams(("arbitrary", "arbitrary")),
        name="moe_ffn",
    )(tile_expert, tile_valid, src, hnp, w_gu, w_gu, w_d)


def _moe_combine_kernel(pos_ref, x_ref, w_ref, osort_hbm, o_ref, abuf, sem):
    tm = x_ref.shape[0]

    def issue(r, c):
        _row_copy(osort_hbm, pos_ref[0, 0, r], abuf, r, sem).start()
        return c

    lax.fori_loop(0, 2 * tm, issue, 0, unroll=8)

    def wait(r, c):
        _row_copy(osort_hbm, 0, abuf, r, sem).wait()
        return c

    lax.fori_loop(0, 2 * tm, wait, 0, unroll=8)
    w = w_ref[...]
    o_ref[...] = x_ref[...] + w[:, 0:1] * abuf[0:tm, :] + w[:, 1:2] * abuf[tm:2 * tm, :]


def moe_combine(pos, x, wts, osort, *, tm=256):
    m = x.shape[0]
    return pl.pallas_call(
        _moe_combine_kernel,
        grid=(m // tm,),
        in_specs=[pl.BlockSpec((1, 1, 2 * tm), lambda i: (i, 0, 0), memory_space=pltpu.SMEM),
                  pl.BlockSpec((tm, D_MODEL), lambda i: (i, 0)),
                  pl.BlockSpec((tm, LANES), lambda i: (i, 0)),
                  pl.BlockSpec(memory_space=pl.ANY)],
        out_specs=pl.BlockSpec((tm, D_MODEL), lambda i: (i, 0)),
        out_shape=jax.ShapeDtypeStruct((m, D_MODEL), F32),
        scratch_shapes=[pltpu.VMEM((2 * tm, D_MODEL), F32), pltpu.SemaphoreType.DMA],
        compiler_params=_cparams(("arbitrary",)),
        name="moe_combine",
    )(pos, x, wts, osort)


def _dispatch_plan(ids, tm):
    m = ids.shape[0]
    nt = (2 * m) // tm + N_EXPERTS
    flat = ids.reshape(-1)
    onehot = (flat[:, None] == jnp.arange(N_EXPERTS, dtype=jnp.int32)[None, :]).astype(jnp.int32)
    csum = jnp.cumsum(onehot, axis=0)
    rank = jnp.sum((csum - onehot) * onehot, axis=1)
    counts = csum[-1]
    padded = ((counts + tm - 1) // tm) * tm
    ends = jnp.cumsum(padded)
    pos = (ends - padded)[flat] + rank
    src = jnp.zeros((nt * tm,), jnp.int32).at[pos].set(jnp.arange(2 * m, dtype=jnp.int32) // 2)
    starts = jnp.arange(nt, dtype=jnp.int32) * tm
    tile_expert = jnp.minimum(jnp.sum((starts[:, None] >= ends[None, :]).astype(jnp.int32), axis=1),
                              N_EXPERTS - 1)
    real_end = (ends - padded + counts)[tile_expert]
    tile_rows = jnp.where(starts < ends[-1], jnp.clip(real_end - starts, 0, tm), 0).astype(jnp.int32)
    last_valid = jnp.maximum(jnp.sum((tile_rows != 0).astype(jnp.int32)) - 1, 0)
    tile_expert = jnp.where(tile_rows != 0, tile_expert, tile_expert[last_valid])
    return tile_expert, tile_rows, src.reshape(nt, 1, tm), pos.reshape(m, 2)


def _combine_pos(pos, tm):
    m = pos.shape[0]
    return jnp.transpose(pos.reshape(m // tm, tm, 2), (0, 2, 1)).reshape(m // tm, 1, 2 * tm)


def _t5_bucket_np(rel):
    nb = N_BUCKETS // 2
    max_exact = nb // 2
    ret = np.where(rel > 0, nb, 0)
    n = np.abs(rel)
    nf = np.maximum(n, 1).astype(np.float32)
    large = max_exact + (np.log(nf / np.float32(max_exact)) / np.float32(math.log(MAX_DISTANCE / max_exact))
                         * np.float32(nb - max_exact)).astype(np.int32)
    large = np.minimum(large, nb - 1)
    return (ret + np.where(n < max_exact, n, large)).astype(np.int32)


def _bias_tensor(rel_table, n_q, valid):
    rel = np.arange(KEY_TILE, dtype=np.int32)[None, :] - WINDOW - np.arange(n_q, dtype=np.int32)[:, None]
    onehot = np.eye(N_BUCKETS, dtype=np.float32)[:, _t5_bucket_np(rel).reshape(-1)]
    bias = jnp.dot(rel_table.T[HEAD_PERM], jnp.asarray(onehot), precision=lax.Precision.HIGHEST)
    return jnp.where(jnp.asarray(valid)[None], bias.reshape(A_HEADS, n_q, KEY_TILE), NEG).astype(F32)


def _prompt_valid():
    qc = np.arange(Q_TILE)[:, None] // CHUNK
    kc = np.arange(KEY_TILE)[None, :] // CHUNK
    return (kc >= qc) & (kc <= qc + WINDOW // CHUNK)


def _sample_valid(rows):
    return np.broadcast_to(np.arange(KEY_TILE)[None, :] < WINDOW + rows, (rows, KEY_TILE))


def _half_avg():
    blk = np.kron(np.eye(2), np.ones((A_HEAD_DIM, A_HEAD_DIM))) / A_HEAD_DIM
    return jnp.asarray(blk, BF16)


def _mixer_weights(l, g_mix, w_in, b_gate, q_norm_g, k_norm_g, sinks, w_conv_b, w_conv_c, b_conv_c,
                   ln_c_g, ln_c_b, w_proj_a, w_proj_b, w_proj_c, w_out):
    w = w_in[l]
    wq = w[:, :A_WIDTH].reshape(D_MODEL, A_HEADS, A_HEAD_DIM)[:, HEAD_PERM].reshape(D_MODEL, A_WIDTH)
    w_in_bf = jnp.concatenate([wq, w[:, A_WIDTH:]], axis=1).astype(BF16)
    wpa = w_proj_a[l].reshape(A_HEADS, A_HEAD_DIM, D_MODEL)[HEAD_PERM].reshape(A_WIDTH, D_MODEL)
    return dict(
        g_mix=g_mix[l], w_in=w_in_bf, b_gate=b_gate[l],
        qg2=(jnp.tile(q_norm_g[l], 2) * (A_HEAD_DIM ** -0.5)).reshape(1, LANES), kg2=jnp.tile(k_norm_g[l], 2).reshape(1, LANES),
        sinks=sinks[l][HEAD_PERM],
        w_cb=w_conv_b[l], w_cc=w_conv_c[l], b_cc=b_conv_c[l], ln_g=ln_c_g[l], ln_b=ln_c_b[l],
        wpa=wpa.astype(BF16), wpb=w_proj_b[l].astype(BF16), wpc=w_proj_c[l].astype(BF16),
        wo=w_out[l].astype(BF16))


def _mixer_prompt(x, mw, bias, bd, batch, seq):
    z = norm_matmul(x, mw["g_mix"], mw["w_in"], tm=1024, tn=1024, out_dtype=BF16)
    oa, nk, nv = swa_prompt(z, batch, seq, mw["qg2"], mw["kg2"], bd, bias, mw["sinks"])
    ob, oc, ncb, ncc = conv_prompt(z, batch, seq, mw["w_cb"], mw["w_cc"], mw["b_cc"], mw["ln_g"], mw["ln_b"])
    y = merge(oa, ob, oc, z, mw["b_gate"], mw["wpa"], mw["wpb"], mw["wpc"], mw["wo"], x)
    return y, (nk, nv, ncb[:, 8 - (B_CONV - 1):], ncc[:, HALO - (C_CONV - 1):])


def _mixer_sample(x, mw, bias, bd, batch, rows, cache_k, cache_v, layer, st_b, st_c):
    z = norm_matmul(x, mw["g_mix"], mw["w_in"], tm=x.shape[0], tn=1024, out_dtype=BF16)
    oa, nk, nv = swa_sample(z, batch, rows, cache_k, cache_v, layer, mw["qg2"], mw["kg2"], bd, bias,
                            mw["sinks"])
    stb = jnp.pad(st_b, ((0, 0), (HALO - (B_CONV - 1), 0), (0, 0)))
    stc = jnp.pad(st_c, ((0, 0), (HALO - (C_CONV - 1), 0), (0, 0)))
    ob, oc, ncb, ncc = conv_sample(z, batch, rows, stb, stc, mw["w_cb"], mw["w_cc"], mw["b_cc"],
                                   mw["ln_g"], mw["ln_b"])
    y = merge(oa, ob, oc, z, mw["b_gate"], mw["wpa"], mw["wpb"], mw["wpc"], mw["wo"], x)
    return y, (nk, nv, ncb[:, 8 - (B_CONV - 1):], ncc[:, HALO - (C_CONV - 1):])


def _channel_mixer(yp, ys, l, g_ffn, w_ffn_gu, w_ffn_d, w_router, b_router, w_moe_gu, w_moe_d):
    if l % 2 == 0:
        return (ffn(yp, g_ffn[l], w_ffn_gu[l // 2], w_ffn_d[l // 2]),
                ffn(ys, g_ffn[l], w_ffn_gu[l // 2], w_ffn_d[l // 2]))
    i = l // 2
    w_pad = jnp.pad(w_router[i], ((0, 0), (0, LANES - N_EXPERTS)))
    b_pad = jnp.pad(b_router[i], (0, LANES - N_EXPERTS)).reshape(1, LANES)
    wts_p, ids_p, hnp_p = router(yp, g_ffn[l], w_pad, b_pad)
    wts_s, ids_s, hnp_s = router(ys, g_ffn[l], w_pad, b_pad)
    ids = jnp.concatenate([ids_p[:, :2], ids_s[:, :2]], axis=0)
    hnp = jnp.concatenate([hnp_p, hnp_s], axis=0)
    tile_expert, tile_valid, src, pos = _dispatch_plan(ids, MOE_TM)
    osort = moe_ffn(tile_expert, tile_valid, src, hnp, w_moe_gu[i], w_moe_d[i])
    mp = yp.shape[0]
    tm_p, tm_s = 512, min(512, ys.shape[0])
    return (moe_combine(_combine_pos(pos[:mp], tm_p), yp, wts_p, osort, tm=tm_p),
            moe_combine(_combine_pos(pos[mp:], tm_s), ys, wts_s, osort, tm=tm_s))


def kernel(x_prompt, x_sample, mem_prompt, cache_mem_k, cache_mem_v, cache_swa_k, cache_swa_v, state_conv_b, state_conv_c, rel_table, g_mix, w_in, b_gate, q_norm_g, k_norm_g, sinks, w_conv_b, w_conv_c, b_conv_c, ln_c_g, ln_c_b, w_proj_a, w_proj_b, w_proj_c, w_out, g_xattn, g_mem, w_xq, w_xkv, xq_norm_g, xk_norm_g, w_xo, g_ffn, w_ffn_gu, w_ffn_d, w_router, b_router, w_moe_gu, w_moe_d):
    batch, seq, d = x_prompt.shape
    dec_batch, dec_seq, _ = x_sample.shape
    depth = g_mix.shape[0]
    yp = x_prompt.reshape(batch * seq, d)
    ys = x_sample.reshape(dec_batch * dec_seq, d)
    mem = mem_prompt.reshape(batch * N_MEM, d)
    bd = _half_avg()
    first = np.arange(KEY_TILE)[None, :] >= Q_TILE
    bias_p = jnp.stack([_bias_tensor(rel_table, Q_TILE, _prompt_valid() & first),
                        _bias_tensor(rel_table, Q_TILE, _prompt_valid())])
    bias_s = _bias_tensor(rel_table, dec_seq, _sample_valid(dec_seq))
    outs = [[] for _ in range(10)]
    ck = cache_swa_k.reshape(depth, dec_batch, WINDOW, A_KV_WIDTH)
    cv = cache_swa_v.reshape(depth, dec_batch, WINDOW, A_KV_WIDTH)
    cmk = cache_mem_k.reshape(depth, dec_batch, N_MEM, X_WIDTH)
    cmv = cache_mem_v.reshape(depth, dec_batch, N_MEM, X_WIDTH)
    for l in range(depth):
        mw = _mixer_weights(l, g_mix, w_in, b_gate, q_norm_g, k_norm_g, sinks, w_conv_b, w_conv_c,
                            b_conv_c, ln_c_g, ln_c_b, w_proj_a, w_proj_b, w_proj_c, w_out)
        wq = w_xq[l].astype(BF16)
        wo = w_xo[l].astype(BF16)
        ffn_args = (g_ffn, w_ffn_gu, w_ffn_d, w_router, b_router, w_moe_gu, w_moe_d)
        yp, (nk, nv, ncb, ncc) = _mixer_prompt(yp, mw, bias_p, bd, batch, seq)
        mk, mv = mem_kv(mem, g_mem[l], w_xkv[l].astype(BF16), xk_norm_g[l])
        mk3 = mk.reshape(batch, N_MEM, X_WIDTH)
        mv3 = mv.reshape(batch, N_MEM, X_WIDTH)
        yp = xattn(yp, g_xattn[l], wq, xq_norm_g[l], mk3[None], mv3[None], wo, layer=0, nb=1, rpb=512,
                   tiles_per_mem=seq // 512)
        for lst, v in zip(outs[:6], (mk3.reshape(batch, N_MEM, X_HEADS, X_HEAD_DIM),
                                     mv3.reshape(batch, N_MEM, X_HEADS, X_HEAD_DIM),
                                     nk.reshape(batch, WINDOW, A_KV_HEADS, A_HEAD_DIM),
                                     nv.reshape(batch, WINDOW, A_KV_HEADS, A_HEAD_DIM), ncb, ncc)):
            lst.append(v)
        ys, (nk, nv, ncb, ncc) = _mixer_sample(ys, mw, bias_s, bd, dec_batch, dec_seq, ck, cv, l,
                                               state_conv_b[l], state_conv_c[l])
        ys = xattn(ys, g_xattn[l], wq, xq_norm_g[l], cmk, cmv, wo, layer=l, nb=8, rpb=dec_seq,
                   tiles_per_mem=1)
        yp, ys = _channel_mixer(yp, ys, l, *ffn_args)
        for lst, v in zip(outs[6:], (nk.reshape(dec_batch, dec_seq, A_KV_HEADS, A_HEAD_DIM),
                                     nv.reshape(dec_batch, dec_seq, A_KV_HEADS, A_HEAD_DIM), ncb, ncc)):
            lst.append(v)
    return (yp.reshape(batch, seq, d), ys.reshape(dec_batch, dec_seq, d)) + tuple(jnp.stack(o) for o in outs)
```

```python
import functools
import math

import numpy as np
import jax
import jax.numpy as jnp
from jax import lax
from jax.experimental import pallas as pl
from jax.experimental.pallas import tpu as pltpu

F32 = jnp.float32
BF16 = jnp.bfloat16

D_MODEL = 2048
CHUNK = 64
A_HEADS = 16
A_KV_HEADS = 4
A_HEAD_DIM = 64
A_WIDTH = A_HEADS * A_HEAD_DIM
A_KV_WIDTH = A_KV_HEADS * A_HEAD_DIM
WINDOW = 128
N_BUCKETS = 32
MAX_DISTANCE = 128
B_WIDTH = 512
B_CONV = 3
C_WIDTH = 512
C_CONV = 31
N_MEM = 256
X_HEADS = 4
X_HEAD_DIM = 128
X_WIDTH = X_HEADS * X_HEAD_DIM
D_FF = 5632
N_EXPERTS = 8
EPS = 1e-6

LANES = 128
KEY_TILE = 256
Q_TILE = 128
HALO = 32
NEG = -1e30
VMEM_LIMIT = 56 * 1024 * 1024

COL_Q = 0
COL_K, COL_V = 4, 5
COL_GB, COL_GC, COL_HB, COL_GA, COL_GG = 3, 4, 5, 6, 7
COL_GATE0 = 2
IN_COLS = 4096 + 3 * D_MODEL

HEAD_PERM = np.array([8 * n + (p % 2) * 4 + p // 2 for n in range(2) for p in range(8)])


def _cparams(sem):
    return pltpu.CompilerParams(dimension_semantics=sem, vmem_limit_bytes=VMEM_LIMIT)


def _rms(x, g):
    ms = jnp.mean(x * x, axis=-1, keepdims=True)
    return x * lax.rsqrt(ms + EPS) * g


def _resident(shape):
    nd = len(shape)
    return pl.BlockSpec(shape, lambda *_: (0,) * nd, pipeline_mode=pl.Buffered(1))


def _norm_matmul_kernel(x_ref, g_ref, w_ref, o_ref, hn_ref):
    @pl.when(pl.program_id(1) == 0)
    def _():
        hn_ref[...] = _rms(x_ref[...], g_ref[...]).astype(BF16)

    o_ref[...] = jnp.dot(hn_ref[...], w_ref[...], preferred_element_type=F32).astype(o_ref.dtype)


def norm_matmul(x, g, w, *, tm, tn, out_dtype):
    m, k = x.shape
    n = w.shape[1]
    return pl.pallas_call(
        _norm_matmul_kernel,
        grid=(m // tm, n // tn),
        in_specs=[pl.BlockSpec((tm, k), lambda i, j: (i, 0)),
                  pl.BlockSpec((1, k), lambda i, j: (0, 0)),
                  pl.BlockSpec((k, tn), lambda i, j: (0, j))],
        out_specs=pl.BlockSpec((tm, tn), lambda i, j: (i, j)),
        out_shape=jax.ShapeDtypeStruct((m, n), out_dtype),
        scratch_shapes=[pltpu.VMEM((tm, k), BF16)],
        compiler_params=_cparams(("parallel", "arbitrary")),
        name="norm_matmul",
    )(x, g.reshape(1, k), w)


def _mem_kv_kernel(x_ref, g_ref, w_ref, kg_ref, k_ref, v_ref):
    hn = _rms(x_ref[...], g_ref[...]).astype(BF16)
    kv = jnp.dot(hn, w_ref[...], preferred_element_type=F32)
    for h in range(X_HEADS):
        sl = slice(h * X_HEAD_DIM, (h + 1) * X_HEAD_DIM)
        k_ref[:, sl] = _rms(kv[:, sl], kg_ref[...])
    v_ref[...] = kv[:, X_WIDTH:]


def mem_kv(mem, g, w_bf, kg):
    m, k = mem.shape
    tm = 256
    return pl.pallas_call(
        _mem_kv_kernel,
        grid=(m // tm,),
        in_specs=[pl.BlockSpec((tm, k), lambda i: (i, 0)),
                  pl.BlockSpec((1, k), lambda i: (0, 0)),
                  _resident((k, 2 * X_WIDTH)),
                  pl.BlockSpec((1, X_HEAD_DIM), lambda i: (0, 0))],
        out_specs=[pl.BlockSpec((tm, X_WIDTH), lambda i: (i, 0)),
                   pl.BlockSpec((tm, X_WIDTH), lambda i: (i, 0))],
        out_shape=[jax.ShapeDtypeStruct((m, X_WIDTH), F32)] * 2,
        compiler_params=_cparams(("parallel",)),
        name="mem_kv",
    )(mem, g.reshape(1, k), w_bf, kg.reshape(1, X_HEAD_DIM))


def _half_norm(x, g, bd):
    x2 = x * x
    hi = x2.astype(BF16)
    lo = (x2 - hi.astype(F32)).astype(BF16)
    ms = (jnp.dot(hi, bd, preferred_element_type=F32)
          + jnp.dot(lo, bd, preferred_element_type=F32))
    return x * lax.rsqrt(ms + EPS) * g


def _swa_heads(q_ref, qg, bd, k2, v2, bias_ref, sink_ref, o_ref, *, stack):
    lo_lane = lax.broadcasted_iota(jnp.int32, (1, LANES), 1) < A_HEAD_DIM
    rows = q_ref.shape[0]
    cols = A_WIDTH // LANES // 2
    for n in range(2):
        k_half = (jnp.where(lo_lane, k2[n], 0.0).astype(BF16), jnp.where(lo_lane, 0.0, k2[n]).astype(BF16))
        for g in range(0, cols, stack):
            qs = [_half_norm(q_ref[:, c * LANES:(c + 1) * LANES].astype(F32), qg, bd)
                  for c in range(cols * n + g, cols * n + g + stack)]
            qn = (qs[0] if stack == 1 else jnp.concatenate(qs, axis=0)).astype(BF16)
            at = slice(g * rows, (g + stack) * rows)
            halves = []
            for half in range(2):
                s = lax.dot_general(qn, k_half[half], (((1,), (1,)), ((), ())), preferred_element_type=F32)
                s = s + bias_ref[n, half, at, :]
                if sink_ref is None:
                    m = jnp.max(s, axis=-1, keepdims=True)
                    p = jnp.exp(s - m)
                    den = jnp.sum(p, axis=-1, keepdims=True)
                else:
                    sink = sink_ref[2 * (cols * n + g) + half]
                    m = jnp.maximum(jnp.max(s, axis=-1, keepdims=True), sink)
                    p = jnp.exp(s - m)
                    den = jnp.sum(p, axis=-1, keepdims=True) + jnp.exp(sink - m)
                o = jnp.dot(p.astype(BF16), v2[n], preferred_element_type=F32)
                halves.append(o / den)
            o = jnp.where(lo_lane, halves[0], halves[1]).astype(o_ref.dtype)
            for i in range(stack):
                c = cols * n + g + i
                o_ref[:, c * LANES:(c + 1) * LANES] = o[i * rows:(i + 1) * rows]


def _swa_prompt_kernel(q_ref, kc_ref, kp_ref, vc_ref, vp_ref, qg_ref, kg_ref, bd_ref,
                       bias_ref, sink_ref, o_ref, nk_ref, nv_ref):
    bd = bd_ref[...]
    k2, v2 = [], []
    for n in range(2):
        sl = slice(n * LANES, (n + 1) * LANES)
        kcat = jnp.concatenate([kp_ref[:, sl], kc_ref[:, sl]], axis=0).astype(F32)
        kn = _half_norm(kcat, kg_ref[...], bd)
        nk_ref[0, :, sl] = kn[Q_TILE:]
        k2.append(kn)
        v2.append(jnp.concatenate([vp_ref[:, sl], vc_ref[:, sl]], axis=0))
    nv_ref[0] = vc_ref[...].astype(F32)
    _swa_heads(q_ref, qg_ref[...], bd, k2, v2, bias_ref, sink_ref, o_ref, stack=1)


def swa_prompt(z, batch, seq, qg2, kg2, bd, bias, sinks):
    nt = seq // Q_TILE
    row = lambda b, t: b * nt + t
    prev = lambda b, t: jnp.maximum(b * nt + t - 1, 0)
    return pl.pallas_call(
        _swa_prompt_kernel,
        grid=(batch, nt),
        in_specs=[pl.BlockSpec((Q_TILE, A_WIDTH), lambda b, t: (row(b, t), COL_Q)),
                  pl.BlockSpec((Q_TILE, A_KV_WIDTH), lambda b, t: (row(b, t), COL_K)),
                  pl.BlockSpec((Q_TILE, A_KV_WIDTH), lambda b, t: (prev(b, t), COL_K)),
                  pl.BlockSpec((Q_TILE, A_KV_WIDTH), lambda b, t: (row(b, t), COL_V)),
                  pl.BlockSpec((Q_TILE, A_KV_WIDTH), lambda b, t: (prev(b, t), COL_V)),
                  pl.BlockSpec((1, LANES), lambda b, t: (0, 0)),
                  pl.BlockSpec((1, LANES), lambda b, t: (0, 0)),
                  pl.BlockSpec((LANES, LANES), lambda b, t: (0, 0)),
                  pl.BlockSpec((None, 2, 2, 4 * Q_TILE, KEY_TILE),
                               lambda b, t: (jnp.minimum(t, 1), 0, 0, 0, 0)),
                  pl.BlockSpec(memory_space=pltpu.SMEM)],
        out_specs=[pl.BlockSpec((Q_TILE, A_WIDTH), lambda b, t: (row(b, t), 0)),
                   pl.BlockSpec((1, WINDOW, A_KV_WIDTH), lambda b, t: (b, 0, 0)),
                   pl.BlockSpec((1, WINDOW, A_KV_WIDTH), lambda b, t: (b, 0, 0))],
        out_shape=[jax.ShapeDtypeStruct((batch * seq, A_WIDTH), BF16),
                   jax.ShapeDtypeStruct((batch, WINDOW, A_KV_WIDTH), F32),
                   jax.ShapeDtypeStruct((batch, WINDOW, A_KV_WIDTH), F32)],
        compiler_params=_cparams(("parallel", "arbitrary")),
        name="swa_prompt",
    )(z, z, z, z, z, qg2, kg2, bd, bias, sinks)


def _swa_sample_kernel(q_ref, kn_ref, vn_ref, ck_ref, cv_ref, qg_ref, kg_ref, bd_ref,
                       bias_ref, o_ref, nk_ref, nv_ref):
    bd = bd_ref[...]
    rows = q_ref.shape[0]
    pad = KEY_TILE - WINDOW - rows
    k2, v2 = [], []
    for n in range(2):
        sl = slice(n * LANES, (n + 1) * LANES)
        kn = _half_norm(kn_ref[:, sl].astype(F32), kg_ref[...], bd)
        nk_ref[0, :, sl] = kn
        k2.append(jnp.concatenate([ck_ref[0, :, sl], kn, jnp.zeros((pad, LANES), F32)], axis=0))
        v2.append(jnp.concatenate([cv_ref[0, :, sl].astype(BF16), vn_ref[:, sl],
                                   jnp.zeros((pad, LANES), BF16)], axis=0))
    nv_ref[0] = vn_ref[...].astype(F32)
    _swa_heads(q_ref, qg_ref[...], bd, k2, v2, bias_ref, None, o_ref, stack=4)


def swa_sample(z, batch, rows, cache_k, cache_v, layer, qg2, kg2, bd, bias):
    return pl.pallas_call(
        _swa_sample_kernel,
        grid=(batch,),
        in_specs=[pl.BlockSpec((rows, A_WIDTH), lambda b: (b, COL_Q)),
                  pl.BlockSpec((rows, A_KV_WIDTH), lambda b: (b, COL_K)),
                  pl.BlockSpec((rows, A_KV_WIDTH), lambda b: (b, COL_V)),
                  pl.BlockSpec((None, 1, WINDOW, A_KV_WIDTH), lambda b: (layer, b, 0, 0)),
                  pl.BlockSpec((None, 1, WINDOW, A_KV_WIDTH), lambda b: (layer, b, 0, 0)),
                  pl.BlockSpec((1, LANES), lambda b: (0, 0)),
                  pl.BlockSpec((1, LANES), lambda b: (0, 0)),
                  pl.BlockSpec((LANES, LANES), lambda b: (0, 0)),
                  _resident((2, 2, 4 * rows, KEY_TILE))],
        out_specs=[pl.BlockSpec((rows, A_WIDTH), lambda b: (b, 0)),
                   pl.BlockSpec((1, rows, A_KV_WIDTH), lambda b: (b, 0, 0)),
                   pl.BlockSpec((1, rows, A_KV_WIDTH), lambda b: (b, 0, 0))],
        out_shape=[jax.ShapeDtypeStruct((batch * rows, A_WIDTH), BF16),
                   jax.ShapeDtypeStruct((batch, rows, A_KV_WIDTH), F32),
                   jax.ShapeDtypeStruct((batch, rows, A_KV_WIDTH), F32)],
        compiler_params=_cparams(("parallel",)),
        name="swa_sample",
    )(z, z, z, cache_k, cache_v, qg2, kg2, bd, bias)


def _conv_body(gb_ref, ub_main, uc_main, ub_halo, uc_halo, wb_ref, wc_ref, bc_ref, lg_ref, lb_ref,
               ob_ref, oc_ref, nb_ref, nc_ref, sb_ref, sc_ref, ph_ref, write_state):
    rows = ub_main.shape[0]
    sb_ref[0:HALO] = ub_halo
    sb_ref[HALO:HALO + rows] = ub_main
    sc_ref[0:HALO] = uc_halo
    sc_ref[HALO:HALO + rows] = uc_main
    span = rows + HALO
    sc_ref[span:span + 8] = jnp.zeros((8, C_WIDTH), F32)
    for b in range(1, 8):
        ph_ref[b - 1] = sc_ref[pl.ds(b, span), :]
    sub = min(rows, 32)
    for r0 in range(0, rows, sub):
        yb = jnp.zeros((sub, B_WIDTH), F32)
        for k in range(B_CONV):
            yb = yb + wb_ref[k:k + 1, :] * sb_ref[pl.ds(r0 + HALO - (B_CONV - 1) + k, sub), :]
        ob_ref[r0:r0 + sub, :] = (gb_ref[r0:r0 + sub, :].astype(F32) * yb).astype(ob_ref.dtype)
        yc = jnp.zeros((sub, C_WIDTH), F32)
        for k in range(C_CONV):
            shift, phase = divmod(HALO - (C_CONV - 1) + k, 8)
            at = pl.ds(r0 + 8 * shift, sub)
            taps = sc_ref[at, :] if phase == 0 else ph_ref[phase - 1, at, :]
            yc = yc + wc_ref[k:k + 1, :] * taps
        yc = yc + bc_ref[...]
        mu = jnp.mean(yc, axis=-1, keepdims=True)
        xc = yc - mu
        y = xc * lax.rsqrt(jnp.mean(xc * xc, axis=-1, keepdims=True) + EPS)
        y = y * lg_ref[...] + lb_ref[...]
        oc_ref[r0:r0 + sub, :] = (y * jax.nn.sigmoid(y)).astype(oc_ref.dtype)

    def _state():
        nb_ref[0] = sb_ref[rows + HALO - 8:rows + HALO]
        nc_ref[0] = sc_ref[rows:rows + HALO]

    write_state(_state)


def _conv_prompt_kernel(gb_ref, gc_ref, hb_ref, ga_ref, gg_ref, gch_ref, hbh_ref, gah_ref, ggh_ref,
                        wb_ref, wc_ref, bc_ref, lg_ref, lb_ref,
                        ob_ref, oc_ref, nb_ref, nc_ref, sb_ref, sc_ref, ph_ref):
    t = pl.program_id(1)
    hist = (t > 0).astype(F32)
    ub_main = gc_ref[...].astype(F32) * hb_ref[...].astype(F32)
    uc_main = ga_ref[...].astype(F32) * jax.nn.sigmoid(gg_ref[...].astype(F32))
    ub_halo = gch_ref[...].astype(F32) * hbh_ref[...].astype(F32) * hist
    uc_halo = gah_ref[...].astype(F32) * jax.nn.sigmoid(ggh_ref[...].astype(F32)) * hist
    last = pl.num_programs(1) - 1
    _conv_body(gb_ref, ub_main, uc_main, ub_halo, uc_halo, wb_ref, wc_ref, bc_ref, lg_ref, lb_ref,
               ob_ref, oc_ref, nb_ref, nc_ref, sb_ref, sc_ref, ph_ref,
               lambda f: pl.when(t == last)(f))


def conv_prompt(z, batch, seq, wb, wc, bc, lg, lb, *, tr=128):
    nt = seq // tr
    hp = tr // HALO
    main = lambda c: pl.BlockSpec((tr, B_WIDTH), lambda b, t: (b * nt + t, c))
    halo = lambda c: pl.BlockSpec((HALO, B_WIDTH), lambda b, t: (jnp.maximum((b * nt + t) * hp - 1, 0), c))
    vec = lambda r: pl.BlockSpec((r, B_WIDTH), lambda b, t: (0, 0))
    return pl.pallas_call(
        _conv_prompt_kernel,
        grid=(batch, nt),
        in_specs=[main(COL_GB), main(COL_GC), main(COL_HB), main(COL_GA), main(COL_GG),
                  halo(COL_GC), halo(COL_HB), halo(COL_GA), halo(COL_GG),
                  vec(B_CONV), vec(C_CONV), vec(1), vec(1), vec(1)],
        out_specs=[pl.BlockSpec((tr, B_WIDTH), lambda b, t: (b * nt + t, 0)),
                   pl.BlockSpec((tr, C_WIDTH), lambda b, t: (b * nt + t, 0)),
                   pl.BlockSpec((1, 8, B_WIDTH), lambda b, t: (b, 0, 0)),
                   pl.BlockSpec((1, HALO, C_WIDTH), lambda b, t: (b, 0, 0))],
        out_shape=[jax.ShapeDtypeStruct((batch * seq, B_WIDTH), BF16),
                   jax.ShapeDtypeStruct((batch * seq, C_WIDTH), BF16),
                   jax.ShapeDtypeStruct((batch, 8, B_WIDTH), F32),
                   jax.ShapeDtypeStruct((batch, HALO, C_WIDTH), F32)],
        scratch_shapes=[pltpu.VMEM((tr + HALO, B_WIDTH), F32), pltpu.VMEM((tr + HALO + 8, C_WIDTH), F32),
                        pltpu.VMEM((7, tr + HALO, C_WIDTH), F32)],
        compiler_params=_cparams(("parallel", "arbitrary")),
        name="conv_prompt",
    )(z, z, z, z, z, z, z, z, z, wb, wc, bc.reshape(1, -1), lg.reshape(1, -1), lb.reshape(1, -1))


def _conv_sample_kernel(gb_ref, gc_ref, hb_ref, ga_ref, gg_ref, stb_ref, stc_ref,
                        wb_ref, wc_ref, bc_ref, lg_ref, lb_ref,
                        ob_ref, oc_ref, nb_ref, nc_ref, sb_ref, sc_ref, ph_ref):
    ub_main = gc_ref[...].astype(F32) * hb_ref[...].astype(F32)
    uc_main = ga_ref[...].astype(F32) * jax.nn.sigmoid(gg_ref[...].astype(F32))
    _conv_body(gb_ref, ub_main, uc_main, stb_ref[0], stc_ref[0], wb_ref, wc_ref, bc_ref, lg_ref, lb_ref,
               ob_ref, oc_ref, nb_ref, nc_ref, sb_ref, sc_ref, ph_ref, lambda f: f())


def conv_sample(z, batch, rows, stb, stc, wb, wc, bc, lg, lb):
    main = lambda c: pl.BlockSpec((rows, B_WIDTH), lambda b: (b, c))
    vec = lambda r: pl.BlockSpec((r, B_WIDTH), lambda b: (0, 0))
    return pl.pallas_call(
        _conv_sample_kernel,
        grid=(batch,),
        in_specs=[main(COL_GB), main(COL_GC), main(COL_HB), main(COL_GA), main(COL_GG),
                  pl.BlockSpec((1, HALO, B_WIDTH), lambda b: (b, 0, 0)),
                  pl.BlockSpec((1, HALO, C_WIDTH), lambda b: (b, 0, 0)),
                  vec(B_CONV), vec(C_CONV), vec(1), vec(1), vec(1)],
        out_specs=[pl.BlockSpec((rows, B_WIDTH), lambda b: (b, 0)),
                   pl.BlockSpec((rows, C_WIDTH), lambda b: (b, 0)),
                   pl.BlockSpec((1, 8, B_WIDTH), lambda b: (b, 0, 0)),
                   pl.BlockSpec((1, HALO, C_WIDTH), lambda b: (b, 0, 0))],
        out_shape=[jax.ShapeDtypeStruct((batch * rows, B_WIDTH), BF16),
                   jax.ShapeDtypeStruct((batch * rows, C_WIDTH), BF16),
                   jax.ShapeDtypeStruct((batch, 8, B_WIDTH), F32),
                   jax.ShapeDtypeStruct((batch, HALO, C_WIDTH), F32)],
        scratch_shapes=[pltpu.VMEM((rows + HALO, B_WIDTH), F32), pltpu.VMEM((rows + HALO + 8, C_WIDTH), F32),
                        pltpu.VMEM((7, rows + HALO, C_WIDTH), F32)],
        compiler_params=_cparams(("parallel",)),
        name="conv_sample",
    )(z, z, z, z, z, stb, stc, wb, wc, bc.reshape(1, -1), lg.reshape(1, -1), lb.reshape(1, -1))


def _merge_kernel(oa_ref, ob_ref, oc_ref, l0_ref, l1_ref, l2_ref, bg_ref,
                  wpa_ref, wpb_ref, wpc_ref, wo_ref, x_ref, o_ref):
    def gated(l_ref, i, o_r, w_r):
        gate = jax.nn.sigmoid(l_ref[...].astype(F32) + bg_ref[i:i + 1, :])
        return gate * jnp.dot(o_r[...], w_r[...], preferred_element_type=F32)

    merged = gated(l0_ref, 0, oa_ref, wpa_ref)
    merged = merged + gated(l1_ref, 1, ob_ref, wpb_ref)
    merged = merged + gated(l2_ref, 2, oc_ref, wpc_ref)
    o_ref[...] = x_ref[...] + jnp.dot(merged.astype(BF16), wo_ref[...], preferred_element_type=F32)


def merge(oa, ob, oc, z, bg, wpa, wpb, wpc, wo, x, *, tm=256):
    m = x.shape[0]
    tm = min(tm, m)
    rows = lambda w: pl.BlockSpec((tm, w), lambda i: (i, 0))
    gate = lambda c: pl.BlockSpec((tm, D_MODEL), lambda i: (i, COL_GATE0 + c))
    return pl.pallas_call(
        _merge_kernel,
        grid=(m // tm,),
        in_specs=[rows(A_WIDTH), rows(B_WIDTH), rows(C_WIDTH), gate(0), gate(1), gate(2),
                  pl.BlockSpec((3, D_MODEL), lambda i: (0, 0)),
                  _resident((A_WIDTH, D_MODEL)), _resident((B_WIDTH, D_MODEL)),
                  _resident((C_WIDTH, D_MODEL)), _resident((D_MODEL, D_MODEL)),
                  rows(D_MODEL)],
        out_specs=rows(D_MODEL),
        out_shape=jax.ShapeDtypeStruct((m, D_MODEL), F32),
        compiler_params=_cparams(("parallel",)),
        name="merge",
    )(oa, ob, oc, z, z, z, bg.reshape(3, D_MODEL), wpa, wpb, wpc, wo, x)


def _xattn_kernel(y_ref, g_ref, wq_ref, qg_ref, mk_ref, mv_ref, wo_ref, o_ref, *, nb, rpb):
    y = y_ref[...]
    hn = _rms(y, g_ref[...]).astype(BF16)
    q = jnp.dot(hn, wq_ref[...], preferred_element_type=F32)
    heads = []
    for h in range(X_HEADS):
        sl = slice(h * X_HEAD_DIM, (h + 1) * X_HEAD_DIM)
        qh = _rms(q[:, sl], qg_ref[...])
        per_batch = []
        for b in range(nb):
            qb = qh[b * rpb:(b + 1) * rpb].astype(BF16)
            kh = mk_ref[b, :, sl].astype(BF16)
            vh = mv_ref[b, :, sl].astype(BF16)
            s = lax.dot_general(qb, kh, (((1,), (1,)), ((), ())),
                                preferred_element_type=F32) * (X_HEAD_DIM ** -0.5)
            m = jnp.max(s, axis=-1, keepdims=True)
            p = jnp.exp(s - m)
            den = jnp.sum(p, axis=-1, keepdims=True)
            per_batch.append(jnp.dot(p.astype(BF16), vh, preferred_element_type=F32) / den)
        heads.append(per_batch[0] if nb == 1 else jnp.concatenate(per_batch, axis=0))
    o = jnp.concatenate(heads, axis=1).astype(BF16)
    o_ref[...] = y + jnp.dot(o, wo_ref[...], preferred_element_type=F32)


def xattn(y, g, wq, qg, mk, mv, wo, *, layer, nb, rpb, tiles_per_mem):
    m = y.shape[0]
    tm = nb * rpb
    mem_idx = ((lambda i: (layer, i // tiles_per_mem, 0, 0)) if nb == 1
               else (lambda i: (layer, i, 0, 0)))
    return pl.pallas_call(
        functools.partial(_xattn_kernel, nb=nb, rpb=rpb),
        grid=(m // tm,),
        in_specs=[pl.BlockSpec((tm, D_MODEL), lambda i: (i, 0)),
                  pl.BlockSpec((1, D_MODEL), lambda i: (0, 0)),
                  _resident((D_MODEL, X_WIDTH)),
                  pl.BlockSpec((1, X_HEAD_DIM), lambda i: (0, 0)),
                  pl.BlockSpec((None, nb, N_MEM, X_WIDTH), mem_idx),
                  pl.BlockSpec((None, nb, N_MEM, X_WIDTH), mem_idx),
                  _resident((X_WIDTH, D_MODEL))],
        out_specs=pl.BlockSpec((tm, D_MODEL), lambda i: (i, 0)),
        out_shape=jax.ShapeDtypeStruct((m, D_MODEL), F32),
        compiler_params=_cparams(("parallel",)),
        name="xattn",
    )(y, g.reshape(1, -1), wq, qg.reshape(1, -1), mk, mv, wo)


def _ffn_kernel(x_ref, g_ref, wg_ref, wu_ref, wd_ref, o_ref, hn_ref):
    @pl.when(pl.program_id(1) == 0)
    def _():
        x = x_ref[...]
        hn_ref[...] = _rms(x, g_ref[...]).astype(BF16)
        o_ref[...] = x

    hn = hn_ref[...]
    g = jnp.dot(hn, wg_ref[...].astype(BF16), preferred_element_type=F32)
    u = jnp.dot(hn, wu_ref[...].astype(BF16), preferred_element_type=F32)
    a = (g * jax.nn.sigmoid(g) * u).astype(BF16)
    o_ref[...] += jnp.dot(a, wd_ref[...].astype(BF16), preferred_element_type=F32)


def ffn(x, g, w_gu, w_d, *, tm=1024, tf=512):
    m = x.shape[0]
    tm = min(tm, m)
    nf = D_FF // tf
    return pl.pallas_call(
        _ffn_kernel,
        grid=(m // tm, nf),
        in_specs=[pl.BlockSpec((tm, D_MODEL), lambda i, f: (i, 0), pipeline_mode=pl.Buffered(1)),
                  pl.BlockSpec((1, D_MODEL), lambda i, f: (0, 0)),
                  pl.BlockSpec((D_MODEL, tf), lambda i, f: (0, f)),
                  pl.BlockSpec((D_MODEL, tf), lambda i, f: (0, nf + f)),
                  pl.BlockSpec((tf, D_MODEL), lambda i, f: (f, 0))],
        out_specs=pl.BlockSpec((tm, D_MODEL), lambda i, f: (i, 0), pipeline_mode=pl.Buffered(1)),
        out_shape=jax.ShapeDtypeStruct((m, D_MODEL), F32),
        scratch_shapes=[pltpu.VMEM((tm, D_MODEL), BF16)],
        compiler_params=_cparams(("parallel", "arbitrary")),
        name="ffn",
    )(x, g.reshape(1, -1), w_gu, w_gu, w_d)


def _split3(x):
    hi = x.astype(BF16)
    lo = (x - hi.astype(F32)).astype(BF16)
    return hi, lo


def _router_kernel(x_ref, g_ref, w_ref, b_ref, wts_ref, ids_ref, hnp_ref):
    hn = _rms(x_ref[...], g_ref[...])
    h_hi, h_lo = _split3(hn)
    w_hi, w_lo = _split3(w_ref[...])
    dot = functools.partial(jnp.dot, preferred_element_type=F32)
    logits = dot(h_hi, w_hi) + dot(h_hi, w_lo) + dot(h_lo, w_hi) + b_ref[...]
    lane = lax.broadcasted_iota(jnp.int32, logits.shape, 1).astype(F32)
    logits = jnp.where(lane < N_EXPERTS, logits, -jnp.inf)
    v1 = jnp.max(logits, axis=-1, keepdims=True)
    i1 = jnp.min(jnp.where(logits == v1, lane, float(LANES)), axis=-1, keepdims=True)
    rest = jnp.where(lane == i1, -jnp.inf, logits)
    v2 = jnp.max(rest, axis=-1, keepdims=True)
    i2 = jnp.min(jnp.where(rest == v2, lane, float(LANES)), axis=-1, keepdims=True)
    e2 = jnp.exp(v2 - v1)
    den = 1.0 + e2
    wts_ref[...] = jnp.where(lane == 0.0, 1.0 / den, 0.0) + jnp.where(lane == 1.0, e2 / den, 0.0)
    ids_ref[...] = (jnp.where(lane == 0.0, i1, 0.0) + jnp.where(lane == 1.0, i2, 0.0)).astype(jnp.int32)
    half = D_MODEL // 2
    lo_bits = lax.bitcast_convert_type(h_hi[:, :half].astype(F32), jnp.uint32)
    hi_bits = lax.bitcast_convert_type(h_hi[:, half:].astype(F32), jnp.uint32)
    hnp_ref[...] = (lo_bits >> 16) | (hi_bits & jnp.uint32(0xFFFF0000))


def router(x, g, w_pad, b_pad, *, tm=512):
    m = x.shape[0]
    tm = min(tm, m)
    return pl.pallas_call(
        _router_kernel,
        grid=(m // tm,),
        in_specs=[pl.BlockSpec((tm, D_MODEL), lambda i: (i, 0)),
                  pl.BlockSpec((1, D_MODEL), lambda i: (0, 0)),
                  pl.BlockSpec((D_MODEL, LANES), lambda i: (0, 0)),
                  pl.BlockSpec((1, LANES), lambda i: (0, 0))],
        out_specs=[pl.BlockSpec((tm, LANES), lambda i: (i, 0)),
                   pl.BlockSpec((tm, LANES), lambda i: (i, 0)),
                   pl.BlockSpec((tm, D_MODEL // 2), lambda i: (i, 0))],
        out_shape=[jax.ShapeDtypeStruct((m, LANES), F32),
                   jax.ShapeDtypeStruct((m, LANES), jnp.int32),
                   jax.ShapeDtypeStruct((m, D_MODEL // 2), jnp.uint32)],
        compiler_params=_cparams(("parallel",)),
        name="router",
    )(x, g.reshape(1, -1), w_pad, b_pad)


MOE_TM = 1024
MOE_TF = 512


def _row_copy(src_hbm, row, dst_vmem, r, sem):
    return pltpu.make_async_copy(src_hbm.at[pl.ds(row, 1)], dst_vmem.at[pl.ds(r, 1)], sem)


def _moe_ffn_kernel(te_ref, tv_ref, src_ref, hnp_hbm, wg_ref, wu_ref, wd_ref, o_ref, xbuf, hn_ref, sem):
    t = pl.program_id(0)
    f = pl.program_id(1)
    valid = tv_ref[t] != 0
    rows = xbuf.shape[0]
    half = D_MODEL // 2

    @pl.when(f == 0)
    def _():
        o_ref[...] = jnp.zeros_like(o_ref)

    @pl.when(jnp.logical_and(valid, f == 0))
    def _():
        def issue(r, c):
            _row_copy(hnp_hbm, src_ref[0, 0, r], xbuf, r, sem).start()
            return c

        lax.fori_loop(0, rows, issue, 0, unroll=8)

        def wait(r, c):
            _row_copy(hnp_hbm, 0, xbuf, r, sem).wait()
            return c

        lax.fori_loop(0, rows, wait, 0, unroll=8)
        xu = xbuf[...]
        hn_ref[:, :half] = lax.bitcast_convert_type(xu << 16, F32).astype(BF16)
        hn_ref[:, half:] = lax.bitcast_convert_type(xu & jnp.uint32(0xFFFF0000), F32).astype(BF16)

    def swiglu(r):
        hn = hn_ref[0:r]
        g = jnp.dot(hn, wg_ref[...].astype(BF16), preferred_element_type=F32)
        u = jnp.dot(hn, wu_ref[...].astype(BF16), preferred_element_type=F32)
        a = (g * jax.nn.sigmoid(g) * u).astype(BF16)
        o_ref[0:r] += jnp.dot(a, wd_ref[...].astype(BF16), preferred_element_type=F32)

    nv = tv_ref[t]

    @pl.when(nv > rows // 2)
    def _():
        swiglu(rows)

    @pl.when(jnp.logical_and(nv > rows // 4, nv <= rows // 2))
    def _():
        swiglu(rows // 2)

    @pl.when(jnp.logical_and(valid, nv <= rows // 4))
    def _():
        swiglu(rows // 4)


def moe_ffn(tile_expert, tile_valid, src, hnp, w_gu, w_d):
    nt, _, tm = src.shape
    nf = D_FF // MOE_TF
    last = nf - 1
    col = lambda f, tv, t: jnp.where(tv[t] != 0, f, last)
    grid_spec = pltpu.PrefetchScalarGridSpec(
        num_scalar_prefetch=2,
        grid=(nt, nf),
        in_specs=[pl.BlockSpec((1, 1, tm), lambda t, f, te, tv: (t, 0, 0), memory_space=pltpu.SMEM),
                  pl.BlockSpec(memory_space=pl.ANY),
                  pl.BlockSpec((None, D_MODEL, MOE_TF), lambda t, f, te, tv: (te[t], 0, col(f, tv, t))),
                  pl.BlockSpec((None, D_MODEL, MOE_TF), lambda t, f, te, tv: (te[t], 0, nf + col(f, tv, t))),
                  pl.BlockSpec((None, MOE_TF, D_MODEL), lambda t, f, te, tv: (te[t], col(f, tv, t), 0))],
        out_specs=pl.BlockSpec((tm, D_MODEL), lambda t, f, te, tv: (t, 0)),
        scratch_shapes=[pltpu.VMEM((tm, D_MODEL // 2), jnp.uint32),
                        pltpu.VMEM((tm, D_MODEL), BF16),
                        pltpu.SemaphoreType.DMA],
    )
    return pl.pallas_call(
        _moe_ffn_kernel,
        grid_spec=grid_spec,
        out_shape=jax.ShapeDtypeStruct((nt * tm, D_MODEL), F32),
        compiler_params=_cparams(("arbitrary", "arbitrary")),
        name="moe_ffn",
    )(tile_expert, tile_valid, src, hnp, w_gu, w_gu, w_d)


def _moe_combine_kernel(pos_ref, x_ref, w_ref, osort_hbm, o_ref, abuf, sem):
    tm = x_ref.shape[0]

    def issue(r, c):
        _row_copy(osort_hbm, pos_ref[0, 0, r], abuf, r, sem).start()
        return c

    lax.fori_loop(0, 2 * tm, issue, 0, unroll=8)

    def wait(r, c):
        _row_copy(osort_hbm, 0, abuf, r, sem).wait()
        return c

    lax.fori_loop(0, 2 * tm, wait, 0, unroll=8)
    w = w_ref[...]
    o_ref[...] = x_ref[...] + w[:, 0:1] * abuf[0:tm, :] + w[:, 1:2] * abuf[tm:2 * tm, :]


def moe_combine(pos, x, wts, osort, *, tm=256):
    m = x.shape[0]
    return pl.pallas_call(
        _moe_combine_kernel,
        grid=(m // tm,),
        in_specs=[pl.BlockSpec((1, 1, 2 * tm), lambda i: (i, 0, 0), memory_space=pltpu.SMEM),
                  pl.BlockSpec((tm, D_MODEL), lambda i: (i, 0)),
                  pl.BlockSpec((tm, LANES), lambda i: (i, 0)),
                  pl.BlockSpec(memory_space=pl.ANY)],
        out_specs=pl.BlockSpec((tm, D_MODEL), lambda i: (i, 0)),
        out_shape=jax.ShapeDtypeStruct((m, D_MODEL), F32),
        scratch_shapes=[pltpu.VMEM((2 * tm, D_MODEL), F32), pltpu.SemaphoreType.DMA],
        compiler_params=_cparams(("arbitrary",)),
        name="moe_combine",
    )(pos, x, wts, osort)


def _dispatch_plan(ids, tm):
    m = ids.shape[0]
    nt = (2 * m) // tm + N_EXPERTS
    flat = ids.reshape(-1)
    onehot = (flat[:, None] == jnp.arange(N_EXPERTS, dtype=jnp.int32)[None, :]).astype(jnp.int32)
    csum = jnp.cumsum(onehot, axis=0)
    rank = jnp.sum((csum - onehot) * onehot, axis=1)
    counts = csum[-1]
    padded = ((counts + tm - 1) // tm) * tm
    ends = jnp.cumsum(padded)
    pos = (ends - padded)[flat] + rank
    src = jnp.zeros((nt * tm,), jnp.int32).at[pos].set(jnp.arange(2 * m, dtype=jnp.int32) // 2)
    starts = jnp.arange(nt, dtype=jnp.int32) * tm
    tile_expert = jnp.minimum(jnp.sum((starts[:, None] >= ends[None, :]).astype(jnp.int32), axis=1),
                              N_EXPERTS - 1)
    real_end = (ends - padded + counts)[tile_expert]
    tile_rows = jnp.where(starts < ends[-1], jnp.clip(real_end - starts, 0, tm), 0).astype(jnp.int32)
    last_valid = jnp.maximum(jnp.sum((tile_rows != 0).astype(jnp.int32)) - 1, 0)
    tile_expert = jnp.where(tile_rows != 0, tile_expert, tile_expert[last_valid])
    return tile_expert, tile_rows, src.reshape(nt, 1, tm), pos.reshape(m, 2)


def _combine_pos(pos, tm):
    m = pos.shape[0]
    return jnp.transpose(pos.reshape(m // tm, tm, 2), (0, 2, 1)).reshape(m // tm, 1, 2 * tm)


def _t5_bucket_np(rel):
    nb = N_BUCKETS // 2
    max_exact = nb // 2
    ret = np.where(rel > 0, nb, 0)
    n = np.abs(rel)
    nf = np.maximum(n, 1).astype(np.float32)
    large = max_exact + (np.log(nf / np.float32(max_exact)) / np.float32(math.log(MAX_DISTANCE / max_exact))
                         * np.float32(nb - max_exact)).astype(np.int32)
    large = np.minimum(large, nb - 1)
    return (ret + np.where(n < max_exact, n, large)).astype(np.int32)


def _bias_tensor(rel_table, n_q, valid):
    rel = np.arange(KEY_TILE, dtype=np.int32)[None, :] - WINDOW - np.arange(n_q, dtype=np.int32)[:, None]
    onehot = np.eye(N_BUCKETS, dtype=np.float32)[:, _t5_bucket_np(rel).reshape(-1)]
    bias = jnp.dot(rel_table.T[HEAD_PERM], jnp.asarray(onehot), precision=lax.Precision.HIGHEST)
    return jnp.where(jnp.asarray(valid)[None], bias.reshape(A_HEADS, n_q, KEY_TILE), NEG).astype(F32)


def _stack_heads(x):
    _, r, c = x.shape
    return jnp.transpose(x.reshape(2, 4, 2, r, c), (0, 2, 1, 3, 4)).reshape(2, 2, 4 * r, c)


def _with_sink_column(bias, sinks_perm):
    return _stack_heads(bias.at[:, :, KEY_TILE - 1].set(sinks_perm[:, None]))


def _prompt_valid():
    qc = np.arange(Q_TILE)[:, None] // CHUNK
    kc = np.arange(KEY_TILE)[None, :] // CHUNK
    return (kc >= qc) & (kc <= qc + WINDOW // CHUNK)


def _sample_valid(rows):
    return np.broadcast_to(np.arange(KEY_TILE)[None, :] < WINDOW + rows, (rows, KEY_TILE))


def _half_avg():
    blk = np.kron(np.eye(2), np.ones((A_HEAD_DIM, A_HEAD_DIM))) / A_HEAD_DIM
    return jnp.asarray(blk, BF16)


def _mixer_weights(l, g_mix, w_in, b_gate, q_norm_g, k_norm_g, sinks, w_conv_b, w_conv_c, b_conv_c,
                   ln_c_g, ln_c_b, w_proj_a, w_proj_b, w_proj_c, w_out):
    w = w_in[l]
    wq = w[:, :A_WIDTH].reshape(D_MODEL, A_HEADS, A_HEAD_DIM)[:, HEAD_PERM].reshape(D_MODEL, A_WIDTH)
    w_in_bf = jnp.concatenate([wq, w[:, A_WIDTH:]], axis=1).astype(BF16)
    wpa = w_proj_a[l].reshape(A_HEADS, A_HEAD_DIM, D_MODEL)[HEAD_PERM].reshape(A_WIDTH, D_MODEL)
    return dict(
        g_mix=g_mix[l], w_in=w_in_bf, b_gate=b_gate[l],
        qg2=(jnp.tile(q_norm_g[l], 2) * (A_HEAD_DIM ** -0.5)).reshape(1, LANES), kg2=jnp.tile(k_norm_g[l], 2).reshape(1, LANES),
        sinks=sinks[l][HEAD_PERM],
        w_cb=w_conv_b[l], w_cc=w_conv_c[l], b_cc=b_conv_c[l], ln_g=ln_c_g[l], ln_b=ln_c_b[l],
        wpa=wpa.astype(BF16), wpb=w_proj_b[l].astype(BF16), wpc=w_proj_c[l].astype(BF16),
        wo=w_out[l].astype(BF16))


def _mixer_prompt(x, mw, bias, bd, batch, seq):
    z = norm_matmul(x, mw["g_mix"], mw["w_in"], tm=1024, tn=1024, out_dtype=BF16)
    oa, nk, nv = swa_prompt(z, batch, seq, mw["qg2"], mw["kg2"], bd, bias, mw["sinks"])
    ob, oc, ncb, ncc = conv_prompt(z, batch, seq, mw["w_cb"], mw["w_cc"], mw["b_cc"], mw["ln_g"], mw["ln_b"])
    y = merge(oa, ob, oc, z, mw["b_gate"], mw["wpa"], mw["wpb"], mw["wpc"], mw["wo"], x)
    return y, (nk, nv, ncb[:, 8 - (B_CONV - 1):], ncc[:, HALO - (C_CONV - 1):])


def _mixer_sample(x, mw, bias, bd, batch, rows, cache_k, cache_v, layer, st_b, st_c):
    z = norm_matmul(x, mw["g_mix"], mw["w_in"], tm=x.shape[0], tn=1024, out_dtype=BF16)
    oa, nk, nv = swa_sample(z, batch, rows, cache_k, cache_v, layer, mw["qg2"], mw["kg2"], bd,
                            _with_sink_column(bias, mw["sinks"]))
    stb = jnp.pad(st_b, ((0, 0), (HALO - (B_CONV - 1), 0), (0, 0)))
    stc = jnp.pad(st_c, ((0, 0), (HALO - (C_CONV - 1), 0), (0, 0)))
    ob, oc, ncb, ncc = conv_sample(z, batch, rows, stb, stc, mw["w_cb"], mw["w_cc"], mw["b_cc"],
                                   mw["ln_g"], mw["ln_b"])
    y = merge(oa, ob, oc, z, mw["b_gate"], mw["wpa"], mw["wpb"], mw["wpc"], mw["wo"], x)
    return y, (nk, nv, ncb[:, 8 - (B_CONV - 1):], ncc[:, HALO - (C_CONV - 1):])


def _channel_mixer(yp, ys, l, g_ffn, w_ffn_gu, w_ffn_d, w_router, b_router, w_moe_gu, w_moe_d):
    if l % 2 == 0:
        return (ffn(yp, g_ffn[l], w_ffn_gu[l // 2], w_ffn_d[l // 2]),
                ffn(ys, g_ffn[l], w_ffn_gu[l // 2], w_ffn_d[l // 2]))
    i = l // 2
    w_pad = jnp.pad(w_router[i], ((0, 0), (0, LANES - N_EXPERTS)))
    b_pad = jnp.pad(b_router[i], (0, LANES - N_EXPERTS)).reshape(1, LANES)
    wts_p, ids_p, hnp_p = router(yp, g_ffn[l], w_pad, b_pad)
    wts_s, ids_s, hnp_s = router(ys, g_ffn[l], w_pad, b_pad)
    ids = jnp.concatenate([ids_p[:, :2], ids_s[:, :2]], axis=0)
    hnp = jnp.concatenate([hnp_p, hnp_s], axis=0)
    tile_expert, tile_valid, src, pos = _dispatch_plan(ids, MOE_TM)
    osort = moe_ffn(tile_expert, tile_valid, src, hnp, w_moe_gu[i], w_moe_d[i])
    mp = yp.shape[0]
    tm_p, tm_s = 512, min(512, ys.shape[0])
    return (moe_combine(_combine_pos(pos[:mp], tm_p), yp, wts_p, osort, tm=tm_p),
            moe_combine(_combine_pos(pos[mp:], tm_s), ys, wts_s, osort, tm=tm_s))


def kernel(x_prompt, x_sample, mem_prompt, cache_mem_k, cache_mem_v, cache_swa_k, cache_swa_v, state_conv_b, state_conv_c, rel_table, g_mix, w_in, b_gate, q_norm_g, k_norm_g, sinks, w_conv_b, w_conv_c, b_conv_c, ln_c_g, ln_c_b, w_proj_a, w_proj_b, w_proj_c, w_out, g_xattn, g_mem, w_xq, w_xkv, xq_norm_g, xk_norm_g, w_xo, g_ffn, w_ffn_gu, w_ffn_d, w_router, b_router, w_moe_gu, w_moe_d):
    batch, seq, d = x_prompt.shape
    dec_batch, dec_seq, _ = x_sample.shape
    depth = g_mix.shape[0]
    yp = x_prompt.reshape(batch * seq, d)
    ys = x_sample.reshape(dec_batch * dec_seq, d)
    mem = mem_prompt.reshape(batch * N_MEM, d)
    bd = _half_avg()
    first = np.arange(KEY_TILE)[None, :] >= Q_TILE
    bias_p = jnp.stack([_stack_heads(_bias_tensor(rel_table, Q_TILE, _prompt_valid() & first)),
                        _stack_heads(_bias_tensor(rel_table, Q_TILE, _prompt_valid()))])
    bias_s = _bias_tensor(rel_table, dec_seq, _sample_valid(dec_seq))
    outs = [[] for _ in range(10)]
    ck = cache_swa_k.reshape(depth, dec_batch, WINDOW, A_KV_WIDTH)
    cv = cache_swa_v.reshape(depth, dec_batch, WINDOW, A_KV_WIDTH)
    cmk = cache_mem_k.reshape(depth, dec_batch, N_MEM, X_WIDTH)
    cmv = cache_mem_v.reshape(depth, dec_batch, N_MEM, X_WIDTH)
    for l in range(depth):
        mw = _mixer_weights(l, g_mix, w_in, b_gate, q_norm_g, k_norm_g, sinks, w_conv_b, w_conv_c,
                            b_conv_c, ln_c_g, ln_c_b, w_proj_a, w_proj_b, w_proj_c, w_out)
        wq = w_xq[l].astype(BF16)
        wo = w_xo[l].astype(BF16)
        ffn_args = (g_ffn, w_ffn_gu, w_ffn_d, w_router, b_router, w_moe_gu, w_moe_d)
        yp, (nk, nv, ncb, ncc) = _mixer_prompt(yp, mw, bias_p, bd, batch, seq)
        mk, mv = mem_kv(mem, g_mem[l], w_xkv[l].astype(BF16), xk_norm_g[l])
        mk3 = mk.reshape(batch, N_MEM, X_WIDTH)
        mv3 = mv.reshape(batch, N_MEM, X_WIDTH)
        yp = xattn(yp, g_xattn[l], wq, xq_norm_g[l], mk3[None], mv3[None], wo, layer=0, nb=1, rpb=512,
                   tiles_per_mem=seq // 512)
        for lst, v in zip(outs[:6], (mk3.reshape(batch, N_MEM, X_HEADS, X_HEAD_DIM),
                                     mv3.reshape(batch, N_MEM, X_HEADS, X_HEAD_DIM),
                                     nk.reshape(batch, WINDOW, A_KV_HEADS, A_HEAD_DIM),
                                     nv.reshape(batch, WINDOW, A_KV_HEADS, A_HEAD_DIM), ncb, ncc)):
            lst.append(v)
        ys, (nk, nv, ncb, ncc) = _mixer_sample(ys, mw, bias_s, bd, dec_batch, dec_seq, ck, cv, l,
                                               state_conv_b[l], state_conv_c[l])
        ys = xattn(ys, g_xattn[l], wq, xq_norm_g[l], cmk, cmv, wo, layer=l, nb=8, rpb=dec_seq,
                   tiles_per_mem=1)
        yp, ys = _channel_mixer(yp, ys, l, *ffn_args)
        for lst, v in zip(outs[6:], (nk.reshape(dec_batch, dec_seq, A_KV_HEADS, A_HEAD_DIM),
                                     nv.reshape(dec_batch, dec_seq, A_KV_HEADS, A_HEAD_DIM), ncb, ncc)):
            lst.append(v)
    return (yp.reshape(batch, seq, d), ys.reshape(dec_batch, dec_seq, d)) + tuple(jnp.stack(o) for o in outs)
```

```python
import functools
import math

import numpy as np
import jax
import jax.numpy as jnp
from jax import lax
from jax.experimental import pallas as pl
from jax.experimental.pallas import tpu as pltpu

F32 = jnp.float32
BF16 = jnp.bfloat16

D_MODEL = 2048
CHUNK = 64
A_HEADS = 16
A_KV_HEADS = 4
A_HEAD_DIM = 64
A_WIDTH = A_HEADS * A_HEAD_DIM
A_KV_WIDTH = A_KV_HEADS * A_HEAD_DIM
WINDOW = 128
N_BUCKETS = 32
MAX_DISTANCE = 128
B_WIDTH = 512
B_CONV = 3
C_WIDTH = 512
C_CONV = 31
N_MEM = 256
X_HEADS = 4
X_HEAD_DIM = 128
X_WIDTH = X_HEADS * X_HEAD_DIM
D_FF = 5632
N_EXPERTS = 8
EPS = 1e-6

LANES = 128
KEY_TILE = 256
Q_TILE = 128
HALO = 32
NEG = -1e30
VMEM_LIMIT = 56 * 1024 * 1024

COL_Q = 0
COL_K, COL_V = 4, 5
COL_GB, COL_GC, COL_HB, COL_GA, COL_GG = 3, 4, 5, 6, 7
COL_GATE0 = 2
IN_COLS = 4096 + 3 * D_MODEL

HEAD_PERM = np.array([8 * n + (p % 2) * 4 + p // 2 for n in range(2) for p in range(8)])


def _cparams(sem):
    return pltpu.CompilerParams(dimension_semantics=sem, vmem_limit_bytes=VMEM_LIMIT)


def _rms(x, g):
    ms = jnp.mean(x * x, axis=-1, keepdims=True)
    return x * lax.rsqrt(ms + EPS) * g


def _resident(shape):
    nd = len(shape)
    return pl.BlockSpec(shape, lambda *_: (0,) * nd, pipeline_mode=pl.Buffered(1))


def _norm_matmul_kernel(x_ref, g_ref, w_ref, o_ref, hn_ref):
    @pl.when(pl.program_id(1) == 0)
    def _():
        hn_ref[...] = _rms(x_ref[...], g_ref[...]).astype(BF16)

    o_ref[...] = jnp.dot(hn_ref[...], w_ref[...], preferred_element_type=F32).astype(o_ref.dtype)


def norm_matmul(x, g, w, *, tm, tn, out_dtype):
    m, k = x.shape
    n = w.shape[1]
    return pl.pallas_call(
        _norm_matmul_kernel,
        grid=(m // tm, n // tn),
        in_specs=[pl.BlockSpec((tm, k), lambda i, j: (i, 0)),
                  pl.BlockSpec((1, k), lambda i, j: (0, 0)),
                  pl.BlockSpec((k, tn), lambda i, j: (0, j))],
        out_specs=pl.BlockSpec((tm, tn), lambda i, j: (i, j)),
        out_shape=jax.ShapeDtypeStruct((m, n), out_dtype),
        scratch_shapes=[pltpu.VMEM((tm, k), BF16)],
        compiler_params=_cparams(("parallel", "arbitrary")),
        name="norm_matmul",
    )(x, g.reshape(1, k), w)


def _mem_kv_kernel(x_ref, g_ref, w_ref, kg_ref, k_ref, v_ref):
    hn = _rms(x_ref[...], g_ref[...]).astype(BF16)
    kv = jnp.dot(hn, w_ref[...], preferred_element_type=F32)
    for h in range(X_HEADS):
        sl = slice(h * X_HEAD_DIM, (h + 1) * X_HEAD_DIM)
        k_ref[:, sl] = _rms(kv[:, sl], kg_ref[...])
    v_ref[...] = kv[:, X_WIDTH:]


def mem_kv(mem, g, w_bf, kg):
    m, k = mem.shape
    tm = 256
    return pl.pallas_call(
        _mem_kv_kernel,
        grid=(m // tm,),
        in_specs=[pl.BlockSpec((tm, k), lambda i: (i, 0)),
                  pl.BlockSpec((1, k), lambda i: (0, 0)),
                  _resident((k, 2 * X_WIDTH)),
                  pl.BlockSpec((1, X_HEAD_DIM), lambda i: (0, 0))],
        out_specs=[pl.BlockSpec((tm, X_WIDTH), lambda i: (i, 0)),
                   pl.BlockSpec((tm, X_WIDTH), lambda i: (i, 0))],
        out_shape=[jax.ShapeDtypeStruct((m, X_WIDTH), F32)] * 2,
        compiler_params=_cparams(("parallel",)),
        name="mem_kv",
    )(mem, g.reshape(1, k), w_bf, kg.reshape(1, X_HEAD_DIM))


def _half_norm(x, g, bd):
    x2 = x * x
    hi = x2.astype(BF16)
    lo = (x2 - hi.astype(F32)).astype(BF16)
    ms = (jnp.dot(hi, bd, preferred_element_type=F32)
          + jnp.dot(lo, bd, preferred_element_type=F32))
    return x * lax.rsqrt(ms + EPS) * g


def _swa_heads(q_ref, qg, bd, k2, v2, bias_ref, sink_ref, o_ref, *, stack):
    lo_lane = lax.broadcasted_iota(jnp.int32, (1, LANES), 1) < A_HEAD_DIM
    rows = q_ref.shape[0]
    cols = A_WIDTH // LANES // 2
    for n in range(2):
        k_half = (jnp.where(lo_lane, k2[n], 0.0).astype(BF16), jnp.where(lo_lane, 0.0, k2[n]).astype(BF16))
        for g in range(0, cols, stack):
            qs = [_half_norm(q_ref[:, c * LANES:(c + 1) * LANES].astype(F32), qg, bd)
                  for c in range(cols * n + g, cols * n + g + stack)]
            qn = (qs[0] if stack == 1 else jnp.concatenate(qs, axis=0)).astype(BF16)
            at = slice(g * rows, (g + stack) * rows)
            halves = []
            for half in range(2):
                s = lax.dot_general(qn, k_half[half], (((1,), (1,)), ((), ())), preferred_element_type=F32)
                s = s + bias_ref[n, half, at, :]
                if sink_ref is None:
                    m = jnp.max(s, axis=-1, keepdims=True)
                    p = jnp.exp(s - m)
                    den = jnp.sum(p, axis=-1, keepdims=True)
                else:
                    sink = sink_ref[2 * (cols * n + g) + half]
                    m = jnp.maximum(jnp.max(s, axis=-1, keepdims=True), sink)
                    p = jnp.exp(s - m)
                    den = jnp.sum(p, axis=-1, keepdims=True) + jnp.exp(sink - m)
                o = jnp.dot(p.astype(BF16), v2[n], preferred_element_type=F32)
                halves.append(o / den)
            o = jnp.where(lo_lane, halves[0], halves[1]).astype(o_ref.dtype)
            for i in range(stack):
                c = cols * n + g + i
                o_ref[:, c * LANES:(c + 1) * LANES] = o[i * rows:(i + 1) * rows]


def _swa_prompt_kernel(q_ref, kc_ref, kp_ref, vc_ref, vp_ref, qg_ref, kg_ref, bd_ref,
                       bias_ref, sink_ref, o_ref, nk_ref, nv_ref):
    bd = bd_ref[...]
    k2, v2 = [], []
    for n in range(2):
        sl = slice(n * LANES, (n + 1) * LANES)
        kcat = jnp.concatenate([kp_ref[:, sl], kc_ref[:, sl]], axis=0).astype(F32)
        kn = _half_norm(kcat, kg_ref[...], bd)
        nk_ref[0, :, sl] = kn[Q_TILE:]
        k2.append(kn)
        v2.append(jnp.concatenate([vp_ref[:, sl], vc_ref[:, sl]], axis=0))
    nv_ref[0] = vc_ref[...].astype(F32)
    _swa_heads(q_ref, qg_ref[...], bd, k2, v2, bias_ref, sink_ref, o_ref, stack=1)


def swa_prompt(z, batch, seq, qg2, kg2, bd, bias, sinks):
    nt = seq // Q_TILE
    row = lambda b, t: b * nt + t
    prev = lambda b, t: jnp.maximum(b * nt + t - 1, 0)
    return pl.pallas_call(
        _swa_prompt_kernel,
        grid=(batch, nt),
        in_specs=[pl.BlockSpec((Q_TILE, A_WIDTH), lambda b, t: (row(b, t), COL_Q)),
                  pl.BlockSpec((Q_TILE, A_KV_WIDTH), lambda b, t: (row(b, t), COL_K)),
                  pl.BlockSpec((Q_TILE, A_KV_WIDTH), lambda b, t: (prev(b, t), COL_K)),
                  pl.BlockSpec((Q_TILE, A_KV_WIDTH), lambda b, t: (row(b, t), COL_V)),
                  pl.BlockSpec((Q_TILE, A_KV_WIDTH), lambda b, t: (prev(b, t), COL_V)),
                  pl.BlockSpec((1, LANES), lambda b, t: (0, 0)),
                  pl.BlockSpec((1, LANES), lambda b, t: (0, 0)),
                  pl.BlockSpec((LANES, LANES), lambda b, t: (0, 0)),
                  pl.BlockSpec((None, 2, 2, 4 * Q_TILE, KEY_TILE),
                               lambda b, t: (jnp.minimum(t, 1), 0, 0, 0, 0)),
                  pl.BlockSpec(memory_space=pltpu.SMEM)],
        out_specs=[pl.BlockSpec((Q_TILE, A_WIDTH), lambda b, t: (row(b, t), 0)),
                   pl.BlockSpec((1, WINDOW, A_KV_WIDTH), lambda b, t: (b, 0, 0)),
                   pl.BlockSpec((1, WINDOW, A_KV_WIDTH), lambda b, t: (b, 0, 0))],
        out_shape=[jax.ShapeDtypeStruct((batch * seq, A_WIDTH), BF16),
                   jax.ShapeDtypeStruct((batch, WINDOW, A_KV_WIDTH), F32),
                   jax.ShapeDtypeStruct((batch, WINDOW, A_KV_WIDTH), F32)],
        compiler_params=_cparams(("parallel", "arbitrary")),
        name="swa_prompt",
    )(z, z, z, z, z, qg2, kg2, bd, bias, sinks)


def _swa_sample_kernel(q_ref, kn_ref, vn_ref, ck_ref, cv_ref, qg_ref, kg_ref, bd_ref,
                       bias_ref, o_ref, nk_ref, nv_ref):
    bd = bd_ref[...]
    rows = q_ref.shape[0]
    pad = KEY_TILE - WINDOW - rows
    k2, v2 = [], []
    for n in range(2):
        sl = slice(n * LANES, (n + 1) * LANES)
        kn = _half_norm(kn_ref[:, sl].astype(F32), kg_ref[...], bd)
        nk_ref[0, :, sl] = kn
        k2.append(jnp.concatenate([ck_ref[0, :, sl], kn, jnp.zeros((pad, LANES), F32)], axis=0))
        v2.append(jnp.concatenate([cv_ref[0, :, sl].astype(BF16), vn_ref[:, sl],
                                   jnp.zeros((pad, LANES), BF16)], axis=0))
    nv_ref[0] = vn_ref[...].astype(F32)
    _swa_heads(q_ref, qg_ref[...], bd, k2, v2, bias_ref, None, o_ref, stack=4)


def swa_sample(z, batch, rows, cache_k, cache_v, layer, qg2, kg2, bd, bias):
    return pl.pallas_call(
        _swa_sample_kernel,
        grid=(batch,),
        in_specs=[pl.BlockSpec((rows, A_WIDTH), lambda b: (b, COL_Q)),
                  pl.BlockSpec((rows, A_KV_WIDTH), lambda b: (b, COL_K)),
                  pl.BlockSpec((rows, A_KV_WIDTH), lambda b: (b, COL_V)),
                  pl.BlockSpec((None, 1, WINDOW, A_KV_WIDTH), lambda b: (layer, b, 0, 0)),
                  pl.BlockSpec((None, 1, WINDOW, A_KV_WIDTH), lambda b: (layer, b, 0, 0)),
                  pl.BlockSpec((1, LANES), lambda b: (0, 0)),
                  pl.BlockSpec((1, LANES), lambda b: (0, 0)),
                  pl.BlockSpec((LANES, LANES), lambda b: (0, 0)),
                  _resident((2, 2, 4 * rows, KEY_TILE))],
        out_specs=[pl.BlockSpec((rows, A_WIDTH), lambda b: (b, 0)),
                   pl.BlockSpec((1, rows, A_KV_WIDTH), lambda b: (b, 0, 0)),
                   pl.BlockSpec((1, rows, A_KV_WIDTH), lambda b: (b, 0, 0))],
        out_shape=[jax.ShapeDtypeStruct((batch * rows, A_WIDTH), BF16),
                   jax.ShapeDtypeStruct((batch, rows, A_KV_WIDTH), F32),
                   jax.ShapeDtypeStruct((batch, rows, A_KV_WIDTH), F32)],
        compiler_params=_cparams(("parallel",)),
        name="swa_sample",
    )(z, z, z, cache_k, cache_v, qg2, kg2, bd, bias)


def _conv_body(gb_ref, ub_main, uc_main, ub_halo, uc_halo, wb_ref, wc_ref, bc_ref, lg_ref, lb_ref,
               ob_ref, oc_ref, nb_ref, nc_ref, sb_ref, sc_ref, ph_ref, write_state):
    rows = ub_main.shape[0]
    sb_ref[0:HALO] = ub_halo
    sb_ref[HALO:HALO + rows] = ub_main
    sc_ref[0:HALO] = uc_halo
    sc_ref[HALO:HALO + rows] = uc_main
    span = rows + HALO
    sc_ref[span:span + 8] = jnp.zeros((8, C_WIDTH), F32)
    for b in range(1, 8):
        ph_ref[b - 1] = sc_ref[pl.ds(b, span), :]
    sub = min(rows, 32)
    for r0 in range(0, rows, sub):
        yb = jnp.zeros((sub, B_WIDTH), F32)
        for k in range(B_CONV):
            yb = yb + wb_ref[k:k + 1, :] * sb_ref[pl.ds(r0 + HALO - (B_CONV - 1) + k, sub), :]
        ob_ref[r0:r0 + sub, :] = (gb_ref[r0:r0 + sub, :].astype(F32) * yb).astype(ob_ref.dtype)
        yc = jnp.zeros((sub, C_WIDTH), F32)
        for k in range(C_CONV):
            shift, phase = divmod(HALO - (C_CONV - 1) + k, 8)
            at = pl.ds(r0 + 8 * shift, sub)
            taps = sc_ref[at, :] if phase == 0 else ph_ref[phase - 1, at, :]
            yc = yc + wc_ref[k:k + 1, :] * taps
        yc = yc + bc_ref[...]
        mu = jnp.mean(yc, axis=-1, keepdims=True)
        xc = yc - mu
        y = xc * lax.rsqrt(jnp.mean(xc * xc, axis=-1, keepdims=True) + EPS)
        y = y * lg_ref[...] + lb_ref[...]
        oc_ref[r0:r0 + sub, :] = (y * jax.nn.sigmoid(y)).astype(oc_ref.dtype)

    def _state():
        nb_ref[0] = sb_ref[rows + HALO - 8:rows + HALO]
        nc_ref[0] = sc_ref[rows:rows + HALO]

    write_state(_state)


def _conv_prompt_kernel(gb_ref, gc_ref, hb_ref, ga_ref, gg_ref, gch_ref, hbh_ref, gah_ref, ggh_ref,
                        wb_ref, wc_ref, bc_ref, lg_ref, lb_ref,
                        ob_ref, oc_ref, nb_ref, nc_ref, sb_ref, sc_ref, ph_ref):
    t = pl.program_id(1)
    hist = (t > 0).astype(F32)
    ub_main = gc_ref[...].astype(F32) * hb_ref[...].astype(F32)
    uc_main = ga_ref[...].astype(F32) * jax.nn.sigmoid(gg_ref[...].astype(F32))
    ub_halo = gch_ref[...].astype(F32) * hbh_ref[...].astype(F32) * hist
    uc_halo = gah_ref[...].astype(F32) * jax.nn.sigmoid(ggh_ref[...].astype(F32)) * hist
    last = pl.num_programs(1) - 1
    _conv_body(gb_ref, ub_main, uc_main, ub_halo, uc_halo, wb_ref, wc_ref, bc_ref, lg_ref, lb_ref,
               ob_ref, oc_ref, nb_ref, nc_ref, sb_ref, sc_ref, ph_ref,
               lambda f: pl.when(t == last)(f))


def conv_prompt(z, batch, seq, wb, wc, bc, lg, lb, *, tr=128):
    nt = seq // tr
    hp = tr // HALO
    main = lambda c: pl.BlockSpec((tr, B_WIDTH), lambda b, t: (b * nt + t, c))
    halo = lambda c: pl.BlockSpec((HALO, B_WIDTH), lambda b, t: (jnp.maximum((b * nt + t) * hp - 1, 0), c))
    vec = lambda r: pl.BlockSpec((r, B_WIDTH), lambda b, t: (0, 0))
    return pl.pallas_call(
        _conv_prompt_kernel,
        grid=(batch, nt),
        in_specs=[main(COL_GB), main(COL_GC), main(COL_HB), main(COL_GA), main(COL_GG),
                  halo(COL_GC), halo(COL_HB), halo(COL_GA), halo(COL_GG),
                  vec(B_CONV), vec(C_CONV), vec(1), vec(1), vec(1)],
        out_specs=[pl.BlockSpec((tr, B_WIDTH), lambda b, t: (b * nt + t, 0)),
                   pl.BlockSpec((tr, C_WIDTH), lambda b, t: (b * nt + t, 0)),
                   pl.BlockSpec((1, 8, B_WIDTH), lambda b, t: (b, 0, 0)),
                   pl.BlockSpec((1, HALO, C_WIDTH), lambda b, t: (b, 0, 0))],
        out_shape=[jax.ShapeDtypeStruct((batch * seq, B_WIDTH), BF16),
                   jax.ShapeDtypeStruct((batch * seq, C_WIDTH), BF16),
                   jax.ShapeDtypeStruct((batch, 8, B_WIDTH), F32),
                   jax.ShapeDtypeStruct((batch, HALO, C_WIDTH), F32)],
        scratch_shapes=[pltpu.VMEM((tr + HALO, B_WIDTH), F32), pltpu.VMEM((tr + HALO + 8, C_WIDTH), F32),
                        pltpu.VMEM((7, tr + HALO, C_WIDTH), F32)],
        compiler_params=_cparams(("parallel", "arbitrary")),
        name="conv_prompt",
    )(z, z, z, z, z, z, z, z, z, wb, wc, bc.reshape(1, -1), lg.reshape(1, -1), lb.reshape(1, -1))


def _conv_sample_kernel(gb_ref, gc_ref, hb_ref, ga_ref, gg_ref, stb_ref, stc_ref,
                        wb_ref, wc_ref, bc_ref, lg_ref, lb_ref,
                        ob_ref, oc_ref, nb_ref, nc_ref, sb_ref, sc_ref, ph_ref):
    ub_main = gc_ref[...].astype(F32) * hb_ref[...].astype(F32)
    uc_main = ga_ref[...].astype(F32) * jax.nn.sigmoid(gg_ref[...].astype(F32))
    _conv_body(gb_ref, ub_main, uc_main, stb_ref[0], stc_ref[0], wb_ref, wc_ref, bc_ref, lg_ref, lb_ref,
               ob_ref, oc_ref, nb_ref, nc_ref, sb_ref, sc_ref, ph_ref, lambda f: f())


def conv_sample(z, batch, rows, stb, stc, wb, wc, bc, lg, lb):
    main = lambda c: pl.BlockSpec((rows, B_WIDTH), lambda b: (b, c))
    vec = lambda r: pl.BlockSpec((r, B_WIDTH), lambda b: (0, 0))
    return pl.pallas_call(
        _conv_sample_kernel,
        grid=(batch,),
        in_specs=[main(COL_GB), main(COL_GC), main(COL_HB), main(COL_GA), main(COL_GG),
                  pl.BlockSpec((1, HALO, B_WIDTH), lambda b: (b, 0, 0)),
                  pl.BlockSpec((1, HALO, C_WIDTH), lambda b: (b, 0, 0)),
                  vec(B_CONV), vec(C_CONV), vec(1), vec(1), vec(1)],
        out_specs=[pl.BlockSpec((rows, B_WIDTH), lambda b: (b, 0)),
                   pl.BlockSpec((rows, C_WIDTH), lambda b: (b, 0)),
                   pl.BlockSpec((1, 8, B_WIDTH), lambda b: (b, 0, 0)),
                   pl.BlockSpec((1, HALO, C_WIDTH), lambda b: (b, 0, 0))],
        out_shape=[jax.ShapeDtypeStruct((batch * rows, B_WIDTH), BF16),
                   jax.ShapeDtypeStruct((batch * rows, C_WIDTH), BF16),
                   jax.ShapeDtypeStruct((batch, 8, B_WIDTH), F32),
                   jax.ShapeDtypeStruct((batch, HALO, C_WIDTH), F32)],
        scratch_shapes=[pltpu.VMEM((rows + HALO, B_WIDTH), F32), pltpu.VMEM((rows + HALO + 8, C_WIDTH), F32),
                        pltpu.VMEM((7, rows + HALO, C_WIDTH), F32)],
        compiler_params=_cparams(("parallel",)),
        name="conv_sample",
    )(z, z, z, z, z, stb, stc, wb, wc, bc.reshape(1, -1), lg.reshape(1, -1), lb.reshape(1, -1))


def _merge_kernel(oa_ref, ob_ref, oc_ref, l0_ref, l1_ref, l2_ref, bg_ref,
                  wpa_ref, wpb_ref, wpc_ref, wo_ref, x_ref, o_ref):
    def gated(l_ref, i, o_r, w_r):
        gate = jax.nn.sigmoid(l_ref[...].astype(F32) + bg_ref[i:i + 1, :])
        return gate * jnp.dot(o_r[...], w_r[...], preferred_element_type=F32)

    merged = gated(l0_ref, 0, oa_ref, wpa_ref)
    merged = merged + gated(l1_ref, 1, ob_ref, wpb_ref)
    merged = merged + gated(l2_ref, 2, oc_ref, wpc_ref)
    o_ref[...] = x_ref[...] + jnp.dot(merged.astype(BF16), wo_ref[...], preferred_element_type=F32)


def merge(oa, ob, oc, z, bg, wpa, wpb, wpc, wo, x, *, tm=256):
    m = x.shape[0]
    tm = min(tm, m)
    rows = lambda w: pl.BlockSpec((tm, w), lambda i: (i, 0))
    gate = lambda c: pl.BlockSpec((tm, D_MODEL), lambda i: (i, COL_GATE0 + c))
    return pl.pallas_call(
        _merge_kernel,
        grid=(m // tm,),
        in_specs=[rows(A_WIDTH), rows(B_WIDTH), rows(C_WIDTH), gate(0), gate(1), gate(2),
                  pl.BlockSpec((3, D_MODEL), lambda i: (0, 0)),
                  _resident((A_WIDTH, D_MODEL)), _resident((B_WIDTH, D_MODEL)),
                  _resident((C_WIDTH, D_MODEL)), _resident((D_MODEL, D_MODEL)),
                  rows(D_MODEL)],
        out_specs=rows(D_MODEL),
        out_shape=jax.ShapeDtypeStruct((m, D_MODEL), F32),
        compiler_params=_cparams(("parallel",)),
        name="merge",
    )(oa, ob, oc, z, z, z, bg.reshape(3, D_MODEL), wpa, wpb, wpc, wo, x)


def _xattn_kernel(y_ref, g_ref, wq_ref, qg_ref, mk_ref, mv_ref, wo_ref, o_ref, *, nb, rpb):
    y = y_ref[...]
    hn = _rms(y, g_ref[...]).astype(BF16)
    q = jnp.dot(hn, wq_ref[...], preferred_element_type=F32)
    heads = []
    for h in range(X_HEADS):
        sl = slice(h * X_HEAD_DIM, (h + 1) * X_HEAD_DIM)
        qh = _rms(q[:, sl], qg_ref[...])
        per_batch = []
        for b in range(nb):
            qb = qh[b * rpb:(b + 1) * rpb].astype(BF16)
            kh = mk_ref[b, :, sl].astype(BF16)
            vh = mv_ref[b, :, sl].astype(BF16)
            s = lax.dot_general(qb, kh, (((1,), (1,)), ((), ())),
                                preferred_element_type=F32) * (X_HEAD_DIM ** -0.5)
            m = jnp.max(s, axis=-1, keepdims=True)
            p = jnp.exp(s - m)
            den = jnp.sum(p, axis=-1, keepdims=True)
            per_batch.append(jnp.dot(p.astype(BF16), vh, preferred_element_type=F32) / den)
        heads.append(per_batch[0] if nb == 1 else jnp.concatenate(per_batch, axis=0))
    o = jnp.concatenate(heads, axis=1).astype(BF16)
    o_ref[...] = y + jnp.dot(o, wo_ref[...], preferred_element_type=F32)


def xattn(y, g, wq, qg, mk, mv, wo, *, layer, nb, rpb, tiles_per_mem):
    m = y.shape[0]
    tm = nb * rpb
    mem_idx = ((lambda i: (layer, i // tiles_per_mem, 0, 0)) if nb == 1
               else (lambda i: (layer, i, 0, 0)))
    return pl.pallas_call(
        functools.partial(_xattn_kernel, nb=nb, rpb=rpb),
        grid=(m // tm,),
        in_specs=[pl.BlockSpec((tm, D_MODEL), lambda i: (i, 0)),
                  pl.BlockSpec((1, D_MODEL), lambda i: (0, 0)),
                  _resident((D_MODEL, X_WIDTH)),
                  pl.BlockSpec((1, X_HEAD_DIM), lambda i: (0, 0)),
                  pl.BlockSpec((None, nb, N_MEM, X_WIDTH), mem_idx),
                  pl.BlockSpec((None, nb, N_MEM, X_WIDTH), mem_idx),
                  _resident((X_WIDTH, D_MODEL))],
        out_specs=pl.BlockSpec((tm, D_MODEL), lambda i: (i, 0)),
        out_shape=jax.ShapeDtypeStruct((m, D_MODEL), F32),
        compiler_params=_cparams(("parallel",)),
        name="xattn",
    )(y, g.reshape(1, -1), wq, qg.reshape(1, -1), mk, mv, wo)


def _ffn_kernel(x_ref, g_ref, wg_ref, wu_ref, wd_ref, o_ref, hn_ref):
    @pl.when(pl.program_id(1) == 0)
    def _():
        x = x_ref[...]
        hn_ref[...] = _rms(x, g_ref[...]).astype(BF16)
        o_ref[...] = x

    hn = hn_ref[...]
    g = jnp.dot(hn, wg_ref[...].astype(BF16), preferred_element_type=F32)
    u = jnp.dot(hn, wu_ref[...].astype(BF16), preferred_element_type=F32)
    a = (g * jax.nn.sigmoid(g) * u).astype(BF16)
    o_ref[...] += jnp.dot(a, wd_ref[...].astype(BF16), preferred_element_type=F32)


def ffn(x, g, w_gu, w_d, *, tm=1024, tf=512):
    m = x.shape[0]
    tm = min(tm, m)
    nf = D_FF // tf
    return pl.pallas_call(
        _ffn_kernel,
        grid=(m // tm, nf),
        in_specs=[pl.BlockSpec((tm, D_MODEL), lambda i, f: (i, 0), pipeline_mode=pl.Buffered(1)),
                  pl.BlockSpec((1, D_MODEL), lambda i, f: (0, 0)),
                  pl.BlockSpec((D_MODEL, tf), lambda i, f: (0, f)),
                  pl.BlockSpec((D_MODEL, tf), lambda i, f: (0, nf + f)),
                  pl.BlockSpec((tf, D_MODEL), lambda i, f: (f, 0))],
        out_specs=pl.BlockSpec((tm, D_MODEL), lambda i, f: (i, 0), pipeline_mode=pl.Buffered(1)),
        out_shape=jax.ShapeDtypeStruct((m, D_MODEL), F32),
        scratch_shapes=[pltpu.VMEM((tm, D_MODEL), BF16)],
        compiler_params=_cparams(("parallel", "arbitrary")),
        name="ffn",
    )(x, g.reshape(1, -1), w_gu, w_gu, w_d)


def _split3(x):
    hi = x.astype(BF16)
    lo = (x - hi.astype(F32)).astype(BF16)
    return hi, lo


def _router_kernel(x_ref, g_ref, w_ref, b_ref, tri_ref, base_ref, wts_ref, ids_ref, rank_ref, cnt_ref, run_ref):
    @pl.when(pl.program_id(0) == 0)
    def _():
        run_ref[...] = base_ref[...]

    hn = _rms(x_ref[...], g_ref[...])
    h_hi, h_lo = _split3(hn)
    w_hi, w_lo = _split3(w_ref[...])
    dot = functools.partial(jnp.dot, preferred_element_type=F32)
    logits = dot(h_hi, w_hi) + dot(h_hi, w_lo) + dot(h_lo, w_hi) + b_ref[...]
    lane = lax.broadcasted_iota(jnp.int32, logits.shape, 1).astype(F32)
    logits = jnp.where(lane < N_EXPERTS, logits, -jnp.inf)
    v1 = jnp.max(logits, axis=-1, keepdims=True)
    i1 = jnp.min(jnp.where(logits == v1, lane, float(LANES)), axis=-1, keepdims=True)
    rest = jnp.where(lane == i1, -jnp.inf, logits)
    v2 = jnp.max(rest, axis=-1, keepdims=True)
    i2 = jnp.min(jnp.where(rest == v2, lane, float(LANES)), axis=-1, keepdims=True)
    e2 = jnp.exp(v2 - v1)
    den = 1.0 + e2
    first, second = lane == 0.0, lane == 1.0
    wts_ref[...] = jnp.where(first, 1.0 / den, 0.0) + jnp.where(second, e2 / den, 0.0)
    ids_ref[...] = (jnp.where(first, i1, 0.0) + jnp.where(second, i2, 0.0)).astype(jnp.int32)
    hit1, hit2 = lane == i1, lane == i2
    hits = jnp.where(jnp.logical_or(hit1, hit2), 1.0, 0.0)
    before = dot(tri_ref[...], hits.astype(BF16)) + run_ref[...]
    r1 = jnp.sum(jnp.where(hit1, before, 0.0), axis=-1, keepdims=True)
    r2 = jnp.sum(jnp.where(hit2, before, 0.0), axis=-1, keepdims=True)
    rank_ref[...] = (jnp.where(first, r1, 0.0) + jnp.where(second, r2, 0.0)).astype(jnp.int32)
    run_ref[...] += jnp.sum(hits, axis=0, keepdims=True)
    cnt_ref[...] = run_ref[...]


def router(x, g, w_pad, b_pad, base, *, tm=512):
    m = x.shape[0]
    tm = min(tm, m)
    tri = jnp.asarray(np.tril(np.ones((tm, tm), np.float32), -1), BF16)
    row = lambda w: pl.BlockSpec((tm, w), lambda i: (i, 0))
    fixed = lambda s: pl.BlockSpec(s, lambda i: (0, 0))
    return pl.pallas_call(
        _router_kernel,
        grid=(m // tm,),
        in_specs=[row(D_MODEL), fixed((1, D_MODEL)), fixed((D_MODEL, LANES)), fixed((1, LANES)),
                  fixed((tm, tm)), fixed((1, LANES))],
        out_specs=[row(LANES), row(LANES), row(LANES), fixed((1, LANES))],
        out_shape=[jax.ShapeDtypeStruct((m, LANES), F32),
                   jax.ShapeDtypeStruct((m, LANES), jnp.int32),
                   jax.ShapeDtypeStruct((m, LANES), jnp.int32),
                   jax.ShapeDtypeStruct((1, LANES), F32)],
        scratch_shapes=[pltpu.VMEM((1, LANES), F32)],
        compiler_params=_cparams(("arbitrary",)),
        name="router",
    )(x, g.reshape(1, -1), w_pad, b_pad, tri, base)


MOE_TM = 1024
MOE_TF = 512


def _row_copy(src, row, dst, r, sem):
    return pltpu.make_async_copy(src.at[pl.ds(row, 1)], dst.at[pl.ds(r, 1)], sem)


def _moe_dispatch_kernel(pos_ref, x_ref, g_ref, xs_in, xs_hbm, pk_ref, sem):
    del xs_in
    tm = x_ref.shape[0]
    half = D_MODEL // 2
    hn = _rms(x_ref[...], g_ref[...]).astype(BF16)
    lo_bits = lax.bitcast_convert_type(hn[:, :half].astype(F32), jnp.uint32)
    hi_bits = lax.bitcast_convert_type(hn[:, half:].astype(F32), jnp.uint32)
    pk_ref[...] = (lo_bits >> 16) | (hi_bits & jnp.uint32(0xFFFF0000))

    def issue(r, c):
        _row_copy(pk_ref, r, xs_hbm, pos_ref[0, 0, r], sem).start()
        _row_copy(pk_ref, r, xs_hbm, pos_ref[0, 0, tm + r], sem).start()
        return c

    lax.fori_loop(0, tm, issue, 0, unroll=8)

    def wait(r, c):
        _row_copy(pk_ref, r, xs_hbm, 0, sem).wait()
        _row_copy(pk_ref, r, xs_hbm, 0, sem).wait()
        return c

    lax.fori_loop(0, tm, wait, 0, unroll=8)


def moe_dispatch(pos, x, g, xs, *, tm):
    m = x.shape[0]
    return pl.pallas_call(
        _moe_dispatch_kernel,
        grid=(m // tm,),
        in_specs=[pl.BlockSpec((1, 1, 2 * tm), lambda i: (i, 0, 0), memory_space=pltpu.SMEM),
                  pl.BlockSpec((tm, D_MODEL), lambda i: (i, 0)),
                  pl.BlockSpec((1, D_MODEL), lambda i: (0, 0)),
                  pl.BlockSpec(memory_space=pl.ANY)],
        out_specs=pl.BlockSpec(memory_space=pl.ANY),
        out_shape=jax.ShapeDtypeStruct(xs.shape, xs.dtype),
        scratch_shapes=[pltpu.VMEM((tm, D_MODEL // 2), jnp.uint32), pltpu.SemaphoreType.DMA],
        input_output_aliases={3: 0},
        compiler_params=_cparams(("arbitrary",)),
        name="moe_dispatch",
    )(pos, x, g.reshape(1, -1), xs)


def _moe_ffn_kernel(te_ref, tv_ref, xs_ref, wg_ref, wu_ref, wd_ref, o_ref, hn_ref):
    t = pl.program_id(0)
    f = pl.program_id(1)
    rows = xs_ref.shape[0]
    half = D_MODEL // 2

    @pl.when(f == 0)
    def _():
        o_ref[...] = jnp.zeros_like(o_ref)
        xu = xs_ref[...]
        hn_ref[:, :half] = lax.bitcast_convert_type(xu << 16, F32).astype(BF16)
        hn_ref[:, half:] = lax.bitcast_convert_type(xu & jnp.uint32(0xFFFF0000), F32).astype(BF16)

    def swiglu(r):
        hn = hn_ref[0:r]
        g = jnp.dot(hn, wg_ref[...].astype(BF16), preferred_element_type=F32)
        u = jnp.dot(hn, wu_ref[...].astype(BF16), preferred_element_type=F32)
        a = (g * jax.nn.sigmoid(g) * u).astype(BF16)
        o_ref[0:r] += jnp.dot(a, wd_ref[...].astype(BF16), preferred_element_type=F32)

    nv = tv_ref[t]

    @pl.when(nv > rows // 2)
    def _():
        swiglu(rows)

    @pl.when(jnp.logical_and(nv > rows // 4, nv <= rows // 2))
    def _():
        swiglu(rows // 2)

    @pl.when(jnp.logical_and(nv > 0, nv <= rows // 4))
    def _():
        swiglu(rows // 4)


def moe_ffn(tile_expert, tile_rows, xs, w_gu, w_d):
    tm = MOE_TM
    nt = xs.shape[0] // tm
    nf = D_FF // MOE_TF
    last = nf - 1
    col = lambda f, tv, t: jnp.where(tv[t] != 0, f, last)
    grid_spec = pltpu.PrefetchScalarGridSpec(
        num_scalar_prefetch=2,
        grid=(nt, nf),
        in_specs=[pl.BlockSpec((tm, D_MODEL // 2), lambda t, f, te, tv: (t, 0), pipeline_mode=pl.Buffered(1)),
                  pl.BlockSpec((None, D_MODEL, MOE_TF), lambda t, f, te, tv: (te[t], 0, col(f, tv, t))),
                  pl.BlockSpec((None, D_MODEL, MOE_TF), lambda t, f, te, tv: (te[t], 0, nf + col(f, tv, t))),
                  pl.BlockSpec((None, MOE_TF, D_MODEL), lambda t, f, te, tv: (te[t], col(f, tv, t), 0))],
        out_specs=pl.BlockSpec((tm, D_MODEL), lambda t, f, te, tv: (t, 0)),
        scratch_shapes=[pltpu.VMEM((tm, D_MODEL), BF16)],
    )
    return pl.pallas_call(
        _moe_ffn_kernel,
        grid_spec=grid_spec,
        out_shape=jax.ShapeDtypeStruct((nt * tm, D_MODEL), F32),
        compiler_params=_cparams(("arbitrary", "arbitrary")),
        name="moe_ffn",
    )(tile_expert, tile_rows, xs, w_gu, w_gu, w_d)


def _moe_combine_kernel(pos_ref, x_ref, w_ref, osort_hbm, o_ref, abuf, sem):
    tm = x_ref.shape[0]

    def issue(r, c):
        _row_copy(osort_hbm, pos_ref[0, 0, r], abuf, r, sem).start()
        return c

    lax.fori_loop(0, 2 * tm, issue, 0, unroll=8)

    def wait(r, c):
        _row_copy(osort_hbm, 0, abuf, r, sem).wait()
        return c

    lax.fori_loop(0, 2 * tm, wait, 0, unroll=8)
    w = w_ref[...]
    o_ref[...] = x_ref[...] + w[:, 0:1] * abuf[0:tm, :] + w[:, 1:2] * abuf[tm:2 * tm, :]


def moe_combine(pos, x, wts, osort, *, tm=256):
    m = x.shape[0]
    return pl.pallas_call(
        _moe_combine_kernel,
        grid=(m // tm,),
        in_specs=[pl.BlockSpec((1, 1, 2 * tm), lambda i: (i, 0, 0), memory_space=pltpu.SMEM),
                  pl.BlockSpec((tm, D_MODEL), lambda i: (i, 0)),
                  pl.BlockSpec((tm, LANES), lambda i: (i, 0)),
                  pl.BlockSpec(memory_space=pl.ANY)],
        out_specs=pl.BlockSpec((tm, D_MODEL), lambda i: (i, 0)),
        out_shape=jax.ShapeDtypeStruct((m, D_MODEL), F32),
        scratch_shapes=[pltpu.VMEM((2 * tm, D_MODEL), F32), pltpu.SemaphoreType.DMA],
        compiler_params=_cparams(("arbitrary",)),
        name="moe_combine",
    )(pos, x, wts, osort)


def _dispatch_plan(counts, ids, rank, tm, nt):
    padded = ((counts + tm - 1) // tm) * tm
    ends = jnp.cumsum(padded)
    offs = ends - padded
    off_of = jnp.zeros_like(ids)
    for e in range(N_EXPERTS):
        off_of = jnp.where(ids == e, offs[e], off_of)
    pos = off_of + rank
    starts = jnp.arange(nt, dtype=jnp.int32) * tm
    tile_expert = jnp.minimum(jnp.sum((starts[:, None] >= ends[None, :]).astype(jnp.int32), axis=1),
                              N_EXPERTS - 1)
    real_end = (offs + counts)[tile_expert]
    tile_rows = jnp.where(starts < ends[-1], jnp.clip(real_end - starts, 0, tm), 0).astype(jnp.int32)
    last_valid = jnp.maximum(jnp.sum((tile_rows != 0).astype(jnp.int32)) - 1, 0)
    tile_expert = jnp.where(tile_rows != 0, tile_expert, tile_expert[last_valid])
    return tile_expert, tile_rows, pos


def _combine_pos(pos, tm):
    m = pos.shape[0]
    return jnp.transpose(pos.reshape(m // tm, tm, 2), (0, 2, 1)).reshape(m // tm, 1, 2 * tm)


def _t5_bucket_np(rel):
    nb = N_BUCKETS // 2
    max_exact = nb // 2
    ret = np.where(rel > 0, nb, 0)
    n = np.abs(rel)
    nf = np.maximum(n, 1).astype(np.float32)
    large = max_exact + (np.log(nf / np.float32(max_exact)) / np.float32(math.log(MAX_DISTANCE / max_exact))
                         * np.float32(nb - max_exact)).astype(np.int32)
    large = np.minimum(large, nb - 1)
    return (ret + np.where(n < max_exact, n, large)).astype(np.int32)


def _bias_tensor(rel_table, n_q, valid):
    rel = np.arange(KEY_TILE, dtype=np.int32)[None, :] - WINDOW - np.arange(n_q, dtype=np.int32)[:, None]
    onehot = np.eye(N_BUCKETS, dtype=np.float32)[:, _t5_bucket_np(rel).reshape(-1)]
    bias = jnp.dot(rel_table.T[HEAD_PERM], jnp.asarray(onehot), precision=lax.Precision.HIGHEST)
    return jnp.where(jnp.asarray(valid)[None], bias.reshape(A_HEADS, n_q, KEY_TILE), NEG).astype(F32)


def _stack_heads(x):
    _, r, c = x.shape
    return jnp.transpose(x.reshape(2, 4, 2, r, c), (0, 2, 1, 3, 4)).reshape(2, 2, 4 * r, c)


def _with_sink_column(bias, sinks_perm):
    return _stack_heads(bias.at[:, :, KEY_TILE - 1].set(sinks_perm[:, None]))


def _prompt_valid():
    qc = np.arange(Q_TILE)[:, None] // CHUNK
    kc = np.arange(KEY_TILE)[None, :] // CHUNK
    return (kc >= qc) & (kc <= qc + WINDOW // CHUNK)


def _sample_valid(rows):
    return np.broadcast_to(np.arange(KEY_TILE)[None, :] < WINDOW + rows, (rows, KEY_TILE))


def _half_avg():
    blk = np.kron(np.eye(2), np.ones((A_HEAD_DIM, A_HEAD_DIM))) / A_HEAD_DIM
    return jnp.asarray(blk, BF16)


def _mixer_weights(l, g_mix, w_in, b_gate, q_norm_g, k_norm_g, sinks, w_conv_b, w_conv_c, b_conv_c,
                   ln_c_g, ln_c_b, w_proj_a, w_proj_b, w_proj_c, w_out):
    w = w_in[l]
    wq = w[:, :A_WIDTH].reshape(D_MODEL, A_HEADS, A_HEAD_DIM)[:, HEAD_PERM].reshape(D_MODEL, A_WIDTH)
    w_in_bf = jnp.concatenate([wq, w[:, A_WIDTH:]], axis=1).astype(BF16)
    wpa = w_proj_a[l].reshape(A_HEADS, A_HEAD_DIM, D_MODEL)[HEAD_PERM].reshape(A_WIDTH, D_MODEL)
    return dict(
        g_mix=g_mix[l], w_in=w_in_bf, b_gate=b_gate[l],
        qg2=(jnp.tile(q_norm_g[l], 2) * (A_HEAD_DIM ** -0.5)).reshape(1, LANES), kg2=jnp.tile(k_norm_g[l], 2).reshape(1, LANES),
        sinks=sinks[l][HEAD_PERM],
        w_cb=w_conv_b[l], w_cc=w_conv_c[l], b_cc=b_conv_c[l], ln_g=ln_c_g[l], ln_b=ln_c_b[l],
        wpa=wpa.astype(BF16), wpb=w_proj_b[l].astype(BF16), wpc=w_proj_c[l].astype(BF16),
        wo=w_out[l].astype(BF16))


def _mixer_prompt(x, mw, bias, bd, batch, seq):
    z = norm_matmul(x, mw["g_mix"], mw["w_in"], tm=1024, tn=1024, out_dtype=BF16)
    oa, nk, nv = swa_prompt(z, batch, seq, mw["qg2"], mw["kg2"], bd, bias, mw["sinks"])
    ob, oc, ncb, ncc = conv_prompt(z, batch, seq, mw["w_cb"], mw["w_cc"], mw["b_cc"], mw["ln_g"], mw["ln_b"])
    y = merge(oa, ob, oc, z, mw["b_gate"], mw["wpa"], mw["wpb"], mw["wpc"], mw["wo"], x)
    return y, (nk, nv, ncb[:, 8 - (B_CONV - 1):], ncc[:, HALO - (C_CONV - 1):])


def _mixer_sample(x, mw, bias, bd, batch, rows, cache_k, cache_v, layer, st_b, st_c):
    z = norm_matmul(x, mw["g_mix"], mw["w_in"], tm=x.shape[0], tn=1024, out_dtype=BF16)
    oa, nk, nv = swa_sample(z, batch, rows, cache_k, cache_v, layer, mw["qg2"], mw["kg2"], bd,
                            _with_sink_column(bias, mw["sinks"]))
    stb = jnp.pad(st_b, ((0, 0), (HALO - (B_CONV - 1), 0), (0, 0)))
    stc = jnp.pad(st_c, ((0, 0), (HALO - (C_CONV - 1), 0), (0, 0)))
    ob, oc, ncb, ncc = conv_sample(z, batch, rows, stb, stc, mw["w_cb"], mw["w_cc"], mw["b_cc"],
                                   mw["ln_g"], mw["ln_b"])
    y = merge(oa, ob, oc, z, mw["b_gate"], mw["wpa"], mw["wpb"], mw["wpc"], mw["wo"], x)
    return y, (nk, nv, ncb[:, 8 - (B_CONV - 1):], ncc[:, HALO - (C_CONV - 1):])


def _channel_mixer(yp, ys, l, g_ffn, w_ffn_gu, w_ffn_d, w_router, b_router, w_moe_gu, w_moe_d):
    if l % 2 == 0:
        return (ffn(yp, g_ffn[l], w_ffn_gu[l // 2], w_ffn_d[l // 2]),
                ffn(ys, g_ffn[l], w_ffn_gu[l // 2], w_ffn_d[l // 2]))
    i = l // 2
    w_pad = jnp.pad(w_router[i], ((0, 0), (0, LANES - N_EXPERTS)))
    b_pad = jnp.pad(b_router[i], (0, LANES - N_EXPERTS)).reshape(1, LANES)
    wts_p, ids_p, rank_p, cnt_p = router(yp, g_ffn[l], w_pad, b_pad, jnp.zeros((1, LANES), F32))
    wts_s, ids_s, rank_s, cnt = router(ys, g_ffn[l], w_pad, b_pad, cnt_p)
    mp, ms = yp.shape[0], ys.shape[0]
    nt = (2 * (mp + ms)) // MOE_TM + N_EXPERTS
    ids = jnp.concatenate([ids_p[:, :2], ids_s[:, :2]], axis=0)
    rank = jnp.concatenate([rank_p[:, :2], rank_s[:, :2]], axis=0)
    tile_expert, tile_rows, pos = _dispatch_plan(cnt[0, :N_EXPERTS].astype(jnp.int32), ids, rank, MOE_TM, nt)
    tm_p, tm_s = 512, min(512, ms)
    pos_p, pos_s = _combine_pos(pos[:mp], tm_p), _combine_pos(pos[mp:], tm_s)
    xs = jnp.zeros((nt * MOE_TM, D_MODEL // 2), jnp.uint32)
    xs = moe_dispatch(pos_p, yp, g_ffn[l], xs, tm=tm_p)
    xs = moe_dispatch(pos_s, ys, g_ffn[l], xs, tm=tm_s)
    osort = moe_ffn(tile_expert, tile_rows, xs, w_moe_gu[i], w_moe_d[i])
    return (moe_combine(pos_p, yp, wts_p, osort, tm=tm_p), moe_combine(pos_s, ys, wts_s, osort, tm=tm_s))


def kernel(x_prompt, x_sample, mem_prompt, cache_mem_k, cache_mem_v, cache_swa_k, cache_swa_v, state_conv_b, state_conv_c, rel_table, g_mix, w_in, b_gate, q_norm_g, k_norm_g, sinks, w_conv_b, w_conv_c, b_conv_c, ln_c_g, ln_c_b, w_proj_a, w_proj_b, w_proj_c, w_out, g_xattn, g_mem, w_xq, w_xkv, xq_norm_g, xk_norm_g, w_xo, g_ffn, w_ffn_gu, w_ffn_d, w_router, b_router, w_moe_gu, w_moe_d):
    batch, seq, d = x_prompt.shape
    dec_batch, dec_seq, _ = x_sample.shape
    depth = g_mix.shape[0]
    yp = x_prompt.reshape(batch * seq, d)
    ys = x_sample.reshape(dec_batch * dec_seq, d)
    mem = mem_prompt.reshape(batch * N_MEM, d)
    bd = _half_avg()
    first = np.arange(KEY_TILE)[None, :] >= Q_TILE
    bias_p = jnp.stack([_stack_heads(_bias_tensor(rel_table, Q_TILE, _prompt_valid() & first)),
                        _stack_heads(_bias_tensor(rel_table, Q_TILE, _prompt_valid()))])
    bias_s = _bias_tensor(rel_table, dec_seq, _sample_valid(dec_seq))
    outs = [[] for _ in range(10)]
    ck = cache_swa_k.reshape(depth, dec_batch, WINDOW, A_KV_WIDTH)
    cv = cache_swa_v.reshape(depth, dec_batch, WINDOW, A_KV_WIDTH)
    cmk = cache_mem_k.reshape(depth, dec_batch, N_MEM, X_WIDTH)
    cmv = cache_mem_v.reshape(depth, dec_batch, N_MEM, X_WIDTH)
    for l in range(depth):
        mw = _mixer_weights(l, g_mix, w_in, b_gate, q_norm_g, k_norm_g, sinks, w_conv_b, w_conv_c,
                            b_conv_c, ln_c_g, ln_c_b, w_proj_a, w_proj_b, w_proj_c, w_out)
        wq = w_xq[l].astype(BF16)
        wo = w_xo[l].astype(BF16)
        ffn_args = (g_ffn, w_ffn_gu, w_ffn_d, w_router, b_router, w_moe_gu, w_moe_d)
        yp, (nk, nv, ncb, ncc) = _mixer_prompt(yp, mw, bias_p, bd, batch, seq)
        mk, mv = mem_kv(mem, g_mem[l], w_xkv[l].astype(BF16), xk_norm_g[l])
        mk3 = mk.reshape(batch, N_MEM, X_WIDTH)
        mv3 = mv.reshape(batch, N_MEM, X_WIDTH)
        yp = xattn(yp, g_xattn[l], wq, xq_norm_g[l], mk3[None], mv3[None], wo, layer=0, nb=1, rpb=512,
                   tiles_per_mem=seq // 512)
        for lst, v in zip(outs[:6], (mk3.reshape(batch, N_MEM, X_HEADS, X_HEAD_DIM),
                                     mv3.reshape(batch, N_MEM, X_HEADS, X_HEAD_DIM),
                                     nk.reshape(batch, WINDOW, A_KV_HEADS, A_HEAD_DIM),
                                     nv.reshape(batch, WINDOW, A_KV_HEADS, A_HEAD_DIM), ncb, ncc)):
            lst.append(v)
        ys, (nk, nv, ncb, ncc) = _mixer_sample(ys, mw, bias_s, bd, dec_batch, dec_seq, ck, cv, l,
                                               state_conv_b[l], state_conv_c[l])
        ys = xattn(ys, g_xattn[l], wq, xq_norm_g[l], cmk, cmv, wo, layer=l, nb=8, rpb=dec_seq,
                   tiles_per_mem=1)
        yp, ys = _channel_mixer(yp, ys, l, *ffn_args)
        for lst, v in zip(outs[6:], (nk.reshape(dec_batch, dec_seq, A_KV_HEADS, A_HEAD_DIM),
                                     nv.reshape(dec_batch, dec_seq, A_KV_HEADS, A_HEAD_DIM), ncb, ncc)):
            lst.append(v)
    return (yp.reshape(batch, seq, d), ys.reshape(dec_batch, dec_seq, d)) + tuple(jnp.stack(o) for o in outs)
```

```python
import functools
import math

import numpy as np
import jax
import jax.numpy as jnp
from jax import lax
from jax.experimental import pallas as pl
from jax.experimental.pallas import tpu as pltpu

F32 = jnp.float32
BF16 = jnp.bfloat16

D_MODEL = 2048
CHUNK = 64
A_HEADS = 16
A_KV_HEADS = 4
A_HEAD_DIM = 64
A_WIDTH = A_HEADS * A_HEAD_DIM
A_KV_WIDTH = A_KV_HEADS * A_HEAD_DIM
WINDOW = 128
N_BUCKETS = 32
MAX_DISTANCE = 128
B_WIDTH = 512
B_CONV = 3
C_WIDTH = 512
C_CONV = 31
N_MEM = 256
X_HEADS = 4
X_HEAD_DIM = 128
X_WIDTH = X_HEADS * X_HEAD_DIM
D_FF = 5632
N_EXPERTS = 8
EPS = 1e-6

LANES = 128
KEY_TILE = 256
Q_TILE = 128
HALO = 32
NEG = -1e30
VMEM_LIMIT = 56 * 1024 * 1024

COL_Q = 0
COL_K, COL_V = 4, 5
COL_GB, COL_GC, COL_HB, COL_GA, COL_GG = 3, 4, 5, 6, 7
COL_GATE0 = 2
IN_COLS = 4096 + 3 * D_MODEL

HEAD_PERM = np.array([8 * n + (p % 2) * 4 + p // 2 for n in range(2) for p in range(8)])


def _cparams(sem):
    return pltpu.CompilerParams(dimension_semantics=sem, vmem_limit_bytes=VMEM_LIMIT)


def _rms(x, g):
    ms = jnp.mean(x * x, axis=-1, keepdims=True)
    return x * lax.rsqrt(ms + EPS) * g


def _resident(shape):
    nd = len(shape)
    return pl.BlockSpec(shape, lambda *_: (0,) * nd, pipeline_mode=pl.Buffered(1))


def _norm_matmul_kernel(x_ref, g_ref, w_ref, o_ref, hn_ref):
    @pl.when(pl.program_id(1) == 0)
    def _():
        hn_ref[...] = _rms(x_ref[...], g_ref[...]).astype(BF16)

    o_ref[...] = jnp.dot(hn_ref[...], w_ref[...], preferred_element_type=F32).astype(o_ref.dtype)


def norm_matmul(x, g, w, *, tm, tn, out_dtype):
    m, k = x.shape
    n = w.shape[1]
    return pl.pallas_call(
        _norm_matmul_kernel,
        grid=(m // tm, n // tn),
        in_specs=[pl.BlockSpec((tm, k), lambda i, j: (i, 0)),
                  pl.BlockSpec((1, k), lambda i, j: (0, 0)),
                  pl.BlockSpec((k, tn), lambda i, j: (0, j))],
        out_specs=pl.BlockSpec((tm, tn), lambda i, j: (i, j)),
        out_shape=jax.ShapeDtypeStruct((m, n), out_dtype),
        scratch_shapes=[pltpu.VMEM((tm, k), BF16)],
        compiler_params=_cparams(("parallel", "arbitrary")),
        name="norm_matmul",
    )(x, g.reshape(1, k), w)


def _mem_kv_kernel(x_ref, g_ref, w_ref, kg_ref, k_ref, v_ref):
    hn = _rms(x_ref[...], g_ref[...]).astype(BF16)
    kv = jnp.dot(hn, w_ref[...], preferred_element_type=F32)
    for h in range(X_HEADS):
        sl = slice(h * X_HEAD_DIM, (h + 1) * X_HEAD_DIM)
        k_ref[:, sl] = _rms(kv[:, sl], kg_ref[...])
    v_ref[...] = kv[:, X_WIDTH:]


def mem_kv(mem, g, w_bf, kg):
    m, k = mem.shape
    tm = 256
    return pl.pallas_call(
        _mem_kv_kernel,
        grid=(m // tm,),
        in_specs=[pl.BlockSpec((tm, k), lambda i: (i, 0)),
                  pl.BlockSpec((1, k), lambda i: (0, 0)),
                  _resident((k, 2 * X_WIDTH)),
                  pl.BlockSpec((1, X_HEAD_DIM), lambda i: (0, 0))],
        out_specs=[pl.BlockSpec((tm, X_WIDTH), lambda i: (i, 0)),
                   pl.BlockSpec((tm, X_WIDTH), lambda i: (i, 0))],
        out_shape=[jax.ShapeDtypeStruct((m, X_WIDTH), F32)] * 2,
        compiler_params=_cparams(("parallel",)),
        name="mem_kv",
    )(mem, g.reshape(1, k), w_bf, kg.reshape(1, X_HEAD_DIM))


def _half_norm(x, g):
    x2 = x * x
    lo_lane = lax.broadcasted_iota(jnp.int32, (1, LANES), 1) < A_HEAD_DIM
    s_lo = jnp.sum(jnp.where(lo_lane, x2, 0.0), axis=-1, keepdims=True)
    s_hi = jnp.sum(jnp.where(lo_lane, 0.0, x2), axis=-1, keepdims=True)
    ms = jnp.where(lo_lane, s_lo, s_hi) * (1.0 / A_HEAD_DIM)
    return x * lax.rsqrt(ms + EPS) * g


def _swa_heads(q_ref, qg, k2, v2, bias_ref, sink_ref, o_ref, *, stack):
    lo_lane = lax.broadcasted_iota(jnp.int32, (1, LANES), 1) < A_HEAD_DIM
    rows = q_ref.shape[0]
    cols = A_WIDTH // LANES // 2
    for n in range(2):
        k_half = (jnp.where(lo_lane, k2[n], 0.0).astype(BF16), jnp.where(lo_lane, 0.0, k2[n]).astype(BF16))
        for g in range(0, cols, stack):
            qs = [_half_norm(q_ref[:, c * LANES:(c + 1) * LANES].astype(F32), qg)
                  for c in range(cols * n + g, cols * n + g + stack)]
            qn = (qs[0] if stack == 1 else jnp.concatenate(qs, axis=0)).astype(BF16)
            at = slice(g * rows, (g + stack) * rows)
            halves = []
            for half in range(2):
                s = lax.dot_general(qn, k_half[half], (((1,), (1,)), ((), ())), preferred_element_type=F32)
                s = s + bias_ref[n, half, at, :]
                if sink_ref is None:
                    m = jnp.max(s, axis=-1, keepdims=True)
                    p = jnp.exp(s - m)
                    den = jnp.sum(p, axis=-1, keepdims=True)
                else:
                    sink = sink_ref[2 * (cols * n + g) + half]
                    m = jnp.maximum(jnp.max(s, axis=-1, keepdims=True), sink)
                    p = jnp.exp(s - m)
                    den = jnp.sum(p, axis=-1, keepdims=True) + jnp.exp(sink - m)
                o = jnp.dot(p.astype(BF16), v2[n], preferred_element_type=F32)
                halves.append(o / den)
            o = jnp.where(lo_lane, halves[0], halves[1]).astype(o_ref.dtype)
            for i in range(stack):
                c = cols * n + g + i
                o_ref[:, c * LANES:(c + 1) * LANES] = o[i * rows:(i + 1) * rows]


def _swa_prompt_kernel(q_ref, kc_ref, kp_ref, vc_ref, vp_ref, qg_ref, kg_ref,
                       bias_ref, sink_ref, o_ref, nk_ref, nv_ref):
    k2, v2 = [], []
    for n in range(2):
        sl = slice(n * LANES, (n + 1) * LANES)
        kcat = jnp.concatenate([kp_ref[:, sl], kc_ref[:, sl]], axis=0).astype(F32)
        kn = _half_norm(kcat, kg_ref[...])
        nk_ref[0, :, sl] = kn[Q_TILE:]
        k2.append(kn)
        v2.append(jnp.concatenate([vp_ref[:, sl], vc_ref[:, sl]], axis=0))
    nv_ref[0] = vc_ref[...].astype(F32)
    _swa_heads(q_ref, qg_ref[...], k2, v2, bias_ref, sink_ref, o_ref, stack=1)


def swa_prompt(z, batch, seq, qg2, kg2, bias, sinks):
    nt = seq // Q_TILE
    row = lambda b, t: b * nt + t
    prev = lambda b, t: jnp.maximum(b * nt + t - 1, 0)
    return pl.pallas_call(
        _swa_prompt_kernel,
        grid=(batch, nt),
        in_specs=[pl.BlockSpec((Q_TILE, A_WIDTH), lambda b, t: (row(b, t), COL_Q)),
                  pl.BlockSpec((Q_TILE, A_KV_WIDTH), lambda b, t: (row(b, t), COL_K)),
                  pl.BlockSpec((Q_TILE, A_KV_WIDTH), lambda b, t: (prev(b, t), COL_K)),
                  pl.BlockSpec((Q_TILE, A_KV_WIDTH), lambda b, t: (row(b, t), COL_V)),
                  pl.BlockSpec((Q_TILE, A_KV_WIDTH), lambda b, t: (prev(b, t), COL_V)),
                  pl.BlockSpec((1, LANES), lambda b, t: (0, 0)),
                  pl.BlockSpec((1, LANES), lambda b, t: (0, 0)),
                  pl.BlockSpec((None, 2, 2, 4 * Q_TILE, KEY_TILE),
                               lambda b, t: (jnp.minimum(t, 1), 0, 0, 0, 0)),
                  pl.BlockSpec(memory_space=pltpu.SMEM)],
        out_specs=[pl.BlockSpec((Q_TILE, A_WIDTH), lambda b, t: (row(b, t), 0)),
                   pl.BlockSpec((1, WINDOW, A_KV_WIDTH), lambda b, t: (b, 0, 0)),
                   pl.BlockSpec((1, WINDOW, A_KV_WIDTH), lambda b, t: (b, 0, 0))],
        out_shape=[jax.ShapeDtypeStruct((batch * seq, A_WIDTH), BF16),
                   jax.ShapeDtypeStruct((batch, WINDOW, A_KV_WIDTH), F32),
                   jax.ShapeDtypeStruct((batch, WINDOW, A_KV_WIDTH), F32)],
        compiler_params=_cparams(("parallel", "arbitrary")),
        name="swa_prompt",
    )(z, z, z, z, z, qg2, kg2, bias, sinks)


def _swa_sample_kernel(q_ref, kn_ref, vn_ref, ck_ref, cv_ref, qg_ref, kg_ref,
                       bias_ref, o_ref, nk_ref, nv_ref):
    rows = q_ref.shape[0]
    pad = KEY_TILE - WINDOW - rows
    k2, v2 = [], []
    for n in range(2):
        sl = slice(n * LANES, (n + 1) * LANES)
        kn = _half_norm(kn_ref[:, sl].astype(F32), kg_ref[...])
        nk_ref[0, :, sl] = kn
        k2.append(jnp.concatenate([ck_ref[0, :, sl], kn, jnp.zeros((pad, LANES), F32)], axis=0))
        v2.append(jnp.concatenate([cv_ref[0, :, sl].astype(BF16), vn_ref[:, sl],
                                   jnp.zeros((pad, LANES), BF16)], axis=0))
    nv_ref[0] = vn_ref[...].astype(F32)
    _swa_heads(q_ref, qg_ref[...], k2, v2, bias_ref, None, o_ref, stack=4)


def swa_sample(z, batch, rows, cache_k, cache_v, layer, qg2, kg2, bias):
    return pl.pallas_call(
        _swa_sample_kernel,
        grid=(batch,),
        in_specs=[pl.BlockSpec((rows, A_WIDTH), lambda b: (b, COL_Q)),
                  pl.BlockSpec((rows, A_KV_WIDTH), lambda b: (b, COL_K)),
                  pl.BlockSpec((rows, A_KV_WIDTH), lambda b: (b, COL_V)),
                  pl.BlockSpec((None, 1, WINDOW, A_KV_WIDTH), lambda b: (layer, b, 0, 0)),
                  pl.BlockSpec((None, 1, WINDOW, A_KV_WIDTH), lambda b: (layer, b, 0, 0)),
                  pl.BlockSpec((1, LANES), lambda b: (0, 0)),
                  pl.BlockSpec((1, LANES), lambda b: (0, 0)),
                  _resident((2, 2, 4 * rows, KEY_TILE))],
        out_specs=[pl.BlockSpec((rows, A_WIDTH), lambda b: (b, 0)),
                   pl.BlockSpec((1, rows, A_KV_WIDTH), lambda b: (b, 0, 0)),
                   pl.BlockSpec((1, rows, A_KV_WIDTH), lambda b: (b, 0, 0))],
        out_shape=[jax.ShapeDtypeStruct((batch * rows, A_WIDTH), BF16),
                   jax.ShapeDtypeStruct((batch, rows, A_KV_WIDTH), F32),
                   jax.ShapeDtypeStruct((batch, rows, A_KV_WIDTH), F32)],
        compiler_params=_cparams(("parallel",)),
        name="swa_sample",
    )(z, z, z, cache_k, cache_v, qg2, kg2, bias)


def _conv_body(gb_ref, ub_main, uc_main, ub_halo, uc_halo, wb_ref, wc_ref, bc_ref, lg_ref, lb_ref,
               ob_ref, oc_ref, nb_ref, nc_ref, sb_ref, sc_ref, ph_ref, write_state):
    rows = ub_main.shape[0]
    sb_ref[0:HALO] = ub_halo
    sb_ref[HALO:HALO + rows] = ub_main
    sc_ref[0:HALO] = uc_halo
    sc_ref[HALO:HALO + rows] = uc_main
    span = rows + HALO
    sc_ref[span:span + 8] = jnp.zeros((8, C_WIDTH), F32)
    for b in range(1, 8):
        ph_ref[b - 1] = sc_ref[pl.ds(b, span), :]
    sub = min(rows, 32)
    for r0 in range(0, rows, sub):
        yb = jnp.zeros((sub, B_WIDTH), F32)
        for k in range(B_CONV):
            yb = yb + wb_ref[k:k + 1, :] * sb_ref[pl.ds(r0 + HALO - (B_CONV - 1) + k, sub), :]
        ob_ref[r0:r0 + sub, :] = (gb_ref[r0:r0 + sub, :].astype(F32) * yb).astype(ob_ref.dtype)
        yc = jnp.zeros((sub, C_WIDTH), F32)
        for k in range(C_CONV):
            shift, phase = divmod(HALO - (C_CONV - 1) + k, 8)
            at = pl.ds(r0 + 8 * shift, sub)
            taps = sc_ref[at, :] if phase == 0 else ph_ref[phase - 1, at, :]
            yc = yc + wc_ref[k:k + 1, :] * taps
        yc = yc + bc_ref[...]
        mu = jnp.mean(yc, axis=-1, keepdims=True)
        xc = yc - mu
        y = xc * lax.rsqrt(jnp.mean(xc * xc, axis=-1, keepdims=True) + EPS)
        y = y * lg_ref[...] + lb_ref[...]
        oc_ref[r0:r0 + sub, :] = (y * jax.nn.sigmoid(y)).astype(oc_ref.dtype)

    def _state():
        nb_ref[0] = sb_ref[rows + HALO - 8:rows + HALO]
        nc_ref[0] = sc_ref[rows:rows + HALO]

    write_state(_state)


def _conv_prompt_kernel(gb_ref, gc_ref, hb_ref, ga_ref, gg_ref, gch_ref, hbh_ref, gah_ref, ggh_ref,
                        wb_ref, wc_ref, bc_ref, lg_ref, lb_ref,
                        ob_ref, oc_ref, nb_ref, nc_ref, sb_ref, sc_ref, ph_ref):
    t = pl.program_id(1)
    hist = (t > 0).astype(F32)
    ub_main = gc_ref[...].astype(F32) * hb_ref[...].astype(F32)
    uc_main = ga_ref[...].astype(F32) * jax.nn.sigmoid(gg_ref[...].astype(F32))
    ub_halo = gch_ref[...].astype(F32) * hbh_ref[...].astype(F32) * hist
    uc_halo = gah_ref[...].astype(F32) * jax.nn.sigmoid(ggh_ref[...].astype(F32)) * hist
    last = pl.num_programs(1) - 1
    _conv_body(gb_ref, ub_main, uc_main, ub_halo, uc_halo, wb_ref, wc_ref, bc_ref, lg_ref, lb_ref,
               ob_ref, oc_ref, nb_ref, nc_ref, sb_ref, sc_ref, ph_ref,
               lambda f: pl.when(t == last)(f))


def conv_prompt(z, batch, seq, wb, wc, bc, lg, lb, *, tr=256):
    nt = seq // tr
    hp = tr // HALO
    main = lambda c: pl.BlockSpec((tr, B_WIDTH), lambda b, t: (b * nt + t, c))
    halo = lambda c: pl.BlockSpec((HALO, B_WIDTH), lambda b, t: (jnp.maximum((b * nt + t) * hp - 1, 0), c))
    vec = lambda r: pl.BlockSpec((r, B_WIDTH), lambda b, t: (0, 0))
    return pl.pallas_call(
        _conv_prompt_kernel,
        grid=(batch, nt),
        in_specs=[main(COL_GB), main(COL_GC), main(COL_HB), main(COL_GA), main(COL_GG),
                  halo(COL_GC), halo(COL_HB), halo(COL_GA), halo(COL_GG),
                  vec(B_CONV), vec(C_CONV), vec(1), vec(1), vec(1)],
        out_specs=[pl.BlockSpec((tr, B_WIDTH), lambda b, t: (b * nt + t, 0)),
                   pl.BlockSpec((tr, C_WIDTH), lambda b, t: (b * nt + t, 0)),
                   pl.BlockSpec((1, 8, B_WIDTH), lambda b, t: (b, 0, 0)),
                   pl.BlockSpec((1, HALO, C_WIDTH), lambda b, t: (b, 0, 0))],
        out_shape=[jax.ShapeDtypeStruct((batch * seq, B_WIDTH), BF16),
                   jax.ShapeDtypeStruct((batch * seq, C_WIDTH), BF16),
                   jax.ShapeDtypeStruct((batch, 8, B_WIDTH), F32),
                   jax.ShapeDtypeStruct((batch, HALO, C_WIDTH), F32)],
        scratch_shapes=[pltpu.VMEM((tr + HALO, B_WIDTH), F32), pltpu.VMEM((tr + HALO + 8, C_WIDTH), F32),
                        pltpu.VMEM((7, tr + HALO, C_WIDTH), F32)],
        compiler_params=_cparams(("parallel", "arbitrary")),
        name="conv_prompt",
    )(z, z, z, z, z, z, z, z, z, wb, wc, bc.reshape(1, -1), lg.reshape(1, -1), lb.reshape(1, -1))


def _conv_sample_kernel(gb_ref, gc_ref, hb_ref, ga_ref, gg_ref, stb_ref, stc_ref,
                        wb_ref, wc_ref, bc_ref, lg_ref, lb_ref,
                        ob_ref, oc_ref, nb_ref, nc_ref, sb_ref, sc_ref, ph_ref):
    ub_main = gc_ref[...].astype(F32) * hb_ref[...].astype(F32)
    uc_main = ga_ref[...].astype(F32) * jax.nn.sigmoid(gg_ref[...].astype(F32))
    _conv_body(gb_ref, ub_main, uc_main, stb_ref[0], stc_ref[0], wb_ref, wc_ref, bc_ref, lg_ref, lb_ref,
               ob_ref, oc_ref, nb_ref, nc_ref, sb_ref, sc_ref, ph_ref, lambda f: f())


def conv_sample(z, batch, rows, stb, stc, wb, wc, bc, lg, lb):
    main = lambda c: pl.BlockSpec((rows, B_WIDTH), lambda b: (b, c))
    vec = lambda r: pl.BlockSpec((r, B_WIDTH), lambda b: (0, 0))
    return pl.pallas_call(
        _conv_sample_kernel,
        grid=(batch,),
        in_specs=[main(COL_GB), main(COL_GC), main(COL_HB), main(COL_GA), main(COL_GG),
                  pl.BlockSpec((1, HALO, B_WIDTH), lambda b: (b, 0, 0)),
                  pl.BlockSpec((1, HALO, C_WIDTH), lambda b: (b, 0, 0)),
                  vec(B_CONV), vec(C_CONV), vec(1), vec(1), vec(1)],
        out_specs=[pl.BlockSpec((rows, B_WIDTH), lambda b: (b, 0)),
                   pl.BlockSpec((rows, C_WIDTH), lambda b: (b, 0)),
                   pl.BlockSpec((1, 8, B_WIDTH), lambda b: (b, 0, 0)),
                   pl.BlockSpec((1, HALO, C_WIDTH), lambda b: (b, 0, 0))],
        out_shape=[jax.ShapeDtypeStruct((batch * rows, B_WIDTH), BF16),
                   jax.ShapeDtypeStruct((batch * rows, C_WIDTH), BF16),
                   jax.ShapeDtypeStruct((batch, 8, B_WIDTH), F32),
                   jax.ShapeDtypeStruct((batch, HALO, C_WIDTH), F32)],
        scratch_shapes=[pltpu.VMEM((rows + HALO, B_WIDTH), F32), pltpu.VMEM((rows + HALO + 8, C_WIDTH), F32),
                        pltpu.VMEM((7, rows + HALO, C_WIDTH), F32)],
        compiler_params=_cparams(("parallel",)),
        name="conv_sample",
    )(z, z, z, z, z, stb, stc, wb, wc, bc.reshape(1, -1), lg.reshape(1, -1), lb.reshape(1, -1))


def _merge_kernel(oa_ref, ob_ref, oc_ref, l0_ref, l1_ref, l2_ref, bg_ref,
                  wpa_ref, wpb_ref, wpc_ref, wo_ref, x_ref, o_ref):
    def gated(l_ref, i, o_r, w_r):
        gate = jax.nn.sigmoid(l_ref[...].astype(F32) + bg_ref[i:i + 1, :])
        return gate * jnp.dot(o_r[...], w_r[...], preferred_element_type=F32)

    merged = gated(l0_ref, 0, oa_ref, wpa_ref)
    merged = merged + gated(l1_ref, 1, ob_ref, wpb_ref)
    merged = merged + gated(l2_ref, 2, oc_ref, wpc_ref)
    o_ref[...] = x_ref[...] + jnp.dot(merged.astype(BF16), wo_ref[...], preferred_element_type=F32)


def merge(oa, ob, oc, z, bg, wpa, wpb, wpc, wo, x, *, tm=256):
    m = x.shape[0]
    tm = min(tm, m)
    rows = lambda w: pl.BlockSpec((tm, w), lambda i: (i, 0))
    gate = lambda c: pl.BlockSpec((tm, D_MODEL), lambda i: (i, COL_GATE0 + c))
    return pl.pallas_call(
        _merge_kernel,
        grid=(m // tm,),
        in_specs=[rows(A_WIDTH), rows(B_WIDTH), rows(C_WIDTH), gate(0), gate(1), gate(2),
                  pl.BlockSpec((3, D_MODEL), lambda i: (0, 0)),
                  _resident((A_WIDTH, D_MODEL)), _resident((B_WIDTH, D_MODEL)),
                  _resident((C_WIDTH, D_MODEL)), _resident((D_MODEL, D_MODEL)),
                  rows(D_MODEL)],
        out_specs=rows(D_MODEL),
        out_shape=jax.ShapeDtypeStruct((m, D_MODEL), F32),
        compiler_params=_cparams(("parallel",)),
        name="merge",
    )(oa, ob, oc, z, z, z, bg.reshape(3, D_MODEL), wpa, wpb, wpc, wo, x)


def _xattn_kernel(y_ref, g_ref, wq_ref, qg_ref, mk_ref, mv_ref, wo_ref, o_ref, *, nb, rpb):
    y = y_ref[...]
    hn = _rms(y, g_ref[...]).astype(BF16)
    q = jnp.dot(hn, wq_ref[...], preferred_element_type=F32)
    heads = []
    for h in range(X_HEADS):
        sl = slice(h * X_HEAD_DIM, (h + 1) * X_HEAD_DIM)
        qh = _rms(q[:, sl], qg_ref[...])
        per_batch = []
        for b in range(nb):
            qb = qh[b * rpb:(b + 1) * rpb].astype(BF16)
            kh = mk_ref[b, :, sl].astype(BF16)
            vh = mv_ref[b, :, sl].astype(BF16)
            s = lax.dot_general(qb, kh, (((1,), (1,)), ((), ())),
                                preferred_element_type=F32) * (X_HEAD_DIM ** -0.5)
            m = jnp.max(s, axis=-1, keepdims=True)
            p = jnp.exp(s - m)
            den = jnp.sum(p, axis=-1, keepdims=True)
            per_batch.append(jnp.dot(p.astype(BF16), vh, preferred_element_type=F32) / den)
        heads.append(per_batch[0] if nb == 1 else jnp.concatenate(per_batch, axis=0))
    o = jnp.concatenate(heads, axis=1).astype(BF16)
    o_ref[...] = y + jnp.dot(o, wo_ref[...], preferred_element_type=F32)


def xattn(y, g, wq, qg, mk, mv, wo, *, layer, nb, rpb, tiles_per_mem):
    m = y.shape[0]
    tm = nb * rpb
    mem_idx = ((lambda i: (layer, i // tiles_per_mem, 0, 0)) if nb == 1
               else (lambda i: (layer, i, 0, 0)))
    return pl.pallas_call(
        functools.partial(_xattn_kernel, nb=nb, rpb=rpb),
        grid=(m // tm,),
        in_specs=[pl.BlockSpec((tm, D_MODEL), lambda i: (i, 0)),
                  pl.BlockSpec((1, D_MODEL), lambda i: (0, 0)),
                  _resident((D_MODEL, X_WIDTH)),
                  pl.BlockSpec((1, X_HEAD_DIM), lambda i: (0, 0)),
                  pl.BlockSpec((None, nb, N_MEM, X_WIDTH), mem_idx),
                  pl.BlockSpec((None, nb, N_MEM, X_WIDTH), mem_idx),
                  _resident((X_WIDTH, D_MODEL))],
        out_specs=pl.BlockSpec((tm, D_MODEL), lambda i: (i, 0)),
        out_shape=jax.ShapeDtypeStruct((m, D_MODEL), F32),
        compiler_params=_cparams(("parallel",)),
        name="xattn",
    )(y, g.reshape(1, -1), wq, qg.reshape(1, -1), mk, mv, wo)


def _ffn_kernel(x_ref, g_ref, wg_ref, wu_ref, wd_ref, o_ref, hn_ref):
    @pl.when(pl.program_id(1) == 0)
    def _():
        x = x_ref[...]
        hn_ref[...] = _rms(x, g_ref[...]).astype(BF16)
        o_ref[...] = x

    hn = hn_ref[...]
    g = jnp.dot(hn, wg_ref[...].astype(BF16), preferred_element_type=F32)
    u = jnp.dot(hn, wu_ref[...].astype(BF16), preferred_element_type=F32)
    a = (g * jax.nn.sigmoid(g) * u).astype(BF16)
    o_ref[...] += jnp.dot(a, wd_ref[...].astype(BF16), preferred_element_type=F32)


def ffn(x, g, w_gu, w_d, *, tm=1024, tf=512):
    m = x.shape[0]
    tm = min(tm, m)
    nf = D_FF // tf
    return pl.pallas_call(
        _ffn_kernel,
        grid=(m // tm, nf),
        in_specs=[pl.BlockSpec((tm, D_MODEL), lambda i, f: (i, 0), pipeline_mode=pl.Buffered(1)),
                  pl.BlockSpec((1, D_MODEL), lambda i, f: (0, 0)),
                  pl.BlockSpec((D_MODEL, tf), lambda i, f: (0, f)),
                  pl.BlockSpec((D_MODEL, tf), lambda i, f: (0, nf + f)),
                  pl.BlockSpec((tf, D_MODEL), lambda i, f: (f, 0))],
        out_specs=pl.BlockSpec((tm, D_MODEL), lambda i, f: (i, 0), pipeline_mode=pl.Buffered(1)),
        out_shape=jax.ShapeDtypeStruct((m, D_MODEL), F32),
        scratch_shapes=[pltpu.VMEM((tm, D_MODEL), BF16)],
        compiler_params=_cparams(("parallel", "arbitrary")),
        name="ffn",
    )(x, g.reshape(1, -1), w_gu, w_gu, w_d)


def _split3(x):
    hi = x.astype(BF16)
    lo = (x - hi.astype(F32)).astype(BF16)
    return hi, lo


def _router_kernel(x_ref, g_ref, w_ref, b_ref, tri_ref, base_ref, wts_ref, ids_ref, rank_ref, cnt_ref, run_ref):
    @pl.when(pl.program_id(0) == 0)
    def _():
        run_ref[...] = base_ref[...]

    hn = _rms(x_ref[...], g_ref[...])
    h_hi, h_lo = _split3(hn)
    w_hi, w_lo = _split3(w_ref[...])
    dot = functools.partial(jnp.dot, preferred_element_type=F32)
    logits = dot(h_hi, w_hi) + dot(h_hi, w_lo) + dot(h_lo, w_hi) + b_ref[...]
    lane = lax.broadcasted_iota(jnp.int32, logits.shape, 1).astype(F32)
    logits = jnp.where(lane < N_EXPERTS, logits, -jnp.inf)
    v1 = jnp.max(logits, axis=-1, keepdims=True)
    i1 = jnp.min(jnp.where(logits == v1, lane, float(LANES)), axis=-1, keepdims=True)
    rest = jnp.where(lane == i1, -jnp.inf, logits)
    v2 = jnp.max(rest, axis=-1, keepdims=True)
    i2 = jnp.min(jnp.where(rest == v2, lane, float(LANES)), axis=-1, keepdims=True)
    e2 = jnp.exp(v2 - v1)
    den = 1.0 + e2
    first, second = lane == 0.0, lane == 1.0
    wts_ref[...] = jnp.where(first, 1.0 / den, 0.0) + jnp.where(second, e2 / den, 0.0)
    ids_ref[...] = (jnp.where(first, i1, 0.0) + jnp.where(second, i2, 0.0)).astype(jnp.int32)
    hit1, hit2 = lane == i1, lane == i2
    hits = jnp.where(jnp.logical_or(hit1, hit2), 1.0, 0.0)
    before = dot(tri_ref[...], hits.astype(BF16)) + run_ref[...]
    r1 = jnp.sum(jnp.where(hit1, before, 0.0), axis=-1, keepdims=True)
    r2 = jnp.sum(jnp.where(hit2, before, 0.0), axis=-1, keepdims=True)
    rank_ref[...] = (jnp.where(first, r1, 0.0) + jnp.where(second, r2, 0.0)).astype(jnp.int32)
    run_ref[...] += jnp.sum(hits, axis=0, keepdims=True)
    cnt_ref[...] = run_ref[...]


def router(x, g, w_pad, b_pad, base, *, tm=512):
    m = x.shape[0]
    tm = min(tm, m)
    tri = jnp.asarray(np.tril(np.ones((tm, tm), np.float32), -1), BF16)
    row = lambda w: pl.BlockSpec((tm, w), lambda i: (i, 0))
    fixed = lambda s: pl.BlockSpec(s, lambda i: (0, 0))
    return pl.pallas_call(
        _router_kernel,
        grid=(m // tm,),
        in_specs=[row(D_MODEL), fixed((1, D_MODEL)), fixed((D_MODEL, LANES)), fixed((1, LANES)),
                  fixed((tm, tm)), fixed((1, LANES))],
        out_specs=[row(LANES), row(LANES), row(LANES), fixed((1, LANES))],
        out_shape=[jax.ShapeDtypeStruct((m, LANES), F32),
                   jax.ShapeDtypeStruct((m, LANES), jnp.int32),
                   jax.ShapeDtypeStruct((m, LANES), jnp.int32),
                   jax.ShapeDtypeStruct((1, LANES), F32)],
        scratch_shapes=[pltpu.VMEM((1, LANES), F32)],
        compiler_params=_cparams(("arbitrary",)),
        name="router",
    )(x, g.reshape(1, -1), w_pad, b_pad, tri, base)


MOE_TM = 1024
MOE_TF = 512


def _row_copy(src, row, dst, r, sem):
    return pltpu.make_async_copy(src.at[pl.ds(row, 1)], dst.at[pl.ds(r, 1)], sem)


def _moe_dispatch_kernel(pos_ref, x_ref, g_ref, xs_in, xs_hbm, pk_ref, sem):
    del xs_in
    tm = x_ref.shape[0]
    half = D_MODEL // 2
    hn = _rms(x_ref[...], g_ref[...]).astype(BF16)
    lo_bits = lax.bitcast_convert_type(hn[:, :half].astype(F32), jnp.uint32)
    hi_bits = lax.bitcast_convert_type(hn[:, half:].astype(F32), jnp.uint32)
    pk_ref[...] = (lo_bits >> 16) | (hi_bits & jnp.uint32(0xFFFF0000))

    def issue(r, c):
        _row_copy(pk_ref, r, xs_hbm, pos_ref[0, 0, r], sem).start()
        _row_copy(pk_ref, r, xs_hbm, pos_ref[0, 0, tm + r], sem).start()
        return c

    lax.fori_loop(0, tm, issue, 0, unroll=8)

    def wait(r, c):
        _row_copy(pk_ref, r, xs_hbm, 0, sem).wait()
        _row_copy(pk_ref, r, xs_hbm, 0, sem).wait()
        return c

    lax.fori_loop(0, tm, wait, 0, unroll=8)


def moe_dispatch(pos, x, g, xs, *, tm):
    m = x.shape[0]
    return pl.pallas_call(
        _moe_dispatch_kernel,
        grid=(m // tm,),
        in_specs=[pl.BlockSpec((1, 1, 2 * tm), lambda i: (i, 0, 0), memory_space=pltpu.SMEM),
                  pl.BlockSpec((tm, D_MODEL), lambda i: (i, 0)),
                  pl.BlockSpec((1, D_MODEL), lambda i: (0, 0)),
                  pl.BlockSpec(memory_space=pl.ANY)],
        out_specs=pl.BlockSpec(memory_space=pl.ANY),
        out_shape=jax.ShapeDtypeStruct(xs.shape, xs.dtype),
        scratch_shapes=[pltpu.VMEM((tm, D_MODEL // 2), jnp.uint32), pltpu.SemaphoreType.DMA],
        input_output_aliases={3: 0},
        compiler_params=_cparams(("arbitrary",)),
        name="moe_dispatch",
    )(pos, x, g.reshape(1, -1), xs)


def _moe_ffn_kernel(te_ref, tv_ref, xs_ref, wg_ref, wu_ref, wd_ref, o_ref, hn_ref):
    t = pl.program_id(0)
    f = pl.program_id(1)
    rows = xs_ref.shape[0]
    half = D_MODEL // 2

    @pl.when(f == 0)
    def _():
        o_ref[...] = jnp.zeros_like(o_ref)
        xu = xs_ref[...]
        hn_ref[:, :half] = lax.bitcast_convert_type(xu << 16, F32).astype(BF16)
        hn_ref[:, half:] = lax.bitcast_convert_type(xu & jnp.uint32(0xFFFF0000), F32).astype(BF16)

    def swiglu(r):
        hn = hn_ref[0:r]
        g = jnp.dot(hn, wg_ref[...].astype(BF16), preferred_element_type=F32)
        u = jnp.dot(hn, wu_ref[...].astype(BF16), preferred_element_type=F32)
        a = (g * jax.nn.sigmoid(g) * u).astype(BF16)
        o_ref[0:r] += jnp.dot(a, wd_ref[...].astype(BF16), preferred_element_type=F32)

    nv = tv_ref[t]

    @pl.when(nv > rows // 2)
    def _():
        swiglu(rows)

    @pl.when(jnp.logical_and(nv > rows // 4, nv <= rows // 2))
    def _():
        swiglu(rows // 2)

    @pl.when(jnp.logical_and(nv > 0, nv <= rows // 4))
    def _():
        swiglu(rows // 4)


def moe_ffn(tile_expert, tile_rows, xs, w_gu, w_d):
    tm = MOE_TM
    nt = xs.shape[0] // tm
    nf = D_FF // MOE_TF
    last = nf - 1
    col = lambda f, tv, t: jnp.where(tv[t] != 0, f, last)
    grid_spec = pltpu.PrefetchScalarGridSpec(
        num_scalar_prefetch=2,
        grid=(nt, nf),
        in_specs=[pl.BlockSpec((tm, D_MODEL // 2), lambda t, f, te, tv: (t, 0), pipeline_mode=pl.Buffered(1)),
                  pl.BlockSpec((None, D_MODEL, MOE_TF), lambda t, f, te, tv: (te[t], 0, col(f, tv, t))),
                  pl.BlockSpec((None, D_MODEL, MOE_TF), lambda t, f, te, tv: (te[t], 0, nf + col(f, tv, t))),
                  pl.BlockSpec((None, MOE_TF, D_MODEL), lambda t, f, te, tv: (te[t], col(f, tv, t), 0))],
        out_specs=pl.BlockSpec((tm, D_MODEL), lambda t, f, te, tv: (t, 0)),
        scratch_shapes=[pltpu.VMEM((tm, D_MODEL), BF16)],
    )
    return pl.pallas_call(
        _moe_ffn_kernel,
        grid_spec=grid_spec,
        out_shape=jax.ShapeDtypeStruct((nt * tm, D_MODEL), F32),
        compiler_params=_cparams(("arbitrary", "arbitrary")),
        name="moe_ffn",
    )(tile_expert, tile_rows, xs, w_gu, w_gu, w_d)


def _moe_combine_kernel(pos_ref, x_ref, w_ref, osort_hbm, o_ref, abuf, sem):
    tm = x_ref.shape[0]

    def issue(r, c):
        _row_copy(osort_hbm, pos_ref[0, 0, r], abuf, r, sem).start()
        return c

    lax.fori_loop(0, 2 * tm, issue, 0, unroll=8)

    def wait(r, c):
        _row_copy(osort_hbm, 0, abuf, r, sem).wait()
        return c

    lax.fori_loop(0, 2 * tm, wait, 0, unroll=8)
    w = w_ref[...]
    o_ref[...] = x_ref[...] + w[:, 0:1] * abuf[0:tm, :] + w[:, 1:2] * abuf[tm:2 * tm, :]


def moe_combine(pos, x, wts, osort, *, tm=256):
    m = x.shape[0]
    return pl.pallas_call(
        _moe_combine_kernel,
        grid=(m // tm,),
        in_specs=[pl.BlockSpec((1, 1, 2 * tm), lambda i: (i, 0, 0), memory_space=pltpu.SMEM),
                  pl.BlockSpec((tm, D_MODEL), lambda i: (i, 0)),
                  pl.BlockSpec((tm, LANES), lambda i: (i, 0)),
                  pl.BlockSpec(memory_space=pl.ANY)],
        out_specs=pl.BlockSpec((tm, D_MODEL), lambda i: (i, 0)),
        out_shape=jax.ShapeDtypeStruct((m, D_MODEL), F32),
        scratch_shapes=[pltpu.VMEM((2 * tm, D_MODEL), F32), pltpu.SemaphoreType.DMA],
        compiler_params=_cparams(("arbitrary",)),
        name="moe_combine",
    )(pos, x, wts, osort)


def _dispatch_plan(counts, ids, rank, tm, nt):
    padded = ((counts + tm - 1) // tm) * tm
    ends = jnp.cumsum(padded)
    offs = ends - padded
    off_of = jnp.zeros_like(ids)
    for e in range(N_EXPERTS):
        off_of = jnp.where(ids == e, offs[e], off_of)
    pos = off_of + rank
    starts = jnp.arange(nt, dtype=jnp.int32) * tm
    tile_expert = jnp.minimum(jnp.sum((starts[:, None] >= ends[None, :]).astype(jnp.int32), axis=1),
                              N_EXPERTS - 1)
    real_end = (offs + counts)[tile_expert]
    tile_rows = jnp.where(starts < ends[-1], jnp.clip(real_end - starts, 0, tm), 0).astype(jnp.int32)
    last_valid = jnp.maximum(jnp.sum((tile_rows != 0).astype(jnp.int32)) - 1, 0)
    tile_expert = jnp.where(tile_rows != 0, tile_expert, tile_expert[last_valid])
    return tile_expert, tile_rows, pos


def _combine_pos(pos, tm):
    m = pos.shape[0]
    return jnp.transpose(pos.reshape(m // tm, tm, 2), (0, 2, 1)).reshape(m // tm, 1, 2 * tm)


def _t5_bucket_np(rel):
    nb = N_BUCKETS // 2
    max_exact = nb // 2
    ret = np.where(rel > 0, nb, 0)
    n = np.abs(rel)
    nf = np.maximum(n, 1).astype(np.float32)
    large = max_exact + (np.log(nf / np.float32(max_exact)) / np.float32(math.log(MAX_DISTANCE / max_exact))
                         * np.float32(nb - max_exact)).astype(np.int32)
    large = np.minimum(large, nb - 1)
    return (ret + np.where(n < max_exact, n, large)).astype(np.int32)


def _bias_tensor(rel_table, n_q, valid):
    rel = np.arange(KEY_TILE, dtype=np.int32)[None, :] - WINDOW - np.arange(n_q, dtype=np.int32)[:, None]
    onehot = np.eye(N_BUCKETS, dtype=np.float32)[:, _t5_bucket_np(rel).reshape(-1)]
    bias = jnp.dot(rel_table.T[HEAD_PERM], jnp.asarray(onehot), precision=lax.Precision.HIGHEST)
    return jnp.where(jnp.asarray(valid)[None], bias.reshape(A_HEADS, n_q, KEY_TILE), NEG).astype(F32)


def _stack_heads(x):
    _, r, c = x.shape
    return jnp.transpose(x.reshape(2, 4, 2, r, c), (0, 2, 1, 3, 4)).reshape(2, 2, 4 * r, c)


def _with_sink_column(bias, sinks_perm):
    return _stack_heads(bias.at[:, :, KEY_TILE - 1].set(sinks_perm[:, None]))


def _prompt_valid():
    qc = np.arange(Q_TILE)[:, None] // CHUNK
    kc = np.arange(KEY_TILE)[None, :] // CHUNK
    return (kc >= qc) & (kc <= qc + WINDOW // CHUNK)


def _sample_valid(rows):
    return np.broadcast_to(np.arange(KEY_TILE)[None, :] < WINDOW + rows, (rows, KEY_TILE))


def _mixer_weights(l, g_mix, w_in, b_gate, q_norm_g, k_norm_g, sinks, w_conv_b, w_conv_c, b_conv_c,
                   ln_c_g, ln_c_b, w_proj_a, w_proj_b, w_proj_c, w_out):
    w = w_in[l]
    wq = w[:, :A_WIDTH].reshape(D_MODEL, A_HEADS, A_HEAD_DIM)[:, HEAD_PERM].reshape(D_MODEL, A_WIDTH)
    w_in_bf = jnp.concatenate([wq, w[:, A_WIDTH:]], axis=1).astype(BF16)
    wpa = w_proj_a[l].reshape(A_HEADS, A_HEAD_DIM, D_MODEL)[HEAD_PERM].reshape(A_WIDTH, D_MODEL)
    return dict(
        g_mix=g_mix[l], w_in=w_in_bf, b_gate=b_gate[l],
        qg2=(jnp.tile(q_norm_g[l], 2) * (A_HEAD_DIM ** -0.5)).reshape(1, LANES), kg2=jnp.tile(k_norm_g[l], 2).reshape(1, LANES),
        sinks=sinks[l][HEAD_PERM],
        w_cb=w_conv_b[l], w_cc=w_conv_c[l], b_cc=b_conv_c[l], ln_g=ln_c_g[l], ln_b=ln_c_b[l],
        wpa=wpa.astype(BF16), wpb=w_proj_b[l].astype(BF16), wpc=w_proj_c[l].astype(BF16),
        wo=w_out[l].astype(BF16))


def _mixer_prompt(x, mw, bias, batch, seq):
    z = norm_matmul(x, mw["g_mix"], mw["w_in"], tm=1024, tn=2048, out_dtype=BF16)
    oa, nk, nv = swa_prompt(z, batch, seq, mw["qg2"], mw["kg2"], bias, mw["sinks"])
    ob, oc, ncb, ncc = conv_prompt(z, batch, seq, mw["w_cb"], mw["w_cc"], mw["b_cc"], mw["ln_g"], mw["ln_b"])
    y = merge(oa, ob, oc, z, mw["b_gate"], mw["wpa"], mw["wpb"], mw["wpc"], mw["wo"], x)
    return y, (nk, nv, ncb[:, 8 - (B_CONV - 1):], ncc[:, HALO - (C_CONV - 1):])


def _mixer_sample(x, mw, bias, batch, rows, cache_k, cache_v, layer, st_b, st_c):
    z = norm_matmul(x, mw["g_mix"], mw["w_in"], tm=x.shape[0], tn=1024, out_dtype=BF16)
    oa, nk, nv = swa_sample(z, batch, rows, cache_k, cache_v, layer, mw["qg2"], mw["kg2"],
                            _with_sink_column(bias, mw["sinks"]))
    stb = jnp.pad(st_b, ((0, 0), (HALO - (B_CONV - 1), 0), (0, 0)))
    stc = jnp.pad(st_c, ((0, 0), (HALO - (C_CONV - 1), 0), (0, 0)))
    ob, oc, ncb, ncc = conv_sample(z, batch, rows, stb, stc, mw["w_cb"], mw["w_cc"], mw["b_cc"],
                                   mw["ln_g"], mw["ln_b"])
    y = merge(oa, ob, oc, z, mw["b_gate"], mw["wpa"], mw["wpb"], mw["wpc"], mw["wo"], x)
    return y, (nk, nv, ncb[:, 8 - (B_CONV - 1):], ncc[:, HALO - (C_CONV - 1):])


def _channel_mixer(yp, ys, l, g_ffn, w_ffn_gu, w_ffn_d, w_router, b_router, w_moe_gu, w_moe_d):
    if l % 2 == 0:
        return (ffn(yp, g_ffn[l], w_ffn_gu[l // 2], w_ffn_d[l // 2]),
                ffn(ys, g_ffn[l], w_ffn_gu[l // 2], w_ffn_d[l // 2]))
    i = l // 2
    w_pad = jnp.pad(w_router[i], ((0, 0), (0, LANES - N_EXPERTS)))
    b_pad = jnp.pad(b_router[i], (0, LANES - N_EXPERTS)).reshape(1, LANES)
    wts_p, ids_p, rank_p, cnt_p = router(yp, g_ffn[l], w_pad, b_pad, jnp.zeros((1, LANES), F32))
    wts_s, ids_s, rank_s, cnt = router(ys, g_ffn[l], w_pad, b_pad, cnt_p)
    mp, ms = yp.shape[0], ys.shape[0]
    nt = (2 * (mp + ms)) // MOE_TM + N_EXPERTS
    ids = jnp.concatenate([ids_p[:, :2], ids_s[:, :2]], axis=0)
    rank = jnp.concatenate([rank_p[:, :2], rank_s[:, :2]], axis=0)
    tile_expert, tile_rows, pos = _dispatch_plan(cnt[0, :N_EXPERTS].astype(jnp.int32), ids, rank, MOE_TM, nt)
    tm_p, tm_s = 512, min(512, ms)
    pos_p, pos_s = _combine_pos(pos[:mp], tm_p), _combine_pos(pos[mp:], tm_s)
    xs = jnp.zeros((nt * MOE_TM, D_MODEL // 2), jnp.uint32)
    xs = moe_dispatch(pos_p, yp, g_ffn[l], xs, tm=tm_p)
    xs = moe_dispatch(pos_s, ys, g_ffn[l], xs, tm=tm_s)
    osort = moe_ffn(tile_expert, tile_rows, xs, w_moe_gu[i], w_moe_d[i])
    return (moe_combine(pos_p, yp, wts_p, osort, tm=tm_p), moe_combine(pos_s, ys, wts_s, osort, tm=tm_s))


def kernel(x_prompt, x_sample, mem_prompt, cache_mem_k, cache_mem_v, cache_swa_k, cache_swa_v, state_conv_b, state_conv_c, rel_table, g_mix, w_in, b_gate, q_norm_g, k_norm_g, sinks, w_conv_b, w_conv_c, b_conv_c, ln_c_g, ln_c_b, w_proj_a, w_proj_b, w_proj_c, w_out, g_xattn, g_mem, w_xq, w_xkv, xq_norm_g, xk_norm_g, w_xo, g_ffn, w_ffn_gu, w_ffn_d, w_router, b_router, w_moe_gu, w_moe_d):
    batch, seq, d = x_prompt.shape
    dec_batch, dec_seq, _ = x_sample.shape
    depth = g_mix.shape[0]
    yp = x_prompt.reshape(batch * seq, d)
    ys = x_sample.reshape(dec_batch * dec_seq, d)
    mem = mem_prompt.reshape(batch * N_MEM, d)
    first = np.arange(KEY_TILE)[None, :] >= Q_TILE
    bias_p = jnp.stack([_stack_heads(_bias_tensor(rel_table, Q_TILE, _prompt_valid() & first)),
                        _stack_heads(_bias_tensor(rel_table, Q_TILE, _prompt_valid()))])
    bias_s = _bias_tensor(rel_table, dec_seq, _sample_valid(dec_seq))
    outs = [[] for _ in range(10)]
    ck = cache_swa_k.reshape(depth, dec_batch, WINDOW, A_KV_WIDTH)
    cv = cache_swa_v.reshape(depth, dec_batch, WINDOW, A_KV_WIDTH)
    cmk = cache_mem_k.reshape(depth, dec_batch, N_MEM, X_WIDTH)
    cmv = cache_mem_v.reshape(depth, dec_batch, N_MEM, X_WIDTH)
    for l in range(depth):
        mw = _mixer_weights(l, g_mix, w_in, b_gate, q_norm_g, k_norm_g, sinks, w_conv_b, w_conv_c,
                            b_conv_c, ln_c_g, ln_c_b, w_proj_a, w_proj_b, w_proj_c, w_out)
        wq = w_xq[l].astype(BF16)
        wo = w_xo[l].astype(BF16)
        ffn_args = (g_ffn, w_ffn_gu, w_ffn_d, w_router, b_router, w_moe_gu, w_moe_d)
        yp, (nk, nv, ncb, ncc) = _mixer_prompt(yp, mw, bias_p, batch, seq)
        mk, mv = mem_kv(mem, g_mem[l], w_xkv[l].astype(BF16), xk_norm_g[l])
        mk3 = mk.reshape(batch, N_MEM, X_WIDTH)
        mv3 = mv.reshape(batch, N_MEM, X_WIDTH)
        yp = xattn(yp, g_xattn[l], wq, xq_norm_g[l], mk3[None], mv3[None], wo, layer=0, nb=1, rpb=512,
                   tiles_per_mem=seq // 512)
        for lst, v in zip(outs[:6], (mk3.reshape(batch, N_MEM, X_HEADS, X_HEAD_DIM),
                                     mv3.reshape(batch, N_MEM, X_HEADS, X_HEAD_DIM),
                                     nk.reshape(batch, WINDOW, A_KV_HEADS, A_HEAD_DIM),
                                     nv.reshape(batch, WINDOW, A_KV_HEADS, A_HEAD_DIM), ncb, ncc)):
            lst.append(v)
        ys, (nk, nv, ncb, ncc) = _mixer_sample(ys, mw, bias_s, dec_batch, dec_seq, ck, cv, l,
                                               state_conv_b[l], state_conv_c[l])
        ys = xattn(ys, g_xattn[l], wq, xq_norm_g[l], cmk, cmv, wo, layer=l, nb=8, rpb=dec_seq,
                   tiles_per_mem=1)
        yp, ys = _channel_mixer(yp, ys, l, *ffn_args)
        for lst, v in zip(outs[6:], (nk.reshape(dec_batch, dec_seq, A_KV_HEADS, A_HEAD_DIM),
                                     nv.reshape(dec_batch, dec_seq, A_KV_HEADS, A_HEAD_DIM), ncb, ncc)):
            lst.append(v)
    return (yp.reshape(batch, seq, d), ys.reshape(dec_batch, dec_seq, d)) + tuple(jnp.stack(o) for o in outs)
```

```python
import functools
import math

import numpy as np
import jax
import jax.numpy as jnp
from jax import lax
from jax.experimental import pallas as pl
from jax.experimental.pallas import tpu as pltpu

F32 = jnp.float32
BF16 = jnp.bfloat16

D_MODEL = 2048
CHUNK = 64
A_HEADS = 16
A_KV_HEADS = 4
A_HEAD_DIM = 64
A_WIDTH = A_HEADS * A_HEAD_DIM
A_KV_WIDTH = A_KV_HEADS * A_HEAD_DIM
WINDOW = 128
N_BUCKETS = 32
MAX_DISTANCE = 128
B_WIDTH = 512
B_CONV = 3
C_WIDTH = 512
C_CONV = 31
N_MEM = 256
X_HEADS = 4
X_HEAD_DIM = 128
X_WIDTH = X_HEADS * X_HEAD_DIM
D_FF = 5632
N_EXPERTS = 8
EPS = 1e-6

LANES = 128
KEY_TILE = 256
Q_TILE = 128
HALO = 32
NEG = -1e30
VMEM_LIMIT = 56 * 1024 * 1024

COL_Q = 0
COL_K, COL_V = 4, 5
COL_GB, COL_GC, COL_HB, COL_GA, COL_GG = 3, 4, 5, 6, 7
COL_GATE0 = 2
IN_COLS = 4096 + 3 * D_MODEL

HEAD_PERM = np.array([8 * n + (p % 2) * 4 + p // 2 for n in range(2) for p in range(8)])


def _cparams(sem):
    return pltpu.CompilerParams(dimension_semantics=sem, vmem_limit_bytes=VMEM_LIMIT)


def _rms(x, g):
    ms = jnp.mean(x * x, axis=-1, keepdims=True)
    return x * lax.rsqrt(ms + EPS) * g


def _resident(shape):
    nd = len(shape)
    return pl.BlockSpec(shape, lambda *_: (0,) * nd, pipeline_mode=pl.Buffered(1))


def _norm_matmul_kernel(x_ref, g_ref, w_ref, o_ref, hn_ref):
    @pl.when(pl.program_id(1) == 0)
    def _():
        hn_ref[...] = _rms(x_ref[...], g_ref[...]).astype(BF16)

    o_ref[...] = jnp.dot(hn_ref[...], w_ref[...], preferred_element_type=F32).astype(o_ref.dtype)


def norm_matmul(x, g, w, *, tm, tn, out_dtype):
    m, k = x.shape
    n = w.shape[1]
    return pl.pallas_call(
        _norm_matmul_kernel,
        grid=(m // tm, n // tn),
        in_specs=[pl.BlockSpec((tm, k), lambda i, j: (i, 0)),
                  pl.BlockSpec((1, k), lambda i, j: (0, 0)),
                  pl.BlockSpec((k, tn), lambda i, j: (0, j))],
        out_specs=pl.BlockSpec((tm, tn), lambda i, j: (i, j)),
        out_shape=jax.ShapeDtypeStruct((m, n), out_dtype),
        scratch_shapes=[pltpu.VMEM((tm, k), BF16)],
        compiler_params=_cparams(("parallel", "arbitrary")),
        name="norm_matmul",
    )(x, g.reshape(1, k), w)


def _mem_kv_kernel(x_ref, g_ref, w_ref, kg_ref, k_ref, v_ref):
    hn = _rms(x_ref[...], g_ref[...]).astype(BF16)
    kv = jnp.dot(hn, w_ref[...], preferred_element_type=F32)
    for h in range(X_HEADS):
        sl = slice(h * X_HEAD_DIM, (h + 1) * X_HEAD_DIM)
        k_ref[:, sl] = _rms(kv[:, sl], kg_ref[...])
    v_ref[...] = kv[:, X_WIDTH:]


def mem_kv(mem, g, w_bf, kg):
    m, k = mem.shape
    tm = 256
    return pl.pallas_call(
        _mem_kv_kernel,
        grid=(m // tm,),
        in_specs=[pl.BlockSpec((tm, k), lambda i: (i, 0)),
                  pl.BlockSpec((1, k), lambda i: (0, 0)),
                  _resident((k, 2 * X_WIDTH)),
                  pl.BlockSpec((1, X_HEAD_DIM), lambda i: (0, 0))],
        out_specs=[pl.BlockSpec((tm, X_WIDTH), lambda i: (i, 0)),
                   pl.BlockSpec((tm, X_WIDTH), lambda i: (i, 0))],
        out_shape=[jax.ShapeDtypeStruct((m, X_WIDTH), F32)] * 2,
        compiler_params=_cparams(("parallel",)),
        name="mem_kv",
    )(mem, g.reshape(1, k), w_bf, kg.reshape(1, X_HEAD_DIM))


def _half_norm(x, g):
    x2 = x * x
    lo_lane = lax.broadcasted_iota(jnp.int32, (1, LANES), 1) < A_HEAD_DIM
    s_lo = jnp.sum(jnp.where(lo_lane, x2, 0.0), axis=-1, keepdims=True)
    s_hi = jnp.sum(jnp.where(lo_lane, 0.0, x2), axis=-1, keepdims=True)
    ms = jnp.where(lo_lane, s_lo, s_hi) * (1.0 / A_HEAD_DIM)
    return x * lax.rsqrt(ms + EPS) * g


def _swa_heads(q_ref, qg, k2, v2, bias_ref, sink_ref, o_ref, *, stack):
    lo_lane = lax.broadcasted_iota(jnp.int32, (1, LANES), 1) < A_HEAD_DIM
    rows = q_ref.shape[0]
    cols = A_WIDTH // LANES // 2
    for n in range(2):
        k_half = (jnp.where(lo_lane, k2[n], 0.0).astype(BF16), jnp.where(lo_lane, 0.0, k2[n]).astype(BF16))
        for g in range(0, cols, stack):
            qs = [_half_norm(q_ref[:, c * LANES:(c + 1) * LANES].astype(F32), qg)
                  for c in range(cols * n + g, cols * n + g + stack)]
            qn = (qs[0] if stack == 1 else jnp.concatenate(qs, axis=0)).astype(BF16)
            at = slice(g * rows, (g + stack) * rows)
            halves = []
            for half in range(2):
                s = lax.dot_general(qn, k_half[half], (((1,), (1,)), ((), ())), preferred_element_type=F32)
                s = s + bias_ref[n, half, at, :]
                if sink_ref is None:
                    m = jnp.max(s, axis=-1, keepdims=True)
                    p = jnp.exp(s - m)
                    den = jnp.sum(p, axis=-1, keepdims=True)
                else:
                    sink = sink_ref[2 * (cols * n + g) + half]
                    m = jnp.maximum(jnp.max(s, axis=-1, keepdims=True), sink)
                    p = jnp.exp(s - m)
                    den = jnp.sum(p, axis=-1, keepdims=True) + jnp.exp(sink - m)
                o = jnp.dot(p.astype(BF16), v2[n], preferred_element_type=F32)
                halves.append(o / den)
            o = jnp.where(lo_lane, halves[0], halves[1]).astype(o_ref.dtype)
            for i in range(stack):
                c = cols * n + g + i
                o_ref[:, c * LANES:(c + 1) * LANES] = o[i * rows:(i + 1) * rows]


def _swa_prompt_kernel(q_ref, kc_ref, kp_ref, vc_ref, vp_ref, qg_ref, kg_ref,
                       bias_ref, sink_ref, o_ref, nk_ref, nv_ref):
    k2, v2 = [], []
    for n in range(2):
        sl = slice(n * LANES, (n + 1) * LANES)
        kcat = jnp.concatenate([kp_ref[:, sl], kc_ref[:, sl]], axis=0).astype(F32)
        kn = _half_norm(kcat, kg_ref[...])
        nk_ref[0, :, sl] = kn[Q_TILE:]
        k2.append(kn)
        v2.append(jnp.concatenate([vp_ref[:, sl], vc_ref[:, sl]], axis=0))
    nv_ref[0] = vc_ref[...].astype(F32)
    _swa_heads(q_ref, qg_ref[...], k2, v2, bias_ref, sink_ref, o_ref, stack=1)


def swa_prompt(z, batch, seq, qg2, kg2, bias, sinks):
    nt = seq // Q_TILE
    row = lambda b, t: b * nt + t
    prev = lambda b, t: jnp.maximum(b * nt + t - 1, 0)
    return pl.pallas_call(
        _swa_prompt_kernel,
        grid=(batch, nt),
        in_specs=[pl.BlockSpec((Q_TILE, A_WIDTH), lambda b, t: (row(b, t), COL_Q)),
                  pl.BlockSpec((Q_TILE, A_KV_WIDTH), lambda b, t: (row(b, t), COL_K)),
                  pl.BlockSpec((Q_TILE, A_KV_WIDTH), lambda b, t: (prev(b, t), COL_K)),
                  pl.BlockSpec((Q_TILE, A_KV_WIDTH), lambda b, t: (row(b, t), COL_V)),
                  pl.BlockSpec((Q_TILE, A_KV_WIDTH), lambda b, t: (prev(b, t), COL_V)),
                  pl.BlockSpec((1, LANES), lambda b, t: (0, 0)),
                  pl.BlockSpec((1, LANES), lambda b, t: (0, 0)),
                  pl.BlockSpec((None, 2, 2, 4 * Q_TILE, KEY_TILE),
                               lambda b, t: (jnp.minimum(t, 1), 0, 0, 0, 0)),
                  pl.BlockSpec(memory_space=pltpu.SMEM)],
        out_specs=[pl.BlockSpec((Q_TILE, A_WIDTH), lambda b, t: (row(b, t), 0)),
                   pl.BlockSpec((1, WINDOW, A_KV_WIDTH), lambda b, t: (b, 0, 0)),
                   pl.BlockSpec((1, WINDOW, A_KV_WIDTH), lambda b, t: (b, 0, 0))],
        out_shape=[jax.ShapeDtypeStruct((batch * seq, A_WIDTH), BF16),
                   jax.ShapeDtypeStruct((batch, WINDOW, A_KV_WIDTH), F32),
                   jax.ShapeDtypeStruct((batch, WINDOW, A_KV_WIDTH), F32)],
        compiler_params=_cparams(("parallel", "arbitrary")),
        name="swa_prompt",
    )(z, z, z, z, z, qg2, kg2, bias, sinks)


def _swa_sample_kernel(q_ref, kn_ref, vn_ref, ck_ref, cv_ref, qg_ref, kg_ref,
                       bias_ref, o_ref, nk_ref, nv_ref):
    rows = q_ref.shape[0]
    pad = KEY_TILE - WINDOW - rows
    k2, v2 = [], []
    for n in range(2):
        sl = slice(n * LANES, (n + 1) * LANES)
        kn = _half_norm(kn_ref[:, sl].astype(F32), kg_ref[...])
        nk_ref[0, :, sl] = kn
        k2.append(jnp.concatenate([ck_ref[0, :, sl], kn, jnp.zeros((pad, LANES), F32)], axis=0))
        v2.append(jnp.concatenate([cv_ref[0, :, sl].astype(BF16), vn_ref[:, sl],
                                   jnp.zeros((pad, LANES), BF16)], axis=0))
    nv_ref[0] = vn_ref[...].astype(F32)
    _swa_heads(q_ref, qg_ref[...], k2, v2, bias_ref, None, o_ref, stack=4)


def swa_sample(z, batch, rows, cache_k, cache_v, layer, qg2, kg2, bias):
    return pl.pallas_call(
        _swa_sample_kernel,
        grid=(batch,),
        in_specs=[pl.BlockSpec((rows, A_WIDTH), lambda b: (b, COL_Q)),
                  pl.BlockSpec((rows, A_KV_WIDTH), lambda b: (b, COL_K)),
                  pl.BlockSpec((rows, A_KV_WIDTH), lambda b: (b, COL_V)),
                  pl.BlockSpec((None, 1, WINDOW, A_KV_WIDTH), lambda b: (layer, b, 0, 0)),
                  pl.BlockSpec((None, 1, WINDOW, A_KV_WIDTH), lambda b: (layer, b, 0, 0)),
                  pl.BlockSpec((1, LANES), lambda b: (0, 0)),
                  pl.BlockSpec((1, LANES), lambda b: (0, 0)),
                  _resident((2, 2, 4 * rows, KEY_TILE))],
        out_specs=[pl.BlockSpec((rows, A_WIDTH), lambda b: (b, 0)),
                   pl.BlockSpec((1, rows, A_KV_WIDTH), lambda b: (b, 0, 0)),
                   pl.BlockSpec((1, rows, A_KV_WIDTH), lambda b: (b, 0, 0))],
        out_shape=[jax.ShapeDtypeStruct((batch * rows, A_WIDTH), BF16),
                   jax.ShapeDtypeStruct((batch, rows, A_KV_WIDTH), F32),
                   jax.ShapeDtypeStruct((batch, rows, A_KV_WIDTH), F32)],
        compiler_params=_cparams(("parallel",)),
        name="swa_sample",
    )(z, z, z, cache_k, cache_v, qg2, kg2, bias)


def _conv_body(gb_ref, ub_main, uc_main, ub_halo, uc_halo, wb_ref, wc_ref, bc_ref, lg_ref, lb_ref,
               ob_ref, oc_ref, nb_ref, nc_ref, sb_ref, sc_ref, ph_ref, write_state):
    rows = ub_main.shape[0]
    sb_ref[0:HALO] = ub_halo
    sb_ref[HALO:HALO + rows] = ub_main
    sc_ref[0:HALO] = uc_halo
    sc_ref[HALO:HALO + rows] = uc_main
    span = rows + HALO
    sc_ref[span:span + 8] = jnp.zeros((8, C_WIDTH), F32)
    for b in range(1, 8):
        ph_ref[b - 1] = sc_ref[pl.ds(b, span), :]
    sub = min(rows, 32)
    for r0 in range(0, rows, sub):
        yb = jnp.zeros((sub, B_WIDTH), F32)
        for k in range(B_CONV):
            yb = yb + wb_ref[k:k + 1, :] * sb_ref[pl.ds(r0 + HALO - (B_CONV - 1) + k, sub), :]
        ob_ref[r0:r0 + sub, :] = (gb_ref[r0:r0 + sub, :].astype(F32) * yb).astype(ob_ref.dtype)
        yc = jnp.zeros((sub, C_WIDTH), F32)
        for k in range(C_CONV):
            shift, phase = divmod(HALO - (C_CONV - 1) + k, 8)
            at = pl.ds(r0 + 8 * shift, sub)
            taps = sc_ref[at, :] if phase == 0 else ph_ref[phase - 1, at, :]
            yc = yc + wc_ref[k:k + 1, :] * taps
        yc = yc + bc_ref[...]
        mu = jnp.mean(yc, axis=-1, keepdims=True)
        xc = yc - mu
        y = xc * lax.rsqrt(jnp.mean(xc * xc, axis=-1, keepdims=True) + EPS)
        y = y * lg_ref[...] + lb_ref[...]
        oc_ref[r0:r0 + sub, :] = (y * jax.nn.sigmoid(y)).astype(oc_ref.dtype)

    def _state():
        nb_ref[0] = sb_ref[rows + HALO - 8:rows + HALO]
        nc_ref[0] = sc_ref[rows:rows + HALO]

    write_state(_state)


def _conv_prompt_kernel(gb_ref, gc_ref, hb_ref, ga_ref, gg_ref, gch_ref, hbh_ref, gah_ref, ggh_ref,
                        wb_ref, wc_ref, bc_ref, lg_ref, lb_ref,
                        ob_ref, oc_ref, nb_ref, nc_ref, sb_ref, sc_ref, ph_ref):
    t = pl.program_id(1)
    hist = (t > 0).astype(F32)
    ub_main = gc_ref[...].astype(F32) * hb_ref[...].astype(F32)
    uc_main = ga_ref[...].astype(F32) * jax.nn.sigmoid(gg_ref[...].astype(F32))
    ub_halo = gch_ref[...].astype(F32) * hbh_ref[...].astype(F32) * hist
    uc_halo = gah_ref[...].astype(F32) * jax.nn.sigmoid(ggh_ref[...].astype(F32)) * hist
    last = pl.num_programs(1) - 1
    _conv_body(gb_ref, ub_main, uc_main, ub_halo, uc_halo, wb_ref, wc_ref, bc_ref, lg_ref, lb_ref,
               ob_ref, oc_ref, nb_ref, nc_ref, sb_ref, sc_ref, ph_ref,
               lambda f: pl.when(t == last)(f))


def conv_prompt(z, batch, seq, wb, wc, bc, lg, lb, *, tr=256):
    nt = seq // tr
    hp = tr // HALO
    main = lambda c: pl.BlockSpec((tr, B_WIDTH), lambda b, t: (b * nt + t, c))
    halo = lambda c: pl.BlockSpec((HALO, B_WIDTH), lambda b, t: (jnp.maximum((b * nt + t) * hp - 1, 0), c))
    vec = lambda r: pl.BlockSpec((r, B_WIDTH), lambda b, t: (0, 0))
    return pl.pallas_call(
        _conv_prompt_kernel,
        grid=(batch, nt),
        in_specs=[main(COL_GB), main(COL_GC), main(COL_HB), main(COL_GA), main(COL_GG),
                  halo(COL_GC), halo(COL_HB), halo(COL_GA), halo(COL_GG),
                  vec(B_CONV), vec(C_CONV), vec(1), vec(1), vec(1)],
        out_specs=[pl.BlockSpec((tr, B_WIDTH), lambda b, t: (b * nt + t, 0)),
                   pl.BlockSpec((tr, C_WIDTH), lambda b, t: (b * nt + t, 0)),
                   pl.BlockSpec((1, 8, B_WIDTH), lambda b, t: (b, 0, 0)),
                   pl.BlockSpec((1, HALO, C_WIDTH), lambda b, t: (b, 0, 0))],
        out_shape=[jax.ShapeDtypeStruct((batch * seq, B_WIDTH), BF16),
                   jax.ShapeDtypeStruct((batch * seq, C_WIDTH), BF16),
                   jax.ShapeDtypeStruct((batch, 8, B_WIDTH), F32),
                   jax.ShapeDtypeStruct((batch, HALO, C_WIDTH), F32)],
        scratch_shapes=[pltpu.VMEM((tr + HALO, B_WIDTH), F32), pltpu.VMEM((tr + HALO + 8, C_WIDTH), F32),
                        pltpu.VMEM((7, tr + HALO, C_WIDTH), F32)],
        compiler_params=_cparams(("parallel", "arbitrary")),
        name="conv_prompt",
    )(z, z, z, z, z, z, z, z, z, wb, wc, bc.reshape(1, -1), lg.reshape(1, -1), lb.reshape(1, -1))


def _conv_sample_kernel(gb_ref, gc_ref, hb_ref, ga_ref, gg_ref, stb_ref, stc_ref,
                        wb_ref, wc_ref, bc_ref, lg_ref, lb_ref,
                        ob_ref, oc_ref, nb_ref, nc_ref, sb_ref, sc_ref, ph_ref):
    ub_main = gc_ref[...].astype(F32) * hb_ref[...].astype(F32)
    uc_main = ga_ref[...].astype(F32) * jax.nn.sigmoid(gg_ref[...].astype(F32))
    _conv_body(gb_ref, ub_main, uc_main, stb_ref[0], stc_ref[0], wb_ref, wc_ref, bc_ref, lg_ref, lb_ref,
               ob_ref, oc_ref, nb_ref, nc_ref, sb_ref, sc_ref, ph_ref, lambda f: f())


def conv_sample(z, batch, rows, stb, stc, wb, wc, bc, lg, lb):
    main = lambda c: pl.BlockSpec((rows, B_WIDTH), lambda b: (b, c))
    vec = lambda r: pl.BlockSpec((r, B_WIDTH), lambda b: (0, 0))
    return pl.pallas_call(
        _conv_sample_kernel,
        grid=(batch,),
        in_specs=[main(COL_GB), main(COL_GC), main(COL_HB), main(COL_GA), main(COL_GG),
                  pl.BlockSpec((1, HALO, B_WIDTH), lambda b: (b, 0, 0)),
                  pl.BlockSpec((1, HALO, C_WIDTH), lambda b: (b, 0, 0)),
                  vec(B_CONV), vec(C_CONV), vec(1), vec(1), vec(1)],
        out_specs=[pl.BlockSpec((rows, B_WIDTH), lambda b: (b, 0)),
                   pl.BlockSpec((rows, C_WIDTH), lambda b: (b, 0)),
                   pl.BlockSpec((1, 8, B_WIDTH), lambda b: (b, 0, 0)),
                   pl.BlockSpec((1, HALO, C_WIDTH), lambda b: (b, 0, 0))],
        out_shape=[jax.ShapeDtypeStruct((batch * rows, B_WIDTH), BF16),
                   jax.ShapeDtypeStruct((batch * rows, C_WIDTH), BF16),
                   jax.ShapeDtypeStruct((batch, 8, B_WIDTH), F32),
                   jax.ShapeDtypeStruct((batch, HALO, C_WIDTH), F32)],
        scratch_shapes=[pltpu.VMEM((rows + HALO, B_WIDTH), F32), pltpu.VMEM((rows + HALO + 8, C_WIDTH), F32),
                        pltpu.VMEM((7, rows + HALO, C_WIDTH), F32)],
        compiler_params=_cparams(("parallel",)),
        name="conv_sample",
    )(z, z, z, z, z, stb, stc, wb, wc, bc.reshape(1, -1), lg.reshape(1, -1), lb.reshape(1, -1))


def _merge_kernel(oa_ref, ob_ref, oc_ref, l0_ref, l1_ref, l2_ref, bg_ref,
                  wpa_ref, wpb_ref, wpc_ref, wo_ref, x_ref, o_ref):
    def gated(l_ref, i, o_r, w_r):
        gate = jax.nn.sigmoid(l_ref[...].astype(F32) + bg_ref[i:i + 1, :])
        return gate * jnp.dot(o_r[...], w_r[...], preferred_element_type=F32)

    merged = gated(l0_ref, 0, oa_ref, wpa_ref)
    merged = merged + gated(l1_ref, 1, ob_ref, wpb_ref)
    merged = merged + gated(l2_ref, 2, oc_ref, wpc_ref)
    o_ref[...] = x_ref[...] + jnp.dot(merged.astype(BF16), wo_ref[...], preferred_element_type=F32)


def merge(oa, ob, oc, z, bg, wpa, wpb, wpc, wo, x, *, tm=512):
    m = x.shape[0]
    tm = min(tm, m)
    rows = lambda w: pl.BlockSpec((tm, w), lambda i: (i, 0))
    gate = lambda c: pl.BlockSpec((tm, D_MODEL), lambda i: (i, COL_GATE0 + c))
    return pl.pallas_call(
        _merge_kernel,
        grid=(m // tm,),
        in_specs=[rows(A_WIDTH), rows(B_WIDTH), rows(C_WIDTH), gate(0), gate(1), gate(2),
                  pl.BlockSpec((3, D_MODEL), lambda i: (0, 0)),
                  _resident((A_WIDTH, D_MODEL)), _resident((B_WIDTH, D_MODEL)),
                  _resident((C_WIDTH, D_MODEL)), _resident((D_MODEL, D_MODEL)),
                  rows(D_MODEL)],
        out_specs=rows(D_MODEL),
        out_shape=jax.ShapeDtypeStruct((m, D_MODEL), F32),
        compiler_params=_cparams(("parallel",)),
        name="merge",
    )(oa, ob, oc, z, z, z, bg.reshape(3, D_MODEL), wpa, wpb, wpc, wo, x)


def _xattn_kernel(y_ref, g_ref, wq_ref, qg_ref, mk_ref, mv_ref, wo_ref, o_ref, *, nb, rpb):
    y = y_ref[...]
    hn = _rms(y, g_ref[...]).astype(BF16)
    q = jnp.dot(hn, wq_ref[...], preferred_element_type=F32)
    heads = []
    for h in range(X_HEADS):
        sl = slice(h * X_HEAD_DIM, (h + 1) * X_HEAD_DIM)
        qh = _rms(q[:, sl], qg_ref[...])
        per_batch = []
        for b in range(nb):
            qb = qh[b * rpb:(b + 1) * rpb].astype(BF16)
            kh = mk_ref[b, :, sl].astype(BF16)
            vh = mv_ref[b, :, sl].astype(BF16)
            s = lax.dot_general(qb, kh, (((1,), (1,)), ((), ())),
                                preferred_element_type=F32) * (X_HEAD_DIM ** -0.5)
            m = jnp.max(s, axis=-1, keepdims=True)
            p = jnp.exp(s - m)
            den = jnp.sum(p, axis=-1, keepdims=True)
            per_batch.append(jnp.dot(p.astype(BF16), vh, preferred_element_type=F32) / den)
        heads.append(per_batch[0] if nb == 1 else jnp.concatenate(per_batch, axis=0))
    o = jnp.concatenate(heads, axis=1).astype(BF16)
    o_ref[...] = y + jnp.dot(o, wo_ref[...], preferred_element_type=F32)


def xattn(y, g, wq, qg, mk, mv, wo, *, layer, nb, rpb, tiles_per_mem):
    m = y.shape[0]
    tm = nb * rpb
    mem_idx = ((lambda i: (layer, i // tiles_per_mem, 0, 0)) if nb == 1
               else (lambda i: (layer, i, 0, 0)))
    return pl.pallas_call(
        functools.partial(_xattn_kernel, nb=nb, rpb=rpb),
        grid=(m // tm,),
        in_specs=[pl.BlockSpec((tm, D_MODEL), lambda i: (i, 0)),
                  pl.BlockSpec((1, D_MODEL), lambda i: (0, 0)),
                  _resident((D_MODEL, X_WIDTH)),
                  pl.BlockSpec((1, X_HEAD_DIM), lambda i: (0, 0)),
                  pl.BlockSpec((None, nb, N_MEM, X_WIDTH), mem_idx),
                  pl.BlockSpec((None, nb, N_MEM, X_WIDTH), mem_idx),
                  _resident((X_WIDTH, D_MODEL))],
        out_specs=pl.BlockSpec((tm, D_MODEL), lambda i: (i, 0)),
        out_shape=jax.ShapeDtypeStruct((m, D_MODEL), F32),
        compiler_params=_cparams(("parallel",)),
        name="xattn",
    )(y, g.reshape(1, -1), wq, qg.reshape(1, -1), mk, mv, wo)


def _ffn_kernel(x_ref, g_ref, wg_ref, wu_ref, wd_ref, o_ref, hn_ref):
    @pl.when(pl.program_id(1) == 0)
    def _():
        x = x_ref[...]
        hn_ref[...] = _rms(x, g_ref[...]).astype(BF16)
        o_ref[...] = x

    hn = hn_ref[...]
    g = jnp.dot(hn, wg_ref[...].astype(BF16), preferred_element_type=F32)
    u = jnp.dot(hn, wu_ref[...].astype(BF16), preferred_element_type=F32)
    a = (g * jax.nn.sigmoid(g) * u).astype(BF16)
    o_ref[...] += jnp.dot(a, wd_ref[...].astype(BF16), preferred_element_type=F32)


def ffn(x, g, w_gu, w_d, *, tm=1024, tf=512):
    m = x.shape[0]
    tm = min(tm, m)
    nf = D_FF // tf
    return pl.pallas_call(
        _ffn_kernel,
        grid=(m // tm, nf),
        in_specs=[pl.BlockSpec((tm, D_MODEL), lambda i, f: (i, 0), pipeline_mode=pl.Buffered(1)),
                  pl.BlockSpec((1, D_MODEL), lambda i, f: (0, 0)),
                  pl.BlockSpec((D_MODEL, tf), lambda i, f: (0, f)),
                  pl.BlockSpec((D_MODEL, tf), lambda i, f: (0, nf + f)),
                  pl.BlockSpec((tf, D_MODEL), lambda i, f: (f, 0))],
        out_specs=pl.BlockSpec((tm, D_MODEL), lambda i, f: (i, 0), pipeline_mode=pl.Buffered(1)),
        out_shape=jax.ShapeDtypeStruct((m, D_MODEL), F32),
        scratch_shapes=[pltpu.VMEM((tm, D_MODEL), BF16)],
        compiler_params=_cparams(("parallel", "arbitrary")),
        name="ffn",
    )(x, g.reshape(1, -1), w_gu, w_gu, w_d)


def _split3(x):
    hi = x.astype(BF16)
    lo = (x - hi.astype(F32)).astype(BF16)
    return hi, lo


def _router_kernel(x_ref, g_ref, w_ref, b_ref, tri_ref, base_ref, wts_ref, ids_ref, rank_ref, cnt_ref, run_ref):
    @pl.when(pl.program_id(0) == 0)
    def _():
        run_ref[...] = base_ref[...]

    hn = _rms(x_ref[...], g_ref[...])
    h_hi, h_lo = _split3(hn)
    w_hi, w_lo = _split3(w_ref[...])
    dot = functools.partial(jnp.dot, preferred_element_type=F32)
    logits = dot(h_hi, w_hi) + dot(h_hi, w_lo) + dot(h_lo, w_hi) + b_ref[...]
    lane = lax.broadcasted_iota(jnp.int32, logits.shape, 1).astype(F32)
    logits = jnp.where(lane < N_EXPERTS, logits, -jnp.inf)
    v1 = jnp.max(logits, axis=-1, keepdims=True)
    i1 = jnp.min(jnp.where(logits == v1, lane, float(LANES)), axis=-1, keepdims=True)
    rest = jnp.where(lane == i1, -jnp.inf, logits)
    v2 = jnp.max(rest, axis=-1, keepdims=True)
    i2 = jnp.min(jnp.where(rest == v2, lane, float(LANES)), axis=-1, keepdims=True)
    e2 = jnp.exp(v2 - v1)
    den = 1.0 + e2
    first, second = lane == 0.0, lane == 1.0
    wts_ref[...] = jnp.where(first, 1.0 / den, 0.0) + jnp.where(second, e2 / den, 0.0)
    ids_ref[...] = (jnp.where(first, i1, 0.0) + jnp.where(second, i2, 0.0)).astype(jnp.int32)
    hit1, hit2 = lane == i1, lane == i2
    hits = jnp.where(jnp.logical_or(hit1, hit2), 1.0, 0.0)
    before = dot(tri_ref[...], hits.astype(BF16)) + run_ref[...]
    r1 = jnp.sum(jnp.where(hit1, before, 0.0), axis=-1, keepdims=True)
    r2 = jnp.sum(jnp.where(hit2, before, 0.0), axis=-1, keepdims=True)
    rank_ref[...] = (jnp.where(first, r1, 0.0) + jnp.where(second, r2, 0.0)).astype(jnp.int32)
    run_ref[...] += jnp.sum(hits, axis=0, keepdims=True)
    cnt_ref[...] = run_ref[...]


def router(x, g, w_pad, b_pad, base, *, tm=512):
    m = x.shape[0]
    tm = min(tm, m)
    tri = jnp.asarray(np.tril(np.ones((tm, tm), np.float32), -1), BF16)
    row = lambda w: pl.BlockSpec((tm, w), lambda i: (i, 0))
    fixed = lambda s: pl.BlockSpec(s, lambda i: (0, 0))
    return pl.pallas_call(
        _router_kernel,
        grid=(m // tm,),
        in_specs=[row(D_MODEL), fixed((1, D_MODEL)), fixed((D_MODEL, LANES)), fixed((1, LANES)),
                  fixed((tm, tm)), fixed((1, LANES))],
        out_specs=[row(LANES), row(LANES), row(LANES), fixed((1, LANES))],
        out_shape=[jax.ShapeDtypeStruct((m, LANES), F32),
                   jax.ShapeDtypeStruct((m, LANES), jnp.int32),
                   jax.ShapeDtypeStruct((m, LANES), jnp.int32),
                   jax.ShapeDtypeStruct((1, LANES), F32)],
        scratch_shapes=[pltpu.VMEM((1, LANES), F32)],
        compiler_params=_cparams(("arbitrary",)),
        name="router",
    )(x, g.reshape(1, -1), w_pad, b_pad, tri, base)


MOE_TF = 512
MOE_TILES_PER_EXPERT = 4
MOE_TILE_SLACK = 1.03
BF16_ROWS = 16


def _round_up(x, m):
    return -(-int(x) // m) * m


def _moe_tile_rows(tokens):
    per_expert = 2 * tokens / N_EXPERTS
    return _round_up(per_expert / MOE_TILES_PER_EXPERT * MOE_TILE_SLACK, BF16_ROWS)


def _row_copy(src, row, dst, r, sem):
    return pltpu.make_async_copy(src.at[pl.ds(row, 1)], dst.at[pl.ds(r, 1)], sem)


def _moe_dispatch_kernel(pos_ref, x_ref, g_ref, xs_in, xs_hbm, pk_ref, sem):
    del xs_in
    tm = x_ref.shape[0]
    half = D_MODEL // 2
    hn = _rms(x_ref[...], g_ref[...]).astype(BF16)
    lo_bits = lax.bitcast_convert_type(hn[:, :half].astype(F32), jnp.uint32)
    hi_bits = lax.bitcast_convert_type(hn[:, half:].astype(F32), jnp.uint32)
    pk_ref[...] = (lo_bits >> 16) | (hi_bits & jnp.uint32(0xFFFF0000))

    def issue(r, c):
        _row_copy(pk_ref, r, xs_hbm, pos_ref[0, 0, r], sem).start()
        _row_copy(pk_ref, r, xs_hbm, pos_ref[0, 0, tm + r], sem).start()
        return c

    lax.fori_loop(0, tm, issue, 0, unroll=8)

    def wait(r, c):
        _row_copy(pk_ref, r, xs_hbm, 0, sem).wait()
        _row_copy(pk_ref, r, xs_hbm, 0, sem).wait()
        return c

    lax.fori_loop(0, tm, wait, 0, unroll=8)


def moe_dispatch(pos, x, g, xs, *, tm):
    m = x.shape[0]
    return pl.pallas_call(
        _moe_dispatch_kernel,
        grid=(m // tm,),
        in_specs=[pl.BlockSpec((1, 1, 2 * tm), lambda i: (i, 0, 0), memory_space=pltpu.SMEM),
                  pl.BlockSpec((tm, D_MODEL), lambda i: (i, 0)),
                  pl.BlockSpec((1, D_MODEL), lambda i: (0, 0)),
                  pl.BlockSpec(memory_space=pl.ANY)],
        out_specs=pl.BlockSpec(memory_space=pl.ANY),
        out_shape=jax.ShapeDtypeStruct(xs.shape, xs.dtype),
        scratch_shapes=[pltpu.VMEM((tm, D_MODEL // 2), jnp.uint32), pltpu.SemaphoreType.DMA],
        input_output_aliases={3: 0},
        compiler_params=_cparams(("arbitrary",)),
        name="moe_dispatch",
    )(pos, x, g.reshape(1, -1), xs)


def _moe_ffn_kernel(te_ref, tv_ref, xs_ref, wg_ref, wu_ref, wd_ref, o_ref, hn_ref):
    t = pl.program_id(0)
    f = pl.program_id(1)
    rows = xs_ref.shape[0]
    half = D_MODEL // 2

    @pl.when(f == 0)
    def _():
        o_ref[...] = jnp.zeros_like(o_ref)
        xu = xs_ref[...]
        hn_ref[:, :half] = lax.bitcast_convert_type(xu << 16, F32).astype(BF16)
        hn_ref[:, half:] = lax.bitcast_convert_type(xu & jnp.uint32(0xFFFF0000), F32).astype(BF16)

    def swiglu(r):
        hn = hn_ref[0:r]
        g = jnp.dot(hn, wg_ref[...].astype(BF16), preferred_element_type=F32)
        u = jnp.dot(hn, wu_ref[...].astype(BF16), preferred_element_type=F32)
        a = (g * jax.nn.sigmoid(g) * u).astype(BF16)
        o_ref[0:r] += jnp.dot(a, wd_ref[...].astype(BF16), preferred_element_type=F32)

    nv = tv_ref[t]
    half_rows = _round_up(rows / 2, BF16_ROWS)
    quarter_rows = _round_up(rows / 4, BF16_ROWS)

    @pl.when(nv > half_rows)
    def _():
        swiglu(rows)

    @pl.when(jnp.logical_and(nv > quarter_rows, nv <= half_rows))
    def _():
        swiglu(half_rows)

    @pl.when(jnp.logical_and(nv > 0, nv <= quarter_rows))
    def _():
        swiglu(quarter_rows)


def moe_ffn(tile_expert, tile_rows, xs, w_gu, w_d, *, tm):
    nt = xs.shape[0] // tm
    nf = D_FF // MOE_TF
    last = nf - 1
    col = lambda f, tv, t: jnp.where(tv[t] != 0, f, last)
    grid_spec = pltpu.PrefetchScalarGridSpec(
        num_scalar_prefetch=2,
        grid=(nt, nf),
        in_specs=[pl.BlockSpec((tm, D_MODEL // 2), lambda t, f, te, tv: (t, 0), pipeline_mode=pl.Buffered(1)),
                  pl.BlockSpec((None, D_MODEL, MOE_TF), lambda t, f, te, tv: (te[t], 0, col(f, tv, t))),
                  pl.BlockSpec((None, D_MODEL, MOE_TF), lambda t, f, te, tv: (te[t], 0, nf + col(f, tv, t))),
                  pl.BlockSpec((None, MOE_TF, D_MODEL), lambda t, f, te, tv: (te[t], col(f, tv, t), 0))],
        out_specs=pl.BlockSpec((tm, D_MODEL), lambda t, f, te, tv: (t, 0)),
        scratch_shapes=[pltpu.VMEM((tm, D_MODEL), BF16)],
    )
    return pl.pallas_call(
        _moe_ffn_kernel,
        grid_spec=grid_spec,
        out_shape=jax.ShapeDtypeStruct((nt * tm, D_MODEL), F32),
        compiler_params=_cparams(("arbitrary", "arbitrary")),
        name="moe_ffn",
    )(tile_expert, tile_rows, xs, w_gu, w_gu, w_d)


def _moe_combine_kernel(pos_ref, x_ref, w_ref, osort_hbm, o_ref, abuf, sem):
    tm = x_ref.shape[0]

    def issue(r, c):
        _row_copy(osort_hbm, pos_ref[0, 0, r], abuf, r, sem).start()
        return c

    lax.fori_loop(0, 2 * tm, issue, 0, unroll=8)

    def wait(r, c):
        _row_copy(osort_hbm, 0, abuf, r, sem).wait()
        return c

    lax.fori_loop(0, 2 * tm, wait, 0, unroll=8)
    w = w_ref[...]
    o_ref[...] = x_ref[...] + w[:, 0:1] * abuf[0:tm, :] + w[:, 1:2] * abuf[tm:2 * tm, :]


def moe_combine(pos, x, wts, osort, *, tm=256):
    m = x.shape[0]
    return pl.pallas_call(
        _moe_combine_kernel,
        grid=(m // tm,),
        in_specs=[pl.BlockSpec((1, 1, 2 * tm), lambda i: (i, 0, 0), memory_space=pltpu.SMEM),
                  pl.BlockSpec((tm, D_MODEL), lambda i: (i, 0)),
                  pl.BlockSpec((tm, LANES), lambda i: (i, 0)),
                  pl.BlockSpec(memory_space=pl.ANY)],
        out_specs=pl.BlockSpec((tm, D_MODEL), lambda i: (i, 0)),
        out_shape=jax.ShapeDtypeStruct((m, D_MODEL), F32),
        scratch_shapes=[pltpu.VMEM((2 * tm, D_MODEL), F32), pltpu.SemaphoreType.DMA],
        compiler_params=_cparams(("arbitrary",)),
        name="moe_combine",
    )(pos, x, wts, osort)


def _dispatch_plan(counts, ids, rank, tm, nt):
    padded = ((counts + tm - 1) // tm) * tm
    ends = jnp.cumsum(padded)
    offs = ends - padded
    off_of = jnp.zeros_like(ids)
    for e in range(N_EXPERTS):
        off_of = jnp.where(ids == e, offs[e], off_of)
    pos = off_of + rank
    starts = jnp.arange(nt, dtype=jnp.int32) * tm
    tile_expert = jnp.minimum(jnp.sum((starts[:, None] >= ends[None, :]).astype(jnp.int32), axis=1),
                              N_EXPERTS - 1)
    real_end = (offs + counts)[tile_expert]
    tile_rows = jnp.where(starts < ends[-1], jnp.clip(real_end - starts, 0, tm), 0).astype(jnp.int32)
    last_valid = jnp.maximum(jnp.sum((tile_rows != 0).astype(jnp.int32)) - 1, 0)
    tile_expert = jnp.where(tile_rows != 0, tile_expert, tile_expert[last_valid])
    return tile_expert, tile_rows, pos


def _combine_pos(pos, tm):
    m = pos.shape[0]
    return jnp.transpose(pos.reshape(m // tm, tm, 2), (0, 2, 1)).reshape(m // tm, 1, 2 * tm)


def _t5_bucket_np(rel):
    nb = N_BUCKETS // 2
    max_exact = nb // 2
    ret = np.where(rel > 0, nb, 0)
    n = np.abs(rel)
    nf = np.maximum(n, 1).astype(np.float32)
    large = max_exact + (np.log(nf / np.float32(max_exact)) / np.float32(math.log(MAX_DISTANCE / max_exact))
                         * np.float32(nb - max_exact)).astype(np.int32)
    large = np.minimum(large, nb - 1)
    return (ret + np.where(n < max_exact, n, large)).astype(np.int32)


def _bias_tensor(rel_table, n_q, valid):
    rel = np.arange(KEY_TILE, dtype=np.int32)[None, :] - WINDOW - np.arange(n_q, dtype=np.int32)[:, None]
    onehot = np.eye(N_BUCKETS, dtype=np.float32)[:, _t5_bucket_np(rel).reshape(-1)]
    bias = jnp.dot(rel_table.T[HEAD_PERM], jnp.asarray(onehot), precision=lax.Precision.HIGHEST)
    return jnp.where(jnp.asarray(valid)[None], bias.reshape(A_HEADS, n_q, KEY_TILE), NEG).astype(F32)


def _stack_heads(x):
    _, r, c = x.shape
    return jnp.transpose(x.reshape(2, 4, 2, r, c), (0, 2, 1, 3, 4)).reshape(2, 2, 4 * r, c)


def _with_sink_column(bias, sinks_perm):
    return _stack_heads(bias.at[:, :, KEY_TILE - 1].set(sinks_perm[:, None]))


def _prompt_valid():
    qc = np.arange(Q_TILE)[:, None] // CHUNK
    kc = np.arange(KEY_TILE)[None, :] // CHUNK
    return (kc >= qc) & (kc <= qc + WINDOW // CHUNK)


def _sample_valid(rows):
    return np.broadcast_to(np.arange(KEY_TILE)[None, :] < WINDOW + rows, (rows, KEY_TILE))


def _mixer_weights(l, g_mix, w_in, b_gate, q_norm_g, k_norm_g, sinks, w_conv_b, w_conv_c, b_conv_c,
                   ln_c_g, ln_c_b, w_proj_a, w_proj_b, w_proj_c, w_out):
    w = w_in[l]
    wq = w[:, :A_WIDTH].reshape(D_MODEL, A_HEADS, A_HEAD_DIM)[:, HEAD_PERM].reshape(D_MODEL, A_WIDTH)
    w_in_bf = jnp.concatenate([wq, w[:, A_WIDTH:]], axis=1).astype(BF16)
    wpa = w_proj_a[l].reshape(A_HEADS, A_HEAD_DIM, D_MODEL)[HEAD_PERM].reshape(A_WIDTH, D_MODEL)
    return dict(
        g_mix=g_mix[l], w_in=w_in_bf, b_gate=b_gate[l],
        qg2=(jnp.tile(q_norm_g[l], 2) * (A_HEAD_DIM ** -0.5)).reshape(1, LANES), kg2=jnp.tile(k_norm_g[l], 2).reshape(1, LANES),
        sinks=sinks[l][HEAD_PERM],
        w_cb=w_conv_b[l], w_cc=w_conv_c[l], b_cc=b_conv_c[l], ln_g=ln_c_g[l], ln_b=ln_c_b[l],
        wpa=wpa.astype(BF16), wpb=w_proj_b[l].astype(BF16), wpc=w_proj_c[l].astype(BF16),
        wo=w_out[l].astype(BF16))


def _mixer_prompt(x, mw, bias, batch, seq):
    z = norm_matmul(x, mw["g_mix"], mw["w_in"], tm=1024, tn=2048, out_dtype=BF16)
    oa, nk, nv = swa_prompt(z, batch, seq, mw["qg2"], mw["kg2"], bias, mw["sinks"])
    ob, oc, ncb, ncc = conv_prompt(z, batch, seq, mw["w_cb"], mw["w_cc"], mw["b_cc"], mw["ln_g"], mw["ln_b"])
    y = merge(oa, ob, oc, z, mw["b_gate"], mw["wpa"], mw["wpb"], mw["wpc"], mw["wo"], x)
    return y, (nk, nv, ncb[:, 8 - (B_CONV - 1):], ncc[:, HALO - (C_CONV - 1):])


def _mixer_sample(x, mw, bias, batch, rows, cache_k, cache_v, layer, st_b, st_c):
    z = norm_matmul(x, mw["g_mix"], mw["w_in"], tm=x.shape[0], tn=1024, out_dtype=BF16)
    oa, nk, nv = swa_sample(z, batch, rows, cache_k, cache_v, layer, mw["qg2"], mw["kg2"],
                            _with_sink_column(bias, mw["sinks"]))
    stb = jnp.pad(st_b, ((0, 0), (HALO - (B_CONV - 1), 0), (0, 0)))
    stc = jnp.pad(st_c, ((0, 0), (HALO - (C_CONV - 1), 0), (0, 0)))
    ob, oc, ncb, ncc = conv_sample(z, batch, rows, stb, stc, mw["w_cb"], mw["w_cc"], mw["b_cc"],
                                   mw["ln_g"], mw["ln_b"])
    y = merge(oa, ob, oc, z, mw["b_gate"], mw["wpa"], mw["wpb"], mw["wpc"], mw["wo"], x)
    return y, (nk, nv, ncb[:, 8 - (B_CONV - 1):], ncc[:, HALO - (C_CONV - 1):])


def _channel_mixer(yp, ys, l, g_ffn, w_ffn_gu, w_ffn_d, w_router, b_router, w_moe_gu, w_moe_d):
    if l % 2 == 0:
        return (ffn(yp, g_ffn[l], w_ffn_gu[l // 2], w_ffn_d[l // 2]),
                ffn(ys, g_ffn[l], w_ffn_gu[l // 2], w_ffn_d[l // 2]))
    i = l // 2
    w_pad = jnp.pad(w_router[i], ((0, 0), (0, LANES - N_EXPERTS)))
    b_pad = jnp.pad(b_router[i], (0, LANES - N_EXPERTS)).reshape(1, LANES)
    wts_p, ids_p, rank_p, cnt_p = router(yp, g_ffn[l], w_pad, b_pad, jnp.zeros((1, LANES), F32))
    wts_s, ids_s, rank_s, cnt = router(ys, g_ffn[l], w_pad, b_pad, cnt_p)
    mp, ms = yp.shape[0], ys.shape[0]
    tm_e = _moe_tile_rows(mp + ms)
    nt = (2 * (mp + ms)) // tm_e + N_EXPERTS
    ids = jnp.concatenate([ids_p[:, :2], ids_s[:, :2]], axis=0)
    rank = jnp.concatenate([rank_p[:, :2], rank_s[:, :2]], axis=0)
    tile_expert, tile_rows, pos = _dispatch_plan(cnt[0, :N_EXPERTS].astype(jnp.int32), ids, rank, tm_e, nt)
    tm_p, tm_s = 512, min(512, ms)
    pos_p, pos_s = _combine_pos(pos[:mp], tm_p), _combine_pos(pos[mp:], tm_s)
    xs = jnp.zeros((nt * tm_e, D_MODEL // 2), jnp.uint32)
    xs = moe_dispatch(pos_p, yp, g_ffn[l], xs, tm=tm_p)
    xs = moe_dispatch(pos_s, ys, g_ffn[l], xs, tm=tm_s)
    osort = moe_ffn(tile_expert, tile_rows, xs, w_moe_gu[i], w_moe_d[i], tm=tm_e)
    return (moe_combine(pos_p, yp, wts_p, osort, tm=tm_p), moe_combine(pos_s, ys, wts_s, osort, tm=tm_s))


def kernel(x_prompt, x_sample, mem_prompt, cache_mem_k, cache_mem_v, cache_swa_k, cache_swa_v, state_conv_b, state_conv_c, rel_table, g_mix, w_in, b_gate, q_norm_g, k_norm_g, sinks, w_conv_b, w_conv_c, b_conv_c, ln_c_g, ln_c_b, w_proj_a, w_proj_b, w_proj_c, w_out, g_xattn, g_mem, w_xq, w_xkv, xq_norm_g, xk_norm_g, w_xo, g_ffn, w_ffn_gu, w_ffn_d, w_router, b_router, w_moe_gu, w_moe_d):
    batch, seq, d = x_prompt.shape
    dec_batch, dec_seq, _ = x_sample.shape
    depth = g_mix.shape[0]
    yp = x_prompt.reshape(batch * seq, d)
    ys = x_sample.reshape(dec_batch * dec_seq, d)
    mem = mem_prompt.reshape(batch * N_MEM, d)
    first = np.arange(KEY_TILE)[None, :] >= Q_TILE
    bias_p = jnp.stack([_stack_heads(_bias_tensor(rel_table, Q_TILE, _prompt_valid() & first)),
                        _stack_heads(_bias_tensor(rel_table, Q_TILE, _prompt_valid()))])
    bias_s = _bias_tensor(rel_table, dec_seq, _sample_valid(dec_seq))
    outs = [[] for _ in range(10)]
    ck = cache_swa_k.reshape(depth, dec_batch, WINDOW, A_KV_WIDTH)
    cv = cache_swa_v.reshape(depth, dec_batch, WINDOW, A_KV_WIDTH)
    cmk = cache_mem_k.reshape(depth, dec_batch, N_MEM, X_WIDTH)
    cmv = cache_mem_v.reshape(depth, dec_batch, N_MEM, X_WIDTH)
    for l in range(depth):
        mw = _mixer_weights(l, g_mix, w_in, b_gate, q_norm_g, k_norm_g, sinks, w_conv_b, w_conv_c,
                            b_conv_c, ln_c_g, ln_c_b, w_proj_a, w_proj_b, w_proj_c, w_out)
        wq = w_xq[l].astype(BF16)
        wo = w_xo[l].astype(BF16)
        ffn_args = (g_ffn, w_ffn_gu, w_ffn_d, w_router, b_router, w_moe_gu, w_moe_d)
        yp, (nk, nv, ncb, ncc) = _mixer_prompt(yp, mw, bias_p, batch, seq)
        mk, mv = mem_kv(mem, g_mem[l], w_xkv[l].astype(BF16), xk_norm_g[l])
        mk3 = mk.reshape(batch, N_MEM, X_WIDTH)
        mv3 = mv.reshape(batch, N_MEM, X_WIDTH)
        yp = xattn(yp, g_xattn[l], wq, xq_norm_g[l], mk3[None], mv3[None], wo, layer=0, nb=1, rpb=512,
                   tiles_per_mem=seq // 512)
        for lst, v in zip(outs[:6], (mk3.reshape(batch, N_MEM, X_HEADS, X_HEAD_DIM),
                                     mv3.reshape(batch, N_MEM, X_HEADS, X_HEAD_DIM),
                                     nk.reshape(batch, WINDOW, A_KV_HEADS, A_HEAD_DIM),
                                     nv.reshape(batch, WINDOW, A_KV_HEADS, A_HEAD_DIM), ncb, ncc)):
            lst.append(v)
        ys, (nk, nv, ncb, ncc) = _mixer_sample(ys, mw, bias_s, dec_batch, dec_seq, ck, cv, l,
                                               state_conv_b[l], state_conv_c[l])
        ys = xattn(ys, g_xattn[l], wq, xq_norm_g[l], cmk, cmv, wo, layer=l, nb=8, rpb=dec_seq,
                   tiles_per_mem=1)
        yp, ys = _channel_mixer(yp, ys, l, *ffn_args)
        for lst, v in zip(outs[6:], (nk.reshape(dec_batch, dec_seq, A_KV_HEADS, A_HEAD_DIM),
                                     nv.reshape(dec_batch, dec_seq, A_KV_HEADS, A_HEAD_DIM), ncb, ncc)):
            lst.append(v)
    return (yp.reshape(batch, seq, d), ys.reshape(dec_batch, dec_seq, d)) + tuple(jnp.stack(o) for o in outs)
```

```python
import functools
import math

import numpy as np
import jax
import jax.numpy as jnp
from jax import lax
from jax.experimental import pallas as pl
from jax.experimental.pallas import tpu as pltpu

F32 = jnp.float32
BF16 = jnp.bfloat16

D_MODEL = 2048
CHUNK = 64
A_HEADS = 16
A_KV_HEADS = 4
A_HEAD_DIM = 64
A_WIDTH = A_HEADS * A_HEAD_DIM
A_KV_WIDTH = A_KV_HEADS * A_HEAD_DIM
WINDOW = 128
N_BUCKETS = 32
MAX_DISTANCE = 128
B_WIDTH = 512
B_CONV = 3
C_WIDTH = 512
C_CONV = 31
N_MEM = 256
X_HEADS = 4
X_HEAD_DIM = 128
X_WIDTH = X_HEADS * X_HEAD_DIM
D_FF = 5632
N_EXPERTS = 8
EPS = 1e-6

LANES = 128
KEY_TILE = 256
Q_TILE = 128
HALO = 32
NEG = -1e30
VMEM_LIMIT = 56 * 1024 * 1024

COL_Q = 0
COL_K, COL_V = 4, 5
COL_GB, COL_GC, COL_HB, COL_GA, COL_GG = 3, 4, 5, 6, 7
COL_GATE0 = 2
IN_COLS = 4096 + 3 * D_MODEL


def _cparams(sem):
    return pltpu.CompilerParams(dimension_semantics=sem, vmem_limit_bytes=VMEM_LIMIT)


def _rms(x, g):
    ms = jnp.mean(x * x, axis=-1, keepdims=True)
    return x * lax.rsqrt(ms + EPS) * g


def _resident(shape):
    nd = len(shape)
    return pl.BlockSpec(shape, lambda *_: (0,) * nd, pipeline_mode=pl.Buffered(1))


def _norm_matmul_kernel(x_ref, g_ref, w_ref, o_ref, hn_ref):
    @pl.when(pl.program_id(1) == 0)
    def _():
        hn_ref[...] = _rms(x_ref[...], g_ref[...]).astype(BF16)

    o_ref[...] = jnp.dot(hn_ref[...], w_ref[...], preferred_element_type=F32).astype(o_ref.dtype)


def norm_matmul(x, g, w, *, tm, tn, out_dtype):
    m, k = x.shape
    n = w.shape[1]
    return pl.pallas_call(
        _norm_matmul_kernel,
        grid=(m // tm, n // tn),
        in_specs=[pl.BlockSpec((tm, k), lambda i, j: (i, 0)),
                  pl.BlockSpec((1, k), lambda i, j: (0, 0)),
                  pl.BlockSpec((k, tn), lambda i, j: (0, j))],
        out_specs=pl.BlockSpec((tm, tn), lambda i, j: (i, j)),
        out_shape=jax.ShapeDtypeStruct((m, n), out_dtype),
        scratch_shapes=[pltpu.VMEM((tm, k), BF16)],
        compiler_params=_cparams(("parallel", "arbitrary")),
        name="norm_matmul",
    )(x, g.reshape(1, k), w)


def _mem_kv_kernel(x_ref, g_ref, w_ref, kg_ref, k_ref, v_ref):
    hn = _rms(x_ref[...], g_ref[...]).astype(BF16)
    kv = jnp.dot(hn, w_ref[...], preferred_element_type=F32)
    for h in range(X_HEADS):
        sl = slice(h * X_HEAD_DIM, (h + 1) * X_HEAD_DIM)
        k_ref[:, sl] = _rms(kv[:, sl], kg_ref[...])
    v_ref[...] = kv[:, X_WIDTH:]


def mem_kv(mem, g, w_bf, kg):
    m, k = mem.shape
    tm = 256
    return pl.pallas_call(
        _mem_kv_kernel,
        grid=(m // tm,),
        in_specs=[pl.BlockSpec((tm, k), lambda i: (i, 0)),
                  pl.BlockSpec((1, k), lambda i: (0, 0)),
                  _resident((k, 2 * X_WIDTH)),
                  pl.BlockSpec((1, X_HEAD_DIM), lambda i: (0, 0))],
        out_specs=[pl.BlockSpec((tm, X_WIDTH), lambda i: (i, 0)),
                   pl.BlockSpec((tm, X_WIDTH), lambda i: (i, 0))],
        out_shape=[jax.ShapeDtypeStruct((m, X_WIDTH), F32)] * 2,
        compiler_params=_cparams(("parallel",)),
        name="mem_kv",
    )(mem, g.reshape(1, k), w_bf, kg.reshape(1, X_HEAD_DIM))


def _half_norm(x, g):
    x2 = x * x
    lo_lane = lax.broadcasted_iota(jnp.int32, (1, LANES), 1) < A_HEAD_DIM
    s_lo = jnp.sum(jnp.where(lo_lane, x2, 0.0), axis=-1, keepdims=True)
    s_hi = jnp.sum(jnp.where(lo_lane, 0.0, x2), axis=-1, keepdims=True)
    ms = jnp.where(lo_lane, s_lo, s_hi) * (1.0 / A_HEAD_DIM)
    return x * lax.rsqrt(ms + EPS) * g


def _swa_heads(q_ref, qg, k2, v2, bias_ref, sink_ref, o_ref, *, stack):
    lo_lane = lax.broadcasted_iota(jnp.int32, (1, LANES), 1) < A_HEAD_DIM
    rows = q_ref.shape[0]
    for n in range(2):
        k_swap = pltpu.roll(k2[n], A_HEAD_DIM, 1)
        v_swap = pltpu.roll(v2[n], A_HEAD_DIM, 1)
        for side in range(2):
            g = 2 * n + side
            k_here, k_there = (k2[n], k_swap) if side == 0 else (k_swap, k2[n])
            v_here, v_there = (v2[n], v_swap) if side == 0 else (v_swap, v2[n])
            k_half = (jnp.where(lo_lane, k_here, 0.0).astype(BF16), jnp.where(lo_lane, 0.0, k_there).astype(BF16))
            v_both = jnp.where(lo_lane, v_here, v_there).astype(BF16)
            for i in range(0, 2, stack):
                qs = [_half_norm(q_ref[:, c * LANES:(c + 1) * LANES].astype(F32), qg)
                      for c in range(2 * g + i, 2 * g + i + stack)]
                qn = (qs[0] if stack == 1 else jnp.concatenate(qs, axis=0)).astype(BF16)
                at = slice(i * rows, (i + stack) * rows)
                halves = []
                for half in range(2):
                    s = lax.dot_general(qn, k_half[half], (((1,), (1,)), ((), ())), preferred_element_type=F32)
                    s = s + bias_ref[g, half, at, :]
                    if sink_ref is None:
                        m = jnp.max(s, axis=-1, keepdims=True)
                        p = jnp.exp(s - m)
                        den = jnp.sum(p, axis=-1, keepdims=True)
                    else:
                        sink = sink_ref[2 * (2 * g + i) + half]
                        m = jnp.maximum(jnp.max(s, axis=-1, keepdims=True), sink)
                        p = jnp.exp(s - m)
                        den = jnp.sum(p, axis=-1, keepdims=True) + jnp.exp(sink - m)
                    o = jnp.dot(p.astype(BF16), v_both, preferred_element_type=F32)
                    halves.append(o / den)
                o = jnp.where(lo_lane, halves[0], halves[1]).astype(o_ref.dtype)
                for j in range(stack):
                    c = 2 * g + i + j
                    o_ref[:, c * LANES:(c + 1) * LANES] = o[j * rows:(j + 1) * rows]


def _swa_prompt_kernel(q_ref, kc_ref, kp_ref, vc_ref, vp_ref, qg_ref, kg_ref,
                       bias_ref, sink_ref, o_ref, nk_ref, nv_ref):
    k2, v2 = [], []
    for n in range(2):
        sl = slice(n * LANES, (n + 1) * LANES)
        kcat = jnp.concatenate([kp_ref[:, sl], kc_ref[:, sl]], axis=0).astype(F32)
        kn = _half_norm(kcat, kg_ref[...])
        nk_ref[0, :, sl] = kn[Q_TILE:]
        k2.append(kn)
        v2.append(jnp.concatenate([vp_ref[:, sl], vc_ref[:, sl]], axis=0).astype(F32))
    nv_ref[0] = vc_ref[...].astype(F32)
    _swa_heads(q_ref, qg_ref[...], k2, v2, bias_ref, sink_ref, o_ref, stack=1)


def swa_prompt(z, batch, seq, qg2, kg2, bias, sinks):
    nt = seq // Q_TILE
    row = lambda b, t: b * nt + t
    prev = lambda b, t: jnp.maximum(b * nt + t - 1, 0)
    return pl.pallas_call(
        _swa_prompt_kernel,
        grid=(batch, nt),
        in_specs=[pl.BlockSpec((Q_TILE, A_WIDTH), lambda b, t: (row(b, t), COL_Q)),
                  pl.BlockSpec((Q_TILE, A_KV_WIDTH), lambda b, t: (row(b, t), COL_K)),
                  pl.BlockSpec((Q_TILE, A_KV_WIDTH), lambda b, t: (prev(b, t), COL_K)),
                  pl.BlockSpec((Q_TILE, A_KV_WIDTH), lambda b, t: (row(b, t), COL_V)),
                  pl.BlockSpec((Q_TILE, A_KV_WIDTH), lambda b, t: (prev(b, t), COL_V)),
                  pl.BlockSpec((1, LANES), lambda b, t: (0, 0)),
                  pl.BlockSpec((1, LANES), lambda b, t: (0, 0)),
                  pl.BlockSpec((None, A_KV_HEADS, 2, 2 * Q_TILE, KEY_TILE),
                               lambda b, t: (jnp.minimum(t, 1), 0, 0, 0, 0)),
                  pl.BlockSpec(memory_space=pltpu.SMEM)],
        out_specs=[pl.BlockSpec((Q_TILE, A_WIDTH), lambda b, t: (row(b, t), 0)),
                   pl.BlockSpec((1, WINDOW, A_KV_WIDTH), lambda b, t: (b, 0, 0)),
                   pl.BlockSpec((1, WINDOW, A_KV_WIDTH), lambda b, t: (b, 0, 0))],
        out_shape=[jax.ShapeDtypeStruct((batch * seq, A_WIDTH), BF16),
                   jax.ShapeDtypeStruct((batch, WINDOW, A_KV_WIDTH), F32),
                   jax.ShapeDtypeStruct((batch, WINDOW, A_KV_WIDTH), F32)],
        compiler_params=_cparams(("parallel", "arbitrary")),
        name="swa_prompt",
    )(z, z, z, z, z, qg2, kg2, bias, sinks)


def _swa_sample_kernel(q_ref, kn_ref, vn_ref, ck_ref, cv_ref, qg_ref, kg_ref,
                       bias_ref, o_ref, nk_ref, nv_ref):
    rows = q_ref.shape[0]
    pad = KEY_TILE - WINDOW - rows
    k2, v2 = [], []
    for n in range(2):
        sl = slice(n * LANES, (n + 1) * LANES)
        kn = _half_norm(kn_ref[:, sl].astype(F32), kg_ref[...])
        nk_ref[0, :, sl] = kn
        k2.append(jnp.concatenate([ck_ref[0, :, sl], kn, jnp.zeros((pad, LANES), F32)], axis=0))
        v2.append(jnp.concatenate([cv_ref[0, :, sl], vn_ref[:, sl].astype(F32),
                                   jnp.zeros((pad, LANES), F32)], axis=0))
    nv_ref[0] = vn_ref[...].astype(F32)
    _swa_heads(q_ref, qg_ref[...], k2, v2, bias_ref, None, o_ref, stack=2)


def swa_sample(z, batch, rows, cache_k, cache_v, layer, qg2, kg2, bias):
    return pl.pallas_call(
        _swa_sample_kernel,
        grid=(batch,),
        in_specs=[pl.BlockSpec((rows, A_WIDTH), lambda b: (b, COL_Q)),
                  pl.BlockSpec((rows, A_KV_WIDTH), lambda b: (b, COL_K)),
                  pl.BlockSpec((rows, A_KV_WIDTH), lambda b: (b, COL_V)),
                  pl.BlockSpec((None, 1, WINDOW, A_KV_WIDTH), lambda b: (layer, b, 0, 0)),
                  pl.BlockSpec((None, 1, WINDOW, A_KV_WIDTH), lambda b: (layer, b, 0, 0)),
                  pl.BlockSpec((1, LANES), lambda b: (0, 0)),
                  pl.BlockSpec((1, LANES), lambda b: (0, 0)),
                  _resident((A_KV_HEADS, 2, 2 * rows, KEY_TILE))],
        out_specs=[pl.BlockSpec((rows, A_WIDTH), lambda b: (b, 0)),
                   pl.BlockSpec((1, rows, A_KV_WIDTH), lambda b: (b, 0, 0)),
                   pl.BlockSpec((1, rows, A_KV_WIDTH), lambda b: (b, 0, 0))],
        out_shape=[jax.ShapeDtypeStruct((batch * rows, A_WIDTH), BF16),
                   jax.ShapeDtypeStruct((batch, rows, A_KV_WIDTH), F32),
                   jax.ShapeDtypeStruct((batch, rows, A_KV_WIDTH), F32)],
        compiler_params=_cparams(("parallel",)),
        name="swa_sample",
    )(z, z, z, cache_k, cache_v, qg2, kg2, bias)


def _conv_body(gb_ref, ub_main, uc_main, ub_halo, uc_halo, wb_ref, wc_ref, bc_ref, lg_ref, lb_ref,
               ob_ref, oc_ref, nb_ref, nc_ref, sb_ref, sc_ref, ph_ref, write_state):
    rows = ub_main.shape[0]
    sb_ref[0:HALO] = ub_halo
    sb_ref[HALO:HALO + rows] = ub_main
    sc_ref[0:HALO] = uc_halo
    sc_ref[HALO:HALO + rows] = uc_main
    span = rows + HALO
    sc_ref[span:span + 8] = jnp.zeros((8, C_WIDTH), F32)
    for b in range(1, 8):
        ph_ref[b - 1] = sc_ref[pl.ds(b, span), :]
    sub = min(rows, 32)
    for r0 in range(0, rows, sub):
        yb = jnp.zeros((sub, B_WIDTH), F32)
        for k in range(B_CONV):
            yb = yb + wb_ref[k:k + 1, :] * sb_ref[pl.ds(r0 + HALO - (B_CONV - 1) + k, sub), :]
        ob_ref[r0:r0 + sub, :] = (gb_ref[r0:r0 + sub, :].astype(F32) * yb).astype(ob_ref.dtype)
        yc = jnp.zeros((sub, C_WIDTH), F32)
        for k in range(C_CONV):
            shift, phase = divmod(HALO - (C_CONV - 1) + k, 8)
            at = pl.ds(r0 + 8 * shift, sub)
            taps = sc_ref[at, :] if phase == 0 else ph_ref[phase - 1, at, :]
            yc = yc + wc_ref[k:k + 1, :] * taps
        yc = yc + bc_ref[...]
        mu = jnp.mean(yc, axis=-1, keepdims=True)
        xc = yc - mu
        y = xc * lax.rsqrt(jnp.mean(xc * xc, axis=-1, keepdims=True) + EPS)
        y = y * lg_ref[...] + lb_ref[...]
        oc_ref[r0:r0 + sub, :] = (y * jax.nn.sigmoid(y)).astype(oc_ref.dtype)

    def _state():
        nb_ref[0] = sb_ref[rows + HALO - 8:rows + HALO]
        nc_ref[0] = sc_ref[rows:rows + HALO]

    write_state(_state)


def _conv_prompt_kernel(gb_ref, gc_ref, hb_ref, ga_ref, gg_ref, gch_ref, hbh_ref, gah_ref, ggh_ref,
                        wb_ref, wc_ref, bc_ref, lg_ref, lb_ref,
                        ob_ref, oc_ref, nb_ref, nc_ref, sb_ref, sc_ref, ph_ref):
    t = pl.program_id(1)
    hist = (t > 0).astype(F32)
    ub_main = gc_ref[...].astype(F32) * hb_ref[...].astype(F32)
    uc_main = ga_ref[...].astype(F32) * jax.nn.sigmoid(gg_ref[...].astype(F32))
    ub_halo = gch_ref[...].astype(F32) * hbh_ref[...].astype(F32) * hist
    uc_halo = gah_ref[...].astype(F32) * jax.nn.sigmoid(ggh_ref[...].astype(F32)) * hist
    last = pl.num_programs(1) - 1
    _conv_body(gb_ref, ub_main, uc_main, ub_halo, uc_halo, wb_ref, wc_ref, bc_ref, lg_ref, lb_ref,
               ob_ref, oc_ref, nb_ref, nc_ref, sb_ref, sc_ref, ph_ref,
               lambda f: pl.when(t == last)(f))


def conv_prompt(z, batch, seq, wb, wc, bc, lg, lb, *, tr=256):
    nt = seq // tr
    hp = tr // HALO
    main = lambda c: pl.BlockSpec((tr, B_WIDTH), lambda b, t: (b * nt + t, c))
    halo = lambda c: pl.BlockSpec((HALO, B_WIDTH), lambda b, t: (jnp.maximum((b * nt + t) * hp - 1, 0), c))
    vec = lambda r: pl.BlockSpec((r, B_WIDTH), lambda b, t: (0, 0))
    return pl.pallas_call(
        _conv_prompt_kernel,
        grid=(batch, nt),
        in_specs=[main(COL_GB), main(COL_GC), main(COL_HB), main(COL_GA), main(COL_GG),
                  halo(COL_GC), halo(COL_HB), halo(COL_GA), halo(COL_GG),
                  vec(B_CONV), vec(C_CONV), vec(1), vec(1), vec(1)],
        out_specs=[pl.BlockSpec((tr, B_WIDTH), lambda b, t: (b * nt + t, 0)),
                   pl.BlockSpec((tr, C_WIDTH), lambda b, t: (b * nt + t, 0)),
                   pl.BlockSpec((1, 8, B_WIDTH), lambda b, t: (b, 0, 0)),
                   pl.BlockSpec((1, HALO, C_WIDTH), lambda b, t: (b, 0, 0))],
        out_shape=[jax.ShapeDtypeStruct((batch * seq, B_WIDTH), BF16),
                   jax.ShapeDtypeStruct((batch * seq, C_WIDTH), BF16),
                   jax.ShapeDtypeStruct((batch, 8, B_WIDTH), F32),
                   jax.ShapeDtypeStruct((batch, HALO, C_WIDTH), F32)],
        scratch_shapes=[pltpu.VMEM((tr + HALO, B_WIDTH), F32), pltpu.VMEM((tr + HALO + 8, C_WIDTH), F32),
                        pltpu.VMEM((7, tr + HALO, C_WIDTH), F32)],
        compiler_params=_cparams(("parallel", "arbitrary")),
        name="conv_prompt",
    )(z, z, z, z, z, z, z, z, z, wb, wc, bc.reshape(1, -1), lg.reshape(1, -1), lb.reshape(1, -1))


def _conv_sample_kernel(gb_ref, gc_ref, hb_ref, ga_ref, gg_ref, stb_ref, stc_ref,
                        wb_ref, wc_ref, bc_ref, lg_ref, lb_ref,
                        ob_ref, oc_ref, nb_ref, nc_ref, sb_ref, sc_ref, ph_ref):
    ub_main = gc_ref[...].astype(F32) * hb_ref[...].astype(F32)
    uc_main = ga_ref[...].astype(F32) * jax.nn.sigmoid(gg_ref[...].astype(F32))
    _conv_body(gb_ref, ub_main, uc_main, stb_ref[0], stc_ref[0], wb_ref, wc_ref, bc_ref, lg_ref, lb_ref,
               ob_ref, oc_ref, nb_ref, nc_ref, sb_ref, sc_ref, ph_ref, lambda f: f())


def conv_sample(z, batch, rows, stb, stc, wb, wc, bc, lg, lb):
    main = lambda c: pl.BlockSpec((rows, B_WIDTH), lambda b: (b, c))
    vec = lambda r: pl.BlockSpec((r, B_WIDTH), lambda b: (0, 0))
    return pl.pallas_call(
        _conv_sample_kernel,
        grid=(batch,),
        in_specs=[main(COL_GB), main(COL_GC), main(COL_HB), main(COL_GA), main(COL_GG),
                  pl.BlockSpec((1, HALO, B_WIDTH), lambda b: (b, 0, 0)),
                  pl.BlockSpec((1, HALO, C_WIDTH), lambda b: (b, 0, 0)),
                  vec(B_CONV), vec(C_CONV), vec(1), vec(1), vec(1)],
        out_specs=[pl.BlockSpec((rows, B_WIDTH), lambda b: (b, 0)),
                   pl.BlockSpec((rows, C_WIDTH), lambda b: (b, 0)),
                   pl.BlockSpec((1, 8, B_WIDTH), lambda b: (b, 0, 0)),
                   pl.BlockSpec((1, HALO, C_WIDTH), lambda b: (b, 0, 0))],
        out_shape=[jax.ShapeDtypeStruct((batch * rows, B_WIDTH), BF16),
                   jax.ShapeDtypeStruct((batch * rows, C_WIDTH), BF16),
                   jax.ShapeDtypeStruct((batch, 8, B_WIDTH), F32),
                   jax.ShapeDtypeStruct((batch, HALO, C_WIDTH), F32)],
        scratch_shapes=[pltpu.VMEM((rows + HALO, B_WIDTH), F32), pltpu.VMEM((rows + HALO + 8, C_WIDTH), F32),
                        pltpu.VMEM((7, rows + HALO, C_WIDTH), F32)],
        compiler_params=_cparams(("parallel",)),
        name="conv_sample",
    )(z, z, z, z, z, stb, stc, wb, wc, bc.reshape(1, -1), lg.reshape(1, -1), lb.reshape(1, -1))


def _merge_kernel(oa_ref, ob_ref, oc_ref, l0_ref, l1_ref, l2_ref, bg_ref,
                  wpa_ref, wpb_ref, wpc_ref, wo_ref, x_ref, o_ref):
    def gated(l_ref, i, o_r, w_r):
        gate = jax.nn.sigmoid(l_ref[...].astype(F32) + bg_ref[i:i + 1, :])
        return gate * jnp.dot(o_r[...], w_r[...], preferred_element_type=F32)

    merged = gated(l0_ref, 0, oa_ref, wpa_ref)
    merged = merged + gated(l1_ref, 1, ob_ref, wpb_ref)
    merged = merged + gated(l2_ref, 2, oc_ref, wpc_ref)
    o_ref[...] = x_ref[...] + jnp.dot(merged.astype(BF16), wo_ref[...], preferred_element_type=F32)


def merge(oa, ob, oc, z, bg, wpa, wpb, wpc, wo, x, *, tm=512):
    m = x.shape[0]
    tm = min(tm, m)
    rows = lambda w: pl.BlockSpec((tm, w), lambda i: (i, 0))
    gate = lambda c: pl.BlockSpec((tm, D_MODEL), lambda i: (i, COL_GATE0 + c))
    return pl.pallas_call(
        _merge_kernel,
        grid=(m // tm,),
        in_specs=[rows(A_WIDTH), rows(B_WIDTH), rows(C_WIDTH), gate(0), gate(1), gate(2),
                  pl.BlockSpec((3, D_MODEL), lambda i: (0, 0)),
                  _resident((A_WIDTH, D_MODEL)), _resident((B_WIDTH, D_MODEL)),
                  _resident((C_WIDTH, D_MODEL)), _resident((D_MODEL, D_MODEL)),
                  rows(D_MODEL)],
        out_specs=rows(D_MODEL),
        out_shape=jax.ShapeDtypeStruct((m, D_MODEL), F32),
        compiler_params=_cparams(("parallel",)),
        name="merge",
    )(oa, ob, oc, z, z, z, bg.reshape(3, D_MODEL), wpa, wpb, wpc, wo, x)


def _xattn_kernel(y_ref, g_ref, wq_ref, qg_ref, mk_ref, mv_ref, wo_ref, o_ref, *, nb, rpb):
    y = y_ref[...]
    hn = _rms(y, g_ref[...]).astype(BF16)
    q = jnp.dot(hn, wq_ref[...], preferred_element_type=F32)
    heads = []
    for h in range(X_HEADS):
        sl = slice(h * X_HEAD_DIM, (h + 1) * X_HEAD_DIM)
        qh = _rms(q[:, sl], qg_ref[...])
        per_batch = []
        for b in range(nb):
            qb = qh[b * rpb:(b + 1) * rpb].astype(BF16)
            kh = mk_ref[b, :, sl].astype(BF16)
            vh = mv_ref[b, :, sl].astype(BF16)
            s = lax.dot_general(qb, kh, (((1,), (1,)), ((), ())),
                                preferred_element_type=F32) * (X_HEAD_DIM ** -0.5)
            m = jnp.max(s, axis=-1, keepdims=True)
            p = jnp.exp(s - m)
            den = jnp.sum(p, axis=-1, keepdims=True)
            per_batch.append(jnp.dot(p.astype(BF16), vh, preferred_element_type=F32) / den)
        heads.append(per_batch[0] if nb == 1 else jnp.concatenate(per_batch, axis=0))
    o = jnp.concatenate(heads, axis=1).astype(BF16)
    o_ref[...] = y + jnp.dot(o, wo_ref[...], preferred_element_type=F32)


def xattn(y, g, wq, qg, mk, mv, wo, *, layer, nb, rpb, tiles_per_mem):
    m = y.shape[0]
    tm = nb * rpb
    mem_idx = ((lambda i: (layer, i // tiles_per_mem, 0, 0)) if nb == 1
               else (lambda i: (layer, i, 0, 0)))
    return pl.pallas_call(
        functools.partial(_xattn_kernel, nb=nb, rpb=rpb),
        grid=(m // tm,),
        in_specs=[pl.BlockSpec((tm, D_MODEL), lambda i: (i, 0)),
                  pl.BlockSpec((1, D_MODEL), lambda i: (0, 0)),
                  _resident((D_MODEL, X_WIDTH)),
                  pl.BlockSpec((1, X_HEAD_DIM), lambda i: (0, 0)),
                  pl.BlockSpec((None, nb, N_MEM, X_WIDTH), mem_idx),
                  pl.BlockSpec((None, nb, N_MEM, X_WIDTH), mem_idx),
                  _resident((X_WIDTH, D_MODEL))],
        out_specs=pl.BlockSpec((tm, D_MODEL), lambda i: (i, 0)),
        out_shape=jax.ShapeDtypeStruct((m, D_MODEL), F32),
        compiler_params=_cparams(("parallel",)),
        name="xattn",
    )(y, g.reshape(1, -1), wq, qg.reshape(1, -1), mk, mv, wo)


def _ffn_kernel(x_ref, g_ref, wg_ref, wu_ref, wd_ref, o_ref, hn_ref):
    @pl.when(pl.program_id(1) == 0)
    def _():
        x = x_ref[...]
        hn_ref[...] = _rms(x, g_ref[...]).astype(BF16)
        o_ref[...] = x

    hn = hn_ref[...]
    g = jnp.dot(hn, wg_ref[...].astype(BF16), preferred_element_type=F32)
    u = jnp.dot(hn, wu_ref[...].astype(BF16), preferred_element_type=F32)
    a = (g * jax.nn.sigmoid(g) * u).astype(BF16)
    o_ref[...] += jnp.dot(a, wd_ref[...].astype(BF16), preferred_element_type=F32)


def ffn(x, g, w_gu, w_d, *, tm=1024, tf=512):
    m = x.shape[0]
    tm = min(tm, m)
    nf = D_FF // tf
    return pl.pallas_call(
        _ffn_kernel,
        grid=(m // tm, nf),
        in_specs=[pl.BlockSpec((tm, D_MODEL), lambda i, f: (i, 0), pipeline_mode=pl.Buffered(1)),
                  pl.BlockSpec((1, D_MODEL), lambda i, f: (0, 0)),
                  pl.BlockSpec((D_MODEL, tf), lambda i, f: (0, f)),
                  pl.BlockSpec((D_MODEL, tf), lambda i, f: (0, nf + f)),
                  pl.BlockSpec((tf, D_MODEL), lambda i, f: (f, 0))],
        out_specs=pl.BlockSpec((tm, D_MODEL), lambda i, f: (i, 0), pipeline_mode=pl.Buffered(1)),
        out_shape=jax.ShapeDtypeStruct((m, D_MODEL), F32),
        scratch_shapes=[pltpu.VMEM((tm, D_MODEL), BF16)],
        compiler_params=_cparams(("parallel", "arbitrary")),
        name="ffn",
    )(x, g.reshape(1, -1), w_gu, w_gu, w_d)


def _split3(x):
    hi = x.astype(BF16)
    lo = (x - hi.astype(F32)).astype(BF16)
    return hi, lo


def _router_kernel(x_ref, g_ref, w_ref, b_ref, tri_ref, base_ref, wts_ref, ids_ref, rank_ref, cnt_ref, run_ref):
    @pl.when(pl.program_id(0) == 0)
    def _():
        run_ref[...] = base_ref[...]

    hn = _rms(x_ref[...], g_ref[...])
    h_hi, h_lo = _split3(hn)
    w_hi, w_lo = _split3(w_ref[...])
    dot = functools.partial(jnp.dot, preferred_element_type=F32)
    logits = dot(h_hi, w_hi) + dot(h_hi, w_lo) + dot(h_lo, w_hi) + b_ref[...]
    lane = lax.broadcasted_iota(jnp.int32, logits.shape, 1).astype(F32)
    logits = jnp.where(lane < N_EXPERTS, logits, -jnp.inf)
    v1 = jnp.max(logits, axis=-1, keepdims=True)
    i1 = jnp.min(jnp.where(logits == v1, lane, float(LANES)), axis=-1, keepdims=True)
    rest = jnp.where(lane == i1, -jnp.inf, logits)
    v2 = jnp.max(rest, axis=-1, keepdims=True)
    i2 = jnp.min(jnp.where(rest == v2, lane, float(LANES)), axis=-1, keepdims=True)
    e2 = jnp.exp(v2 - v1)
    den = 1.0 + e2
    first, second = lane == 0.0, lane == 1.0
    wts_ref[...] = jnp.where(first, 1.0 / den, 0.0) + jnp.where(second, e2 / den, 0.0)
    ids_ref[...] = (jnp.where(first, i1, 0.0) + jnp.where(second, i2, 0.0)).astype(jnp.int32)
    hit1, hit2 = lane == i1, lane == i2
    hits = jnp.where(jnp.logical_or(hit1, hit2), 1.0, 0.0)
    before = dot(tri_ref[...], hits.astype(BF16)) + run_ref[...]
    r1 = jnp.sum(jnp.where(hit1, before, 0.0), axis=-1, keepdims=True)
    r2 = jnp.sum(jnp.where(hit2, before, 0.0), axis=-1, keepdims=True)
    rank_ref[...] = (jnp.where(first, r1, 0.0) + jnp.where(second, r2, 0.0)).astype(jnp.int32)
    run_ref[...] += jnp.sum(hits, axis=0, keepdims=True)
    cnt_ref[...] = run_ref[...]


def router(x, g, w_pad, b_pad, base, *, tm=512):
    m = x.shape[0]
    tm = min(tm, m)
    tri = jnp.asarray(np.tril(np.ones((tm, tm), np.float32), -1), BF16)
    row = lambda w: pl.BlockSpec((tm, w), lambda i: (i, 0))
    fixed = lambda s: pl.BlockSpec(s, lambda i: (0, 0))
    return pl.pallas_call(
        _router_kernel,
        grid=(m // tm,),
        in_specs=[row(D_MODEL), fixed((1, D_MODEL)), fixed((D_MODEL, LANES)), fixed((1, LANES)),
                  fixed((tm, tm)), fixed((1, LANES))],
        out_specs=[row(LANES), row(LANES), row(LANES), fixed((1, LANES))],
        out_shape=[jax.ShapeDtypeStruct((m, LANES), F32),
                   jax.ShapeDtypeStruct((m, LANES), jnp.int32),
                   jax.ShapeDtypeStruct((m, LANES), jnp.int32),
                   jax.ShapeDtypeStruct((1, LANES), F32)],
        scratch_shapes=[pltpu.VMEM((1, LANES), F32)],
        compiler_params=_cparams(("arbitrary",)),
        name="router",
    )(x, g.reshape(1, -1), w_pad, b_pad, tri, base)


MOE_TM = 1024
MOE_TF = 512
BF16_ROWS = 16


def _round_up(x, m):
    return -(-int(x) // m) * m


def _row_copy(src, row, dst, r, sem):
    return pltpu.make_async_copy(src.at[pl.ds(row, 1)], dst.at[pl.ds(r, 1)], sem)


def _moe_dispatch_kernel(pos_ref, x_ref, g_ref, xs_in, xs_hbm, pk_ref, sem):
    del xs_in
    tm = x_ref.shape[0]
    half = D_MODEL // 2
    hn = _rms(x_ref[...], g_ref[...]).astype(BF16)
    lo_bits = lax.bitcast_convert_type(hn[:, :half].astype(F32), jnp.uint32)
    hi_bits = lax.bitcast_convert_type(hn[:, half:].astype(F32), jnp.uint32)
    pk_ref[...] = (lo_bits >> 16) | (hi_bits & jnp.uint32(0xFFFF0000))

    def issue(r, c):
        _row_copy(pk_ref, r, xs_hbm, pos_ref[0, 0, r], sem).start()
        _row_copy(pk_ref, r, xs_hbm, pos_ref[0, 0, tm + r], sem).start()
        return c

    lax.fori_loop(0, tm, issue, 0, unroll=8)

    def wait(r, c):
        _row_copy(pk_ref, r, xs_hbm, 0, sem).wait()
        _row_copy(pk_ref, r, xs_hbm, 0, sem).wait()
        return c

    lax.fori_loop(0, tm, wait, 0, unroll=8)


def moe_dispatch(pos, x, g, xs, *, tm):
    m = x.shape[0]
    return pl.pallas_call(
        _moe_dispatch_kernel,
        grid=(m // tm,),
        in_specs=[pl.BlockSpec((1, 1, 2 * tm), lambda i: (i, 0, 0), memory_space=pltpu.SMEM),
                  pl.BlockSpec((tm, D_MODEL), lambda i: (i, 0)),
                  pl.BlockSpec((1, D_MODEL), lambda i: (0, 0)),
                  pl.BlockSpec(memory_space=pl.ANY)],
        out_specs=pl.BlockSpec(memory_space=pl.ANY),
        out_shape=jax.ShapeDtypeStruct(xs.shape, xs.dtype),
        scratch_shapes=[pltpu.VMEM((tm, D_MODEL // 2), jnp.uint32), pltpu.SemaphoreType.DMA],
        input_output_aliases={3: 0},
        compiler_params=_cparams(("arbitrary",)),
        name="moe_dispatch",
    )(pos, x, g.reshape(1, -1), xs)


def _moe_ffn_kernel(te_ref, tv_ref, xs_ref, wg_ref, wu_ref, wd_ref, o_ref, hn_ref):
    t = pl.program_id(0)
    f = pl.program_id(1)
    rows = xs_ref.shape[0]
    half = D_MODEL // 2

    @pl.when(f == 0)
    def _():
        o_ref[...] = jnp.zeros_like(o_ref)
        xu = xs_ref[...]
        hn_ref[:, :half] = lax.bitcast_convert_type(xu << 16, F32).astype(BF16)
        hn_ref[:, half:] = lax.bitcast_convert_type(xu & jnp.uint32(0xFFFF0000), F32).astype(BF16)

    def swiglu(r):
        hn = hn_ref[0:r]
        g = jnp.dot(hn, wg_ref[...].astype(BF16), preferred_element_type=F32)
        u = jnp.dot(hn, wu_ref[...].astype(BF16), preferred_element_type=F32)
        a = (g * jax.nn.sigmoid(g) * u).astype(BF16)
        o_ref[0:r] += jnp.dot(a, wd_ref[...].astype(BF16), preferred_element_type=F32)

    nv = tv_ref[t]
    half_rows = _round_up(rows / 2, BF16_ROWS)
    quarter_rows = _round_up(rows / 4, BF16_ROWS)

    @pl.when(nv > half_rows)
    def _():
        swiglu(rows)

    @pl.when(jnp.logical_and(nv > quarter_rows, nv <= half_rows))
    def _():
        swiglu(half_rows)

    @pl.when(jnp.logical_and(nv > 0, nv <= quarter_rows))
    def _():
        swiglu(quarter_rows)


def moe_ffn(tile_expert, tile_rows, xs, w_gu, w_d, *, tm):
    nt = xs.shape[0] // tm
    nf = D_FF // MOE_TF
    last = nf - 1
    col = lambda f, tv, t: jnp.where(tv[t] != 0, f, last)
    grid_spec = pltpu.PrefetchScalarGridSpec(
        num_scalar_prefetch=2,
        grid=(nt, nf),
        in_specs=[pl.BlockSpec((tm, D_MODEL // 2), lambda t, f, te, tv: (t, 0), pipeline_mode=pl.Buffered(1)),
                  pl.BlockSpec((None, D_MODEL, MOE_TF), lambda t, f, te, tv: (te[t], 0, col(f, tv, t))),
                  pl.BlockSpec((None, D_MODEL, MOE_TF), lambda t, f, te, tv: (te[t], 0, nf + col(f, tv, t))),
                  pl.BlockSpec((None, MOE_TF, D_MODEL), lambda t, f, te, tv: (te[t], col(f, tv, t), 0))],
        out_specs=pl.BlockSpec((tm, D_MODEL), lambda t, f, te, tv: (t, 0)),
        scratch_shapes=[pltpu.VMEM((tm, D_MODEL), BF16)],
    )
    return pl.pallas_call(
        _moe_ffn_kernel,
        grid_spec=grid_spec,
        out_shape=jax.ShapeDtypeStruct((nt * tm, D_MODEL), F32),
        compiler_params=_cparams(("arbitrary", "arbitrary")),
        name="moe_ffn",
    )(tile_expert, tile_rows, xs, w_gu, w_gu, w_d)


def _moe_combine_kernel(pos_ref, x_ref, w_ref, osort_hbm, o_ref, abuf, sem):
    tm = x_ref.shape[0]

    def issue(r, c):
        _row_copy(osort_hbm, pos_ref[0, 0, r], abuf, r, sem).start()
        return c

    lax.fori_loop(0, 2 * tm, issue, 0, unroll=8)

    def wait(r, c):
        _row_copy(osort_hbm, 0, abuf, r, sem).wait()
        return c

    lax.fori_loop(0, 2 * tm, wait, 0, unroll=8)
    w = w_ref[...]
    o_ref[...] = x_ref[...] + w[:, 0:1] * abuf[0:tm, :] + w[:, 1:2] * abuf[tm:2 * tm, :]


def moe_combine(pos, x, wts, osort, *, tm=256):
    m = x.shape[0]
    return pl.pallas_call(
        _moe_combine_kernel,
        grid=(m // tm,),
        in_specs=[pl.BlockSpec((1, 1, 2 * tm), lambda i: (i, 0, 0), memory_space=pltpu.SMEM),
                  pl.BlockSpec((tm, D_MODEL), lambda i: (i, 0)),
                  pl.BlockSpec((tm, LANES), lambda i: (i, 0)),
                  pl.BlockSpec(memory_space=pl.ANY)],
        out_specs=pl.BlockSpec((tm, D_MODEL), lambda i: (i, 0)),
        out_shape=jax.ShapeDtypeStruct((m, D_MODEL), F32),
        scratch_shapes=[pltpu.VMEM((2 * tm, D_MODEL), F32), pltpu.SemaphoreType.DMA],
        compiler_params=_cparams(("arbitrary",)),
        name="moe_combine",
    )(pos, x, wts, osort)


def _dispatch_plan(counts, ids, rank, tm, nt):
    padded = ((counts + tm - 1) // tm) * tm
    ends = jnp.cumsum(padded)
    offs = ends - padded
    off_of = jnp.zeros_like(ids)
    for e in range(N_EXPERTS):
        off_of = jnp.where(ids == e, offs[e], off_of)
    pos = off_of + rank
    starts = jnp.arange(nt, dtype=jnp.int32) * tm
    tile_expert = jnp.minimum(jnp.sum((starts[:, None] >= ends[None, :]).astype(jnp.int32), axis=1),
                              N_EXPERTS - 1)
    real_end = (offs + counts)[tile_expert]
    tile_rows = jnp.where(starts < ends[-1], jnp.clip(real_end - starts, 0, tm), 0).astype(jnp.int32)
    last_valid = jnp.maximum(jnp.sum((tile_rows != 0).astype(jnp.int32)) - 1, 0)
    tile_expert = jnp.where(tile_rows != 0, tile_expert, tile_expert[last_valid])
    return tile_expert, tile_rows, pos


def _combine_pos(pos, tm):
    m = pos.shape[0]
    return jnp.transpose(pos.reshape(m // tm, tm, 2), (0, 2, 1)).reshape(m // tm, 1, 2 * tm)


def _t5_bucket_np(rel):
    nb = N_BUCKETS // 2
    max_exact = nb // 2
    ret = np.where(rel > 0, nb, 0)
    n = np.abs(rel)
    nf = np.maximum(n, 1).astype(np.float32)
    large = max_exact + (np.log(nf / np.float32(max_exact)) / np.float32(math.log(MAX_DISTANCE / max_exact))
                         * np.float32(nb - max_exact)).astype(np.int32)
    large = np.minimum(large, nb - 1)
    return (ret + np.where(n < max_exact, n, large)).astype(np.int32)


def _bias_tensor(rel_table, n_q, valid):
    rel = np.arange(KEY_TILE, dtype=np.int32)[None, :] - WINDOW - np.arange(n_q, dtype=np.int32)[:, None]
    onehot = np.eye(N_BUCKETS, dtype=np.float32)[:, _t5_bucket_np(rel).reshape(-1)]
    bias = jnp.dot(rel_table.T, jnp.asarray(onehot), precision=lax.Precision.HIGHEST)
    return jnp.where(jnp.asarray(valid)[None], bias.reshape(A_HEADS, n_q, KEY_TILE), NEG).astype(F32)


def _stack_heads(x):
    _, r, c = x.shape
    return jnp.transpose(x.reshape(A_KV_HEADS, 2, 2, r, c), (0, 2, 1, 3, 4)).reshape(A_KV_HEADS, 2, 2 * r, c)


def _with_sink_column(bias, sinks):
    return _stack_heads(bias.at[:, :, KEY_TILE - 1].set(sinks[:, None]))


def _prompt_valid():
    qc = np.arange(Q_TILE)[:, None] // CHUNK
    kc = np.arange(KEY_TILE)[None, :] // CHUNK
    return (kc >= qc) & (kc <= qc + WINDOW // CHUNK)


def _sample_valid(rows):
    return np.broadcast_to(np.arange(KEY_TILE)[None, :] < WINDOW + rows, (rows, KEY_TILE))


def _mixer_weights(l, g_mix, w_in, b_gate, q_norm_g, k_norm_g, sinks, w_conv_b, w_conv_c, b_conv_c,
                   ln_c_g, ln_c_b, w_proj_a, w_proj_b, w_proj_c, w_out):
    return dict(
        g_mix=g_mix[l], w_in=w_in[l].astype(BF16), b_gate=b_gate[l],
        qg2=(jnp.tile(q_norm_g[l], 2) * (A_HEAD_DIM ** -0.5)).reshape(1, LANES), kg2=jnp.tile(k_norm_g[l], 2).reshape(1, LANES),
        sinks=sinks[l],
        w_cb=w_conv_b[l], w_cc=w_conv_c[l], b_cc=b_conv_c[l], ln_g=ln_c_g[l], ln_b=ln_c_b[l],
        wpa=w_proj_a[l].astype(BF16), wpb=w_proj_b[l].astype(BF16), wpc=w_proj_c[l].astype(BF16),
        wo=w_out[l].astype(BF16))


def _mixer_prompt(x, mw, bias, batch, seq):
    z = norm_matmul(x, mw["g_mix"], mw["w_in"], tm=1024, tn=2048, out_dtype=BF16)
    oa, nk, nv = swa_prompt(z, batch, seq, mw["qg2"], mw["kg2"], bias, mw["sinks"])
    ob, oc, ncb, ncc = conv_prompt(z, batch, seq, mw["w_cb"], mw["w_cc"], mw["b_cc"], mw["ln_g"], mw["ln_b"])
    y = merge(oa, ob, oc, z, mw["b_gate"], mw["wpa"], mw["wpb"], mw["wpc"], mw["wo"], x)
    return y, (nk, nv, ncb[:, 8 - (B_CONV - 1):], ncc[:, HALO - (C_CONV - 1):])


def _mixer_sample(x, mw, bias, batch, rows, cache_k, cache_v, layer, st_b, st_c):
    z = norm_matmul(x, mw["g_mix"], mw["w_in"], tm=x.shape[0], tn=1024, out_dtype=BF16)
    oa, nk, nv = swa_sample(z, batch, rows, cache_k, cache_v, layer, mw["qg2"], mw["kg2"],
                            _with_sink_column(bias, mw["sinks"]))
    stb = jnp.pad(st_b, ((0, 0), (HALO - (B_CONV - 1), 0), (0, 0)))
    stc = jnp.pad(st_c, ((0, 0), (HALO - (C_CONV - 1), 0), (0, 0)))
    ob, oc, ncb, ncc = conv_sample(z, batch, rows, stb, stc, mw["w_cb"], mw["w_cc"], mw["b_cc"],
                                   mw["ln_g"], mw["ln_b"])
    y = merge(oa, ob, oc, z, mw["b_gate"], mw["wpa"], mw["wpb"], mw["wpc"], mw["wo"], x)
    return y, (nk, nv, ncb[:, 8 - (B_CONV - 1):], ncc[:, HALO - (C_CONV - 1):])


def _channel_mixer(yp, ys, l, g_ffn, w_ffn_gu, w_ffn_d, w_router, b_router, w_moe_gu, w_moe_d):
    if l % 2 == 0:
        return (ffn(yp, g_ffn[l], w_ffn_gu[l // 2], w_ffn_d[l // 2]),
                ffn(ys, g_ffn[l], w_ffn_gu[l // 2], w_ffn_d[l // 2]))
    i = l // 2
    w_pad = jnp.pad(w_router[i], ((0, 0), (0, LANES - N_EXPERTS)))
    b_pad = jnp.pad(b_router[i], (0, LANES - N_EXPERTS)).reshape(1, LANES)
    wts_p, ids_p, rank_p, cnt_p = router(yp, g_ffn[l], w_pad, b_pad, jnp.zeros((1, LANES), F32))
    wts_s, ids_s, rank_s, cnt = router(ys, g_ffn[l], w_pad, b_pad, cnt_p)
    mp, ms = yp.shape[0], ys.shape[0]
    tm_e = MOE_TM
    nt = (2 * (mp + ms)) // tm_e + N_EXPERTS
    ids = jnp.concatenate([ids_p[:, :2], ids_s[:, :2]], axis=0)
    rank = jnp.concatenate([rank_p[:, :2], rank_s[:, :2]], axis=0)
    tile_expert, tile_rows, pos = _dispatch_plan(cnt[0, :N_EXPERTS].astype(jnp.int32), ids, rank, tm_e, nt)
    tm_p, tm_s = 512, min(512, ms)
    pos_p, pos_s = _combine_pos(pos[:mp], tm_p), _combine_pos(pos[mp:], tm_s)
    xs = jnp.zeros((nt * tm_e, D_MODEL // 2), jnp.uint32)
    xs = moe_dispatch(pos_p, yp, g_ffn[l], xs, tm=tm_p)
    xs = moe_dispatch(pos_s, ys, g_ffn[l], xs, tm=tm_s)
    osort = moe_ffn(tile_expert, tile_rows, xs, w_moe_gu[i], w_moe_d[i], tm=tm_e)
    return (moe_combine(pos_p, yp, wts_p, osort, tm=tm_p), moe_combine(pos_s, ys, wts_s, osort, tm=tm_s))


def kernel(x_prompt, x_sample, mem_prompt, cache_mem_k, cache_mem_v, cache_swa_k, cache_swa_v, state_conv_b, state_conv_c, rel_table, g_mix, w_in, b_gate, q_norm_g, k_norm_g, sinks, w_conv_b, w_conv_c, b_conv_c, ln_c_g, ln_c_b, w_proj_a, w_proj_b, w_proj_c, w_out, g_xattn, g_mem, w_xq, w_xkv, xq_norm_g, xk_norm_g, w_xo, g_ffn, w_ffn_gu, w_ffn_d, w_router, b_router, w_moe_gu, w_moe_d):
    batch, seq, d = x_prompt.shape
    dec_batch, dec_seq, _ = x_sample.shape
    depth = g_mix.shape[0]
    yp = x_prompt.reshape(batch * seq, d)
    ys = x_sample.reshape(dec_batch * dec_seq, d)
    mem = mem_prompt.reshape(batch * N_MEM, d)
    first = np.arange(KEY_TILE)[None, :] >= Q_TILE
    bias_p = jnp.stack([_stack_heads(_bias_tensor(rel_table, Q_TILE, _prompt_valid() & first)),
                        _stack_heads(_bias_tensor(rel_table, Q_TILE, _prompt_valid()))])
    bias_s = _bias_tensor(rel_table, dec_seq, _sample_valid(dec_seq))
    outs = [[] for _ in range(10)]
    ck = cache_swa_k.reshape(depth, dec_batch, WINDOW, A_KV_WIDTH)
    cv = cache_swa_v.reshape(depth, dec_batch, WINDOW, A_KV_WIDTH)
    cmk = cache_mem_k.reshape(depth, dec_batch, N_MEM, X_WIDTH)
    cmv = cache_mem_v.reshape(depth, dec_batch, N_MEM, X_WIDTH)
    for l in range(depth):
        mw = _mixer_weights(l, g_mix, w_in, b_gate, q_norm_g, k_norm_g, sinks, w_conv_b, w_conv_c,
                            b_conv_c, ln_c_g, ln_c_b, w_proj_a, w_proj_b, w_proj_c, w_out)
        wq = w_xq[l].astype(BF16)
        wo = w_xo[l].astype(BF16)
        ffn_args = (g_ffn, w_ffn_gu, w_ffn_d, w_router, b_router, w_moe_gu, w_moe_d)
        yp, (nk, nv, ncb, ncc) = _mixer_prompt(yp, mw, bias_p, batch, seq)
        mk, mv = mem_kv(mem, g_mem[l], w_xkv[l].astype(BF16), xk_norm_g[l])
        mk3 = mk.reshape(batch, N_MEM, X_WIDTH)
        mv3 = mv.reshape(batch, N_MEM, X_WIDTH)
        yp = xattn(yp, g_xattn[l], wq, xq_norm_g[l], mk3[None], mv3[None], wo, layer=0, nb=1, rpb=512,
                   tiles_per_mem=seq // 512)
        for lst, v in zip(outs[:6], (mk3.reshape(batch, N_MEM, X_HEADS, X_HEAD_DIM),
                                     mv3.reshape(batch, N_MEM, X_HEADS, X_HEAD_DIM),
                                     nk.reshape(batch, WINDOW, A_KV_HEADS, A_HEAD_DIM),
                                     nv.reshape(batch, WINDOW, A_KV_HEADS, A_HEAD_DIM), ncb, ncc)):
            lst.append(v)
        ys, (nk, nv, ncb, ncc) = _mixer_sample(ys, mw, bias_s, dec_batch, dec_seq, ck, cv, l,
                                               state_conv_b[l], state_conv_c[l])
        ys = xattn(ys, g_xattn[l], wq, xq_norm_g[l], cmk, cmv, wo, layer=l, nb=8, rpb=dec_seq,
                   tiles_per_mem=1)
        yp, ys = _channel_mixer(yp, ys, l, *ffn_args)
        for lst, v in zip(outs[6:], (nk.reshape(dec_batch, dec_seq, A_KV_HEADS, A_HEAD_DIM),
                                     nv.reshape(dec_batch, dec_seq, A_KV_HEADS, A_HEAD_DIM), ncb, ncc)):
            lst.append(v)
    return (yp.reshape(batch, seq, d), ys.reshape(dec_batch, dec_seq, d)) + tuple(jnp.stack(o) for o in outs)
```

```python
import functools
import math

import numpy as np
import jax
import jax.numpy as jnp
from jax import lax
from jax.experimental import pallas as pl
from jax.experimental.pallas import tpu as pltpu

F32 = jnp.float32
BF16 = jnp.bfloat16

D_MODEL = 2048
CHUNK = 64
A_HEADS = 16
A_KV_HEADS = 4
A_HEAD_DIM = 64
A_WIDTH = A_HEADS * A_HEAD_DIM
A_KV_WIDTH = A_KV_HEADS * A_HEAD_DIM
WINDOW = 128
N_BUCKETS = 32
MAX_DISTANCE = 128
B_WIDTH = 512
B_CONV = 3
C_WIDTH = 512
C_CONV = 31
N_MEM = 256
X_HEADS = 4
X_HEAD_DIM = 128
X_WIDTH = X_HEADS * X_HEAD_DIM
D_FF = 5632
N_EXPERTS = 8
EPS = 1e-6

LANES = 128
KEY_TILE = 256
Q_TILE = 128
HALO = 32
NEG = -1e30
VMEM_LIMIT = 56 * 1024 * 1024

COL_Q = 0
COL_K, COL_V = 4, 5
COL_GB, COL_GC, COL_HB, COL_GA, COL_GG = 3, 4, 5, 6, 7
COL_GATE0 = 2
IN_COLS = 4096 + 3 * D_MODEL

HEAD_PERM = np.array([8 * n + (p % 2) * 4 + p // 2 for n in range(2) for p in range(8)])


def _cparams(sem):
    return pltpu.CompilerParams(dimension_semantics=sem, vmem_limit_bytes=VMEM_LIMIT)


def _rms(x, g):
    ms = jnp.mean(x * x, axis=-1, keepdims=True)
    return x * lax.rsqrt(ms + EPS) * g


def _resident(shape):
    nd = len(shape)
    return pl.BlockSpec(shape, lambda *_: (0,) * nd, pipeline_mode=pl.Buffered(1))


def _permuted_q_columns(acc):
    lo_lane = lax.broadcasted_iota(jnp.int32, (1, LANES), 1) < A_HEAD_DIM
    nat = [acc[:, c * LANES:(c + 1) * LANES] for c in range(A_WIDTH // LANES)]
    swapped = [pltpu.roll(x, A_HEAD_DIM, 1) for x in nat]
    cols = []
    for n in range(2):
        for m in range(4):
            ca, cb = 4 * n + m // 2, 4 * n + 2 + m // 2
            cols.append(jnp.where(lo_lane, nat[ca], swapped[cb]) if m % 2 == 0
                        else jnp.where(lo_lane, swapped[ca], nat[cb]))
    return cols


def _norm_matmul_kernel(x_ref, g_ref, w_ref, o_ref, hn_ref):
    j = pl.program_id(1)

    @pl.when(j == 0)
    def _():
        hn_ref[...] = _rms(x_ref[...], g_ref[...]).astype(BF16)
        acc = jnp.dot(hn_ref[...], w_ref[...], preferred_element_type=F32)
        for c, col in enumerate(_permuted_q_columns(acc)):
            o_ref[:, c * LANES:(c + 1) * LANES] = col.astype(o_ref.dtype)
        if o_ref.shape[1] > A_WIDTH:
            o_ref[:, A_WIDTH:] = acc[:, A_WIDTH:].astype(o_ref.dtype)

    @pl.when(j > 0)
    def _():
        o_ref[...] = jnp.dot(hn_ref[...], w_ref[...], preferred_element_type=F32).astype(o_ref.dtype)


def norm_matmul(x, g, w, layer, *, tm, tn, out_dtype):
    m, k = x.shape
    n = w.shape[2]
    return pl.pallas_call(
        _norm_matmul_kernel,
        grid=(m // tm, n // tn),
        in_specs=[pl.BlockSpec((tm, k), lambda i, j: (i, 0)),
                  pl.BlockSpec((1, k), lambda i, j: (0, 0)),
                  pl.BlockSpec((None, k, tn), lambda i, j: (layer, 0, j))],
        out_specs=pl.BlockSpec((tm, tn), lambda i, j: (i, j)),
        out_shape=jax.ShapeDtypeStruct((m, n), out_dtype),
        scratch_shapes=[pltpu.VMEM((tm, k), BF16)],
        compiler_params=_cparams(("parallel", "arbitrary")),
        name="norm_matmul",
    )(x, g.reshape(1, k), w)


def _mem_kv_kernel(x_ref, g_ref, w_ref, kg_ref, k_ref, v_ref):
    hn = _rms(x_ref[...], g_ref[...]).astype(BF16)
    kv = jnp.dot(hn, w_ref[...], preferred_element_type=F32)
    for h in range(X_HEADS):
        sl = slice(h * X_HEAD_DIM, (h + 1) * X_HEAD_DIM)
        k_ref[:, sl] = _rms(kv[:, sl], kg_ref[...])
    v_ref[...] = kv[:, X_WIDTH:]


def mem_kv(mem, g, w_bf, kg):
    m, k = mem.shape
    tm = 256
    return pl.pallas_call(
        _mem_kv_kernel,
        grid=(m // tm,),
        in_specs=[pl.BlockSpec((tm, k), lambda i: (i, 0)),
                  pl.BlockSpec((1, k), lambda i: (0, 0)),
                  _resident((k, 2 * X_WIDTH)),
                  pl.BlockSpec((1, X_HEAD_DIM), lambda i: (0, 0))],
        out_specs=[pl.BlockSpec((tm, X_WIDTH), lambda i: (i, 0)),
                   pl.BlockSpec((tm, X_WIDTH), lambda i: (i, 0))],
        out_shape=[jax.ShapeDtypeStruct((m, X_WIDTH), F32)] * 2,
        compiler_params=_cparams(("parallel",)),
        name="mem_kv",
    )(mem, g.reshape(1, k), w_bf, kg.reshape(1, X_HEAD_DIM))


def _half_norm(x, g):
    x2 = x * x
    lo_lane = lax.broadcasted_iota(jnp.int32, (1, LANES), 1) < A_HEAD_DIM
    s_lo = jnp.sum(jnp.where(lo_lane, x2, 0.0), axis=-1, keepdims=True)
    s_hi = jnp.sum(jnp.where(lo_lane, 0.0, x2), axis=-1, keepdims=True)
    ms = jnp.where(lo_lane, s_lo, s_hi) * (1.0 / A_HEAD_DIM)
    return x * lax.rsqrt(ms + EPS) * g


def _swa_heads(q_ref, qg, k2, v2, bias_ref, sink_ref, o_ref, *, stack):
    lo_lane = lax.broadcasted_iota(jnp.int32, (1, LANES), 1) < A_HEAD_DIM
    rows = q_ref.shape[0]
    cols = A_WIDTH // LANES // 2
    for n in range(2):
        k_half = (jnp.where(lo_lane, k2[n], 0.0).astype(BF16), jnp.where(lo_lane, 0.0, k2[n]).astype(BF16))
        for g in range(0, cols, stack):
            qs = [_half_norm(q_ref[:, c * LANES:(c + 1) * LANES].astype(F32), qg)
                  for c in range(cols * n + g, cols * n + g + stack)]
            qn = (qs[0] if stack == 1 else jnp.concatenate(qs, axis=0)).astype(BF16)
            at = slice(g * rows, (g + stack) * rows)
            halves = []
            for half in range(2):
                s = lax.dot_general(qn, k_half[half], (((1,), (1,)), ((), ())), preferred_element_type=F32)
                s = s + bias_ref[n, half, at, :]
                if sink_ref is None:
                    m = jnp.max(s, axis=-1, keepdims=True)
                    p = jnp.exp(s - m)
                    den = jnp.sum(p, axis=-1, keepdims=True)
                else:
                    sink = sink_ref[2 * (cols * n + g) + half]
                    m = jnp.maximum(jnp.max(s, axis=-1, keepdims=True), sink)
                    p = jnp.exp(s - m)
                    den = jnp.sum(p, axis=-1, keepdims=True) + jnp.exp(sink - m)
                o = jnp.dot(p.astype(BF16), v2[n], preferred_element_type=F32)
                halves.append(o / den)
            o = jnp.where(lo_lane, halves[0], halves[1]).astype(o_ref.dtype)
            for i in range(stack):
                c = cols * n + g + i
                o_ref[:, c * LANES:(c + 1) * LANES] = o[i * rows:(i + 1) * rows]


def _swa_prompt_kernel(q_ref, kc_ref, kp_ref, vc_ref, vp_ref, qg_ref, kg_ref,
                       bias_ref, sink_ref, o_ref, nk_ref, nv_ref):
    k2, v2 = [], []
    for n in range(2):
        sl = slice(n * LANES, (n + 1) * LANES)
        kcat = jnp.concatenate([kp_ref[:, sl], kc_ref[:, sl]], axis=0).astype(F32)
        kn = _half_norm(kcat, kg_ref[...])
        nk_ref[0, :, sl] = kn[Q_TILE:]
        k2.append(kn)
        v2.append(jnp.concatenate([vp_ref[:, sl], vc_ref[:, sl]], axis=0))
    nv_ref[0] = vc_ref[...].astype(F32)
    _swa_heads(q_ref, qg_ref[...], k2, v2, bias_ref, sink_ref, o_ref, stack=1)


def swa_prompt(z, batch, seq, qg2, kg2, bias, sinks):
    nt = seq // Q_TILE
    row = lambda b, t: b * nt + t
    prev = lambda b, t: jnp.maximum(b * nt + t - 1, 0)
    return pl.pallas_call(
        _swa_prompt_kernel,
        grid=(batch, nt),
        in_specs=[pl.BlockSpec((Q_TILE, A_WIDTH), lambda b, t: (row(b, t), COL_Q)),
                  pl.BlockSpec((Q_TILE, A_KV_WIDTH), lambda b, t: (row(b, t), COL_K)),
                  pl.BlockSpec((Q_TILE, A_KV_WIDTH), lambda b, t: (prev(b, t), COL_K)),
                  pl.BlockSpec((Q_TILE, A_KV_WIDTH), lambda b, t: (row(b, t), COL_V)),
                  pl.BlockSpec((Q_TILE, A_KV_WIDTH), lambda b, t: (prev(b, t), COL_V)),
                  pl.BlockSpec((1, LANES), lambda b, t: (0, 0)),
                  pl.BlockSpec((1, LANES), lambda b, t: (0, 0)),
                  pl.BlockSpec((None, 2, 2, 4 * Q_TILE, KEY_TILE),
                               lambda b, t: (jnp.minimum(t, 1), 0, 0, 0, 0)),
                  pl.BlockSpec(memory_space=pltpu.SMEM)],
        out_specs=[pl.BlockSpec((Q_TILE, A_WIDTH), lambda b, t: (row(b, t), 0)),
                   pl.BlockSpec((1, WINDOW, A_KV_WIDTH), lambda b, t: (b, 0, 0)),
                   pl.BlockSpec((1, WINDOW, A_KV_WIDTH), lambda b, t: (b, 0, 0))],
        out_shape=[jax.ShapeDtypeStruct((batch * seq, A_WIDTH), BF16),
                   jax.ShapeDtypeStruct((batch, WINDOW, A_KV_WIDTH), F32),
                   jax.ShapeDtypeStruct((batch, WINDOW, A_KV_WIDTH), F32)],
        compiler_params=_cparams(("parallel", "arbitrary")),
        name="swa_prompt",
    )(z, z, z, z, z, qg2, kg2, bias, sinks)


def _swa_sample_kernel(q_ref, kn_ref, vn_ref, ck_ref, cv_ref, qg_ref, kg_ref,
                       bias_ref, o_ref, nk_ref, nv_ref):
    rows = q_ref.shape[0]
    pad = KEY_TILE - WINDOW - rows
    k2, v2 = [], []
    for n in range(2):
        sl = slice(n * LANES, (n + 1) * LANES)
        kn = _half_norm(kn_ref[:, sl].astype(F32), kg_ref[...])
        nk_ref[0, :, sl] = kn
        k2.append(jnp.concatenate([ck_ref[0, :, sl], kn, jnp.zeros((pad, LANES), F32)], axis=0))
        v2.append(jnp.concatenate([cv_ref[0, :, sl].astype(BF16), vn_ref[:, sl],
                                   jnp.zeros((pad, LANES), BF16)], axis=0))
    nv_ref[0] = vn_ref[...].astype(F32)
    _swa_heads(q_ref, qg_ref[...], k2, v2, bias_ref, None, o_ref, stack=4)


def swa_sample(z, batch, rows, cache_k, cache_v, layer, qg2, kg2, bias):
    return pl.pallas_call(
        _swa_sample_kernel,
        grid=(batch,),
        in_specs=[pl.BlockSpec((rows, A_WIDTH), lambda b: (b, COL_Q)),
                  pl.BlockSpec((rows, A_KV_WIDTH), lambda b: (b, COL_K)),
                  pl.BlockSpec((rows, A_KV_WIDTH), lambda b: (b, COL_V)),
                  pl.BlockSpec((None, 1, WINDOW, A_KV_WIDTH), lambda b: (layer, b, 0, 0)),
                  pl.BlockSpec((None, 1, WINDOW, A_KV_WIDTH), lambda b: (layer, b, 0, 0)),
                  pl.BlockSpec((1, LANES), lambda b: (0, 0)),
                  pl.BlockSpec((1, LANES), lambda b: (0, 0)),
                  _resident((2, 2, 4 * rows, KEY_TILE))],
        out_specs=[pl.BlockSpec((rows, A_WIDTH), lambda b: (b, 0)),
                   pl.BlockSpec((1, rows, A_KV_WIDTH), lambda b: (b, 0, 0)),
                   pl.BlockSpec((1, rows, A_KV_WIDTH), lambda b: (b, 0, 0))],
        out_shape=[jax.ShapeDtypeStruct((batch * rows, A_WIDTH), BF16),
                   jax.ShapeDtypeStruct((batch, rows, A_KV_WIDTH), F32),
                   jax.ShapeDtypeStruct((batch, rows, A_KV_WIDTH), F32)],
        compiler_params=_cparams(("parallel",)),
        name="swa_sample",
    )(z, z, z, cache_k, cache_v, qg2, kg2, bias)


def _conv_body(gb_ref, ub_main, uc_main, ub_halo, uc_halo, wb_ref, wc_ref, bc_ref, lg_ref, lb_ref,
               ob_ref, oc_ref, nb_ref, nc_ref, sb_ref, sc_ref, ph_ref, write_state):
    rows = ub_main.shape[0]
    sb_ref[0:HALO] = ub_halo
    sb_ref[HALO:HALO + rows] = ub_main
    sc_ref[0:HALO] = uc_halo
    sc_ref[HALO:HALO + rows] = uc_main
    span = rows + HALO
    sc_ref[span:span + 8] = jnp.zeros((8, C_WIDTH), F32)
    for b in range(1, 8):
        ph_ref[b - 1] = sc_ref[pl.ds(b, span), :]
    sub = min(rows, 32)
    for r0 in range(0, rows, sub):
        yb = jnp.zeros((sub, B_WIDTH), F32)
        for k in range(B_CONV):
            yb = yb + wb_ref[k:k + 1, :] * sb_ref[pl.ds(r0 + HALO - (B_CONV - 1) + k, sub), :]
        ob_ref[r0:r0 + sub, :] = (gb_ref[r0:r0 + sub, :].astype(F32) * yb).astype(ob_ref.dtype)
        yc = jnp.zeros((sub, C_WIDTH), F32)
        for k in range(C_CONV):
            shift, phase = divmod(HALO - (C_CONV - 1) + k, 8)
            at = pl.ds(r0 + 8 * shift, sub)
            taps = sc_ref[at, :] if phase == 0 else ph_ref[phase - 1, at, :]
            yc = yc + wc_ref[k:k + 1, :] * taps
        yc = yc + bc_ref[...]
        mu = jnp.mean(yc, axis=-1, keepdims=True)
        xc = yc - mu
        y = xc * lax.rsqrt(jnp.mean(xc * xc, axis=-1, keepdims=True) + EPS)
        y = y * lg_ref[...] + lb_ref[...]
        oc_ref[r0:r0 + sub, :] = (y * jax.nn.sigmoid(y)).astype(oc_ref.dtype)

    def _state():
        nb_ref[0] = sb_ref[rows + HALO - 8:rows + HALO]
        nc_ref[0] = sc_ref[rows:rows + HALO]

    write_state(_state)


def _conv_prompt_kernel(gb_ref, gc_ref, hb_ref, ga_ref, gg_ref, gch_ref, hbh_ref, gah_ref, ggh_ref,
                        wb_ref, wc_ref, bc_ref, lg_ref, lb_ref,
                        ob_ref, oc_ref, nb_ref, nc_ref, sb_ref, sc_ref, ph_ref):
    t = pl.program_id(1)
    hist = (t > 0).astype(F32)
    ub_main = gc_ref[...].astype(F32) * hb_ref[...].astype(F32)
    uc_main = ga_ref[...].astype(F32) * jax.nn.sigmoid(gg_ref[...].astype(F32))
    ub_halo = gch_ref[...].astype(F32) * hbh_ref[...].astype(F32) * hist
    uc_halo = gah_ref[...].astype(F32) * jax.nn.sigmoid(ggh_ref[...].astype(F32)) * hist
    last = pl.num_programs(1) - 1
    _conv_body(gb_ref, ub_main, uc_main, ub_halo, uc_halo, wb_ref, wc_ref, bc_ref, lg_ref, lb_ref,
               ob_ref, oc_ref, nb_ref, nc_ref, sb_ref, sc_ref, ph_ref,
               lambda f: pl.when(t == last)(f))


def conv_prompt(z, batch, seq, wb, wc, bc, lg, lb, *, tr=256):
    nt = seq // tr
    hp = tr // HALO
    main = lambda c: pl.BlockSpec((tr, B_WIDTH), lambda b, t: (b * nt + t, c))
    halo = lambda c: pl.BlockSpec((HALO, B_WIDTH), lambda b, t: (jnp.maximum((b * nt + t) * hp - 1, 0), c))
    vec = lambda r: pl.BlockSpec((r, B_WIDTH), lambda b, t: (0, 0))
    return pl.pallas_call(
        _conv_prompt_kernel,
        grid=(batch, nt),
        in_specs=[main(COL_GB), main(COL_GC), main(COL_HB), main(COL_GA), main(COL_GG),
                  halo(COL_GC), halo(COL_HB), halo(COL_GA), halo(COL_GG),
                  vec(B_CONV), vec(C_CONV), vec(1), vec(1), vec(1)],
        out_specs=[pl.BlockSpec((tr, B_WIDTH), lambda b, t: (b * nt + t, 0)),
                   pl.BlockSpec((tr, C_WIDTH), lambda b, t: (b * nt + t, 0)),
                   pl.BlockSpec((1, 8, B_WIDTH), lambda b, t: (b, 0, 0)),
                   pl.BlockSpec((1, HALO, C_WIDTH), lambda b, t: (b, 0, 0))],
        out_shape=[jax.ShapeDtypeStruct((batch * seq, B_WIDTH), BF16),
                   jax.ShapeDtypeStruct((batch * seq, C_WIDTH), BF16),
                   jax.ShapeDtypeStruct((batch, 8, B_WIDTH), F32),
                   jax.ShapeDtypeStruct((batch, HALO, C_WIDTH), F32)],
        scratch_shapes=[pltpu.VMEM((tr + HALO, B_WIDTH), F32), pltpu.VMEM((tr + HALO + 8, C_WIDTH), F32),
                        pltpu.VMEM((7, tr + HALO, C_WIDTH), F32)],
        compiler_params=_cparams(("parallel", "arbitrary")),
        name="conv_prompt",
    )(z, z, z, z, z, z, z, z, z, wb, wc, bc.reshape(1, -1), lg.reshape(1, -1), lb.reshape(1, -1))


def _conv_sample_kernel(gb_ref, gc_ref, hb_ref, ga_ref, gg_ref, stb_ref, stc_ref,
                        wb_ref, wc_ref, bc_ref, lg_ref, lb_ref,
                        ob_ref, oc_ref, nb_ref, nc_ref, sb_ref, sc_ref, ph_ref):
    ub_main = gc_ref[...].astype(F32) * hb_ref[...].astype(F32)
    uc_main = ga_ref[...].astype(F32) * jax.nn.sigmoid(gg_ref[...].astype(F32))
    _conv_body(gb_ref, ub_main, uc_main, stb_ref[0], stc_ref[0], wb_ref, wc_ref, bc_ref, lg_ref, lb_ref,
               ob_ref, oc_ref, nb_ref, nc_ref, sb_ref, sc_ref, ph_ref, lambda f: f())


def conv_sample(z, batch, rows, stb, stc, wb, wc, bc, lg, lb):
    main = lambda c: pl.BlockSpec((rows, B_WIDTH), lambda b: (b, c))
    vec = lambda r: pl.BlockSpec((r, B_WIDTH), lambda b: (0, 0))
    return pl.pallas_call(
        _conv_sample_kernel,
        grid=(batch,),
        in_specs=[main(COL_GB), main(COL_GC), main(COL_HB), main(COL_GA), main(COL_GG),
                  pl.BlockSpec((1, HALO, B_WIDTH), lambda b: (b, 0, 0)),
                  pl.BlockSpec((1, HALO, C_WIDTH), lambda b: (b, 0, 0)),
                  vec(B_CONV), vec(C_CONV), vec(1), vec(1), vec(1)],
        out_specs=[pl.BlockSpec((rows, B_WIDTH), lambda b: (b, 0)),
                   pl.BlockSpec((rows, C_WIDTH), lambda b: (b, 0)),
                   pl.BlockSpec((1, 8, B_WIDTH), lambda b: (b, 0, 0)),
                   pl.BlockSpec((1, HALO, C_WIDTH), lambda b: (b, 0, 0))],
        out_shape=[jax.ShapeDtypeStruct((batch * rows, B_WIDTH), BF16),
                   jax.ShapeDtypeStruct((batch * rows, C_WIDTH), BF16),
                   jax.ShapeDtypeStruct((batch, 8, B_WIDTH), F32),
                   jax.ShapeDtypeStruct((batch, HALO, C_WIDTH), F32)],
        scratch_shapes=[pltpu.VMEM((rows + HALO, B_WIDTH), F32), pltpu.VMEM((rows + HALO + 8, C_WIDTH), F32),
                        pltpu.VMEM((7, rows + HALO, C_WIDTH), F32)],
        compiler_params=_cparams(("parallel",)),
        name="conv_sample",
    )(z, z, z, z, z, stb, stc, wb, wc, bc.reshape(1, -1), lg.reshape(1, -1), lb.reshape(1, -1))


def _merge_kernel(oa_ref, ob_ref, oc_ref, l0_ref, l1_ref, l2_ref, bg_ref,
                  wpa_ref, wpb_ref, wpc_ref, wo_ref, x_ref, o_ref):
    def gated(l_ref, i, o_r, w_r):
        gate = jax.nn.sigmoid(l_ref[...].astype(F32) + bg_ref[i:i + 1, :])
        return gate * jnp.dot(o_r[...], w_r[...], preferred_element_type=F32)

    merged = gated(l0_ref, 0, oa_ref, wpa_ref)
    merged = merged + gated(l1_ref, 1, ob_ref, wpb_ref)
    merged = merged + gated(l2_ref, 2, oc_ref, wpc_ref)
    o_ref[...] = x_ref[...] + jnp.dot(merged.astype(BF16), wo_ref[...], preferred_element_type=F32)


def merge(oa, ob, oc, z, bg, wpa, wpb, wpc, wo, x, *, tm=512):
    m = x.shape[0]
    tm = min(tm, m)
    rows = lambda w: pl.BlockSpec((tm, w), lambda i: (i, 0))
    gate = lambda c: pl.BlockSpec((tm, D_MODEL), lambda i: (i, COL_GATE0 + c))
    return pl.pallas_call(
        _merge_kernel,
        grid=(m // tm,),
        in_specs=[rows(A_WIDTH), rows(B_WIDTH), rows(C_WIDTH), gate(0), gate(1), gate(2),
                  pl.BlockSpec((3, D_MODEL), lambda i: (0, 0)),
                  _resident((A_WIDTH, D_MODEL)), _resident((B_WIDTH, D_MODEL)),
                  _resident((C_WIDTH, D_MODEL)), _resident((D_MODEL, D_MODEL)),
                  rows(D_MODEL)],
        out_specs=rows(D_MODEL),
        out_shape=jax.ShapeDtypeStruct((m, D_MODEL), F32),
        compiler_params=_cparams(("parallel",)),
        name="merge",
    )(oa, ob, oc, z, z, z, bg.reshape(3, D_MODEL), wpa, wpb, wpc, wo, x)


def _xattn_kernel(y_ref, g_ref, wq_ref, qg_ref, mk_ref, mv_ref, wo_ref, o_ref, *, nb, rpb):
    y = y_ref[...]
    hn = _rms(y, g_ref[...]).astype(BF16)
    q = jnp.dot(hn, wq_ref[...], preferred_element_type=F32)
    heads = []
    for h in range(X_HEADS):
        sl = slice(h * X_HEAD_DIM, (h + 1) * X_HEAD_DIM)
        qh = _rms(q[:, sl], qg_ref[...])
        per_batch = []
        for b in range(nb):
            qb = qh[b * rpb:(b + 1) * rpb].astype(BF16)
            kh = mk_ref[b, :, sl].astype(BF16)
            vh = mv_ref[b, :, sl].astype(BF16)
            s = lax.dot_general(qb, kh, (((1,), (1,)), ((), ())),
                                preferred_element_type=F32) * (X_HEAD_DIM ** -0.5)
            m = jnp.max(s, axis=-1, keepdims=True)
            p = jnp.exp(s - m)
            den = jnp.sum(p, axis=-1, keepdims=True)
            per_batch.append(jnp.dot(p.astype(BF16), vh, preferred_element_type=F32) / den)
        heads.append(per_batch[0] if nb == 1 else jnp.concatenate(per_batch, axis=0))
    o = jnp.concatenate(heads, axis=1).astype(BF16)
    o_ref[...] = y + jnp.dot(o, wo_ref[...], preferred_element_type=F32)


def xattn(y, g, wq, qg, mk, mv, wo, *, layer, nb, rpb, tiles_per_mem):
    m = y.shape[0]
    tm = nb * rpb
    mem_idx = ((lambda i: (layer, i // tiles_per_mem, 0, 0)) if nb == 1
               else (lambda i: (layer, i, 0, 0)))
    return pl.pallas_call(
        functools.partial(_xattn_kernel, nb=nb, rpb=rpb),
        grid=(m // tm,),
        in_specs=[pl.BlockSpec((tm, D_MODEL), lambda i: (i, 0)),
                  pl.BlockSpec((1, D_MODEL), lambda i: (0, 0)),
                  _resident((D_MODEL, X_WIDTH)),
                  pl.BlockSpec((1, X_HEAD_DIM), lambda i: (0, 0)),
                  pl.BlockSpec((None, nb, N_MEM, X_WIDTH), mem_idx),
                  pl.BlockSpec((None, nb, N_MEM, X_WIDTH), mem_idx),
                  _resident((X_WIDTH, D_MODEL))],
        out_specs=pl.BlockSpec((tm, D_MODEL), lambda i: (i, 0)),
        out_shape=jax.ShapeDtypeStruct((m, D_MODEL), F32),
        compiler_params=_cparams(("parallel",)),
        name="xattn",
    )(y, g.reshape(1, -1), wq, qg.reshape(1, -1), mk, mv, wo)


def _ffn_kernel(x_ref, g_ref, wg_ref, wu_ref, wd_ref, o_ref, hn_ref):
    @pl.when(pl.program_id(1) == 0)
    def _():
        x = x_ref[...]
        hn_ref[...] = _rms(x, g_ref[...]).astype(BF16)
        o_ref[...] = x

    hn = hn_ref[...]
    g = jnp.dot(hn, wg_ref[...].astype(BF16), preferred_element_type=F32)
    u = jnp.dot(hn, wu_ref[...].astype(BF16), preferred_element_type=F32)
    a = (g * jax.nn.sigmoid(g) * u).astype(BF16)
    o_ref[...] += jnp.dot(a, wd_ref[...].astype(BF16), preferred_element_type=F32)


def ffn(x, g, w_gu, w_d, *, tm=1024, tf=512):
    m = x.shape[0]
    tm = min(tm, m)
    nf = D_FF // tf
    return pl.pallas_call(
        _ffn_kernel,
        grid=(m // tm, nf),
        in_specs=[pl.BlockSpec((tm, D_MODEL), lambda i, f: (i, 0), pipeline_mode=pl.Buffered(1)),
                  pl.BlockSpec((1, D_MODEL), lambda i, f: (0, 0)),
                  pl.BlockSpec((D_MODEL, tf), lambda i, f: (0, f)),
                  pl.BlockSpec((D_MODEL, tf), lambda i, f: (0, nf + f)),
                  pl.BlockSpec((tf, D_MODEL), lambda i, f: (f, 0))],
        out_specs=pl.BlockSpec((tm, D_MODEL), lambda i, f: (i, 0), pipeline_mode=pl.Buffered(1)),
        out_shape=jax.ShapeDtypeStruct((m, D_MODEL), F32),
        scratch_shapes=[pltpu.VMEM((tm, D_MODEL), BF16)],
        compiler_params=_cparams(("parallel", "arbitrary")),
        name="ffn",
    )(x, g.reshape(1, -1), w_gu, w_gu, w_d)


def _split3(x):
    hi = x.astype(BF16)
    lo = (x - hi.astype(F32)).astype(BF16)
    return hi, lo


def _router_kernel(x_ref, g_ref, w_ref, b_ref, tri_ref, base_ref, wts_ref, ids_ref, rank_ref, cnt_ref, run_ref):
    @pl.when(pl.program_id(0) == 0)
    def _():
        run_ref[...] = base_ref[...]

    hn = _rms(x_ref[...], g_ref[...])
    h_hi, h_lo = _split3(hn)
    w_hi, w_lo = _split3(w_ref[...])
    dot = functools.partial(jnp.dot, preferred_element_type=F32)
    logits = dot(h_hi, w_hi) + dot(h_hi, w_lo) + dot(h_lo, w_hi) + b_ref[...]
    lane = lax.broadcasted_iota(jnp.int32, logits.shape, 1).astype(F32)
    logits = jnp.where(lane < N_EXPERTS, logits, -jnp.inf)
    v1 = jnp.max(logits, axis=-1, keepdims=True)
    i1 = jnp.min(jnp.where(logits == v1, lane, float(LANES)), axis=-1, keepdims=True)
    rest = jnp.where(lane == i1, -jnp.inf, logits)
    v2 = jnp.max(rest, axis=-1, keepdims=True)
    i2 = jnp.min(jnp.where(rest == v2, lane, float(LANES)), axis=-1, keepdims=True)
    e2 = jnp.exp(v2 - v1)
    den = 1.0 + e2
    first, second = lane == 0.0, lane == 1.0
    wts_ref[...] = jnp.where(first, 1.0 / den, 0.0) + jnp.where(second, e2 / den, 0.0)
    ids_ref[...] = (jnp.where(first, i1, 0.0) + jnp.where(second, i2, 0.0)).astype(jnp.int32)
    hit1, hit2 = lane == i1, lane == i2
    hits = jnp.where(jnp.logical_or(hit1, hit2), 1.0, 0.0)
    before = dot(tri_ref[...], hits.astype(BF16)) + run_ref[...]
    r1 = jnp.sum(jnp.where(hit1, before, 0.0), axis=-1, keepdims=True)
    r2 = jnp.sum(jnp.where(hit2, before, 0.0), axis=-1, keepdims=True)
    rank_ref[...] = (jnp.where(first, r1, 0.0) + jnp.where(second, r2, 0.0)).astype(jnp.int32)
    run_ref[...] += jnp.sum(hits, axis=0, keepdims=True)
    cnt_ref[...] = run_ref[...]


def router(x, g, w_pad, b_pad, base, *, tm=512):
    m = x.shape[0]
    tm = min(tm, m)
    tri = jnp.asarray(np.tril(np.ones((tm, tm), np.float32), -1), BF16)
    row = lambda w: pl.BlockSpec((tm, w), lambda i: (i, 0))
    fixed = lambda s: pl.BlockSpec(s, lambda i: (0, 0))
    return pl.pallas_call(
        _router_kernel,
        grid=(m // tm,),
        in_specs=[row(D_MODEL), fixed((1, D_MODEL)), fixed((D_MODEL, LANES)), fixed((1, LANES)),
                  fixed((tm, tm)), fixed((1, LANES))],
        out_specs=[row(LANES), row(LANES), row(LANES), fixed((1, LANES))],
        out_shape=[jax.ShapeDtypeStruct((m, LANES), F32),
                   jax.ShapeDtypeStruct((m, LANES), jnp.int32),
                   jax.ShapeDtypeStruct((m, LANES), jnp.int32),
                   jax.ShapeDtypeStruct((1, LANES), F32)],
        scratch_shapes=[pltpu.VMEM((1, LANES), F32)],
        compiler_params=_cparams(("arbitrary",)),
        name="router",
    )(x, g.reshape(1, -1), w_pad, b_pad, tri, base)


MOE_TM = 1024
MOE_TF = 512


def _row_copy(src, row, dst, r, sem):
    return pltpu.make_async_copy(src.at[pl.ds(row, 1)], dst.at[pl.ds(r, 1)], sem)


def _moe_dispatch_kernel(pos_ref, x_ref, g_ref, xs_in, xs_hbm, pk_ref, sem):
    del xs_in
    tm = x_ref.shape[0]
    half = D_MODEL // 2
    hn = _rms(x_ref[...], g_ref[...]).astype(BF16)
    lo_bits = lax.bitcast_convert_type(hn[:, :half].astype(F32), jnp.uint32)
    hi_bits = lax.bitcast_convert_type(hn[:, half:].astype(F32), jnp.uint32)
    pk_ref[...] = (lo_bits >> 16) | (hi_bits & jnp.uint32(0xFFFF0000))

    def issue(r, c):
        _row_copy(pk_ref, r, xs_hbm, pos_ref[0, 0, r], sem).start()
        _row_copy(pk_ref, r, xs_hbm, pos_ref[0, 0, tm + r], sem).start()
        return c

    lax.fori_loop(0, tm, issue, 0, unroll=8)

    def wait(r, c):
        _row_copy(pk_ref, r, xs_hbm, 0, sem).wait()
        _row_copy(pk_ref, r, xs_hbm, 0, sem).wait()
        return c

    lax.fori_loop(0, tm, wait, 0, unroll=8)


def moe_dispatch(pos, x, g, xs, *, tm):
    m = x.shape[0]
    return pl.pallas_call(
        _moe_dispatch_kernel,
        grid=(m // tm,),
        in_specs=[pl.BlockSpec((1, 1, 2 * tm), lambda i: (i, 0, 0), memory_space=pltpu.SMEM),
                  pl.BlockSpec((tm, D_MODEL), lambda i: (i, 0)),
                  pl.BlockSpec((1, D_MODEL), lambda i: (0, 0)),
                  pl.BlockSpec(memory_space=pl.ANY)],
        out_specs=pl.BlockSpec(memory_space=pl.ANY),
        out_shape=jax.ShapeDtypeStruct(xs.shape, xs.dtype),
        scratch_shapes=[pltpu.VMEM((tm, D_MODEL // 2), jnp.uint32), pltpu.SemaphoreType.DMA],
        input_output_aliases={3: 0},
        compiler_params=_cparams(("arbitrary",)),
        name="moe_dispatch",
    )(pos, x, g.reshape(1, -1), xs)


def _moe_ffn_kernel(te_ref, tv_ref, xs_ref, wg_ref, wu_ref, wd_ref, o_ref, hn_ref):
    t = pl.program_id(0)
    f = pl.program_id(1)
    rows = xs_ref.shape[0]
    half = D_MODEL // 2

    @pl.when(f == 0)
    def _():
        o_ref[...] = jnp.zeros_like(o_ref)
        xu = xs_ref[...]
        hn_ref[:, :half] = lax.bitcast_convert_type(xu << 16, F32).astype(BF16)
        hn_ref[:, half:] = lax.bitcast_convert_type(xu & jnp.uint32(0xFFFF0000), F32).astype(BF16)

    def swiglu(r):
        hn = hn_ref[0:r]
        g = jnp.dot(hn, wg_ref[...].astype(BF16), preferred_element_type=F32)
        u = jnp.dot(hn, wu_ref[...].astype(BF16), preferred_element_type=F32)
        a = (g * jax.nn.sigmoid(g) * u).astype(BF16)
        o_ref[0:r] += jnp.dot(a, wd_ref[...].astype(BF16), preferred_element_type=F32)

    nv = tv_ref[t]

    @pl.when(nv > rows // 2)
    def _():
        swiglu(rows)

    @pl.when(jnp.logical_and(nv > rows // 4, nv <= rows // 2))
    def _():
        swiglu(rows // 2)

    @pl.when(jnp.logical_and(nv > 0, nv <= rows // 4))
    def _():
        swiglu(rows // 4)


def moe_ffn(tile_expert, tile_rows, xs, w_gu, w_d):
    tm = MOE_TM
    nt = xs.shape[0] // tm
    nf = D_FF // MOE_TF
    last = nf - 1
    col = lambda f, tv, t: jnp.where(tv[t] != 0, f, last)
    grid_spec = pltpu.PrefetchScalarGridSpec(
        num_scalar_prefetch=2,
        grid=(nt, nf),
        in_specs=[pl.BlockSpec((tm, D_MODEL // 2), lambda t, f, te, tv: (t, 0), pipeline_mode=pl.Buffered(1)),
                  pl.BlockSpec((None, D_MODEL, MOE_TF), lambda t, f, te, tv: (te[t], 0, col(f, tv, t))),
                  pl.BlockSpec((None, D_MODEL, MOE_TF), lambda t, f, te, tv: (te[t], 0, nf + col(f, tv, t))),
                  pl.BlockSpec((None, MOE_TF, D_MODEL), lambda t, f, te, tv: (te[t], col(f, tv, t), 0))],
        out_specs=pl.BlockSpec((tm, D_MODEL), lambda t, f, te, tv: (t, 0)),
        scratch_shapes=[pltpu.VMEM((tm, D_MODEL), BF16)],
    )
    return pl.pallas_call(
        _moe_ffn_kernel,
        grid_spec=grid_spec,
        out_shape=jax.ShapeDtypeStruct((nt * tm, D_MODEL), F32),
        compiler_params=_cparams(("arbitrary", "arbitrary")),
        name="moe_ffn",
    )(tile_expert, tile_rows, xs, w_gu, w_gu, w_d)


def _moe_combine_kernel(pos_ref, x_ref, w_ref, osort_hbm, o_ref, abuf, sem):
    tm = x_ref.shape[0]

    def issue(r, c):
        _row_copy(osort_hbm, pos_ref[0, 0, r], abuf, r, sem).start()
        return c

    lax.fori_loop(0, 2 * tm, issue, 0, unroll=8)

    def wait(r, c):
        _row_copy(osort_hbm, 0, abuf, r, sem).wait()
        return c

    lax.fori_loop(0, 2 * tm, wait, 0, unroll=8)
    w = w_ref[...]
    o_ref[...] = x_ref[...] + w[:, 0:1] * abuf[0:tm, :] + w[:, 1:2] * abuf[tm:2 * tm, :]


def moe_combine(pos, x, wts, osort, *, tm=256):
    m = x.shape[0]
    return pl.pallas_call(
        _moe_combine_kernel,
        grid=(m // tm,),
        in_specs=[pl.BlockSpec((1, 1, 2 * tm), lambda i: (i, 0, 0), memory_space=pltpu.SMEM),
                  pl.BlockSpec((tm, D_MODEL), lambda i: (i, 0)),
                  pl.BlockSpec((tm, LANES), lambda i: (i, 0)),
                  pl.BlockSpec(memory_space=pl.ANY)],
        out_specs=pl.BlockSpec((tm, D_MODEL), lambda i: (i, 0)),
        out_shape=jax.ShapeDtypeStruct((m, D_MODEL), F32),
        scratch_shapes=[pltpu.VMEM((2 * tm, D_MODEL), F32), pltpu.SemaphoreType.DMA],
        compiler_params=_cparams(("arbitrary",)),
        name="moe_combine",
    )(pos, x, wts, osort)


def _dispatch_plan(counts, ids, rank, tm, nt):
    padded = ((counts + tm - 1) // tm) * tm
    ends = jnp.cumsum(padded)
    offs = ends - padded
    off_of = jnp.zeros_like(ids)
    for e in range(N_EXPERTS):
        off_of = jnp.where(ids == e, offs[e], off_of)
    pos = off_of + rank
    starts = jnp.arange(nt, dtype=jnp.int32) * tm
    tile_expert = jnp.minimum(jnp.sum((starts[:, None] >= ends[None, :]).astype(jnp.int32), axis=1),
                              N_EXPERTS - 1)
    real_end = (offs + counts)[tile_expert]
    tile_rows = jnp.where(starts < ends[-1], jnp.clip(real_end - starts, 0, tm), 0).astype(jnp.int32)
    last_valid = jnp.maximum(jnp.sum((tile_rows != 0).astype(jnp.int32)) - 1, 0)
    tile_expert = jnp.where(tile_rows != 0, tile_expert, tile_expert[last_valid])
    return tile_expert, tile_rows, pos


def _combine_pos(pos, tm):
    m = pos.shape[0]
    return jnp.transpose(pos.reshape(m // tm, tm, 2), (0, 2, 1)).reshape(m // tm, 1, 2 * tm)


def _t5_bucket_np(rel):
    nb = N_BUCKETS // 2
    max_exact = nb // 2
    ret = np.where(rel > 0, nb, 0)
    n = np.abs(rel)
    nf = np.maximum(n, 1).astype(np.float32)
    large = max_exact + (np.log(nf / np.float32(max_exact)) / np.float32(math.log(MAX_DISTANCE / max_exact))
                         * np.float32(nb - max_exact)).astype(np.int32)
    large = np.minimum(large, nb - 1)
    return (ret + np.where(n < max_exact, n, large)).astype(np.int32)


def _bias_tensor(rel_table, n_q, valid):
    rel = np.arange(KEY_TILE, dtype=np.int32)[None, :] - WINDOW - np.arange(n_q, dtype=np.int32)[:, None]
    onehot = np.eye(N_BUCKETS, dtype=np.float32)[:, _t5_bucket_np(rel).reshape(-1)]
    bias = jnp.dot(rel_table.T[HEAD_PERM], jnp.asarray(onehot), precision=lax.Precision.HIGHEST)
    return jnp.where(jnp.asarray(valid)[None], bias.reshape(A_HEADS, n_q, KEY_TILE), NEG).astype(F32)


def _stack_heads(x):
    _, r, c = x.shape
    return jnp.transpose(x.reshape(2, 4, 2, r, c), (0, 2, 1, 3, 4)).reshape(2, 2, 4 * r, c)


def _with_sink_column(bias, sinks_perm):
    return _stack_heads(bias.at[:, :, KEY_TILE - 1].set(sinks_perm[:, None]))


def _prompt_valid():
    qc = np.arange(Q_TILE)[:, None] // CHUNK
    kc = np.arange(KEY_TILE)[None, :] // CHUNK
    return (kc >= qc) & (kc <= qc + WINDOW // CHUNK)


def _sample_valid(rows):
    return np.broadcast_to(np.arange(KEY_TILE)[None, :] < WINDOW + rows, (rows, KEY_TILE))


def _mixer_weights(l, g_mix, w_in_bf, b_gate, q_norm_g, k_norm_g, sinks, w_conv_b, w_conv_c, b_conv_c,
                   ln_c_g, ln_c_b, w_proj_a, w_proj_b, w_proj_c, w_out):
    wpa = w_proj_a[l].reshape(A_HEADS, A_HEAD_DIM, D_MODEL)[HEAD_PERM].reshape(A_WIDTH, D_MODEL)
    return dict(
        g_mix=g_mix[l], w_in=w_in_bf, layer=l, b_gate=b_gate[l],
        qg2=(jnp.tile(q_norm_g[l], 2) * (A_HEAD_DIM ** -0.5)).reshape(1, LANES), kg2=jnp.tile(k_norm_g[l], 2).reshape(1, LANES),
        sinks=sinks[l][HEAD_PERM],
        w_cb=w_conv_b[l], w_cc=w_conv_c[l], b_cc=b_conv_c[l], ln_g=ln_c_g[l], ln_b=ln_c_b[l],
        wpa=wpa.astype(BF16), wpb=w_proj_b[l].astype(BF16), wpc=w_proj_c[l].astype(BF16),
        wo=w_out[l].astype(BF16))


def _mixer_prompt(x, mw, bias, batch, seq):
    z = norm_matmul(x, mw["g_mix"], mw["w_in"], mw["layer"], tm=1024, tn=2048, out_dtype=BF16)
    oa, nk, nv = swa_prompt(z, batch, seq, mw["qg2"], mw["kg2"], bias, mw["sinks"])
    ob, oc, ncb, ncc = conv_prompt(z, batch, seq, mw["w_cb"], mw["w_cc"], mw["b_cc"], mw["ln_g"], mw["ln_b"])
    y = merge(oa, ob, oc, z, mw["b_gate"], mw["wpa"], mw["wpb"], mw["wpc"], mw["wo"], x)
    return y, (nk, nv, ncb[:, 8 - (B_CONV - 1):], ncc[:, HALO - (C_CONV - 1):])


def _mixer_sample(x, mw, bias, batch, rows, cache_k, cache_v, layer, st_b, st_c):
    z = norm_matmul(x, mw["g_mix"], mw["w_in"], mw["layer"], tm=x.shape[0], tn=1024, out_dtype=BF16)
    oa, nk, nv = swa_sample(z, batch, rows, cache_k, cache_v, layer, mw["qg2"], mw["kg2"],
                            _with_sink_column(bias, mw["sinks"]))
    stb = jnp.pad(st_b, ((0, 0), (HALO - (B_CONV - 1), 0), (0, 0)))
    stc = jnp.pad(st_c, ((0, 0), (HALO - (C_CONV - 1), 0), (0, 0)))
    ob, oc, ncb, ncc = conv_sample(z, batch, rows, stb, stc, mw["w_cb"], mw["w_cc"], mw["b_cc"],
                                   mw["ln_g"], mw["ln_b"])
    y = merge(oa, ob, oc, z, mw["b_gate"], mw["wpa"], mw["wpb"], mw["wpc"], mw["wo"], x)
    return y, (nk, nv, ncb[:, 8 - (B_CONV - 1):], ncc[:, HALO - (C_CONV - 1):])


def _channel_mixer(yp, ys, l, g_ffn, w_ffn_gu, w_ffn_d, w_router, b_router, w_moe_gu, w_moe_d):
    if l % 2 == 0:
        return (ffn(yp, g_ffn[l], w_ffn_gu[l // 2], w_ffn_d[l // 2]),
                ffn(ys, g_ffn[l], w_ffn_gu[l // 2], w_ffn_d[l // 2]))
    i = l // 2
    w_pad = jnp.pad(w_router[i], ((0, 0), (0, LANES - N_EXPERTS)))
    b_pad = jnp.pad(b_router[i], (0, LANES - N_EXPERTS)).reshape(1, LANES)
    wts_p, ids_p, rank_p, cnt_p = router(yp, g_ffn[l], w_pad, b_pad, jnp.zeros((1, LANES), F32))
    wts_s, ids_s, rank_s, cnt = router(ys, g_ffn[l], w_pad, b_pad, cnt_p)
    mp, ms = yp.shape[0], ys.shape[0]
    nt = (2 * (mp + ms)) // MOE_TM + N_EXPERTS
    ids = jnp.concatenate([ids_p[:, :2], ids_s[:, :2]], axis=0)
    rank = jnp.concatenate([rank_p[:, :2], rank_s[:, :2]], axis=0)
    tile_expert, tile_rows, pos = _dispatch_plan(cnt[0, :N_EXPERTS].astype(jnp.int32), ids, rank, MOE_TM, nt)
    tm_p, tm_s = 512, min(512, ms)
    pos_p, pos_s = _combine_pos(pos[:mp], tm_p), _combine_pos(pos[mp:], tm_s)
    xs = jnp.zeros((nt * MOE_TM, D_MODEL // 2), jnp.uint32)
    xs = moe_dispatch(pos_p, yp, g_ffn[l], xs, tm=tm_p)
    xs = moe_dispatch(pos_s, ys, g_ffn[l], xs, tm=tm_s)
    osort = moe_ffn(tile_expert, tile_rows, xs, w_moe_gu[i], w_moe_d[i])
    return (moe_combine(pos_p, yp, wts_p, osort, tm=tm_p), moe_combine(pos_s, ys, wts_s, osort, tm=tm_s))


def kernel(x_prompt, x_sample, mem_prompt, cache_mem_k, cache_mem_v, cache_swa_k, cache_swa_v, state_conv_b, state_conv_c, rel_table, g_mix, w_in, b_gate, q_norm_g, k_norm_g, sinks, w_conv_b, w_conv_c, b_conv_c, ln_c_g, ln_c_b, w_proj_a, w_proj_b, w_proj_c, w_out, g_xattn, g_mem, w_xq, w_xkv, xq_norm_g, xk_norm_g, w_xo, g_ffn, w_ffn_gu, w_ffn_d, w_router, b_router, w_moe_gu, w_moe_d):
    batch, seq, d = x_prompt.shape
    dec_batch, dec_seq, _ = x_sample.shape
    depth = g_mix.shape[0]
    yp = x_prompt.reshape(batch * seq, d)
    ys = x_sample.reshape(dec_batch * dec_seq, d)
    mem = mem_prompt.reshape(batch * N_MEM, d)
    first = np.arange(KEY_TILE)[None, :] >= Q_TILE
    bias_p = jnp.stack([_stack_heads(_bias_tensor(rel_table, Q_TILE, _prompt_valid() & first)),
                        _stack_heads(_bias_tensor(rel_table, Q_TILE, _prompt_valid()))])
    bias_s = _bias_tensor(rel_table, dec_seq, _sample_valid(dec_seq))
    outs = [[] for _ in range(10)]
    w_in_bf = w_in.astype(BF16)
    ck = cache_swa_k.reshape(depth, dec_batch, WINDOW, A_KV_WIDTH)
    cv = cache_swa_v.reshape(depth, dec_batch, WINDOW, A_KV_WIDTH)
    cmk = cache_mem_k.reshape(depth, dec_batch, N_MEM, X_WIDTH)
    cmv = cache_mem_v.reshape(depth, dec_batch, N_MEM, X_WIDTH)
    for l in range(depth):
        mw = _mixer_weights(l, g_mix, w_in_bf, b_gate, q_norm_g, k_norm_g, sinks, w_conv_b, w_conv_c,
                            b_conv_c, ln_c_g, ln_c_b, w_proj_a, w_proj_b, w_proj_c, w_out)
        wq = w_xq[l].astype(BF16)
        wo = w_xo[l].astype(BF16)
        ffn_args = (g_ffn, w_ffn_gu, w_ffn_d, w_router, b_router, w_moe_gu, w_moe_d)
        yp, (nk, nv, ncb, ncc) = _mixer_prompt(yp, mw, bias_p, batch, seq)
        mk, mv = mem_kv(mem, g_mem[l], w_xkv[l].astype(BF16), xk_norm_g[l])
        mk3 = mk.reshape(batch, N_MEM, X_WIDTH)
        mv3 = mv.reshape(batch, N_MEM, X_WIDTH)
        yp = xattn(yp, g_xattn[l], wq, xq_norm_g[l], mk3[None], mv3[None], wo, layer=0, nb=1, rpb=512,
                   tiles_per_mem=seq // 512)
        for lst, v in zip(outs[:6], (mk3.reshape(batch, N_MEM, X_HEADS, X_HEAD_DIM),
                                     mv3.reshape(batch, N_MEM, X_HEADS, X_HEAD_DIM),
                                     nk.reshape(batch, WINDOW, A_KV_HEADS, A_HEAD_DIM),
                                     nv.reshape(batch, WINDOW, A_KV_HEADS, A_HEAD_DIM), ncb, ncc)):
            lst.append(v)
        ys, (nk, nv, ncb, ncc) = _mixer_sample(ys, mw, bias_s, dec_batch, dec_seq, ck, cv, l,
                                               state_conv_b[l], state_conv_c[l])
        ys = xattn(ys, g_xattn[l], wq, xq_norm_g[l], cmk, cmv, wo, layer=l, nb=8, rpb=dec_seq,
                   tiles_per_mem=1)
        yp, ys = _channel_mixer(yp, ys, l, *ffn_args)
        for lst, v in zip(outs[6:], (nk.reshape(dec_batch, dec_seq, A_KV_HEADS, A_HEAD_DIM),
                                     nv.reshape(dec_batch, dec_seq, A_KV_HEADS, A_HEAD_DIM), ncb, ncc)):
            lst.append(v)
    return (yp.reshape(batch, seq, d), ys.reshape(dec_batch, dec_seq, d)) + tuple(jnp.stack(o) for o in outs)
```

```python
import functools
import math

import numpy as np
import jax
import jax.numpy as jnp
from jax import lax
from jax.experimental import pallas as pl
from jax.experimental.pallas import tpu as pltpu

F32 = jnp.float32
BF16 = jnp.bfloat16

D_MODEL = 2048
CHUNK = 64
A_HEADS = 16
A_KV_HEADS = 4
A_HEAD_DIM = 64
A_WIDTH = A_HEADS * A_HEAD_DIM
A_KV_WIDTH = A_KV_HEADS * A_HEAD_DIM
WINDOW = 128
N_BUCKETS = 32
MAX_DISTANCE = 128
B_WIDTH = 512
B_CONV = 3
C_WIDTH = 512
C_CONV = 31
N_MEM = 256
X_HEADS = 4
X_HEAD_DIM = 128
X_WIDTH = X_HEADS * X_HEAD_DIM
D_FF = 5632
N_EXPERTS = 8
EPS = 1e-6

LANES = 128
KEY_TILE = 256
Q_TILE = 128
HALO = 32
NEG = -1e30
VMEM_LIMIT = 56 * 1024 * 1024

COL_Q = 0
COL_K, COL_V = 4, 5
COL_GB, COL_GC, COL_HB, COL_GA, COL_GG = 3, 4, 5, 6, 7
COL_GATE0 = 2
IN_COLS = 4096 + 3 * D_MODEL

HEAD_PERM = np.array([8 * n + (p % 2) * 4 + p // 2 for n in range(2) for p in range(8)])


def _cparams(sem):
    return pltpu.CompilerParams(dimension_semantics=sem, vmem_limit_bytes=VMEM_LIMIT)


def _rms(x, g):
    ms = jnp.mean(x * x, axis=-1, keepdims=True)
    return x * lax.rsqrt(ms + EPS) * g


def _resident(shape):
    nd = len(shape)
    return pl.BlockSpec(shape, lambda *_: (0,) * nd, pipeline_mode=pl.Buffered(1))


def _permuted_q_columns(acc):
    lo_lane = lax.broadcasted_iota(jnp.int32, (1, LANES), 1) < A_HEAD_DIM
    nat = [acc[:, c * LANES:(c + 1) * LANES] for c in range(A_WIDTH // LANES)]
    swapped = [pltpu.roll(x, A_HEAD_DIM, 1) for x in nat]
    cols = []
    for n in range(2):
        for m in range(4):
            ca, cb = 4 * n + m // 2, 4 * n + 2 + m // 2
            cols.append(jnp.where(lo_lane, nat[ca], swapped[cb]) if m % 2 == 0
                        else jnp.where(lo_lane, swapped[ca], nat[cb]))
    return cols


def _norm_matmul_kernel(x_ref, g_ref, w_ref, o_ref, hn_ref):
    j = pl.program_id(1)

    @pl.when(j == 0)
    def _():
        hn_ref[...] = _rms(x_ref[...], g_ref[...]).astype(BF16)
        acc = jnp.dot(hn_ref[...], w_ref[...], preferred_element_type=F32)
        for c, col in enumerate(_permuted_q_columns(acc)):
            o_ref[:, c * LANES:(c + 1) * LANES] = col.astype(o_ref.dtype)
        if o_ref.shape[1] > A_WIDTH:
            o_ref[:, A_WIDTH:] = acc[:, A_WIDTH:].astype(o_ref.dtype)

    @pl.when(j > 0)
    def _():
        o_ref[...] = jnp.dot(hn_ref[...], w_ref[...], preferred_element_type=F32).astype(o_ref.dtype)


def norm_matmul(x, g, w, layer, *, tm, tn, out_dtype):
    m, k = x.shape
    n = w.shape[2]
    return pl.pallas_call(
        _norm_matmul_kernel,
        grid=(m // tm, n // tn),
        in_specs=[pl.BlockSpec((tm, k), lambda i, j: (i, 0)),
                  pl.BlockSpec((1, k), lambda i, j: (0, 0)),
                  pl.BlockSpec((None, k, tn), lambda i, j: (layer, 0, j))],
        out_specs=pl.BlockSpec((tm, tn), lambda i, j: (i, j)),
        out_shape=jax.ShapeDtypeStruct((m, n), out_dtype),
        scratch_shapes=[pltpu.VMEM((tm, k), BF16)],
        compiler_params=_cparams(("parallel", "arbitrary")),
        name="norm_matmul",
    )(x, g.reshape(1, k), w)


def _mem_kv_kernel(x_ref, g_ref, w_ref, kg_ref, k_ref, v_ref):
    hn = _rms(x_ref[...], g_ref[...]).astype(BF16)
    kv = jnp.dot(hn, w_ref[...], preferred_element_type=F32)
    for h in range(X_HEADS):
        sl = slice(h * X_HEAD_DIM, (h + 1) * X_HEAD_DIM)
        k_ref[:, sl] = _rms(kv[:, sl], kg_ref[...])
    v_ref[...] = kv[:, X_WIDTH:]


def mem_kv(mem, g, w_bf, kg):
    m, k = mem.shape
    tm = 256
    return pl.pallas_call(
        _mem_kv_kernel,
        grid=(m // tm,),
        in_specs=[pl.BlockSpec((tm, k), lambda i: (i, 0)),
                  pl.BlockSpec((1, k), lambda i: (0, 0)),
                  _resident((k, 2 * X_WIDTH)),
                  pl.BlockSpec((1, X_HEAD_DIM), lambda i: (0, 0))],
        out_specs=[pl.BlockSpec((tm, X_WIDTH), lambda i: (i, 0)),
                   pl.BlockSpec((tm, X_WIDTH), lambda i: (i, 0))],
        out_shape=[jax.ShapeDtypeStruct((m, X_WIDTH), F32)] * 2,
        compiler_params=_cparams(("parallel",)),
        name="mem_kv",
    )(mem, g.reshape(1, k), w_bf, kg.reshape(1, X_HEAD_DIM))


def _half_norm(x, g):
    x2 = x * x
    lo_lane = lax.broadcasted_iota(jnp.int32, (1, LANES), 1) < A_HEAD_DIM
    s_lo = jnp.sum(jnp.where(lo_lane, x2, 0.0), axis=-1, keepdims=True)
    s_hi = jnp.sum(jnp.where(lo_lane, 0.0, x2), axis=-1, keepdims=True)
    ms = jnp.where(lo_lane, s_lo, s_hi) * (1.0 / A_HEAD_DIM)
    return x * lax.rsqrt(ms + EPS) * g


def _swa_heads(q_ref, qg, k2, v2, bias_ref, sink_ref, o_ref, *, stack):
    lo_lane = lax.broadcasted_iota(jnp.int32, (1, LANES), 1) < A_HEAD_DIM
    rows = q_ref.shape[0]
    cols = A_WIDTH // LANES // 2
    for n in range(2):
        k_half = (jnp.where(lo_lane, k2[n], 0.0).astype(BF16), jnp.where(lo_lane, 0.0, k2[n]).astype(BF16))
        for g in range(0, cols, stack):
            qs = [_half_norm(q_ref[:, c * LANES:(c + 1) * LANES].astype(F32), qg)
                  for c in range(cols * n + g, cols * n + g + stack)]
            qn = (qs[0] if stack == 1 else jnp.concatenate(qs, axis=0)).astype(BF16)
            at = slice(g * rows, (g + stack) * rows)
            halves = []
            for half in range(2):
                s = lax.dot_general(qn, k_half[half], (((1,), (1,)), ((), ())), preferred_element_type=F32)
                s = s + bias_ref[n, half, at, :]
                if sink_ref is None:
                    m = jnp.max(s, axis=-1, keepdims=True)
                    p = jnp.exp(s - m)
                    den = jnp.sum(p, axis=-1, keepdims=True)
                else:
                    sink = sink_ref[2 * (cols * n + g) + half]
                    m = jnp.maximum(jnp.max(s, axis=-1, keepdims=True), sink)
                    p = jnp.exp(s - m)
                    den = jnp.sum(p, axis=-1, keepdims=True) + jnp.exp(sink - m)
                o = jnp.dot(p.astype(BF16), v2[n], preferred_element_type=F32)
                halves.append(o / den)
            o = jnp.where(lo_lane, halves[0], halves[1]).astype(o_ref.dtype)
            for i in range(stack):
                c = cols * n + g + i
                o_ref[:, c * LANES:(c + 1) * LANES] = o[i * rows:(i + 1) * rows]


def _swa_prompt_kernel(q_ref, kc_ref, kp_ref, vc_ref, vp_ref, qg_ref, kg_ref,
                       bias_ref, sink_ref, o_ref, nk_ref, nv_ref):
    k2, v2 = [], []
    for n in range(2):
        sl = slice(n * LANES, (n + 1) * LANES)
        kcat = jnp.concatenate([kp_ref[:, sl], kc_ref[:, sl]], axis=0).astype(F32)
        kn = _half_norm(kcat, kg_ref[...])
        nk_ref[0, :, sl] = kn[Q_TILE:]
        k2.append(kn)
        v2.append(jnp.concatenate([vp_ref[:, sl], vc_ref[:, sl]], axis=0))
    nv_ref[0] = vc_ref[...].astype(F32)
    _swa_heads(q_ref, qg_ref[...], k2, v2, bias_ref, sink_ref, o_ref, stack=1)


def swa_prompt(z, batch, seq, qg2, kg2, bias, sinks):
    nt = seq // Q_TILE
    row = lambda b, t: b * nt + t
    prev = lambda b, t: jnp.maximum(b * nt + t - 1, 0)
    return pl.pallas_call(
        _swa_prompt_kernel,
        grid=(batch, nt),
        in_specs=[pl.BlockSpec((Q_TILE, A_WIDTH), lambda b, t: (row(b, t), COL_Q)),
                  pl.BlockSpec((Q_TILE, A_KV_WIDTH), lambda b, t: (row(b, t), COL_K)),
                  pl.BlockSpec((Q_TILE, A_KV_WIDTH), lambda b, t: (prev(b, t), COL_K)),
                  pl.BlockSpec((Q_TILE, A_KV_WIDTH), lambda b, t: (row(b, t), COL_V)),
                  pl.BlockSpec((Q_TILE, A_KV_WIDTH), lambda b, t: (prev(b, t), COL_V)),
                  pl.BlockSpec((1, LANES), lambda b, t: (0, 0)),
                  pl.BlockSpec((1, LANES), lambda b, t: (0, 0)),
                  pl.BlockSpec((None, 2, 2, 4 * Q_TILE, KEY_TILE),
                               lambda b, t: (jnp.minimum(t, 1), 0, 0, 0, 0)),
                  pl.BlockSpec(memory_space=pltpu.SMEM)],
        out_specs=[pl.BlockSpec((Q_TILE, A_WIDTH), lambda b, t: (row(b, t), 0)),
                   pl.BlockSpec((1, WINDOW, A_KV_WIDTH), lambda b, t: (b, 0, 0)),
                   pl.BlockSpec((1, WINDOW, A_KV_WIDTH), lambda b, t: (b, 0, 0))],
        out_shape=[jax.ShapeDtypeStruct((batch * seq, A_WIDTH), BF16),
                   jax.ShapeDtypeStruct((batch, WINDOW, A_KV_WIDTH), F32),
                   jax.ShapeDtypeStruct((batch, WINDOW, A_KV_WIDTH), F32)],
        compiler_params=_cparams(("parallel", "arbitrary")),
        name="swa_prompt",
    )(z, z, z, z, z, qg2, kg2, bias, sinks)


def _swa_sample_kernel(q_ref, kn_ref, vn_ref, ck_ref, cv_ref, qg_ref, kg_ref,
                       bias_ref, o_ref, nk_ref, nv_ref):
    rows = q_ref.shape[0]
    pad = KEY_TILE - WINDOW - rows
    k2, v2 = [], []
    for n in range(2):
        sl = slice(n * LANES, (n + 1) * LANES)
        kn = _half_norm(kn_ref[:, sl].astype(F32), kg_ref[...])
        nk_ref[0, :, sl] = kn
        k2.append(jnp.concatenate([ck_ref[0, :, sl], kn, jnp.zeros((pad, LANES), F32)], axis=0))
        v2.append(jnp.concatenate([cv_ref[0, :, sl].astype(BF16), vn_ref[:, sl],
                                   jnp.zeros((pad, LANES), BF16)], axis=0))
    nv_ref[0] = vn_ref[...].astype(F32)
    _swa_heads(q_ref, qg_ref[...], k2, v2, bias_ref, None, o_ref, stack=4)


def swa_sample(z, batch, rows, cache_k, cache_v, layer, qg2, kg2, bias):
    return pl.pallas_call(
        _swa_sample_kernel,
        grid=(batch,),
        in_specs=[pl.BlockSpec((rows, A_WIDTH), lambda b: (b, COL_Q)),
                  pl.BlockSpec((rows, A_KV_WIDTH), lambda b: (b, COL_K)),
                  pl.BlockSpec((rows, A_KV_WIDTH), lambda b: (b, COL_V)),
                  pl.BlockSpec((None, 1, WINDOW, A_KV_WIDTH), lambda b: (layer, b, 0, 0)),
                  pl.BlockSpec((None, 1, WINDOW, A_KV_WIDTH), lambda b: (layer, b, 0, 0)),
                  pl.BlockSpec((1, LANES), lambda b: (0, 0)),
                  pl.BlockSpec((1, LANES), lambda b: (0, 0)),
                  _resident((2, 2, 4 * rows, KEY_TILE))],
        out_specs=[pl.BlockSpec((rows, A_WIDTH), lambda b: (b, 0)),
                   pl.BlockSpec((1, rows, A_KV_WIDTH), lambda b: (b, 0, 0)),
                   pl.BlockSpec((1, rows, A_KV_WIDTH), lambda b: (b, 0, 0))],
        out_shape=[jax.ShapeDtypeStruct((batch * rows, A_WIDTH), BF16),
                   jax.ShapeDtypeStruct((batch, rows, A_KV_WIDTH), F32),
                   jax.ShapeDtypeStruct((batch, rows, A_KV_WIDTH), F32)],
        compiler_params=_cparams(("parallel",)),
        name="swa_sample",
    )(z, z, z, cache_k, cache_v, qg2, kg2, bias)


def _conv_body(gb_ref, ub_main, uc_main, ub_halo, uc_halo, wb_ref, wc_ref, bc_ref, lg_ref, lb_ref,
               ob_ref, oc_ref, nb_ref, nc_ref, sb_ref, sc_ref, ph_ref, write_state):
    rows = ub_main.shape[0]
    sb_ref[0:HALO] = ub_halo
    sb_ref[HALO:HALO + rows] = ub_main
    sc_ref[0:HALO] = uc_halo
    sc_ref[HALO:HALO + rows] = uc_main
    span = rows + HALO
    sc_ref[span:span + 8] = jnp.zeros((8, C_WIDTH), F32)
    for b in range(1, 8):
        ph_ref[b - 1] = sc_ref[pl.ds(b, span), :]
    sub = min(rows, 32)
    for r0 in range(0, rows, sub):
        yb = jnp.zeros((sub, B_WIDTH), F32)
        for k in range(B_CONV):
            yb = yb + wb_ref[k:k + 1, :] * sb_ref[pl.ds(r0 + HALO - (B_CONV - 1) + k, sub), :]
        ob_ref[r0:r0 + sub, :] = (gb_ref[r0:r0 + sub, :].astype(F32) * yb).astype(ob_ref.dtype)
        yc = jnp.zeros((sub, C_WIDTH), F32)
        for k in range(C_CONV):
            shift, phase = divmod(HALO - (C_CONV - 1) + k, 8)
            at = pl.ds(r0 + 8 * shift, sub)
            taps = sc_ref[at, :] if phase == 0 else ph_ref[phase - 1, at, :]
            yc = yc + wc_ref[k:k + 1, :] * taps
        yc = yc + bc_ref[...]
        mu = jnp.mean(yc, axis=-1, keepdims=True)
        xc = yc - mu
        y = xc * lax.rsqrt(jnp.mean(xc * xc, axis=-1, keepdims=True) + EPS)
        y = y * lg_ref[...] + lb_ref[...]
        oc_ref[r0:r0 + sub, :] = (y * jax.nn.sigmoid(y)).astype(oc_ref.dtype)

    def _state():
        nb_ref[0] = sb_ref[rows + HALO - 8:rows + HALO]
        nc_ref[0] = sc_ref[rows:rows + HALO]

    write_state(_state)


def _conv_prompt_kernel(gb_ref, gc_ref, hb_ref, ga_ref, gg_ref, gch_ref, hbh_ref, gah_ref, ggh_ref,
                        wb_ref, wc_ref, bc_ref, lg_ref, lb_ref,
                        ob_ref, oc_ref, nb_ref, nc_ref, sb_ref, sc_ref, ph_ref):
    t = pl.program_id(1)
    hist = (t > 0).astype(F32)
    ub_main = gc_ref[...].astype(F32) * hb_ref[...].astype(F32)
    uc_main = ga_ref[...].astype(F32) * jax.nn.sigmoid(gg_ref[...].astype(F32))
    ub_halo = gch_ref[...].astype(F32) * hbh_ref[...].astype(F32) * hist
    uc_halo = gah_ref[...].astype(F32) * jax.nn.sigmoid(ggh_ref[...].astype(F32)) * hist
    last = pl.num_programs(1) - 1
    _conv_body(gb_ref, ub_main, uc_main, ub_halo, uc_halo, wb_ref, wc_ref, bc_ref, lg_ref, lb_ref,
               ob_ref, oc_ref, nb_ref, nc_ref, sb_ref, sc_ref, ph_ref,
               lambda f: pl.when(t == last)(f))


def conv_prompt(z, batch, seq, wb, wc, bc, lg, lb, *, tr=256):
    nt = seq // tr
    hp = tr // HALO
    main = lambda c: pl.BlockSpec((tr, B_WIDTH), lambda b, t: (b * nt + t, c))
    halo = lambda c: pl.BlockSpec((HALO, B_WIDTH), lambda b, t: (jnp.maximum((b * nt + t) * hp - 1, 0), c))
    vec = lambda r: pl.BlockSpec((r, B_WIDTH), lambda b, t: (0, 0))
    return pl.pallas_call(
        _conv_prompt_kernel,
        grid=(batch, nt),
        in_specs=[main(COL_GB), main(COL_GC), main(COL_HB), main(COL_GA), main(COL_GG),
                  halo(COL_GC), halo(COL_HB), halo(COL_GA), halo(COL_GG),
                  vec(B_CONV), vec(C_CONV), vec(1), vec(1), vec(1)],
        out_specs=[pl.BlockSpec((tr, B_WIDTH), lambda b, t: (b * nt + t, 0)),
                   pl.BlockSpec((tr, C_WIDTH), lambda b, t: (b * nt + t, 0)),
                   pl.BlockSpec((1, 8, B_WIDTH), lambda b, t: (b, 0, 0)),
                   pl.BlockSpec((1, HALO, C_WIDTH), lambda b, t: (b, 0, 0))],
        out_shape=[jax.ShapeDtypeStruct((batch * seq, B_WIDTH), BF16),
                   jax.ShapeDtypeStruct((batch * seq, C_WIDTH), BF16),
                   jax.ShapeDtypeStruct((batch, 8, B_WIDTH), F32),
                   jax.ShapeDtypeStruct((batch, HALO, C_WIDTH), F32)],
        scratch_shapes=[pltpu.VMEM((tr + HALO, B_WIDTH), F32), pltpu.VMEM((tr + HALO + 8, C_WIDTH), F32),
                        pltpu.VMEM((7, tr + HALO, C_WIDTH), F32)],
        compiler_params=_cparams(("parallel", "arbitrary")),
        name="conv_prompt",
    )(z, z, z, z, z, z, z, z, z, wb, wc, bc.reshape(1, -1), lg.reshape(1, -1), lb.reshape(1, -1))


def _conv_sample_kernel(gb_ref, gc_ref, hb_ref, ga_ref, gg_ref, stb_ref, stc_ref,
                        wb_ref, wc_ref, bc_ref, lg_ref, lb_ref,
                        ob_ref, oc_ref, nb_ref, nc_ref, sb_ref, sc_ref, ph_ref):
    ub_main = gc_ref[...].astype(F32) * hb_ref[...].astype(F32)
    uc_main = ga_ref[...].astype(F32) * jax.nn.sigmoid(gg_ref[...].astype(F32))
    _conv_body(gb_ref, ub_main, uc_main, stb_ref[0], stc_ref[0], wb_ref, wc_ref, bc_ref, lg_ref, lb_ref,
               ob_ref, oc_ref, nb_ref, nc_ref, sb_ref, sc_ref, ph_ref, lambda f: f())


def conv_sample(z, batch, rows, stb, stc, wb, wc, bc, lg, lb):
    main = lambda c: pl.BlockSpec((rows, B_WIDTH), lambda b: (b, c))
    vec = lambda r: pl.BlockSpec((r, B_WIDTH), lambda b: (0, 0))
    return pl.pallas_call(
        _conv_sample_kernel,
        grid=(batch,),
        in_specs=[main(COL_GB), main(COL_GC), main(COL_HB), main(COL_GA), main(COL_GG),
                  pl.BlockSpec((1, HALO, B_WIDTH), lambda b: (b, 0, 0)),
                  pl.BlockSpec((1, HALO, C_WIDTH), lambda b: (b, 0, 0)),
                  vec(B_CONV), vec(C_CONV), vec(1), vec(1), vec(1)],
        out_specs=[pl.BlockSpec((rows, B_WIDTH), lambda b: (b, 0)),
                   pl.BlockSpec((rows, C_WIDTH), lambda b: (b, 0)),
                   pl.BlockSpec((1, 8, B_WIDTH), lambda b: (b, 0, 0)),
                   pl.BlockSpec((1, HALO, C_WIDTH), lambda b: (b, 0, 0))],
        out_shape=[jax.ShapeDtypeStruct((batch * rows, B_WIDTH), BF16),
                   jax.ShapeDtypeStruct((batch * rows, C_WIDTH), BF16),
                   jax.ShapeDtypeStruct((batch, 8, B_WIDTH), F32),
                   jax.ShapeDtypeStruct((batch, HALO, C_WIDTH), F32)],
        scratch_shapes=[pltpu.VMEM((rows + HALO, B_WIDTH), F32), pltpu.VMEM((rows + HALO + 8, C_WIDTH), F32),
                        pltpu.VMEM((7, rows + HALO, C_WIDTH), F32)],
        compiler_params=_cparams(("parallel",)),
        name="conv_sample",
    )(z, z, z, z, z, stb, stc, wb, wc, bc.reshape(1, -1), lg.reshape(1, -1), lb.reshape(1, -1))


def _merge_kernel(oa_ref, ob_ref, oc_ref, l0_ref, l1_ref, l2_ref, bg_ref,
                  wpa_ref, wpb_ref, wpc_ref, wo_ref, x_ref, o_ref):
    def gated(l_ref, i, o_r, w_r):
        gate = jax.nn.sigmoid(l_ref[...].astype(F32) + bg_ref[i:i + 1, :])
        return gate * jnp.dot(o_r[...], w_r[...], preferred_element_type=F32)

    merged = gated(l0_ref, 0, oa_ref, wpa_ref)
    merged = merged + gated(l1_ref, 1, ob_ref, wpb_ref)
    merged = merged + gated(l2_ref, 2, oc_ref, wpc_ref)
    o_ref[...] = x_ref[...] + jnp.dot(merged.astype(BF16), wo_ref[...], preferred_element_type=F32)


def merge(oa, ob, oc, z, bg, wpa, wpb, wpc, wo, x, *, tm=512):
    m = x.shape[0]
    tm = min(tm, m)
    rows = lambda w: pl.BlockSpec((tm, w), lambda i: (i, 0))
    gate = lambda c: pl.BlockSpec((tm, D_MODEL), lambda i: (i, COL_GATE0 + c))
    return pl.pallas_call(
        _merge_kernel,
        grid=(m // tm,),
        in_specs=[rows(A_WIDTH), rows(B_WIDTH), rows(C_WIDTH), gate(0), gate(1), gate(2),
                  pl.BlockSpec((3, D_MODEL), lambda i: (0, 0)),
                  _resident((A_WIDTH, D_MODEL)), _resident((B_WIDTH, D_MODEL)),
                  _resident((C_WIDTH, D_MODEL)), _resident((D_MODEL, D_MODEL)),
                  rows(D_MODEL)],
        out_specs=rows(D_MODEL),
        out_shape=jax.ShapeDtypeStruct((m, D_MODEL), F32),
        compiler_params=_cparams(("parallel",)),
        name="merge",
    )(oa, ob, oc, z, z, z, bg.reshape(3, D_MODEL), wpa, wpb, wpc, wo, x)


def _xattn_kernel(y_ref, g_ref, wq_ref, qg_ref, mk_ref, mv_ref, wo_ref, o_ref, *, nb, rpb):
    y = y_ref[...]
    hn = _rms(y, g_ref[...]).astype(BF16)
    q = jnp.dot(hn, wq_ref[...], preferred_element_type=F32)
    heads = []
    for h in range(X_HEADS):
        sl = slice(h * X_HEAD_DIM, (h + 1) * X_HEAD_DIM)
        qh = _rms(q[:, sl], qg_ref[...])
        per_batch = []
        for b in range(nb):
            qb = qh[b * rpb:(b + 1) * rpb].astype(BF16)
            if len(mk_ref.shape) == 4:
                kh = mk_ref[b, :, h, :].astype(BF16)
                vh = mv_ref[b, :, h, :].astype(BF16)
            else:
                kh = mk_ref[b, :, sl].astype(BF16)
                vh = mv_ref[b, :, sl].astype(BF16)
            s = lax.dot_general(qb, kh, (((1,), (1,)), ((), ())),
                                preferred_element_type=F32) * (X_HEAD_DIM ** -0.5)
            m = jnp.max(s, axis=-1, keepdims=True)
            p = jnp.exp(s - m)
            den = jnp.sum(p, axis=-1, keepdims=True)
            per_batch.append(jnp.dot(p.astype(BF16), vh, preferred_element_type=F32) / den)
        heads.append(per_batch[0] if nb == 1 else jnp.concatenate(per_batch, axis=0))
    o = jnp.concatenate(heads, axis=1).astype(BF16)
    o_ref[...] = y + jnp.dot(o, wo_ref[...], preferred_element_type=F32)


def xattn(y, g, wq, qg, mk, mv, wo, *, layer, nb, rpb, tiles_per_mem):
    m = y.shape[0]
    tm = nb * rpb
    tail = (0,) * (mk.ndim - 2)
    mem_idx = ((lambda i: (layer, i // tiles_per_mem) + tail) if nb == 1
               else (lambda i: (layer, i) + tail))
    mem_block = (None, nb) + mk.shape[2:]
    return pl.pallas_call(
        functools.partial(_xattn_kernel, nb=nb, rpb=rpb),
        grid=(m // tm,),
        in_specs=[pl.BlockSpec((tm, D_MODEL), lambda i: (i, 0)),
                  pl.BlockSpec((1, D_MODEL), lambda i: (0, 0)),
                  _resident((D_MODEL, X_WIDTH)),
                  pl.BlockSpec((1, X_HEAD_DIM), lambda i: (0, 0)),
                  pl.BlockSpec(mem_block, mem_idx),
                  pl.BlockSpec(mem_block, mem_idx),
                  _resident((X_WIDTH, D_MODEL))],
        out_specs=pl.BlockSpec((tm, D_MODEL), lambda i: (i, 0)),
        out_shape=jax.ShapeDtypeStruct((m, D_MODEL), F32),
        compiler_params=_cparams(("parallel",)),
        name="xattn",
    )(y, g.reshape(1, -1), wq, qg.reshape(1, -1), mk, mv, wo)


def _ffn_kernel(x_ref, g_ref, wg_ref, wu_ref, wd_ref, o_ref, hn_ref):
    @pl.when(pl.program_id(1) == 0)
    def _():
        x = x_ref[...]
        hn_ref[...] = _rms(x, g_ref[...]).astype(BF16)
        o_ref[...] = x

    hn = hn_ref[...]
    g = jnp.dot(hn, wg_ref[...].astype(BF16), preferred_element_type=F32)
    u = jnp.dot(hn, wu_ref[...].astype(BF16), preferred_element_type=F32)
    a = (g * jax.nn.sigmoid(g) * u).astype(BF16)
    o_ref[...] += jnp.dot(a, wd_ref[...].astype(BF16), preferred_element_type=F32)


def ffn(x, g, w_gu, w_d, *, tm=1024, tf=512):
    m = x.shape[0]
    tm = min(tm, m)
    nf = D_FF // tf
    return pl.pallas_call(
        _ffn_kernel,
        grid=(m // tm, nf),
        in_specs=[pl.BlockSpec((tm, D_MODEL), lambda i, f: (i, 0), pipeline_mode=pl.Buffered(1)),
                  pl.BlockSpec((1, D_MODEL), lambda i, f: (0, 0)),
                  pl.BlockSpec((D_MODEL, tf), lambda i, f: (0, f)),
                  pl.BlockSpec((D_MODEL, tf), lambda i, f: (0, nf + f)),
                  pl.BlockSpec((tf, D_MODEL), lambda i, f: (f, 0))],
        out_specs=pl.BlockSpec((tm, D_MODEL), lambda i, f: (i, 0), pipeline_mode=pl.Buffered(1)),
        out_shape=jax.ShapeDtypeStruct((m, D_MODEL), F32),
        scratch_shapes=[pltpu.VMEM((tm, D_MODEL), BF16)],
        compiler_params=_cparams(("parallel", "arbitrary")),
        name="ffn",
    )(x, g.reshape(1, -1), w_gu, w_gu, w_d)


def _split3(x):
    hi = x.astype(BF16)
    lo = (x - hi.astype(F32)).astype(BF16)
    return hi, lo


def _router_kernel(x_ref, g_ref, w_ref, b_ref, tri_ref, base_ref, wts_ref, ids_ref, rank_ref, cnt_ref, run_ref):
    @pl.when(pl.program_id(0) == 0)
    def _():
        run_ref[...] = base_ref[...]

    hn = _rms(x_ref[...], g_ref[...])
    h_hi, h_lo = _split3(hn)
    w_hi, w_lo = _split3(w_ref[...])
    dot = functools.partial(jnp.dot, preferred_element_type=F32)
    logits = dot(h_hi, w_hi) + dot(h_hi, w_lo) + dot(h_lo, w_hi) + b_ref[...]
    lane = lax.broadcasted_iota(jnp.int32, logits.shape, 1).astype(F32)
    logits = jnp.where(lane < N_EXPERTS, logits, -jnp.inf)
    v1 = jnp.max(logits, axis=-1, keepdims=True)
    i1 = jnp.min(jnp.where(logits == v1, lane, float(LANES)), axis=-1, keepdims=True)
    rest = jnp.where(lane == i1, -jnp.inf, logits)
    v2 = jnp.max(rest, axis=-1, keepdims=True)
    i2 = jnp.min(jnp.where(rest == v2, lane, float(LANES)), axis=-1, keepdims=True)
    e2 = jnp.exp(v2 - v1)
    den = 1.0 + e2
    first, second = lane == 0.0, lane == 1.0
    wts_ref[...] = jnp.where(first, 1.0 / den, 0.0) + jnp.where(second, e2 / den, 0.0)
    ids_ref[...] = (jnp.where(first, i1, 0.0) + jnp.where(second, i2, 0.0)).astype(jnp.int32)
    hit1, hit2 = lane == i1, lane == i2
    hits = jnp.where(jnp.logical_or(hit1, hit2), 1.0, 0.0)
    before = dot(tri_ref[...], hits.astype(BF16)) + run_ref[...]
    r1 = jnp.sum(jnp.where(hit1, before, 0.0), axis=-1, keepdims=True)
    r2 = jnp.sum(jnp.where(hit2, before, 0.0), axis=-1, keepdims=True)
    rank_ref[...] = (jnp.where(first, r1, 0.0) + jnp.where(second, r2, 0.0)).astype(jnp.int32)
    run_ref[...] += jnp.sum(hits, axis=0, keepdims=True)
    cnt_ref[...] = run_ref[...]


def router(x, g, w_pad, b_pad, base, *, tm=512):
    m = x.shape[0]
    tm = min(tm, m)
    tri = jnp.asarray(np.tril(np.ones((tm, tm), np.float32), -1), BF16)
    row = lambda w: pl.BlockSpec((tm, w), lambda i: (i, 0))
    fixed = lambda s: pl.BlockSpec(s, lambda i: (0, 0))
    return pl.pallas_call(
        _router_kernel,
        grid=(m // tm,),
        in_specs=[row(D_MODEL), fixed((1, D_MODEL)), fixed((D_MODEL, LANES)), fixed((1, LANES)),
                  fixed((tm, tm)), fixed((1, LANES))],
        out_specs=[row(LANES), row(LANES), row(LANES), fixed((1, LANES))],
        out_shape=[jax.ShapeDtypeStruct((m, LANES), F32),
                   jax.ShapeDtypeStruct((m, LANES), jnp.int32),
                   jax.ShapeDtypeStruct((m, LANES), jnp.int32),
                   jax.ShapeDtypeStruct((1, LANES), F32)],
        scratch_shapes=[pltpu.VMEM((1, LANES), F32)],
        compiler_params=_cparams(("arbitrary",)),
        name="router",
    )(x, g.reshape(1, -1), w_pad, b_pad, tri, base)


MOE_TM = 1024
MOE_TF = 512


def _row_copy(src, row, dst, r, sem):
    return pltpu.make_async_copy(src.at[pl.ds(row, 1)], dst.at[pl.ds(r, 1)], sem)


def _moe_dispatch_kernel(pos_ref, x_ref, g_ref, xs_in, xs_hbm, pk_ref, sem):
    del xs_in
    tm = x_ref.shape[0]
    half = D_MODEL // 2
    hn = _rms(x_ref[...], g_ref[...]).astype(BF16)
    lo_bits = lax.bitcast_convert_type(hn[:, :half].astype(F32), jnp.uint32)
    hi_bits = lax.bitcast_convert_type(hn[:, half:].astype(F32), jnp.uint32)
    pk_ref[...] = (lo_bits >> 16) | (hi_bits & jnp.uint32(0xFFFF0000))

    def issue(r, c):
        _row_copy(pk_ref, r, xs_hbm, pos_ref[0, 0, r], sem).start()
        _row_copy(pk_ref, r, xs_hbm, pos_ref[0, 0, tm + r], sem).start()
        return c

    lax.fori_loop(0, tm, issue, 0, unroll=8)

    def wait(r, c):
        _row_copy(pk_ref, r, xs_hbm, 0, sem).wait()
        _row_copy(pk_ref, r, xs_hbm, 0, sem).wait()
        return c

    lax.fori_loop(0, tm, wait, 0, unroll=8)


def moe_dispatch(pos, x, g, xs, *, tm):
    m = x.shape[0]
    return pl.pallas_call(
        _moe_dispatch_kernel,
        grid=(m // tm,),
        in_specs=[pl.BlockSpec((1, 1, 2 * tm), lambda i: (i, 0, 0), memory_space=pltpu.SMEM),
                  pl.BlockSpec((tm, D_MODEL), lambda i: (i, 0)),
                  pl.BlockSpec((1, D_MODEL), lambda i: (0, 0)),
                  pl.BlockSpec(memory_space=pl.ANY)],
        out_specs=pl.BlockSpec(memory_space=pl.ANY),
        out_shape=jax.ShapeDtypeStruct(xs.shape, xs.dtype),
        scratch_shapes=[pltpu.VMEM((tm, D_MODEL // 2), jnp.uint32), pltpu.SemaphoreType.DMA],
        input_output_aliases={3: 0},
        compiler_params=_cparams(("arbitrary",)),
        name="moe_dispatch",
    )(pos, x, g.reshape(1, -1), xs)


def _moe_ffn_kernel(te_ref, tv_ref, xs_ref, wg_ref, wu_ref, wd_ref, o_ref, hn_ref):
    t = pl.program_id(0)
    f = pl.program_id(1)
    rows = xs_ref.shape[0]
    half = D_MODEL // 2

    @pl.when(f == 0)
    def _():
        o_ref[...] = jnp.zeros_like(o_ref)
        xu = xs_ref[...]
        hn_ref[:, :half] = lax.bitcast_convert_type(xu << 16, F32).astype(BF16)
        hn_ref[:, half:] = lax.bitcast_convert_type(xu & jnp.uint32(0xFFFF0000), F32).astype(BF16)

    def swiglu(r):
        hn = hn_ref[0:r]
        g = jnp.dot(hn, wg_ref[...].astype(BF16), preferred_element_type=F32)
        u = jnp.dot(hn, wu_ref[...].astype(BF16), preferred_element_type=F32)
        a = (g * jax.nn.sigmoid(g) * u).astype(BF16)
        o_ref[0:r] += jnp.dot(a, wd_ref[...].astype(BF16), preferred_element_type=F32)

    nv = tv_ref[t]

    quarter = rows // 4
    for k in range(1, 5):
        @pl.when(jnp.logical_and(nv > (k - 1) * quarter, nv <= k * quarter))
        def _(k=k):
            swiglu(k * quarter)


def moe_ffn(tile_expert, tile_rows, xs, w_gu, w_d):
    tm = MOE_TM
    nt = xs.shape[0] // tm
    nf = D_FF // MOE_TF
    last = nf - 1
    col = lambda f, tv, t: jnp.where(tv[t] != 0, f, last)
    grid_spec = pltpu.PrefetchScalarGridSpec(
        num_scalar_prefetch=2,
        grid=(nt, nf),
        in_specs=[pl.BlockSpec((tm, D_MODEL // 2), lambda t, f, te, tv: (t, 0), pipeline_mode=pl.Buffered(1)),
                  pl.BlockSpec((None, D_MODEL, MOE_TF), lambda t, f, te, tv: (te[t], 0, col(f, tv, t))),
                  pl.BlockSpec((None, D_MODEL, MOE_TF), lambda t, f, te, tv: (te[t], 0, nf + col(f, tv, t))),
                  pl.BlockSpec((None, MOE_TF, D_MODEL), lambda t, f, te, tv: (te[t], col(f, tv, t), 0))],
        out_specs=pl.BlockSpec((tm, D_MODEL), lambda t, f, te, tv: (t, 0)),
        scratch_shapes=[pltpu.VMEM((tm, D_MODEL), BF16)],
    )
    return pl.pallas_call(
        _moe_ffn_kernel,
        grid_spec=grid_spec,
        out_shape=jax.ShapeDtypeStruct((nt * tm, D_MODEL), F32),
        compiler_params=_cparams(("arbitrary", "arbitrary")),
        name="moe_ffn",
    )(tile_expert, tile_rows, xs, w_gu, w_gu, w_d)


def _moe_combine_kernel(pos_ref, x_ref, w_ref, osort_hbm, o_ref, abuf, sem):
    tm = x_ref.shape[0]

    def issue(r, c):
        _row_copy(osort_hbm, pos_ref[0, 0, r], abuf, r, sem).start()
        return c

    lax.fori_loop(0, 2 * tm, issue, 0, unroll=8)

    def wait(r, c):
        _row_copy(osort_hbm, 0, abuf, r, sem).wait()
        return c

    lax.fori_loop(0, 2 * tm, wait, 0, unroll=8)
    w = w_ref[...]
    o_ref[...] = x_ref[...] + w[:, 0:1] * abuf[0:tm, :] + w[:, 1:2] * abuf[tm:2 * tm, :]


def moe_combine(pos, x, wts, osort, *, tm=256):
    m = x.shape[0]
    return pl.pallas_call(
        _moe_combine_kernel,
        grid=(m // tm,),
        in_specs=[pl.BlockSpec((1, 1, 2 * tm), lambda i: (i, 0, 0), memory_space=pltpu.SMEM),
                  pl.BlockSpec((tm, D_MODEL), lambda i: (i, 0)),
                  pl.BlockSpec((tm, LANES), lambda i: (i, 0)),
                  pl.BlockSpec(memory_space=pl.ANY)],
        out_specs=pl.BlockSpec((tm, D_MODEL), lambda i: (i, 0)),
        out_shape=jax.ShapeDtypeStruct((m, D_MODEL), F32),
        scratch_shapes=[pltpu.VMEM((2 * tm, D_MODEL), F32), pltpu.SemaphoreType.DMA],
        compiler_params=_cparams(("arbitrary",)),
        name="moe_combine",
    )(pos, x, wts, osort)


def _dispatch_plan(counts, ids, rank, tm, nt):
    padded = ((counts + tm - 1) // tm) * tm
    ends = jnp.cumsum(padded)
    offs = ends - padded
    off_of = jnp.zeros_like(ids)
    for e in range(N_EXPERTS):
        off_of = jnp.where(ids == e, offs[e], off_of)
    pos = off_of + rank
    starts = jnp.arange(nt, dtype=jnp.int32) * tm
    tile_expert = jnp.minimum(jnp.sum((starts[:, None] >= ends[None, :]).astype(jnp.int32), axis=1),
                              N_EXPERTS - 1)
    real_end = (offs + counts)[tile_expert]
    tile_rows = jnp.where(starts < ends[-1], jnp.clip(real_end - starts, 0, tm), 0).astype(jnp.int32)
    last_valid = jnp.maximum(jnp.sum((tile_rows != 0).astype(jnp.int32)) - 1, 0)
    tile_expert = jnp.where(tile_rows != 0, tile_expert, tile_expert[last_valid])
    return tile_expert, tile_rows, pos


def _combine_pos(pos, tm):
    m = pos.shape[0]
    return jnp.transpose(pos.reshape(m // tm, tm, 2), (0, 2, 1)).reshape(m // tm, 1, 2 * tm)


def _t5_bucket_np(rel):
    nb = N_BUCKETS // 2
    max_exact = nb // 2
    ret = np.where(rel > 0, nb, 0)
    n = np.abs(rel)
    nf = np.maximum(n, 1).astype(np.float32)
    large = max_exact + (np.log(nf / np.float32(max_exact)) / np.float32(math.log(MAX_DISTANCE / max_exact))
                         * np.float32(nb - max_exact)).astype(np.int32)
    large = np.minimum(large, nb - 1)
    return (ret + np.where(n < max_exact, n, large)).astype(np.int32)


def _bias_tensor(rel_table, n_q, valid):
    rel = np.arange(KEY_TILE, dtype=np.int32)[None, :] - WINDOW - np.arange(n_q, dtype=np.int32)[:, None]
    onehot = np.eye(N_BUCKETS, dtype=np.float32)[:, _t5_bucket_np(rel).reshape(-1)]
    bias = jnp.dot(rel_table.T[HEAD_PERM], jnp.asarray(onehot), precision=lax.Precision.HIGHEST)
    return jnp.where(jnp.asarray(valid)[None], bias.reshape(A_HEADS, n_q, KEY_TILE), NEG).astype(F32)


def _stack_heads(x):
    _, r, c = x.shape
    return jnp.transpose(x.reshape(2, 4, 2, r, c), (0, 2, 1, 3, 4)).reshape(2, 2, 4 * r, c)


def _with_sink_column(bias, sinks_perm):
    return _stack_heads(bias.at[:, :, KEY_TILE - 1].set(sinks_perm[:, None]))


def _prompt_valid():
    qc = np.arange(Q_TILE)[:, None] // CHUNK
    kc = np.arange(KEY_TILE)[None, :] // CHUNK
    return (kc >= qc) & (kc <= qc + WINDOW // CHUNK)


def _sample_valid(rows):
    return np.broadcast_to(np.arange(KEY_TILE)[None, :] < WINDOW + rows, (rows, KEY_TILE))


def _mixer_weights(l, g_mix, w_in_bf, b_gate, q_norm_g, k_norm_g, sinks, w_conv_b, w_conv_c, b_conv_c,
                   ln_c_g, ln_c_b, w_proj_a, w_proj_b, w_proj_c, w_out):
    wpa = w_proj_a[l].reshape(A_HEADS, A_HEAD_DIM, D_MODEL)[HEAD_PERM].reshape(A_WIDTH, D_MODEL)
    return dict(
        g_mix=g_mix[l], w_in=w_in_bf, layer=l, b_gate=b_gate[l],
        qg2=(jnp.tile(q_norm_g[l], 2) * (A_HEAD_DIM ** -0.5)).reshape(1, LANES), kg2=jnp.tile(k_norm_g[l], 2).reshape(1, LANES),
        sinks=sinks[l][HEAD_PERM],
        w_cb=w_conv_b[l], w_cc=w_conv_c[l], b_cc=b_conv_c[l], ln_g=ln_c_g[l], ln_b=ln_c_b[l],
        wpa=wpa.astype(BF16), wpb=w_proj_b[l].astype(BF16), wpc=w_proj_c[l].astype(BF16),
        wo=w_out[l].astype(BF16))


def _mixer_prompt(x, mw, bias, batch, seq):
    z = norm_matmul(x, mw["g_mix"], mw["w_in"], mw["layer"], tm=1024, tn=2048, out_dtype=BF16)
    oa, nk, nv = swa_prompt(z, batch, seq, mw["qg2"], mw["kg2"], bias, mw["sinks"])
    ob, oc, ncb, ncc = conv_prompt(z, batch, seq, mw["w_cb"], mw["w_cc"], mw["b_cc"], mw["ln_g"], mw["ln_b"])
    y = merge(oa, ob, oc, z, mw["b_gate"], mw["wpa"], mw["wpb"], mw["wpc"], mw["wo"], x)
    return y, (nk, nv, ncb[:, 8 - (B_CONV - 1):], ncc[:, HALO - (C_CONV - 1):])


def _mixer_sample(x, mw, bias, batch, rows, cache_k, cache_v, layer, st_b, st_c):
    z = norm_matmul(x, mw["g_mix"], mw["w_in"], mw["layer"], tm=x.shape[0], tn=1024, out_dtype=BF16)
    oa, nk, nv = swa_sample(z, batch, rows, cache_k, cache_v, layer, mw["qg2"], mw["kg2"],
                            _with_sink_column(bias, mw["sinks"]))
    stb = jnp.pad(st_b, ((0, 0), (HALO - (B_CONV - 1), 0), (0, 0)))
    stc = jnp.pad(st_c, ((0, 0), (HALO - (C_CONV - 1), 0), (0, 0)))
    ob, oc, ncb, ncc = conv_sample(z, batch, rows, stb, stc, mw["w_cb"], mw["w_cc"], mw["b_cc"],
                                   mw["ln_g"], mw["ln_b"])
    y = merge(oa, ob, oc, z, mw["b_gate"], mw["wpa"], mw["wpb"], mw["wpc"], mw["wo"], x)
    return y, (nk, nv, ncb[:, 8 - (B_CONV - 1):], ncc[:, HALO - (C_CONV - 1):])


def _channel_mixer(yp, ys, l, g_ffn, w_ffn_gu, w_ffn_d, w_router, b_router, w_moe_gu, w_moe_d):
    if l % 2 == 0:
        return (ffn(yp, g_ffn[l], w_ffn_gu[l // 2], w_ffn_d[l // 2]),
                ffn(ys, g_ffn[l], w_ffn_gu[l // 2], w_ffn_d[l // 2]))
    i = l // 2
    w_pad = jnp.pad(w_router[i], ((0, 0), (0, LANES - N_EXPERTS)))
    b_pad = jnp.pad(b_router[i], (0, LANES - N_EXPERTS)).reshape(1, LANES)
    wts_p, ids_p, rank_p, cnt_p = router(yp, g_ffn[l], w_pad, b_pad, jnp.zeros((1, LANES), F32))
    wts_s, ids_s, rank_s, cnt = router(ys, g_ffn[l], w_pad, b_pad, cnt_p)
    mp, ms = yp.shape[0], ys.shape[0]
    nt = (2 * (mp + ms)) // MOE_TM + N_EXPERTS
    ids = jnp.concatenate([ids_p[:, :2], ids_s[:, :2]], axis=0)
    rank = jnp.concatenate([rank_p[:, :2], rank_s[:, :2]], axis=0)
    tile_expert, tile_rows, pos = _dispatch_plan(cnt[0, :N_EXPERTS].astype(jnp.int32), ids, rank, MOE_TM, nt)
    tm_p, tm_s = 512, min(512, ms)
    pos_p, pos_s = _combine_pos(pos[:mp], tm_p), _combine_pos(pos[mp:], tm_s)
    xs = jnp.zeros((nt * MOE_TM, D_MODEL // 2), jnp.uint32)
    xs = moe_dispatch(pos_p, yp, g_ffn[l], xs, tm=tm_p)
    xs = moe_dispatch(pos_s, ys, g_ffn[l], xs, tm=tm_s)
    osort = moe_ffn(tile_expert, tile_rows, xs, w_moe_gu[i], w_moe_d[i])
    return (moe_combine(pos_p, yp, wts_p, osort, tm=tm_p), moe_combine(pos_s, ys, wts_s, osort, tm=tm_s))


def kernel(x_prompt, x_sample, mem_prompt, cache_mem_k, cache_mem_v, cache_swa_k, cache_swa_v, state_conv_b, state_conv_c, rel_table, g_mix, w_in, b_gate, q_norm_g, k_norm_g, sinks, w_conv_b, w_conv_c, b_conv_c, ln_c_g, ln_c_b, w_proj_a, w_proj_b, w_proj_c, w_out, g_xattn, g_mem, w_xq, w_xkv, xq_norm_g, xk_norm_g, w_xo, g_ffn, w_ffn_gu, w_ffn_d, w_router, b_router, w_moe_gu, w_moe_d):
    batch, seq, d = x_prompt.shape
    dec_batch, dec_seq, _ = x_sample.shape
    depth = g_mix.shape[0]
    yp = x_prompt.reshape(batch * seq, d)
    ys = x_sample.reshape(dec_batch * dec_seq, d)
    mem = mem_prompt.reshape(batch * N_MEM, d)
    first = np.arange(KEY_TILE)[None, :] >= Q_TILE
    bias_p = jnp.stack([_stack_heads(_bias_tensor(rel_table, Q_TILE, _prompt_valid() & first)),
                        _stack_heads(_bias_tensor(rel_table, Q_TILE, _prompt_valid()))])
    bias_s = _bias_tensor(rel_table, dec_seq, _sample_valid(dec_seq))
    outs = [[] for _ in range(10)]
    w_in_bf = w_in.astype(BF16)
    ck = cache_swa_k.reshape(depth, dec_batch, WINDOW, A_KV_WIDTH)
    cv = cache_swa_v.reshape(depth, dec_batch, WINDOW, A_KV_WIDTH)
    for l in range(depth):
        mw = _mixer_weights(l, g_mix, w_in_bf, b_gate, q_norm_g, k_norm_g, sinks, w_conv_b, w_conv_c,
                            b_conv_c, ln_c_g, ln_c_b, w_proj_a, w_proj_b, w_proj_c, w_out)
        wq = w_xq[l].astype(BF16)
        wo = w_xo[l].astype(BF16)
        ffn_args = (g_ffn, w_ffn_gu, w_ffn_d, w_router, b_router, w_moe_gu, w_moe_d)
        yp, (nk, nv, ncb, ncc) = _mixer_prompt(yp, mw, bias_p, batch, seq)
        mk, mv = mem_kv(mem, g_mem[l], w_xkv[l].astype(BF16), xk_norm_g[l])
        mk3 = mk.reshape(batch, N_MEM, X_WIDTH)
        mv3 = mv.reshape(batch, N_MEM, X_WIDTH)
        yp = xattn(yp, g_xattn[l], wq, xq_norm_g[l], mk3[None], mv3[None], wo, layer=0, nb=1, rpb=512,
                   tiles_per_mem=seq // 512)
        for lst, v in zip(outs[:6], (mk3.reshape(batch, N_MEM, X_HEADS, X_HEAD_DIM),
                                     mv3.reshape(batch, N_MEM, X_HEADS, X_HEAD_DIM),
                                     nk.reshape(batch, WINDOW, A_KV_HEADS, A_HEAD_DIM),
                                     nv.reshape(batch, WINDOW, A_KV_HEADS, A_HEAD_DIM), ncb, ncc)):
            lst.append(v)
        ys, (nk, nv, ncb, ncc) = _mixer_sample(ys, mw, bias_s, dec_batch, dec_seq, ck, cv, l,
                                               state_conv_b[l], state_conv_c[l])
        ys = xattn(ys, g_xattn[l], wq, xq_norm_g[l], cache_mem_k, cache_mem_v, wo, layer=l, nb=8, rpb=dec_seq,
                   tiles_per_mem=1)
        yp, ys = _channel_mixer(yp, ys, l, *ffn_args)
        for lst, v in zip(outs[6:], (nk.reshape(dec_batch, dec_seq, A_KV_HEADS, A_HEAD_DIM),
                                     nv.reshape(dec_batch, dec_seq, A_KV_HEADS, A_HEAD_DIM), ncb, ncc)):
            lst.append(v)
    return (yp.reshape(batch, seq, d), ys.reshape(dec_batch, dec_seq, d)) + tuple(jnp.stack(o) for o in outs)
```

```python
import functools
import math

import numpy as np
import jax
import jax.numpy as jnp
from jax import lax
from jax.experimental import pallas as pl
from jax.experimental.pallas import tpu as pltpu

F32 = jnp.float32
BF16 = jnp.bfloat16

D_MODEL = 2048
CHUNK = 64
A_HEADS = 16
A_KV_HEADS = 4
A_HEAD_DIM = 64
A_WIDTH = A_HEADS * A_HEAD_DIM
A_KV_WIDTH = A_KV_HEADS * A_HEAD_DIM
WINDOW = 128
N_BUCKETS = 32
MAX_DISTANCE = 128
B_WIDTH = 512
B_CONV = 3
C_WIDTH = 512
C_CONV = 31
N_MEM = 256
X_HEADS = 4
X_HEAD_DIM = 128
X_WIDTH = X_HEADS * X_HEAD_DIM
D_FF = 5632
N_EXPERTS = 8
EPS = 1e-6

LANES = 128
KEY_TILE = 256
Q_TILE = 128
HALO = 32
NEG = -1e30
VMEM_LIMIT = 56 * 1024 * 1024
FFN_VMEM_LIMIT = 60 * 1024 * 1024

COL_Q = 0
COL_K, COL_V = 4, 5
COL_GB, COL_GC, COL_HB, COL_GA, COL_GG = 3, 4, 5, 6, 7
COL_GATE0 = 2
IN_COLS = 4096 + 3 * D_MODEL

HEAD_PERM = np.array([8 * n + (p % 2) * 4 + p // 2 for n in range(2) for p in range(8)])


def _cparams(sem, vmem_limit=VMEM_LIMIT):
    return pltpu.CompilerParams(dimension_semantics=sem, vmem_limit_bytes=vmem_limit)


def _rms(x, g):
    ms = jnp.mean(x * x, axis=-1, keepdims=True)
    return x * lax.rsqrt(ms + EPS) * g


def _resident(shape):
    nd = len(shape)
    return pl.BlockSpec(shape, lambda *_: (0,) * nd, pipeline_mode=pl.Buffered(1))


def _permuted_q_columns(acc):
    lo_lane = lax.broadcasted_iota(jnp.int32, (1, LANES), 1) < A_HEAD_DIM
    nat = [acc[:, c * LANES:(c + 1) * LANES] for c in range(A_WIDTH // LANES)]
    swapped = [pltpu.roll(x, A_HEAD_DIM, 1) for x in nat]
    cols = []
    for n in range(2):
        for m in range(4):
            ca, cb = 4 * n + m // 2, 4 * n + 2 + m // 2
            cols.append(jnp.where(lo_lane, nat[ca], swapped[cb]) if m % 2 == 0
                        else jnp.where(lo_lane, swapped[ca], nat[cb]))
    return cols


def _norm_matmul_kernel(x_ref, g_ref, w_ref, o_ref, hn_ref):
    j = pl.program_id(1)

    @pl.when(j == 0)
    def _():
        hn_ref[...] = _rms(x_ref[...], g_ref[...]).astype(BF16)
        acc = jnp.dot(hn_ref[...], w_ref[...], preferred_element_type=F32)
        for c, col in enumerate(_permuted_q_columns(acc)):
            o_ref[:, c * LANES:(c + 1) * LANES] = col.astype(o_ref.dtype)
        if o_ref.shape[1] > A_WIDTH:
            o_ref[:, A_WIDTH:] = acc[:, A_WIDTH:].astype(o_ref.dtype)

    @pl.when(j > 0)
    def _():
        o_ref[...] = jnp.dot(hn_ref[...], w_ref[...], preferred_element_type=F32).astype(o_ref.dtype)


def norm_matmul(x, g, w, layer, *, tm, tn, out_dtype):
    m, k = x.shape
    n = w.shape[2]
    return pl.pallas_call(
        _norm_matmul_kernel,
        grid=(m // tm, n // tn),
        in_specs=[pl.BlockSpec((tm, k), lambda i, j: (i, 0)),
                  pl.BlockSpec((1, k), lambda i, j: (0, 0)),
                  pl.BlockSpec((None, k, tn), lambda i, j: (layer, 0, j))],
        out_specs=pl.BlockSpec((tm, tn), lambda i, j: (i, j)),
        out_shape=jax.ShapeDtypeStruct((m, n), out_dtype),
        scratch_shapes=[pltpu.VMEM((tm, k), BF16)],
        compiler_params=_cparams(("parallel", "arbitrary")),
        name="norm_matmul",
    )(x, g.reshape(1, k), w)


def _mem_kv_kernel(x_ref, g_ref, w_ref, kg_ref, k_ref, v_ref):
    hn = _rms(x_ref[...], g_ref[...]).astype(BF16)
    kv = jnp.dot(hn, w_ref[...], preferred_element_type=F32)
    for h in range(X_HEADS):
        sl = slice(h * X_HEAD_DIM, (h + 1) * X_HEAD_DIM)
        k_ref[:, sl] = _rms(kv[:, sl], kg_ref[...])
    v_ref[...] = kv[:, X_WIDTH:]


def mem_kv(mem, g, w_bf, kg):
    m, k = mem.shape
    tm = 256
    return pl.pallas_call(
        _mem_kv_kernel,
        grid=(m // tm,),
        in_specs=[pl.BlockSpec((tm, k), lambda i: (i, 0)),
                  pl.BlockSpec((1, k), lambda i: (0, 0)),
                  _resident((k, 2 * X_WIDTH)),
                  pl.BlockSpec((1, X_HEAD_DIM), lambda i: (0, 0))],
        out_specs=[pl.BlockSpec((tm, X_WIDTH), lambda i: (i, 0)),
                   pl.BlockSpec((tm, X_WIDTH), lambda i: (i, 0))],
        out_shape=[jax.ShapeDtypeStruct((m, X_WIDTH), F32)] * 2,
        compiler_params=_cparams(("parallel",)),
        name="mem_kv",
    )(mem, g.reshape(1, k), w_bf, kg.reshape(1, X_HEAD_DIM))


def _half_norm(x, g):
    x2 = x * x
    lo_lane = lax.broadcasted_iota(jnp.int32, (1, LANES), 1) < A_HEAD_DIM
    s_lo = jnp.sum(jnp.where(lo_lane, x2, 0.0), axis=-1, keepdims=True)
    s_hi = jnp.sum(jnp.where(lo_lane, 0.0, x2), axis=-1, keepdims=True)
    ms = jnp.where(lo_lane, s_lo, s_hi) * (1.0 / A_HEAD_DIM)
    return x * lax.rsqrt(ms + EPS) * g


def _swa_heads(q_ref, qg, k2, v2, bias_ref, sink_ref, o_ref, *, stack):
    lo_lane = lax.broadcasted_iota(jnp.int32, (1, LANES), 1) < A_HEAD_DIM
    rows = q_ref.shape[0]
    cols = A_WIDTH // LANES // 2
    for n in range(2):
        k_half = (jnp.where(lo_lane, k2[n], 0.0).astype(BF16), jnp.where(lo_lane, 0.0, k2[n]).astype(BF16))
        for g in range(0, cols, stack):
            qs = [_half_norm(q_ref[:, c * LANES:(c + 1) * LANES].astype(F32), qg)
                  for c in range(cols * n + g, cols * n + g + stack)]
            qn = (qs[0] if stack == 1 else jnp.concatenate(qs, axis=0)).astype(BF16)
            at = slice(g * rows, (g + stack) * rows)
            halves = []
            for half in range(2):
                s = lax.dot_general(qn, k_half[half], (((1,), (1,)), ((), ())), preferred_element_type=F32)
                s = s + bias_ref[n, half, at, :]
                if sink_ref is None:
                    m = jnp.max(s, axis=-1, keepdims=True)
                    p = jnp.exp(s - m)
                    den = jnp.sum(p, axis=-1, keepdims=True)
                else:
                    sink = sink_ref[2 * (cols * n + g) + half]
                    m = jnp.maximum(jnp.max(s, axis=-1, keepdims=True), sink)
                    p = jnp.exp(s - m)
                    den = jnp.sum(p, axis=-1, keepdims=True) + jnp.exp(sink - m)
                o = jnp.dot(p.astype(BF16), v2[n], preferred_element_type=F32)
                halves.append(o / den)
            o = jnp.where(lo_lane, halves[0], halves[1]).astype(o_ref.dtype)
            for i in range(stack):
                c = cols * n + g + i
                o_ref[:, c * LANES:(c + 1) * LANES] = o[i * rows:(i + 1) * rows]


def _swa_prompt_kernel(q_ref, kc_ref, kp_ref, vc_ref, vp_ref, qg_ref, kg_ref,
                       bias_ref, sink_ref, o_ref, nk_ref, nv_ref):
    k2, v2 = [], []
    for n in range(2):
        sl = slice(n * LANES, (n + 1) * LANES)
        kcat = jnp.concatenate([kp_ref[:, sl], kc_ref[:, sl]], axis=0).astype(F32)
        kn = _half_norm(kcat, kg_ref[...])
        nk_ref[0, :, sl] = kn[Q_TILE:]
        k2.append(kn)
        v2.append(jnp.concatenate([vp_ref[:, sl], vc_ref[:, sl]], axis=0))
    nv_ref[0] = vc_ref[...].astype(F32)
    _swa_heads(q_ref, qg_ref[...], k2, v2, bias_ref, sink_ref, o_ref, stack=1)


def swa_prompt(z, batch, seq, qg2, kg2, bias, sinks):
    nt = seq // Q_TILE
    row = lambda b, t: b * nt + t
    prev = lambda b, t: jnp.maximum(b * nt + t - 1, 0)
    return pl.pallas_call(
        _swa_prompt_kernel,
        grid=(batch, nt),
        in_specs=[pl.BlockSpec((Q_TILE, A_WIDTH), lambda b, t: (row(b, t), COL_Q)),
                  pl.BlockSpec((Q_TILE, A_KV_WIDTH), lambda b, t: (row(b, t), COL_K)),
                  pl.BlockSpec((Q_TILE, A_KV_WIDTH), lambda b, t: (prev(b, t), COL_K)),
                  pl.BlockSpec((Q_TILE, A_KV_WIDTH), lambda b, t: (row(b, t), COL_V)),
                  pl.BlockSpec((Q_TILE, A_KV_WIDTH), lambda b, t: (prev(b, t), COL_V)),
                  pl.BlockSpec((1, LANES), lambda b, t: (0, 0)),
                  pl.BlockSpec((1, LANES), lambda b, t: (0, 0)),
                  pl.BlockSpec((None, 2, 2, 4 * Q_TILE, KEY_TILE),
                               lambda b, t: (jnp.minimum(t, 1), 0, 0, 0, 0)),
                  pl.BlockSpec(memory_space=pltpu.SMEM)],
        out_specs=[pl.BlockSpec((Q_TILE, A_WIDTH), lambda b, t: (row(b, t), 0)),
                   pl.BlockSpec((1, WINDOW, A_KV_WIDTH), lambda b, t: (b, 0, 0)),
                   pl.BlockSpec((1, WINDOW, A_KV_WIDTH), lambda b, t: (b, 0, 0))],
        out_shape=[jax.ShapeDtypeStruct((batch * seq, A_WIDTH), BF16),
                   jax.ShapeDtypeStruct((batch, WINDOW, A_KV_WIDTH), F32),
                   jax.ShapeDtypeStruct((batch, WINDOW, A_KV_WIDTH), F32)],
        compiler_params=_cparams(("parallel", "arbitrary")),
        name="swa_prompt",
    )(z, z, z, z, z, qg2, kg2, bias, sinks)


def _swa_sample_kernel(q_ref, kn_ref, vn_ref, ck_ref, cv_ref, qg_ref, kg_ref,
                       bias_ref, o_ref, nk_ref, nv_ref):
    rows = q_ref.shape[0]
    pad = KEY_TILE - WINDOW - rows
    k2, v2 = [], []
    for n in range(2):
        sl = slice(n * LANES, (n + 1) * LANES)
        kn = _half_norm(kn_ref[:, sl].astype(F32), kg_ref[...])
        nk_ref[0, :, sl] = kn
        k2.append(jnp.concatenate([ck_ref[0, :, sl], kn, jnp.zeros((pad, LANES), F32)], axis=0))
        v2.append(jnp.concatenate([cv_ref[0, :, sl].astype(BF16), vn_ref[:, sl],
                                   jnp.zeros((pad, LANES), BF16)], axis=0))
    nv_ref[0] = vn_ref[...].astype(F32)
    _swa_heads(q_ref, qg_ref[...], k2, v2, bias_ref, None, o_ref, stack=4)


def swa_sample(z, batch, rows, cache_k, cache_v, layer, qg2, kg2, bias):
    return pl.pallas_call(
        _swa_sample_kernel,
        grid=(batch,),
        in_specs=[pl.BlockSpec((rows, A_WIDTH), lambda b: (b, COL_Q)),
                  pl.BlockSpec((rows, A_KV_WIDTH), lambda b: (b, COL_K)),
                  pl.BlockSpec((rows, A_KV_WIDTH), lambda b: (b, COL_V)),
                  pl.BlockSpec((None, 1, WINDOW, A_KV_WIDTH), lambda b: (layer, b, 0, 0)),
                  pl.BlockSpec((None, 1, WINDOW, A_KV_WIDTH), lambda b: (layer, b, 0, 0)),
                  pl.BlockSpec((1, LANES), lambda b: (0, 0)),
                  pl.BlockSpec((1, LANES), lambda b: (0, 0)),
                  _resident((2, 2, 4 * rows, KEY_TILE))],
        out_specs=[pl.BlockSpec((rows, A_WIDTH), lambda b: (b, 0)),
                   pl.BlockSpec((1, rows, A_KV_WIDTH), lambda b: (b, 0, 0)),
                   pl.BlockSpec((1, rows, A_KV_WIDTH), lambda b: (b, 0, 0))],
        out_shape=[jax.ShapeDtypeStruct((batch * rows, A_WIDTH), BF16),
                   jax.ShapeDtypeStruct((batch, rows, A_KV_WIDTH), F32),
                   jax.ShapeDtypeStruct((batch, rows, A_KV_WIDTH), F32)],
        compiler_params=_cparams(("parallel",)),
        name="swa_sample",
    )(z, z, z, cache_k, cache_v, qg2, kg2, bias)


def _conv_body(gb_ref, ub_main, uc_main, ub_halo, uc_halo, wb_ref, wc_ref, bc_ref, lg_ref, lb_ref,
               ob_ref, oc_ref, nb_ref, nc_ref, sb_ref, sc_ref, ph_ref, write_state):
    rows = ub_main.shape[0]
    sb_ref[0:HALO] = ub_halo
    sb_ref[HALO:HALO + rows] = ub_main
    sc_ref[0:HALO] = uc_halo
    sc_ref[HALO:HALO + rows] = uc_main
    span = rows + HALO
    sc_ref[span:span + 8] = jnp.zeros((8, C_WIDTH), F32)
    for b in range(1, 8):
        ph_ref[b - 1] = sc_ref[pl.ds(b, span), :]
    sub = min(rows, 32)
    for r0 in range(0, rows, sub):
        yb = jnp.zeros((sub, B_WIDTH), F32)
        for k in range(B_CONV):
            yb = yb + wb_ref[k:k + 1, :] * sb_ref[pl.ds(r0 + HALO - (B_CONV - 1) + k, sub), :]
        ob_ref[r0:r0 + sub, :] = (gb_ref[r0:r0 + sub, :].astype(F32) * yb).astype(ob_ref.dtype)
        yc = jnp.zeros((sub, C_WIDTH), F32)
        for k in range(C_CONV):
            shift, phase = divmod(HALO - (C_CONV - 1) + k, 8)
            at = pl.ds(r0 + 8 * shift, sub)
            taps = sc_ref[at, :] if phase == 0 else ph_ref[phase - 1, at, :]
            yc = yc + wc_ref[k:k + 1, :] * taps
        yc = yc + bc_ref[...]
        mu = jnp.mean(yc, axis=-1, keepdims=True)
        xc = yc - mu
        y = xc * lax.rsqrt(jnp.mean(xc * xc, axis=-1, keepdims=True) + EPS)
        y = y * lg_ref[...] + lb_ref[...]
        oc_ref[r0:r0 + sub, :] = (y * jax.nn.sigmoid(y)).astype(oc_ref.dtype)

    def _state():
        nb_ref[0] = sb_ref[rows + HALO - 8:rows + HALO]
        nc_ref[0] = sc_ref[rows:rows + HALO]

    write_state(_state)


def _conv_prompt_kernel(gb_ref, gc_ref, hb_ref, ga_ref, gg_ref, gch_ref, hbh_ref, gah_ref, ggh_ref,
                        wb_ref, wc_ref, bc_ref, lg_ref, lb_ref,
                        ob_ref, oc_ref, nb_ref, nc_ref, sb_ref, sc_ref, ph_ref):
    t = pl.program_id(1)
    hist = (t > 0).astype(F32)
    ub_main = gc_ref[...].astype(F32) * hb_ref[...].astype(F32)
    uc_main = ga_ref[...].astype(F32) * jax.nn.sigmoid(gg_ref[...].astype(F32))
    ub_halo = gch_ref[...].astype(F32) * hbh_ref[...].astype(F32) * hist
    uc_halo = gah_ref[...].astype(F32) * jax.nn.sigmoid(ggh_ref[...].astype(F32)) * hist
    last = pl.num_programs(1) - 1
    _conv_body(gb_ref, ub_main, uc_main, ub_halo, uc_halo, wb_ref, wc_ref, bc_ref, lg_ref, lb_ref,
               ob_ref, oc_ref, nb_ref, nc_ref, sb_ref, sc_ref, ph_ref,
               lambda f: pl.when(t == last)(f))


def conv_prompt(z, batch, seq, wb, wc, bc, lg, lb, *, tr=256):
    nt = seq // tr
    hp = tr // HALO
    main = lambda c: pl.BlockSpec((tr, B_WIDTH), lambda b, t: (b * nt + t, c))
    halo = lambda c: pl.BlockSpec((HALO, B_WIDTH), lambda b, t: (jnp.maximum((b * nt + t) * hp - 1, 0), c))
    vec = lambda r: pl.BlockSpec((r, B_WIDTH), lambda b, t: (0, 0))
    return pl.pallas_call(
        _conv_prompt_kernel,
        grid=(batch, nt),
        in_specs=[main(COL_GB), main(COL_GC), main(COL_HB), main(COL_GA), main(COL_GG),
                  halo(COL_GC), halo(COL_HB), halo(COL_GA), halo(COL_GG),
                  vec(B_CONV), vec(C_CONV), vec(1), vec(1), vec(1)],
        out_specs=[pl.BlockSpec((tr, B_WIDTH), lambda b, t: (b * nt + t, 0)),
                   pl.BlockSpec((tr, C_WIDTH), lambda b, t: (b * nt + t, 0)),
                   pl.BlockSpec((1, 8, B_WIDTH), lambda b, t: (b, 0, 0)),
                   pl.BlockSpec((1, HALO, C_WIDTH), lambda b, t: (b, 0, 0))],
        out_shape=[jax.ShapeDtypeStruct((batch * seq, B_WIDTH), BF16),
                   jax.ShapeDtypeStruct((batch * seq, C_WIDTH), BF16),
                   jax.ShapeDtypeStruct((batch, 8, B_WIDTH), F32),
                   jax.ShapeDtypeStruct((batch, HALO, C_WIDTH), F32)],
        scratch_shapes=[pltpu.VMEM((tr + HALO, B_WIDTH), F32), pltpu.VMEM((tr + HALO + 8, C_WIDTH), F32),
                        pltpu.VMEM((7, tr + HALO, C_WIDTH), F32)],
        compiler_params=_cparams(("parallel", "arbitrary")),
        name="conv_prompt",
    )(z, z, z, z, z, z, z, z, z, wb, wc, bc.reshape(1, -1), lg.reshape(1, -1), lb.reshape(1, -1))


def _conv_sample_kernel(gb_ref, gc_ref, hb_ref, ga_ref, gg_ref, stb_ref, stc_ref,
                        wb_ref, wc_ref, bc_ref, lg_ref, lb_ref,
                        ob_ref, oc_ref, nb_ref, nc_ref, sb_ref, sc_ref, ph_ref):
    ub_main = gc_ref[...].astype(F32) * hb_ref[...].astype(F32)
    uc_main = ga_ref[...].astype(F32) * jax.nn.sigmoid(gg_ref[...].astype(F32))
    _conv_body(gb_ref, ub_main, uc_main, stb_ref[0], stc_ref[0], wb_ref, wc_ref, bc_ref, lg_ref, lb_ref,
               ob_ref, oc_ref, nb_ref, nc_ref, sb_ref, sc_ref, ph_ref, lambda f: f())


def conv_sample(z, batch, rows, stb, stc, wb, wc, bc, lg, lb):
    main = lambda c: pl.BlockSpec((rows, B_WIDTH), lambda b: (b, c))
    vec = lambda r: pl.BlockSpec((r, B_WIDTH), lambda b: (0, 0))
    return pl.pallas_call(
        _conv_sample_kernel,
        grid=(batch,),
        in_specs=[main(COL_GB), main(COL_GC), main(COL_HB), main(COL_GA), main(COL_GG),
                  pl.BlockSpec((1, HALO, B_WIDTH), lambda b: (b, 0, 0)),
                  pl.BlockSpec((1, HALO, C_WIDTH), lambda b: (b, 0, 0)),
                  vec(B_CONV), vec(C_CONV), vec(1), vec(1), vec(1)],
        out_specs=[pl.BlockSpec((rows, B_WIDTH), lambda b: (b, 0)),
                   pl.BlockSpec((rows, C_WIDTH), lambda b: (b, 0)),
                   pl.BlockSpec((1, 8, B_WIDTH), lambda b: (b, 0, 0)),
                   pl.BlockSpec((1, HALO, C_WIDTH), lambda b: (b, 0, 0))],
        out_shape=[jax.ShapeDtypeStruct((batch * rows, B_WIDTH), BF16),
                   jax.ShapeDtypeStruct((batch * rows, C_WIDTH), BF16),
                   jax.ShapeDtypeStruct((batch, 8, B_WIDTH), F32),
                   jax.ShapeDtypeStruct((batch, HALO, C_WIDTH), F32)],
        scratch_shapes=[pltpu.VMEM((rows + HALO, B_WIDTH), F32), pltpu.VMEM((rows + HALO + 8, C_WIDTH), F32),
                        pltpu.VMEM((7, rows + HALO, C_WIDTH), F32)],
        compiler_params=_cparams(("parallel",)),
        name="conv_sample",
    )(z, z, z, z, z, stb, stc, wb, wc, bc.reshape(1, -1), lg.reshape(1, -1), lb.reshape(1, -1))


def _merge_kernel(oa_ref, ob_ref, oc_ref, l0_ref, l1_ref, l2_ref, bg_ref,
                  wpa_ref, wpb_ref, wpc_ref, wo_ref, x_ref, o_ref):
    def gated(l_ref, i, o_r, w_r):
        gate = jax.nn.sigmoid(l_ref[...].astype(F32) + bg_ref[i:i + 1, :])
        return gate * jnp.dot(o_r[...], w_r[...], preferred_element_type=F32)

    merged = gated(l0_ref, 0, oa_ref, wpa_ref)
    merged = merged + gated(l1_ref, 1, ob_ref, wpb_ref)
    merged = merged + gated(l2_ref, 2, oc_ref, wpc_ref)
    o_ref[...] = x_ref[...] + jnp.dot(merged.astype(BF16), wo_ref[...], preferred_element_type=F32)


def merge(oa, ob, oc, z, bg, wpa, wpb, wpc, wo, x, *, tm=512):
    m = x.shape[0]
    tm = min(tm, m)
    rows = lambda w: pl.BlockSpec((tm, w), lambda i: (i, 0))
    gate = lambda c: pl.BlockSpec((tm, D_MODEL), lambda i: (i, COL_GATE0 + c))
    return pl.pallas_call(
        _merge_kernel,
        grid=(m // tm,),
        in_specs=[rows(A_WIDTH), rows(B_WIDTH), rows(C_WIDTH), gate(0), gate(1), gate(2),
                  pl.BlockSpec((3, D_MODEL), lambda i: (0, 0)),
                  _resident((A_WIDTH, D_MODEL)), _resident((B_WIDTH, D_MODEL)),
                  _resident((C_WIDTH, D_MODEL)), _resident((D_MODEL, D_MODEL)),
                  rows(D_MODEL)],
        out_specs=rows(D_MODEL),
        out_shape=jax.ShapeDtypeStruct((m, D_MODEL), F32),
        compiler_params=_cparams(("parallel",)),
        name="merge",
    )(oa, ob, oc, z, z, z, bg.reshape(3, D_MODEL), wpa, wpb, wpc, wo, x)


def _xattn_kernel(y_ref, g_ref, wq_ref, qg_ref, mk_ref, mv_ref, wo_ref, o_ref, *, nb, rpb):
    y = y_ref[...]
    hn = _rms(y, g_ref[...]).astype(BF16)
    q = jnp.dot(hn, wq_ref[...], preferred_element_type=F32)
    heads = []
    for h in range(X_HEADS):
        sl = slice(h * X_HEAD_DIM, (h + 1) * X_HEAD_DIM)
        qh = _rms(q[:, sl], qg_ref[...])
        per_batch = []
        for b in range(nb):
            qb = qh[b * rpb:(b + 1) * rpb].astype(BF16)
            if len(mk_ref.shape) == 4:
                kh = mk_ref[b, :, h, :].astype(BF16)
                vh = mv_ref[b, :, h, :].astype(BF16)
            else:
                kh = mk_ref[b, :, sl].astype(BF16)
                vh = mv_ref[b, :, sl].astype(BF16)
            s = lax.dot_general(qb, kh, (((1,), (1,)), ((), ())),
                                preferred_element_type=F32) * (X_HEAD_DIM ** -0.5)
            m = jnp.max(s, axis=-1, keepdims=True)
            p = jnp.exp(s - m)
            den = jnp.sum(p, axis=-1, keepdims=True)
            per_batch.append(jnp.dot(p.astype(BF16), vh, preferred_element_type=F32) / den)
        heads.append(per_batch[0] if nb == 1 else jnp.concatenate(per_batch, axis=0))
    o = jnp.concatenate(heads, axis=1).astype(BF16)
    o_ref[...] = y + jnp.dot(o, wo_ref[...], preferred_element_type=F32)


def xattn(y, g, wq, qg, mk, mv, wo, *, layer, nb, rpb, tiles_per_mem):
    m = y.shape[0]
    tm = nb * rpb
    tail = (0,) * (mk.ndim - 2)
    mem_idx = ((lambda i: (layer, i // tiles_per_mem) + tail) if nb == 1
               else (lambda i: (layer, i) + tail))
    mem_block = (None, nb) + mk.shape[2:]
    return pl.pallas_call(
        functools.partial(_xattn_kernel, nb=nb, rpb=rpb),
        grid=(m // tm,),
        in_specs=[pl.BlockSpec((tm, D_MODEL), lambda i: (i, 0)),
                  pl.BlockSpec((1, D_MODEL), lambda i: (0, 0)),
                  _resident((D_MODEL, X_WIDTH)),
                  pl.BlockSpec((1, X_HEAD_DIM), lambda i: (0, 0)),
                  pl.BlockSpec(mem_block, mem_idx),
                  pl.BlockSpec(mem_block, mem_idx),
                  _resident((X_WIDTH, D_MODEL))],
        out_specs=pl.BlockSpec((tm, D_MODEL), lambda i: (i, 0)),
        out_shape=jax.ShapeDtypeStruct((m, D_MODEL), F32),
        compiler_params=_cparams(("parallel",)),
        name="xattn",
    )(y, g.reshape(1, -1), wq, qg.reshape(1, -1), mk, mv, wo)


def _ffn_kernel(x_ref, g_ref, wg_ref, wu_ref, wd_ref, o_ref, hn_ref):
    @pl.when(pl.program_id(1) == 0)
    def _():
        x = x_ref[...]
        hn_ref[...] = _rms(x, g_ref[...]).astype(BF16)
        o_ref[...] = x

    hn = hn_ref[...]
    g = jnp.dot(hn, wg_ref[...].astype(BF16), preferred_element_type=F32)
    u = jnp.dot(hn, wu_ref[...].astype(BF16), preferred_element_type=F32)
    a = (g * jax.nn.sigmoid(g) * u).astype(BF16)
    o_ref[...] += jnp.dot(a, wd_ref[...].astype(BF16), preferred_element_type=F32)


def ffn(x, g, w_gu, w_d, *, tm=2048, tf=256):
    m = x.shape[0]
    if m < tm:
        tm, tf = m, 512
    nf = D_FF // tf
    return pl.pallas_call(
        _ffn_kernel,
        grid=(m // tm, nf),
        in_specs=[pl.BlockSpec((tm, D_MODEL), lambda i, f: (i, 0), pipeline_mode=pl.Buffered(1)),
                  pl.BlockSpec((1, D_MODEL), lambda i, f: (0, 0)),
                  pl.BlockSpec((D_MODEL, tf), lambda i, f: (0, f)),
                  pl.BlockSpec((D_MODEL, tf), lambda i, f: (0, nf + f)),
                  pl.BlockSpec((tf, D_MODEL), lambda i, f: (f, 0))],
        out_specs=pl.BlockSpec((tm, D_MODEL), lambda i, f: (i, 0), pipeline_mode=pl.Buffered(1)),
        out_shape=jax.ShapeDtypeStruct((m, D_MODEL), F32),
        scratch_shapes=[pltpu.VMEM((tm, D_MODEL), BF16)],
        compiler_params=_cparams(("parallel", "arbitrary"), FFN_VMEM_LIMIT),
        name="ffn",
    )(x, g.reshape(1, -1), w_gu, w_gu, w_d)


def _split3(x):
    hi = x.astype(BF16)
    lo = (x - hi.astype(F32)).astype(BF16)
    return hi, lo


def _router_kernel(x_ref, g_ref, w_ref, b_ref, tri_ref, base_ref, wts_ref, ids_ref, rank_ref, cnt_ref, run_ref):
    @pl.when(pl.program_id(0) == 0)
    def _():
        run_ref[...] = base_ref[...]

    hn = _rms(x_ref[...], g_ref[...])
    h_hi, h_lo = _split3(hn)
    w_hi, w_lo = _split3(w_ref[...])
    dot = functools.partial(jnp.dot, preferred_element_type=F32)
    logits = dot(h_hi, w_hi) + dot(h_hi, w_lo) + dot(h_lo, w_hi) + b_ref[...]
    lane = lax.broadcasted_iota(jnp.int32, logits.shape, 1).astype(F32)
    logits = jnp.where(lane < N_EXPERTS, logits, -jnp.inf)
    v1 = jnp.max(logits, axis=-1, keepdims=True)
    i1 = jnp.min(jnp.where(logits == v1, lane, float(LANES)), axis=-1, keepdims=True)
    rest = jnp.where(lane == i1, -jnp.inf, logits)
    v2 = jnp.max(rest, axis=-1, keepdims=True)
    i2 = jnp.min(jnp.where(rest == v2, lane, float(LANES)), axis=-1, keepdims=True)
    e2 = jnp.exp(v2 - v1)
    den = 1.0 + e2
    first, second = lane == 0.0, lane == 1.0
    wts_ref[...] = jnp.where(first, 1.0 / den, 0.0) + jnp.where(second, e2 / den, 0.0)
    ids_ref[...] = (jnp.where(first, i1, 0.0) + jnp.where(second, i2, 0.0)).astype(jnp.int32)
    hit1, hit2 = lane == i1, lane == i2
    hits = jnp.where(jnp.logical_or(hit1, hit2), 1.0, 0.0)
    before = dot(tri_ref[...], hits.astype(BF16)) + run_ref[...]
    r1 = jnp.sum(jnp.where(hit1, before, 0.0), axis=-1, keepdims=True)
    r2 = jnp.sum(jnp.where(hit2, before, 0.0), axis=-1, keepdims=True)
    rank_ref[...] = (jnp.where(first, r1, 0.0) + jnp.where(second, r2, 0.0)).astype(jnp.int32)
    run_ref[...] += jnp.sum(hits, axis=0, keepdims=True)
    cnt_ref[...] = run_ref[...]


def router(x, g, w_pad, b_pad, base, *, tm=512):
    m = x.shape[0]
    tm = min(tm, m)
    tri = jnp.asarray(np.tril(np.ones((tm, tm), np.float32), -1), BF16)
    row = lambda w: pl.BlockSpec((tm, w), lambda i: (i, 0))
    fixed = lambda s: pl.BlockSpec(s, lambda i: (0, 0))
    return pl.pallas_call(
        _router_kernel,
        grid=(m // tm,),
        in_specs=[row(D_MODEL), fixed((1, D_MODEL)), fixed((D_MODEL, LANES)), fixed((1, LANES)),
                  fixed((tm, tm)), fixed((1, LANES))],
        out_specs=[row(LANES), row(LANES), row(LANES), fixed((1, LANES))],
        out_shape=[jax.ShapeDtypeStruct((m, LANES), F32),
                   jax.ShapeDtypeStruct((m, LANES), jnp.int32),
                   jax.ShapeDtypeStruct((m, LANES), jnp.int32),
                   jax.ShapeDtypeStruct((1, LANES), F32)],
        scratch_shapes=[pltpu.VMEM((1, LANES), F32)],
        compiler_params=_cparams(("arbitrary",)),
        name="router",
    )(x, g.reshape(1, -1), w_pad, b_pad, tri, base)


MOE_TM = 2048
MOE_TF = 256


def _row_copy(src, row, dst, r, sem):
    return pltpu.make_async_copy(src.at[pl.ds(row, 1)], dst.at[pl.ds(r, 1)], sem)


def _moe_dispatch_kernel(pos_ref, x_ref, g_ref, xs_in, xs_hbm, pk_ref, sem):
    del xs_in
    tm = x_ref.shape[0]
    half = D_MODEL // 2
    hn = _rms(x_ref[...], g_ref[...]).astype(BF16)
    lo_bits = lax.bitcast_convert_type(hn[:, :half].astype(F32), jnp.uint32)
    hi_bits = lax.bitcast_convert_type(hn[:, half:].astype(F32), jnp.uint32)
    pk_ref[...] = (lo_bits >> 16) | (hi_bits & jnp.uint32(0xFFFF0000))

    def issue(r, c):
        _row_copy(pk_ref, r, xs_hbm, pos_ref[0, 0, r], sem).start()
        _row_copy(pk_ref, r, xs_hbm, pos_ref[0, 0, tm + r], sem).start()
        return c

    lax.fori_loop(0, tm, issue, 0, unroll=8)

    def wait(r, c):
        _row_copy(pk_ref, r, xs_hbm, 0, sem).wait()
        _row_copy(pk_ref, r, xs_hbm, 0, sem).wait()
        return c

    lax.fori_loop(0, tm, wait, 0, unroll=8)


def moe_dispatch(pos, x, g, xs, *, tm):
    m = x.shape[0]
    return pl.pallas_call(
        _moe_dispatch_kernel,
        grid=(m // tm,),
        in_specs=[pl.BlockSpec((1, 1, 2 * tm), lambda i: (i, 0, 0), memory_space=pltpu.SMEM),
                  pl.BlockSpec((tm, D_MODEL), lambda i: (i, 0)),
                  pl.BlockSpec((1, D_MODEL), lambda i: (0, 0)),
                  pl.BlockSpec(memory_space=pl.ANY)],
        out_specs=pl.BlockSpec(memory_space=pl.ANY),
        out_shape=jax.ShapeDtypeStruct(xs.shape, xs.dtype),
        scratch_shapes=[pltpu.VMEM((tm, D_MODEL // 2), jnp.uint32), pltpu.SemaphoreType.DMA],
        input_output_aliases={3: 0},
        compiler_params=_cparams(("arbitrary",)),
        name="moe_dispatch",
    )(pos, x, g.reshape(1, -1), xs)


def _moe_ffn_kernel(te_ref, tv_ref, xs_ref, wg_ref, wu_ref, wd_ref, o_ref, hn_ref):
    t = pl.program_id(0)
    f = pl.program_id(1)
    rows = xs_ref.shape[0]
    half = D_MODEL // 2

    @pl.when(f == 0)
    def _():
        o_ref[...] = jnp.zeros_like(o_ref)
        xu = xs_ref[...]
        hn_ref[:, :half] = lax.bitcast_convert_type(xu << 16, F32).astype(BF16)
        hn_ref[:, half:] = lax.bitcast_convert_type(xu & jnp.uint32(0xFFFF0000), F32).astype(BF16)

    def swiglu(r):
        hn = hn_ref[0:r]
        g = jnp.dot(hn, wg_ref[...].astype(BF16), preferred_element_type=F32)
        u = jnp.dot(hn, wu_ref[...].astype(BF16), preferred_element_type=F32)
        a = (g * jax.nn.sigmoid(g) * u).astype(BF16)
        o_ref[0:r] += jnp.dot(a, wd_ref[...].astype(BF16), preferred_element_type=F32)

    nv = tv_ref[t]

    quarter = rows // 4
    for k in range(1, 5):
        @pl.when(jnp.logical_and(nv > (k - 1) * quarter, nv <= k * quarter))
        def _(k=k):
            swiglu(k * quarter)


def moe_ffn(tile_expert, tile_rows, xs, w_gu, w_d):
    tm = MOE_TM
    nt = xs.shape[0] // tm
    nf = D_FF // MOE_TF
    last = nf - 1
    col = lambda f, tv, t: jnp.where(tv[t] != 0, f, last)
    grid_spec = pltpu.PrefetchScalarGridSpec(
        num_scalar_prefetch=2,
        grid=(nt, nf),
        in_specs=[pl.BlockSpec((tm, D_MODEL // 2), lambda t, f, te, tv: (t, 0), pipeline_mode=pl.Buffered(1)),
                  pl.BlockSpec((None, D_MODEL, MOE_TF), lambda t, f, te, tv: (te[t], 0, col(f, tv, t))),
                  pl.BlockSpec((None, D_MODEL, MOE_TF), lambda t, f, te, tv: (te[t], 0, nf + col(f, tv, t))),
                  pl.BlockSpec((None, MOE_TF, D_MODEL), lambda t, f, te, tv: (te[t], col(f, tv, t), 0))],
        out_specs=pl.BlockSpec((tm, D_MODEL), lambda t, f, te, tv: (t, 0), pipeline_mode=pl.Buffered(1)),
        scratch_shapes=[pltpu.VMEM((tm, D_MODEL), BF16)],
    )
    return pl.pallas_call(
        _moe_ffn_kernel,
        grid_spec=grid_spec,
        out_shape=jax.ShapeDtypeStruct((nt * tm, D_MODEL), F32),
        compiler_params=_cparams(("arbitrary", "arbitrary"), FFN_VMEM_LIMIT),
        name="moe_ffn",
    )(tile_expert, tile_rows, xs, w_gu, w_gu, w_d)


def _moe_combine_kernel(pos_ref, x_ref, w_ref, osort_hbm, o_ref, abuf, sem):
    tm = x_ref.shape[0]

    def issue(r, c):
        _row_copy(osort_hbm, pos_ref[0, 0, r], abuf, r, sem).start()
        return c

    lax.fori_loop(0, 2 * tm, issue, 0, unroll=8)

    def wait(r, c):
        _row_copy(osort_hbm, 0, abuf, r, sem).wait()
        return c

    lax.fori_loop(0, 2 * tm, wait, 0, unroll=8)
    w = w_ref[...]
    o_ref[...] = x_ref[...] + w[:, 0:1] * abuf[0:tm, :] + w[:, 1:2] * abuf[tm:2 * tm, :]


def moe_combine(pos, x, wts, osort, *, tm=256):
    m = x.shape[0]
    return pl.pallas_call(
        _moe_combine_kernel,
        grid=(m // tm,),
        in_specs=[pl.BlockSpec((1, 1, 2 * tm), lambda i: (i, 0, 0), memory_space=pltpu.SMEM),
                  pl.BlockSpec((tm, D_MODEL), lambda i: (i, 0)),
                  pl.BlockSpec((tm, LANES), lambda i: (i, 0)),
                  pl.BlockSpec(memory_space=pl.ANY)],
        out_specs=pl.BlockSpec((tm, D_MODEL), lambda i: (i, 0)),
        out_shape=jax.ShapeDtypeStruct((m, D_MODEL), F32),
        scratch_shapes=[pltpu.VMEM((2 * tm, D_MODEL), F32), pltpu.SemaphoreType.DMA],
        compiler_params=_cparams(("arbitrary",)),
        name="moe_combine",
    )(pos, x, wts, osort)


def _dispatch_plan(counts, ids, rank, tm, nt):
    padded = ((counts + tm - 1) // tm) * tm
    ends = jnp.cumsum(padded)
    offs = ends - padded
    off_of = jnp.zeros_like(ids)
    for e in range(N_EXPERTS):
        off_of = jnp.where(ids == e, offs[e], off_of)
    pos = off_of + rank
    starts = jnp.arange(nt, dtype=jnp.int32) * tm
    tile_expert = jnp.minimum(jnp.sum((starts[:, None] >= ends[None, :]).astype(jnp.int32), axis=1),
                              N_EXPERTS - 1)
    real_end = (offs + counts)[tile_expert]
    tile_rows = jnp.where(starts < ends[-1], jnp.clip(real_end - starts, 0, tm), 0).astype(jnp.int32)
    last_valid = jnp.maximum(jnp.sum((tile_rows != 0).astype(jnp.int32)) - 1, 0)
    tile_expert = jnp.where(tile_rows != 0, tile_expert, tile_expert[last_valid])
    return tile_expert, tile_rows, pos


def _combine_pos(pos, tm):
    m = pos.shape[0]
    return jnp.transpose(pos.reshape(m // tm, tm, 2), (0, 2, 1)).reshape(m // tm, 1, 2 * tm)


def _t5_bucket_np(rel):
    nb = N_BUCKETS // 2
    max_exact = nb // 2
    ret = np.where(rel > 0, nb, 0)
    n = np.abs(rel)
    nf = np.maximum(n, 1).astype(np.float32)
    large = max_exact + (np.log(nf / np.float32(max_exact)) / np.float32(math.log(MAX_DISTANCE / max_exact))
                         * np.float32(nb - max_exact)).astype(np.int32)
    large = np.minimum(large, nb - 1)
    return (ret + np.where(n < max_exact, n, large)).astype(np.int32)


def _bias_tensor(rel_table, n_q, valid):
    rel = np.arange(KEY_TILE, dtype=np.int32)[None, :] - WINDOW - np.arange(n_q, dtype=np.int32)[:, None]
    onehot = np.eye(N_BUCKETS, dtype=np.float32)[:, _t5_bucket_np(rel).reshape(-1)]
    bias = jnp.dot(rel_table.T[HEAD_PERM], jnp.asarray(onehot), precision=lax.Precision.HIGHEST)
    return jnp.where(jnp.asarray(valid)[None], bias.reshape(A_HEADS, n_q, KEY_TILE), NEG).astype(F32)


def _stack_heads(x):
    _, r, c = x.shape
    return jnp.transpose(x.reshape(2, 4, 2, r, c), (0, 2, 1, 3, 4)).reshape(2, 2, 4 * r, c)


def _with_sink_column(bias, sinks_perm):
    return _stack_heads(bias.at[:, :, KEY_TILE - 1].set(sinks_perm[:, None]))


def _prompt_valid():
    qc = np.arange(Q_TILE)[:, None] // CHUNK
    kc = np.arange(KEY_TILE)[None, :] // CHUNK
    return (kc >= qc) & (kc <= qc + WINDOW // CHUNK)


def _sample_valid(rows):
    return np.broadcast_to(np.arange(KEY_TILE)[None, :] < WINDOW + rows, (rows, KEY_TILE))


def _mixer_weights(l, g_mix, w_in_bf, b_gate, q_norm_g, k_norm_g, sinks, w_conv_b, w_conv_c, b_conv_c,
                   ln_c_g, ln_c_b, w_proj_a, w_proj_b, w_proj_c, w_out):
    wpa = w_proj_a[l].reshape(A_HEADS, A_HEAD_DIM, D_MODEL)[HEAD_PERM].reshape(A_WIDTH, D_MODEL)
    return dict(
        g_mix=g_mix[l], w_in=w_in_bf, layer=l, b_gate=b_gate[l],
        qg2=(jnp.tile(q_norm_g[l], 2) * (A_HEAD_DIM ** -0.5)).reshape(1, LANES), kg2=jnp.tile(k_norm_g[l], 2).reshape(1, LANES),
        sinks=sinks[l][HEAD_PERM],
        w_cb=w_conv_b[l], w_cc=w_conv_c[l], b_cc=b_conv_c[l], ln_g=ln_c_g[l], ln_b=ln_c_b[l],
        wpa=wpa.astype(BF16), wpb=w_proj_b[l].astype(BF16), wpc=w_proj_c[l].astype(BF16),
        wo=w_out[l].astype(BF16))


def _mixer_prompt(x, mw, bias, batch, seq):
    z = norm_matmul(x, mw["g_mix"], mw["w_in"], mw["layer"], tm=1024, tn=2048, out_dtype=BF16)
    oa, nk, nv = swa_prompt(z, batch, seq, mw["qg2"], mw["kg2"], bias, mw["sinks"])
    ob, oc, ncb, ncc = conv_prompt(z, batch, seq, mw["w_cb"], mw["w_cc"], mw["b_cc"], mw["ln_g"], mw["ln_b"])
    y = merge(oa, ob, oc, z, mw["b_gate"], mw["wpa"], mw["wpb"], mw["wpc"], mw["wo"], x)
    return y, (nk, nv, ncb[:, 8 - (B_CONV - 1):], ncc[:, HALO - (C_CONV - 1):])


def _mixer_sample(x, mw, bias, batch, rows, cache_k, cache_v, layer, st_b, st_c):
    z = norm_matmul(x, mw["g_mix"], mw["w_in"], mw["layer"], tm=x.shape[0], tn=1024, out_dtype=BF16)
    oa, nk, nv = swa_sample(z, batch, rows, cache_k, cache_v, layer, mw["qg2"], mw["kg2"],
                            _with_sink_column(bias, mw["sinks"]))
    stb = jnp.pad(st_b, ((0, 0), (HALO - (B_CONV - 1), 0), (0, 0)))
    stc = jnp.pad(st_c, ((0, 0), (HALO - (C_CONV - 1), 0), (0, 0)))
    ob, oc, ncb, ncc = conv_sample(z, batch, rows, stb, stc, mw["w_cb"], mw["w_cc"], mw["b_cc"],
                                   mw["ln_g"], mw["ln_b"])
    y = merge(oa, ob, oc, z, mw["b_gate"], mw["wpa"], mw["wpb"], mw["wpc"], mw["wo"], x)
    return y, (nk, nv, ncb[:, 8 - (B_CONV - 1):], ncc[:, HALO - (C_CONV - 1):])


def _channel_mixer(yp, ys, l, g_ffn, w_ffn_gu, w_ffn_d, w_router, b_router, w_moe_gu, w_moe_d):
    if l % 2 == 0:
        return (ffn(yp, g_ffn[l], w_ffn_gu[l // 2], w_ffn_d[l // 2]),
                ffn(ys, g_ffn[l], w_ffn_gu[l // 2], w_ffn_d[l // 2]))
    i = l // 2
    w_pad = jnp.pad(w_router[i], ((0, 0), (0, LANES - N_EXPERTS)))
    b_pad = jnp.pad(b_router[i], (0, LANES - N_EXPERTS)).reshape(1, LANES)
    wts_p, ids_p, rank_p, cnt_p = router(yp, g_ffn[l], w_pad, b_pad, jnp.zeros((1, LANES), F32))
    wts_s, ids_s, rank_s, cnt = router(ys, g_ffn[l], w_pad, b_pad, cnt_p)
    mp, ms = yp.shape[0], ys.shape[0]
    nt = (2 * (mp + ms)) // MOE_TM + N_EXPERTS
    ids = jnp.concatenate([ids_p[:, :2], ids_s[:, :2]], axis=0)
    rank = jnp.concatenate([rank_p[:, :2], rank_s[:, :2]], axis=0)
    tile_expert, tile_rows, pos = _dispatch_plan(cnt[0, :N_EXPERTS].astype(jnp.int32), ids, rank, MOE_TM, nt)
    tm_p, tm_s = 512, min(512, ms)
    pos_p, pos_s = _combine_pos(pos[:mp], tm_p), _combine_pos(pos[mp:], tm_s)
    xs = jnp.zeros((nt * MOE_TM, D_MODEL // 2), jnp.uint32)
    xs = moe_dispatch(pos_p, yp, g_ffn[l], xs, tm=tm_p)
    xs = moe_dispatch(pos_s, ys, g_ffn[l], xs, tm=tm_s)
    osort = moe_ffn(tile_expert, tile_rows, xs, w_moe_gu[i], w_moe_d[i])
    return (moe_combine(pos_p, yp, wts_p, osort, tm=tm_p), moe_combine(pos_s, ys, wts_s, osort, tm=tm_s))


def kernel(x_prompt, x_sample, mem_prompt, cache_mem_k, cache_mem_v, cache_swa_k, cache_swa_v, state_conv_b, state_conv_c, rel_table, g_mix, w_in, b_gate, q_norm_g, k_norm_g, sinks, w_conv_b, w_conv_c, b_conv_c, ln_c_g, ln_c_b, w_proj_a, w_proj_b, w_proj_c, w_out, g_xattn, g_mem, w_xq, w_xkv, xq_norm_g, xk_norm_g, w_xo, g_ffn, w_ffn_gu, w_ffn_d, w_router, b_router, w_moe_gu, w_moe_d):
    batch, seq, d = x_prompt.shape
    dec_batch, dec_seq, _ = x_sample.shape
    depth = g_mix.shape[0]
    yp = x_prompt.reshape(batch * seq, d)
    ys = x_sample.reshape(dec_batch * dec_seq, d)
    mem = mem_prompt.reshape(batch * N_MEM, d)
    first = np.arange(KEY_TILE)[None, :] >= Q_TILE
    bias_p = jnp.stack([_stack_heads(_bias_tensor(rel_table, Q_TILE, _prompt_valid() & first)),
                        _stack_heads(_bias_tensor(rel_table, Q_TILE, _prompt_valid()))])
    bias_s = _bias_tensor(rel_table, dec_seq, _sample_valid(dec_seq))
    outs = [[] for _ in range(10)]
    w_in_bf = w_in.astype(BF16)
    ck = cache_swa_k.reshape(depth, dec_batch, WINDOW, A_KV_WIDTH)
    cv = cache_swa_v.reshape(depth, dec_batch, WINDOW, A_KV_WIDTH)
    for l in range(depth):
        mw = _mixer_weights(l, g_mix, w_in_bf, b_gate, q_norm_g, k_norm_g, sinks, w_conv_b, w_conv_c,
                            b_conv_c, ln_c_g, ln_c_b, w_proj_a, w_proj_b, w_proj_c, w_out)
        wq = w_xq[l].astype(BF16)
        wo = w_xo[l].astype(BF16)
        ffn_args = (g_ffn, w_ffn_gu, w_ffn_d, w_router, b_router, w_moe_gu, w_moe_d)
        yp, (nk, nv, ncb, ncc) = _mixer_prompt(yp, mw, bias_p, batch, seq)
        mk, mv = mem_kv(mem, g_mem[l], w_xkv[l].astype(BF16), xk_norm_g[l])
        mk3 = mk.reshape(batch, N_MEM, X_WIDTH)
        mv3 = mv.reshape(batch, N_MEM, X_WIDTH)
        yp = xattn(yp, g_xattn[l], wq, xq_norm_g[l], mk3[None], mv3[None], wo, layer=0, nb=1, rpb=512,
                   tiles_per_mem=seq // 512)
        for lst, v in zip(outs[:6], (mk3.reshape(batch, N_MEM, X_HEADS, X_HEAD_DIM),
                                     mv3.reshape(batch, N_MEM, X_HEADS, X_HEAD_DIM),
                                     nk.reshape(batch, WINDOW, A_KV_HEADS, A_HEAD_DIM),
                                     nv.reshape(batch, WINDOW, A_KV_HEADS, A_HEAD_DIM), ncb, ncc)):
            lst.append(v)
        ys, (nk, nv, ncb, ncc) = _mixer_sample(ys, mw, bias_s, dec_batch, dec_seq, ck, cv, l,
                                               state_conv_b[l], state_conv_c[l])
        ys = xattn(ys, g_xattn[l], wq, xq_norm_g[l], cache_mem_k, cache_mem_v, wo, layer=l, nb=8, rpb=dec_seq,
                   tiles_per_mem=1)
        yp, ys = _channel_mixer(yp, ys, l, *ffn_args)
        for lst, v in zip(outs[6:], (nk.reshape(dec_batch, dec_seq, A_KV_HEADS, A_HEAD_DIM),
                                     nv.reshape(dec_batch, dec_seq, A_KV_HEADS, A_HEAD_DIM), ncb, ncc)):
            lst.append(v)
    return (yp.reshape(batch, seq, d), ys.reshape(dec_batch, dec_seq, d)) + tuple(jnp.stack(o) for o in outs)
```

```python
import functools
import math

import numpy as np
import jax
import jax.numpy as jnp
from jax import lax
from jax.experimental import pallas as pl
from jax.experimental.pallas import tpu as pltpu

F32 = jnp.float32
BF16 = jnp.bfloat16

D_MODEL = 2048
CHUNK = 64
A_HEADS = 16
A_KV_HEADS = 4
A_HEAD_DIM = 64
A_WIDTH = A_HEADS * A_HEAD_DIM
A_KV_WIDTH = A_KV_HEADS * A_HEAD_DIM
WINDOW = 128
N_BUCKETS = 32
MAX_DISTANCE = 128
B_WIDTH = 512
B_CONV = 3
C_WIDTH = 512
C_CONV = 31
N_MEM = 256
X_HEADS = 4
X_HEAD_DIM = 128
X_WIDTH = X_HEADS * X_HEAD_DIM
D_FF = 5632
N_EXPERTS = 8
EPS = 1e-6

LANES = 128
KEY_TILE = 256
Q_TILE = 128
HALO = 32
NEG = -1e30
VMEM_LIMIT = 56 * 1024 * 1024

COL_Q = 0
COL_K, COL_V = 4, 5
COL_GB, COL_GC, COL_HB, COL_GA, COL_GG = 3, 4, 5, 6, 7
COL_GATE0 = 2
IN_COLS = 4096 + 3 * D_MODEL

HEAD_PERM = np.array([8 * n + (p % 2) * 4 + p // 2 for n in range(2) for p in range(8)])


def _cparams(sem):
    return pltpu.CompilerParams(dimension_semantics=sem, vmem_limit_bytes=VMEM_LIMIT)


def _rms(x, g):
    ms = jnp.mean(x * x, axis=-1, keepdims=True)
    return x * lax.rsqrt(ms + EPS) * g


def _resident(shape):
    nd = len(shape)
    return pl.BlockSpec(shape, lambda *_: (0,) * nd, pipeline_mode=pl.Buffered(1))


def _permuted_q_columns(acc):
    lo_lane = lax.broadcasted_iota(jnp.int32, (1, LANES), 1) < A_HEAD_DIM
    nat = [acc[:, c * LANES:(c + 1) * LANES] for c in range(A_WIDTH // LANES)]
    swapped = [pltpu.roll(x, A_HEAD_DIM, 1) for x in nat]
    cols = []
    for n in range(2):
        for m in range(4):
            ca, cb = 4 * n + m // 2, 4 * n + 2 + m // 2
            cols.append(jnp.where(lo_lane, nat[ca], swapped[cb]) if m % 2 == 0
                        else jnp.where(lo_lane, swapped[ca], nat[cb]))
    return cols


def _norm_matmul_kernel(x_ref, g_ref, w_ref, o_ref, hn_ref):
    j = pl.program_id(1)

    @pl.when(j == 0)
    def _():
        hn_ref[...] = _rms(x_ref[...], g_ref[...]).astype(BF16)
        acc = jnp.dot(hn_ref[...], w_ref[...], preferred_element_type=F32)
        for c, col in enumerate(_permuted_q_columns(acc)):
            o_ref[:, c * LANES:(c + 1) * LANES] = col.astype(o_ref.dtype)
        if o_ref.shape[1] > A_WIDTH:
            o_ref[:, A_WIDTH:] = acc[:, A_WIDTH:].astype(o_ref.dtype)

    @pl.when(j > 0)
    def _():
        o_ref[...] = jnp.dot(hn_ref[...], w_ref[...], preferred_element_type=F32).astype(o_ref.dtype)


def norm_matmul(x, g, w, layer, *, tm, tn, out_dtype):
    m, k = x.shape
    n = w.shape[2]
    return pl.pallas_call(
        _norm_matmul_kernel,
        grid=(m // tm, n // tn),
        in_specs=[pl.BlockSpec((tm, k), lambda i, j: (i, 0)),
                  pl.BlockSpec((1, k), lambda i, j: (0, 0)),
                  pl.BlockSpec((None, k, tn), lambda i, j: (layer, 0, j))],
        out_specs=pl.BlockSpec((tm, tn), lambda i, j: (i, j)),
        out_shape=jax.ShapeDtypeStruct((m, n), out_dtype),
        scratch_shapes=[pltpu.VMEM((tm, k), BF16)],
        compiler_params=_cparams(("parallel", "arbitrary")),
        name="norm_matmul",
    )(x, g.reshape(1, k), w)


def _mem_kv_kernel(x_ref, g_ref, w_ref, kg_ref, k_ref, v_ref):
    hn = _rms(x_ref[...], g_ref[...]).astype(BF16)
    kv = jnp.dot(hn, w_ref[...], preferred_element_type=F32)
    for h in range(X_HEADS):
        sl = slice(h * X_HEAD_DIM, (h + 1) * X_HEAD_DIM)
        k_ref[:, sl] = _rms(kv[:, sl], kg_ref[...])
    v_ref[...] = kv[:, X_WIDTH:]


def mem_kv(mem, g, w_bf, kg):
    m, k = mem.shape
    tm = 256
    return pl.pallas_call(
        _mem_kv_kernel,
        grid=(m // tm,),
        in_specs=[pl.BlockSpec((tm, k), lambda i: (i, 0)),
                  pl.BlockSpec((1, k), lambda i: (0, 0)),
                  _resident((k, 2 * X_WIDTH)),
                  pl.BlockSpec((1, X_HEAD_DIM), lambda i: (0, 0))],
        out_specs=[pl.BlockSpec((tm, X_WIDTH), lambda i: (i, 0)),
                   pl.BlockSpec((tm, X_WIDTH), lambda i: (i, 0))],
        out_shape=[jax.ShapeDtypeStruct((m, X_WIDTH), F32)] * 2,
        compiler_params=_cparams(("parallel",)),
        name="mem_kv",
    )(mem, g.reshape(1, k), w_bf, kg.reshape(1, X_HEAD_DIM))


def _half_norm(x, g):
    x2 = x * x
    lo_lane = lax.broadcasted_iota(jnp.int32, (1, LANES), 1) < A_HEAD_DIM
    s_lo = jnp.sum(jnp.where(lo_lane, x2, 0.0), axis=-1, keepdims=True)
    s_hi = jnp.sum(jnp.where(lo_lane, 0.0, x2), axis=-1, keepdims=True)
    ms = jnp.where(lo_lane, s_lo, s_hi) * (1.0 / A_HEAD_DIM)
    return x * lax.rsqrt(ms + EPS) * g


def _swa_heads(q_ref, qg, k2, v2, bias_ref, sink_ref, o_ref, *, stack):
    lo_lane = lax.broadcasted_iota(jnp.int32, (1, LANES), 1) < A_HEAD_DIM
    rows = q_ref.shape[0]
    cols = A_WIDTH // LANES // 2
    for n in range(2):
        k_half = (jnp.where(lo_lane, k2[n], 0.0).astype(BF16), jnp.where(lo_lane, 0.0, k2[n]).astype(BF16))
        for g in range(0, cols, stack):
            qs = [_half_norm(q_ref[:, c * LANES:(c + 1) * LANES].astype(F32), qg)
                  for c in range(cols * n + g, cols * n + g + stack)]
            qn = (qs[0] if stack == 1 else jnp.concatenate(qs, axis=0)).astype(BF16)
            at = slice(g * rows, (g + stack) * rows)
            halves = []
            for half in range(2):
                s = lax.dot_general(qn, k_half[half], (((1,), (1,)), ((), ())), preferred_element_type=F32)
                s = s + bias_ref[n, half, at, :]
                if sink_ref is None:
                    m = jnp.max(s, axis=-1, keepdims=True)
                    p = jnp.exp(s - m)
                    den = jnp.sum(p, axis=-1, keepdims=True)
                else:
                    sink = sink_ref[2 * (cols * n + g) + half]
                    m = jnp.maximum(jnp.max(s, axis=-1, keepdims=True), sink)
                    p = jnp.exp(s - m)
                    den = jnp.sum(p, axis=-1, keepdims=True) + jnp.exp(sink - m)
                o = jnp.dot(p.astype(BF16), v2[n], preferred_element_type=F32)
                halves.append(o / den)
            o = jnp.where(lo_lane, halves[0], halves[1]).astype(o_ref.dtype)
            for i in range(stack):
                c = cols * n + g + i
                o_ref[:, c * LANES:(c + 1) * LANES] = o[i * rows:(i + 1) * rows]


def _swa_prompt_kernel(q_ref, kc_ref, kp_ref, vc_ref, vp_ref, qg_ref, kg_ref,
                       bias_ref, sink_ref, o_ref, nk_ref, nv_ref):
    k2, v2 = [], []
    for n in range(2):
        sl = slice(n * LANES, (n + 1) * LANES)
        kcat = jnp.concatenate([kp_ref[:, sl], kc_ref[:, sl]], axis=0).astype(F32)
        kn = _half_norm(kcat, kg_ref[...])
        nk_ref[0, :, sl] = kn[Q_TILE:]
        k2.append(kn)
        v2.append(jnp.concatenate([vp_ref[:, sl], vc_ref[:, sl]], axis=0))
    nv_ref[0] = vc_ref[...].astype(F32)
    _swa_heads(q_ref, qg_ref[...], k2, v2, bias_ref, sink_ref, o_ref, stack=1)


def swa_prompt(z, batch, seq, qg2, kg2, bias, sinks):
    nt = seq // Q_TILE
    row = lambda b, t: b * nt + t
    prev = lambda b, t: jnp.maximum(b * nt + t - 1, 0)
    return pl.pallas_call(
        _swa_prompt_kernel,
        grid=(batch, nt),
        in_specs=[pl.BlockSpec((Q_TILE, A_WIDTH), lambda b, t: (row(b, t), COL_Q)),
                  pl.BlockSpec((Q_TILE, A_KV_WIDTH), lambda b, t: (row(b, t), COL_K)),
                  pl.BlockSpec((Q_TILE, A_KV_WIDTH), lambda b, t: (prev(b, t), COL_K)),
                  pl.BlockSpec((Q_TILE, A_KV_WIDTH), lambda b, t: (row(b, t), COL_V)),
                  pl.BlockSpec((Q_TILE, A_KV_WIDTH), lambda b, t: (prev(b, t), COL_V)),
                  pl.BlockSpec((1, LANES), lambda b, t: (0, 0)),
                  pl.BlockSpec((1, LANES), lambda b, t: (0, 0)),
                  pl.BlockSpec((None, 2, 2, 4 * Q_TILE, KEY_TILE),
                               lambda b, t: (jnp.minimum(t, 1), 0, 0, 0, 0)),
                  pl.BlockSpec(memory_space=pltpu.SMEM)],
        out_specs=[pl.BlockSpec((Q_TILE, A_WIDTH), lambda b, t: (row(b, t), 0)),
                   pl.BlockSpec((1, WINDOW, A_KV_WIDTH), lambda b, t: (b, 0, 0)),
                   pl.BlockSpec((1, WINDOW, A_KV_WIDTH), lambda b, t: (b, 0, 0))],
        out_shape=[jax.ShapeDtypeStruct((batch * seq, A_WIDTH), BF16),
                   jax.ShapeDtypeStruct((batch, WINDOW, A_KV_WIDTH), F32),
                   jax.ShapeDtypeStruct((batch, WINDOW, A_KV_WIDTH), F32)],
        compiler_params=_cparams(("parallel", "arbitrary")),
        name="swa_prompt",
    )(z, z, z, z, z, qg2, kg2, bias, sinks)


def _swa_sample_kernel(q_ref, kn_ref, vn_ref, ck_ref, cv_ref, qg_ref, kg_ref,
                       bias_ref, o_ref, nk_ref, nv_ref):
    rows = q_ref.shape[0]
    pad = KEY_TILE - WINDOW - rows
    k2, v2 = [], []
    for n in range(2):
        sl = slice(n * LANES, (n + 1) * LANES)
        kn = _half_norm(kn_ref[:, sl].astype(F32), kg_ref[...])
        nk_ref[0, :, sl] = kn
        k2.append(jnp.concatenate([ck_ref[0, :, sl], kn, jnp.zeros((pad, LANES), F32)], axis=0))
        v2.append(jnp.concatenate([cv_ref[0, :, sl].astype(BF16), vn_ref[:, sl],
                                   jnp.zeros((pad, LANES), BF16)], axis=0))
    nv_ref[0] = vn_ref[...].astype(F32)
    _swa_heads(q_ref, qg_ref[...], k2, v2, bias_ref, None, o_ref, stack=4)


def swa_sample(z, batch, rows, cache_k, cache_v, layer, qg2, kg2, bias):
    return pl.pallas_call(
        _swa_sample_kernel,
        grid=(batch,),
        in_specs=[pl.BlockSpec((rows, A_WIDTH), lambda b: (b, COL_Q)),
                  pl.BlockSpec((rows, A_KV_WIDTH), lambda b: (b, COL_K)),
                  pl.BlockSpec((rows, A_KV_WIDTH), lambda b: (b, COL_V)),
                  pl.BlockSpec((None, 1, WINDOW, A_KV_WIDTH), lambda b: (layer, b, 0, 0)),
                  pl.BlockSpec((None, 1, WINDOW, A_KV_WIDTH), lambda b: (layer, b, 0, 0)),
                  pl.BlockSpec((1, LANES), lambda b: (0, 0)),
                  pl.BlockSpec((1, LANES), lambda b: (0, 0)),
                  _resident((2, 2, 4 * rows, KEY_TILE))],
        out_specs=[pl.BlockSpec((rows, A_WIDTH), lambda b: (b, 0)),
                   pl.BlockSpec((1, rows, A_KV_WIDTH), lambda b: (b, 0, 0)),
                   pl.BlockSpec((1, rows, A_KV_WIDTH), lambda b: (b, 0, 0))],
        out_shape=[jax.ShapeDtypeStruct((batch * rows, A_WIDTH), BF16),
                   jax.ShapeDtypeStruct((batch, rows, A_KV_WIDTH), F32),
                   jax.ShapeDtypeStruct((batch, rows, A_KV_WIDTH), F32)],
        compiler_params=_cparams(("parallel",)),
        name="swa_sample",
    )(z, z, z, cache_k, cache_v, qg2, kg2, bias)


def _conv_body(gb_ref, ub_main, uc_main, ub_halo, uc_halo, wb_ref, wc_ref, bc_ref, lg_ref, lb_ref,
               ob_ref, oc_ref, nb_ref, nc_ref, sb_ref, sc_ref, ph_ref, write_state):
    rows = ub_main.shape[0]
    sb_ref[0:HALO] = ub_halo
    sb_ref[HALO:HALO + rows] = ub_main
    sc_ref[0:HALO] = uc_halo
    sc_ref[HALO:HALO + rows] = uc_main
    span = rows + HALO
    sc_ref[span:span + 8] = jnp.zeros((8, C_WIDTH), F32)
    for b in range(1, 8):
        ph_ref[b - 1] = sc_ref[pl.ds(b, span), :]
    sub = min(rows, 32)
    for r0 in range(0, rows, sub):
        yb = jnp.zeros((sub, B_WIDTH), F32)
        for k in range(B_CONV):
            yb = yb + wb_ref[k:k + 1, :] * sb_ref[pl.ds(r0 + HALO - (B_CONV - 1) + k, sub), :]
        ob_ref[r0:r0 + sub, :] = (gb_ref[r0:r0 + sub, :].astype(F32) * yb).astype(ob_ref.dtype)
        yc = jnp.zeros((sub, C_WIDTH), F32)
        for k in range(C_CONV):
            shift, phase = divmod(HALO - (C_CONV - 1) + k, 8)
            at = pl.ds(r0 + 8 * shift, sub)
            taps = sc_ref[at, :] if phase == 0 else ph_ref[phase - 1, at, :]
            yc = yc + wc_ref[k:k + 1, :] * taps
        yc = yc + bc_ref[...]
        mu = jnp.mean(yc, axis=-1, keepdims=True)
        xc = yc - mu
        y = xc * lax.rsqrt(jnp.mean(xc * xc, axis=-1, keepdims=True) + EPS)
        y = y * lg_ref[...] + lb_ref[...]
        oc_ref[r0:r0 + sub, :] = (y * jax.nn.sigmoid(y)).astype(oc_ref.dtype)

    def _state():
        nb_ref[0] = sb_ref[rows + HALO - 8:rows + HALO]
        nc_ref[0] = sc_ref[rows:rows + HALO]

    write_state(_state)


def _conv_prompt_kernel(gb_ref, gc_ref, hb_ref, ga_ref, gg_ref, gch_ref, hbh_ref, gah_ref, ggh_ref,
                        wb_ref, wc_ref, bc_ref, lg_ref, lb_ref,
                        ob_ref, oc_ref, nb_ref, nc_ref, sb_ref, sc_ref, ph_ref):
    t = pl.program_id(1)
    hist = (t > 0).astype(F32)
    ub_main = gc_ref[...].astype(F32) * hb_ref[...].astype(F32)
    uc_main = ga_ref[...].astype(F32) * jax.nn.sigmoid(gg_ref[...].astype(F32))
    ub_halo = gch_ref[...].astype(F32) * hbh_ref[...].astype(F32) * hist
    uc_halo = gah_ref[...].astype(F32) * jax.nn.sigmoid(ggh_ref[...].astype(F32)) * hist
    last = pl.num_programs(1) - 1
    _conv_body(gb_ref, ub_main, uc_main, ub_halo, uc_halo, wb_ref, wc_ref, bc_ref, lg_ref, lb_ref,
               ob_ref, oc_ref, nb_ref, nc_ref, sb_ref, sc_ref, ph_ref,
               lambda f: pl.when(t == last)(f))


def conv_prompt(z, batch, seq, wb, wc, bc, lg, lb, *, tr=256):
    nt = seq // tr
    hp = tr // HALO
    main = lambda c: pl.BlockSpec((tr, B_WIDTH), lambda b, t: (b * nt + t, c))
    halo = lambda c: pl.BlockSpec((HALO, B_WIDTH), lambda b, t: (jnp.maximum((b * nt + t) * hp - 1, 0), c))
    vec = lambda r: pl.BlockSpec((r, B_WIDTH), lambda b, t: (0, 0))
    return pl.pallas_call(
        _conv_prompt_kernel,
        grid=(batch, nt),
        in_specs=[main(COL_GB), main(COL_GC), main(COL_HB), main(COL_GA), main(COL_GG),
                  halo(COL_GC), halo(COL_HB), halo(COL_GA), halo(COL_GG),
                  vec(B_CONV), vec(C_CONV), vec(1), vec(1), vec(1)],
        out_specs=[pl.BlockSpec((tr, B_WIDTH), lambda b, t: (b * nt + t, 0)),
                   pl.BlockSpec((tr, C_WIDTH), lambda b, t: (b * nt + t, 0)),
                   pl.BlockSpec((1, 8, B_WIDTH), lambda b, t: (b, 0, 0)),
                   pl.BlockSpec((1, HALO, C_WIDTH), lambda b, t: (b, 0, 0))],
        out_shape=[jax.ShapeDtypeStruct((batch * seq, B_WIDTH), BF16),
                   jax.ShapeDtypeStruct((batch * seq, C_WIDTH), BF16),
                   jax.ShapeDtypeStruct((batch, 8, B_WIDTH), F32),
                   jax.ShapeDtypeStruct((batch, HALO, C_WIDTH), F32)],
        scratch_shapes=[pltpu.VMEM((tr + HALO, B_WIDTH), F32), pltpu.VMEM((tr + HALO + 8, C_WIDTH), F32),
                        pltpu.VMEM((7, tr + HALO, C_WIDTH), F32)],
        compiler_params=_cparams(("parallel", "arbitrary")),
        name="conv_prompt",
    )(z, z, z, z, z, z, z, z, z, wb, wc, bc.reshape(1, -1), lg.reshape(1, -1), lb.reshape(1, -1))


def _conv_sample_kernel(gb_ref, gc_ref, hb_ref, ga_ref, gg_ref, stb_ref, stc_ref,
                        wb_ref, wc_ref, bc_ref, lg_ref, lb_ref,
                        ob_ref, oc_ref, nb_ref, nc_ref, sb_ref, sc_ref, ph_ref):
    ub_main = gc_ref[...].astype(F32) * hb_ref[...].astype(F32)
    uc_main = ga_ref[...].astype(F32) * jax.nn.sigmoid(gg_ref[...].astype(F32))
    _conv_body(gb_ref, ub_main, uc_main, stb_ref[0], stc_ref[0], wb_ref, wc_ref, bc_ref, lg_ref, lb_ref,
               ob_ref, oc_ref, nb_ref, nc_ref, sb_ref, sc_ref, ph_ref, lambda f: f())


def conv_sample(z, batch, rows, stb, stc, wb, wc, bc, lg, lb):
    main = lambda c: pl.BlockSpec((rows, B_WIDTH), lambda b: (b, c))
    vec = lambda r: pl.BlockSpec((r, B_WIDTH), lambda b: (0, 0))
    return pl.pallas_call(
        _conv_sample_kernel,
        grid=(batch,),
        in_specs=[main(COL_GB), main(COL_GC), main(COL_HB), main(COL_GA), main(COL_GG),
                  pl.BlockSpec((1, HALO, B_WIDTH), lambda b: (b, 0, 0)),
                  pl.BlockSpec((1, HALO, C_WIDTH), lambda b: (b, 0, 0)),
                  vec(B_CONV), vec(C_CONV), vec(1), vec(1), vec(1)],
        out_specs=[pl.BlockSpec((rows, B_WIDTH), lambda b: (b, 0)),
                   pl.BlockSpec((rows, C_WIDTH), lambda b: (b, 0)),
                   pl.BlockSpec((1, 8, B_WIDTH), lambda b: (b, 0, 0)),
                   pl.BlockSpec((1, HALO, C_WIDTH), lambda b: (b, 0, 0))],
        out_shape=[jax.ShapeDtypeStruct((batch * rows, B_WIDTH), BF16),
                   jax.ShapeDtypeStruct((batch * rows, C_WIDTH), BF16),
                   jax.ShapeDtypeStruct((batch, 8, B_WIDTH), F32),
                   jax.ShapeDtypeStruct((batch, HALO, C_WIDTH), F32)],
        scratch_shapes=[pltpu.VMEM((rows + HALO, B_WIDTH), F32), pltpu.VMEM((rows + HALO + 8, C_WIDTH), F32),
                        pltpu.VMEM((7, rows + HALO, C_WIDTH), F32)],
        compiler_params=_cparams(("parallel",)),
        name="conv_sample",
    )(z, z, z, z, z, stb, stc, wb, wc, bc.reshape(1, -1), lg.reshape(1, -1), lb.reshape(1, -1))


def _merge_kernel(oa_ref, ob_ref, oc_ref, l0_ref, l1_ref, l2_ref, bg_ref,
                  wpa_ref, wpb_ref, wpc_ref, wo_ref, x_ref, o_ref):
    def gated(l_ref, i, o_r, w_r):
        gate = jax.nn.sigmoid(l_ref[...].astype(F32) + bg_ref[i:i + 1, :])
        return gate * jnp.dot(o_r[...], w_r[...], preferred_element_type=F32)

    merged = gated(l0_ref, 0, oa_ref, wpa_ref)
    merged = merged + gated(l1_ref, 1, ob_ref, wpb_ref)
    merged = merged + gated(l2_ref, 2, oc_ref, wpc_ref)
    o_ref[...] = x_ref[...] + jnp.dot(merged.astype(BF16), wo_ref[...], preferred_element_type=F32)


def merge(oa, ob, oc, z, bg, wpa, wpb, wpc, wo, x, *, tm=512):
    m = x.shape[0]
    tm = min(tm, m)
    rows = lambda w: pl.BlockSpec((tm, w), lambda i: (i, 0))
    gate = lambda c: pl.BlockSpec((tm, D_MODEL), lambda i: (i, COL_GATE0 + c))
    return pl.pallas_call(
        _merge_kernel,
        grid=(m // tm,),
        in_specs=[rows(A_WIDTH), rows(B_WIDTH), rows(C_WIDTH), gate(0), gate(1), gate(2),
                  pl.BlockSpec((3, D_MODEL), lambda i: (0, 0)),
                  _resident((A_WIDTH, D_MODEL)), _resident((B_WIDTH, D_MODEL)),
                  _resident((C_WIDTH, D_MODEL)), _resident((D_MODEL, D_MODEL)),
                  rows(D_MODEL)],
        out_specs=rows(D_MODEL),
        out_shape=jax.ShapeDtypeStruct((m, D_MODEL), F32),
        compiler_params=_cparams(("parallel",)),
        name="merge",
    )(oa, ob, oc, z, z, z, bg.reshape(3, D_MODEL), wpa, wpb, wpc, wo, x)


def _xattn_kernel(y_ref, g_ref, wq_ref, qg_ref, mk_ref, mv_ref, wo_ref, o_ref, *, nb, rpb):
    y = y_ref[...]
    hn = _rms(y, g_ref[...]).astype(BF16)
    q = jnp.dot(hn, wq_ref[...], preferred_element_type=F32)
    heads = []
    for h in range(X_HEADS):
        sl = slice(h * X_HEAD_DIM, (h + 1) * X_HEAD_DIM)
        qh = _rms(q[:, sl], qg_ref[...])
        per_batch = []
        for b in range(nb):
            qb = qh[b * rpb:(b + 1) * rpb].astype(BF16)
            if len(mk_ref.shape) == 4:
                kh = mk_ref[b, :, h, :].astype(BF16)
                vh = mv_ref[b, :, h, :].astype(BF16)
            else:
                kh = mk_ref[b, :, sl].astype(BF16)
                vh = mv_ref[b, :, sl].astype(BF16)
            s = lax.dot_general(qb, kh, (((1,), (1,)), ((), ())),
                                preferred_element_type=F32) * (X_HEAD_DIM ** -0.5)
            m = jnp.max(s, axis=-1, keepdims=True)
            p = jnp.exp(s - m)
            den = jnp.sum(p, axis=-1, keepdims=True)
            per_batch.append(jnp.dot(p.astype(BF16), vh, preferred_element_type=F32) / den)
        heads.append(per_batch[0] if nb == 1 else jnp.concatenate(per_batch, axis=0))
    o = jnp.concatenate(heads, axis=1).astype(BF16)
    o_ref[...] = y + jnp.dot(o, wo_ref[...], preferred_element_type=F32)


def xattn(y, g, wq, qg, mk, mv, wo, *, layer, nb, rpb, tiles_per_mem):
    m = y.shape[0]
    tm = nb * rpb
    tail = (0,) * (mk.ndim - 2)
    mem_idx = ((lambda i: (layer, i // tiles_per_mem) + tail) if nb == 1
               else (lambda i: (layer, i) + tail))
    mem_block = (None, nb) + mk.shape[2:]
    return pl.pallas_call(
        functools.partial(_xattn_kernel, nb=nb, rpb=rpb),
        grid=(m // tm,),
        in_specs=[pl.BlockSpec((tm, D_MODEL), lambda i: (i, 0)),
                  pl.BlockSpec((1, D_MODEL), lambda i: (0, 0)),
                  _resident((D_MODEL, X_WIDTH)),
                  pl.BlockSpec((1, X_HEAD_DIM), lambda i: (0, 0)),
                  pl.BlockSpec(mem_block, mem_idx),
                  pl.BlockSpec(mem_block, mem_idx),
                  _resident((X_WIDTH, D_MODEL))],
        out_specs=pl.BlockSpec((tm, D_MODEL), lambda i: (i, 0)),
        out_shape=jax.ShapeDtypeStruct((m, D_MODEL), F32),
        compiler_params=_cparams(("parallel",)),
        name="xattn",
    )(y, g.reshape(1, -1), wq, qg.reshape(1, -1), mk, mv, wo)


def _ffn_kernel(x_ref, g_ref, wg_ref, wu_ref, wd_ref, o_ref, hn_ref):
    @pl.when(pl.program_id(1) == 0)
    def _():
        x = x_ref[...]
        hn_ref[...] = _rms(x, g_ref[...]).astype(BF16)
        o_ref[...] = x

    hn = hn_ref[...]
    g = jnp.dot(hn, wg_ref[...].astype(BF16), preferred_element_type=F32)
    u = jnp.dot(hn, wu_ref[...].astype(BF16), preferred_element_type=F32)
    a = (g * jax.nn.sigmoid(g) * u).astype(BF16)
    o_ref[...] += jnp.dot(a, wd_ref[...].astype(BF16), preferred_element_type=F32)


def ffn(x, g, w_gu, w_d, *, tm=1024, tf=512):
    m = x.shape[0]
    tm = min(tm, m)
    nf = D_FF // tf
    return pl.pallas_call(
        _ffn_kernel,
        grid=(m // tm, nf),
        in_specs=[pl.BlockSpec((tm, D_MODEL), lambda i, f: (i, 0), pipeline_mode=pl.Buffered(1)),
                  pl.BlockSpec((1, D_MODEL), lambda i, f: (0, 0)),
                  pl.BlockSpec((D_MODEL, tf), lambda i, f: (0, f)),
                  pl.BlockSpec((D_MODEL, tf), lambda i, f: (0, nf + f)),
                  pl.BlockSpec((tf, D_MODEL), lambda i, f: (f, 0))],
        out_specs=pl.BlockSpec((tm, D_MODEL), lambda i, f: (i, 0), pipeline_mode=pl.Buffered(1)),
        out_shape=jax.ShapeDtypeStruct((m, D_MODEL), F32),
        scratch_shapes=[pltpu.VMEM((tm, D_MODEL), BF16)],
        compiler_params=_cparams(("parallel", "arbitrary")),
        name="ffn",
    )(x, g.reshape(1, -1), w_gu, w_gu, w_d)


def _split3(x):
    hi = x.astype(BF16)
    lo = (x - hi.astype(F32)).astype(BF16)
    return hi, lo


def _router_kernel(x_ref, g_ref, w_ref, b_ref, tri_ref, base_ref, wts_ref, ids_ref, rank_ref, cnt_ref, run_ref):
    @pl.when(pl.program_id(0) == 0)
    def _():
        run_ref[...] = base_ref[...]

    hn = _rms(x_ref[...], g_ref[...])
    h_hi, h_lo = _split3(hn)
    w_hi, w_lo = _split3(w_ref[...])
    dot = functools.partial(jnp.dot, preferred_element_type=F32)
    logits = dot(h_hi, w_hi) + dot(h_hi, w_lo) + dot(h_lo, w_hi) + b_ref[...]
    lane = lax.broadcasted_iota(jnp.int32, logits.shape, 1).astype(F32)
    logits = jnp.where(lane < N_EXPERTS, logits, -jnp.inf)
    v1 = jnp.max(logits, axis=-1, keepdims=True)
    i1 = jnp.min(jnp.where(logits == v1, lane, float(LANES)), axis=-1, keepdims=True)
    rest = jnp.where(lane == i1, -jnp.inf, logits)
    v2 = jnp.max(rest, axis=-1, keepdims=True)
    i2 = jnp.min(jnp.where(rest == v2, lane, float(LANES)), axis=-1, keepdims=True)
    e2 = jnp.exp(v2 - v1)
    den = 1.0 + e2
    first, second = lane == 0.0, lane == 1.0
    wts_ref[...] = jnp.where(first, 1.0 / den, 0.0) + jnp.where(second, e2 / den, 0.0)
    ids_ref[...] = (jnp.where(first, i1, 0.0) + jnp.where(second, i2, 0.0)).astype(jnp.int32)
    hit1, hit2 = lane == i1, lane == i2
    hits = jnp.where(jnp.logical_or(hit1, hit2), 1.0, 0.0)
    before = dot(tri_ref[...], hits.astype(BF16)) + run_ref[...]
    r1 = jnp.sum(jnp.where(hit1, before, 0.0), axis=-1, keepdims=True)
    r2 = jnp.sum(jnp.where(hit2, before, 0.0), axis=-1, keepdims=True)
    rank_ref[...] = (jnp.where(first, r1, 0.0) + jnp.where(second, r2, 0.0)).astype(jnp.int32)
    run_ref[...] += jnp.sum(hits, axis=0, keepdims=True)
    cnt_ref[...] = run_ref[...]


def router(x, g, w_pad, b_pad, base, *, tm=1024):
    m = x.shape[0]
    tm = min(tm, m)
    tri = jnp.asarray(np.tril(np.ones((tm, tm), np.float32), -1), BF16)
    row = lambda w: pl.BlockSpec((tm, w), lambda i: (i, 0))
    fixed = lambda s: pl.BlockSpec(s, lambda i: (0, 0))
    return pl.pallas_call(
        _router_kernel,
        grid=(m // tm,),
        in_specs=[row(D_MODEL), fixed((1, D_MODEL)), fixed((D_MODEL, LANES)), fixed((1, LANES)),
                  fixed((tm, tm)), fixed((1, LANES))],
        out_specs=[row(LANES), row(LANES), row(LANES), fixed((1, LANES))],
        out_shape=[jax.ShapeDtypeStruct((m, LANES), F32),
                   jax.ShapeDtypeStruct((m, LANES), jnp.int32),
                   jax.ShapeDtypeStruct((m, LANES), jnp.int32),
                   jax.ShapeDtypeStruct((1, LANES), F32)],
        scratch_shapes=[pltpu.VMEM((1, LANES), F32)],
        compiler_params=_cparams(("arbitrary",)),
        name="router",
    )(x, g.reshape(1, -1), w_pad, b_pad, tri, base)


MOE_TM = 1024
MOE_TF = 512


def _row_copy(src, row, dst, r, sem):
    return pltpu.make_async_copy(src.at[pl.ds(row, 1)], dst.at[pl.ds(r, 1)], sem)


def _moe_dispatch_kernel(pos_ref, x_ref, g_ref, xs_in, xs_hbm, pk_ref, sem):
    del xs_in
    tm = x_ref.shape[0]
    half = D_MODEL // 2
    hn = _rms(x_ref[...], g_ref[...]).astype(BF16)
    lo_bits = lax.bitcast_convert_type(hn[:, :half].astype(F32), jnp.uint32)
    hi_bits = lax.bitcast_convert_type(hn[:, half:].astype(F32), jnp.uint32)
    pk_ref[...] = (lo_bits >> 16) | (hi_bits & jnp.uint32(0xFFFF0000))

    def issue(r, c):
        _row_copy(pk_ref, r, xs_hbm, pos_ref[0, 0, r], sem).start()
        _row_copy(pk_ref, r, xs_hbm, pos_ref[0, 0, tm + r], sem).start()
        return c

    lax.fori_loop(0, tm, issue, 0, unroll=8)

    def wait(r, c):
        _row_copy(pk_ref, r, xs_hbm, 0, sem).wait()
        _row_copy(pk_ref, r, xs_hbm, 0, sem).wait()
        return c

    lax.fori_loop(0, tm, wait, 0, unroll=8)


def moe_dispatch(pos, x, g, xs, *, tm):
    m = x.shape[0]
    return pl.pallas_call(
        _moe_dispatch_kernel,
        grid=(m // tm,),
        in_specs=[pl.BlockSpec((1, 1, 2 * tm), lambda i: (i, 0, 0), memory_space=pltpu.SMEM),
                  pl.BlockSpec((tm, D_MODEL), lambda i: (i, 0)),
                  pl.BlockSpec((1, D_MODEL), lambda i: (0, 0)),
                  pl.BlockSpec(memory_space=pl.ANY)],
        out_specs=pl.BlockSpec(memory_space=pl.ANY),
        out_shape=jax.ShapeDtypeStruct(xs.shape, xs.dtype),
        scratch_shapes=[pltpu.VMEM((tm, D_MODEL // 2), jnp.uint32), pltpu.SemaphoreType.DMA],
        input_output_aliases={3: 0},
        compiler_params=_cparams(("arbitrary",)),
        name="moe_dispatch",
    )(pos, x, g.reshape(1, -1), xs)


def _moe_ffn_kernel(te_ref, tv_ref, xs_ref, wg_ref, wu_ref, wd_ref, o_ref, hn_ref):
    t = pl.program_id(0)
    f = pl.program_id(1)
    rows = xs_ref.shape[0]
    half = D_MODEL // 2

    @pl.when(f == 0)
    def _():
        o_ref[...] = jnp.zeros_like(o_ref)
        xu = xs_ref[...]
        hn_ref[:, :half] = lax.bitcast_convert_type(xu << 16, F32).astype(BF16)
        hn_ref[:, half:] = lax.bitcast_convert_type(xu & jnp.uint32(0xFFFF0000), F32).astype(BF16)

    def swiglu(r):
        hn = hn_ref[0:r]
        g = jnp.dot(hn, wg_ref[...].astype(BF16), preferred_element_type=F32)
        u = jnp.dot(hn, wu_ref[...].astype(BF16), preferred_element_type=F32)
        a = (g * jax.nn.sigmoid(g) * u).astype(BF16)
        o_ref[0:r] += jnp.dot(a, wd_ref[...].astype(BF16), preferred_element_type=F32)

    nv = tv_ref[t]

    quarter = rows // 4
    for k in range(1, 5):
        @pl.when(jnp.logical_and(nv > (k - 1) * quarter, nv <= k * quarter))
        def _(k=k):
            swiglu(k * quarter)


def moe_ffn(tile_expert, tile_rows, xs, w_gu, w_d):
    tm = MOE_TM
    nt = xs.shape[0] // tm
    nf = D_FF // MOE_TF
    last = nf - 1
    col = lambda f, tv, t: jnp.where(tv[t] != 0, f, last)
    grid_spec = pltpu.PrefetchScalarGridSpec(
        num_scalar_prefetch=2,
        grid=(nt, nf),
        in_specs=[pl.BlockSpec((tm, D_MODEL // 2), lambda t, f, te, tv: (t, 0), pipeline_mode=pl.Buffered(1)),
                  pl.BlockSpec((None, D_MODEL, MOE_TF), lambda t, f, te, tv: (te[t], 0, col(f, tv, t))),
                  pl.BlockSpec((None, D_MODEL, MOE_TF), lambda t, f, te, tv: (te[t], 0, nf + col(f, tv, t))),
                  pl.BlockSpec((None, MOE_TF, D_MODEL), lambda t, f, te, tv: (te[t], col(f, tv, t), 0))],
        out_specs=pl.BlockSpec((tm, D_MODEL), lambda t, f, te, tv: (t, 0)),
        scratch_shapes=[pltpu.VMEM((tm, D_MODEL), BF16)],
    )
    return pl.pallas_call(
        _moe_ffn_kernel,
        grid_spec=grid_spec,
        out_shape=jax.ShapeDtypeStruct((nt * tm, D_MODEL), F32),
        compiler_params=_cparams(("arbitrary", "arbitrary")),
        name="moe_ffn",
    )(tile_expert, tile_rows, xs, w_gu, w_gu, w_d)


def _moe_combine_kernel(pos_ref, x_ref, w_ref, osort_hbm, o_ref, abuf, sem):
    tm = x_ref.shape[0]

    def issue(r, c):
        _row_copy(osort_hbm, pos_ref[0, 0, r], abuf, r, sem).start()
        return c

    lax.fori_loop(0, 2 * tm, issue, 0, unroll=8)

    def wait(r, c):
        _row_copy(osort_hbm, 0, abuf, r, sem).wait()
        return c

    lax.fori_loop(0, 2 * tm, wait, 0, unroll=8)
    w = w_ref[...]
    o_ref[...] = x_ref[...] + w[:, 0:1] * abuf[0:tm, :] + w[:, 1:2] * abuf[tm:2 * tm, :]


def moe_combine(pos, x, wts, osort, *, tm=256):
    m = x.shape[0]
    return pl.pallas_call(
        _moe_combine_kernel,
        grid=(m // tm,),
        in_specs=[pl.BlockSpec((1, 1, 2 * tm), lambda i: (i, 0, 0), memory_space=pltpu.SMEM),
                  pl.BlockSpec((tm, D_MODEL), lambda i: (i, 0)),
                  pl.BlockSpec((tm, LANES), lambda i: (i, 0)),
                  pl.BlockSpec(memory_space=pl.ANY)],
        out_specs=pl.BlockSpec((tm, D_MODEL), lambda i: (i, 0)),
        out_shape=jax.ShapeDtypeStruct((m, D_MODEL), F32),
        scratch_shapes=[pltpu.VMEM((2 * tm, D_MODEL), F32), pltpu.SemaphoreType.DMA],
        compiler_params=_cparams(("arbitrary",)),
        name="moe_combine",
    )(pos, x, wts, osort)


def _dispatch_plan(counts, ids, rank, tm, nt):
    padded = ((counts + tm - 1) // tm) * tm
    ends = jnp.cumsum(padded)
    offs = ends - padded
    off_of = jnp.zeros_like(ids)
    for e in range(N_EXPERTS):
        off_of = jnp.where(ids == e, offs[e], off_of)
    pos = off_of + rank
    starts = jnp.arange(nt, dtype=jnp.int32) * tm
    tile_expert = jnp.minimum(jnp.sum((starts[:, None] >= ends[None, :]).astype(jnp.int32), axis=1),
                              N_EXPERTS - 1)
    real_end = (offs + counts)[tile_expert]
    tile_rows = jnp.where(starts < ends[-1], jnp.clip(real_end - starts, 0, tm), 0).astype(jnp.int32)
    last_valid = jnp.maximum(jnp.sum((tile_rows != 0).astype(jnp.int32)) - 1, 0)
    tile_expert = jnp.where(tile_rows != 0, tile_expert, tile_expert[last_valid])
    return tile_expert, tile_rows, pos


def _combine_pos(pos, tm):
    m = pos.shape[0]
    return jnp.transpose(pos.reshape(m // tm, tm, 2), (0, 2, 1)).reshape(m // tm, 1, 2 * tm)


def _t5_bucket_np(rel):
    nb = N_BUCKETS // 2
    max_exact = nb // 2
    ret = np.where(rel > 0, nb, 0)
    n = np.abs(rel)
    nf = np.maximum(n, 1).astype(np.float32)
    large = max_exact + (np.log(nf / np.float32(max_exact)) / np.float32(math.log(MAX_DISTANCE / max_exact))
                         * np.float32(nb - max_exact)).astype(np.int32)
    large = np.minimum(large, nb - 1)
    return (ret + np.where(n < max_exact, n, large)).astype(np.int32)


def _bias_tensor(rel_table, n_q, valid):
    rel = np.arange(KEY_TILE, dtype=np.int32)[None, :] - WINDOW - np.arange(n_q, dtype=np.int32)[:, None]
    onehot = np.eye(N_BUCKETS, dtype=np.float32)[:, _t5_bucket_np(rel).reshape(-1)]
    bias = jnp.dot(rel_table.T[HEAD_PERM], jnp.asarray(onehot), precision=lax.Precision.HIGHEST)
    return jnp.where(jnp.asarray(valid)[None], bias.reshape(A_HEADS, n_q, KEY_TILE), NEG).astype(F32)


def _stack_heads(x):
    _, r, c = x.shape
    return jnp.transpose(x.reshape(2, 4, 2, r, c), (0, 2, 1, 3, 4)).reshape(2, 2, 4 * r, c)


def _with_sink_column(bias, sinks_perm):
    return _stack_heads(bias.at[:, :, KEY_TILE - 1].set(sinks_perm[:, None]))


def _prompt_valid():
    qc = np.arange(Q_TILE)[:, None] // CHUNK
    kc = np.arange(KEY_TILE)[None, :] // CHUNK
    return (kc >= qc) & (kc <= qc + WINDOW // CHUNK)


def _sample_valid(rows):
    return np.broadcast_to(np.arange(KEY_TILE)[None, :] < WINDOW + rows, (rows, KEY_TILE))


def _mixer_weights(l, g_mix, w_in_bf, b_gate, q_norm_g, k_norm_g, sinks, w_conv_b, w_conv_c, b_conv_c,
                   ln_c_g, ln_c_b, w_proj_a, w_proj_b, w_proj_c, w_out):
    wpa = w_proj_a[l].reshape(A_HEADS, A_HEAD_DIM, D_MODEL)[HEAD_PERM].reshape(A_WIDTH, D_MODEL)
    return dict(
        g_mix=g_mix[l], w_in=w_in_bf, layer=l, b_gate=b_gate[l],
        qg2=(jnp.tile(q_norm_g[l], 2) * (A_HEAD_DIM ** -0.5)).reshape(1, LANES), kg2=jnp.tile(k_norm_g[l], 2).reshape(1, LANES),
        sinks=sinks[l][HEAD_PERM],
        w_cb=w_conv_b[l], w_cc=w_conv_c[l], b_cc=b_conv_c[l], ln_g=ln_c_g[l], ln_b=ln_c_b[l],
        wpa=wpa.astype(BF16), wpb=w_proj_b[l].astype(BF16), wpc=w_proj_c[l].astype(BF16),
        wo=w_out[l].astype(BF16))


def _mixer_prompt(x, mw, bias, batch, seq):
    z = norm_matmul(x, mw["g_mix"], mw["w_in"], mw["layer"], tm=1024, tn=2048, out_dtype=BF16)
    oa, nk, nv = swa_prompt(z, batch, seq, mw["qg2"], mw["kg2"], bias, mw["sinks"])
    ob, oc, ncb, ncc = conv_prompt(z, batch, seq, mw["w_cb"], mw["w_cc"], mw["b_cc"], mw["ln_g"], mw["ln_b"])
    y = merge(oa, ob, oc, z, mw["b_gate"], mw["wpa"], mw["wpb"], mw["wpc"], mw["wo"], x)
    return y, (nk, nv, ncb[:, 8 - (B_CONV - 1):], ncc[:, HALO - (C_CONV - 1):])


def _mixer_sample(x, mw, bias, batch, rows, cache_k, cache_v, layer, st_b, st_c):
    z = norm_matmul(x, mw["g_mix"], mw["w_in"], mw["layer"], tm=x.shape[0], tn=2048, out_dtype=BF16)
    oa, nk, nv = swa_sample(z, batch, rows, cache_k, cache_v, layer, mw["qg2"], mw["kg2"],
                            _with_sink_column(bias, mw["sinks"]))
    stb = jnp.pad(st_b, ((0, 0), (HALO - (B_CONV - 1), 0), (0, 0)))
    stc = jnp.pad(st_c, ((0, 0), (HALO - (C_CONV - 1), 0), (0, 0)))
    ob, oc, ncb, ncc = conv_sample(z, batch, rows, stb, stc, mw["w_cb"], mw["w_cc"], mw["b_cc"],
                                   mw["ln_g"], mw["ln_b"])
    y = merge(oa, ob, oc, z, mw["b_gate"], mw["wpa"], mw["wpb"], mw["wpc"], mw["wo"], x)
    return y, (nk, nv, ncb[:, 8 - (B_CONV - 1):], ncc[:, HALO - (C_CONV - 1):])


def _channel_mixer(yp, ys, l, g_ffn, w_ffn_gu, w_ffn_d, w_router, b_router, w_moe_gu, w_moe_d):
    if l % 2 == 0:
        return (ffn(yp, g_ffn[l], w_ffn_gu[l // 2], w_ffn_d[l // 2]),
                ffn(ys, g_ffn[l], w_ffn_gu[l // 2], w_ffn_d[l // 2]))
    i = l // 2
    w_pad = jnp.pad(w_router[i], ((0, 0), (0, LANES - N_EXPERTS)))
    b_pad = jnp.pad(b_router[i], (0, LANES - N_EXPERTS)).reshape(1, LANES)
    wts_p, ids_p, rank_p, cnt_p = router(yp, g_ffn[l], w_pad, b_pad, jnp.zeros((1, LANES), F32))
    wts_s, ids_s, rank_s, cnt = router(ys, g_ffn[l], w_pad, b_pad, cnt_p)
    mp, ms = yp.shape[0], ys.shape[0]
    nt = (2 * (mp + ms)) // MOE_TM + N_EXPERTS
    ids = jnp.concatenate([ids_p[:, :2], ids_s[:, :2]], axis=0)
    rank = jnp.concatenate([rank_p[:, :2], rank_s[:, :2]], axis=0)
    tile_expert, tile_rows, pos = _dispatch_plan(cnt[0, :N_EXPERTS].astype(jnp.int32), ids, rank, MOE_TM, nt)
    tm_p, tm_s = 512, min(512, ms)
    pos_p, pos_s = _combine_pos(pos[:mp], tm_p), _combine_pos(pos[mp:], tm_s)
    xs = jnp.zeros((nt * MOE_TM, D_MODEL // 2), jnp.uint32)
    xs = moe_dispatch(pos_p, yp, g_ffn[l], xs, tm=tm_p)
    xs = moe_dispatch(pos_s, ys, g_ffn[l], xs, tm=tm_s)
    osort = moe_ffn(tile_expert, tile_rows, xs, w_moe_gu[i], w_moe_d[i])
    return (moe_combine(pos_p, yp, wts_p, osort, tm=tm_p), moe_combine(pos_s, ys, wts_s, osort, tm=tm_s))


def kernel(x_prompt, x_sample, mem_prompt, cache_mem_k, cache_mem_v, cache_swa_k, cache_swa_v, state_conv_b, state_conv_c, rel_table, g_mix, w_in, b_gate, q_norm_g, k_norm_g, sinks, w_conv_b, w_conv_c, b_conv_c, ln_c_g, ln_c_b, w_proj_a, w_proj_b, w_proj_c, w_out, g_xattn, g_mem, w_xq, w_xkv, xq_norm_g, xk_norm_g, w_xo, g_ffn, w_ffn_gu, w_ffn_d, w_router, b_router, w_moe_gu, w_moe_d):
    batch, seq, d = x_prompt.shape
    dec_batch, dec_seq, _ = x_sample.shape
    depth = g_mix.shape[0]
    yp = x_prompt.reshape(batch * seq, d)
    ys = x_sample.reshape(dec_batch * dec_seq, d)
    mem = mem_prompt.reshape(batch * N_MEM, d)
    first = np.arange(KEY_TILE)[None, :] >= Q_TILE
    bias_p = jnp.stack([_stack_heads(_bias_tensor(rel_table, Q_TILE, _prompt_valid() & first)),
                        _stack_heads(_bias_tensor(rel_table, Q_TILE, _prompt_valid()))])
    bias_s = _bias_tensor(rel_table, dec_seq, _sample_valid(dec_seq))
    outs = [[] for _ in range(10)]
    w_in_bf = w_in.astype(BF16)
    ck = cache_swa_k.reshape(depth, dec_batch, WINDOW, A_KV_WIDTH)
    cv = cache_swa_v.reshape(depth, dec_batch, WINDOW, A_KV_WIDTH)
    for l in range(depth):
        mw = _mixer_weights(l, g_mix, w_in_bf, b_gate, q_norm_g, k_norm_g, sinks, w_conv_b, w_conv_c,
                            b_conv_c, ln_c_g, ln_c_b, w_proj_a, w_proj_b, w_proj_c, w_out)
        wq = w_xq[l].astype(BF16)
        wo = w_xo[l].astype(BF16)
        ffn_args = (g_ffn, w_ffn_gu, w_ffn_d, w_router, b_router, w_moe_gu, w_moe_d)
        yp, (nk, nv, ncb, ncc) = _mixer_prompt(yp, mw, bias_p, batch, seq)
        mk, mv = mem_kv(mem, g_mem[l], w_xkv[l].astype(BF16), xk_norm_g[l])
        mk3 = mk.reshape(batch, N_MEM, X_WIDTH)
        mv3 = mv.reshape(batch, N_MEM, X_WIDTH)
        rpb = min(1024, seq)
        yp = xattn(yp, g_xattn[l], wq, xq_norm_g[l], mk3[None], mv3[None], wo, layer=0, nb=1, rpb=rpb,
                   tiles_per_mem=seq // rpb)
        for lst, v in zip(outs[:6], (mk3.reshape(batch, N_MEM, X_HEADS, X_HEAD_DIM),
                                     mv3.reshape(batch, N_MEM, X_HEADS, X_HEAD_DIM),
                                     nk.reshape(batch, WINDOW, A_KV_HEADS, A_HEAD_DIM),
                                     nv.reshape(batch, WINDOW, A_KV_HEADS, A_HEAD_DIM), ncb, ncc)):
            lst.append(v)
        ys, (nk, nv, ncb, ncc) = _mixer_sample(ys, mw, bias_s, dec_batch, dec_seq, ck, cv, l,
                                               state_conv_b[l], state_conv_c[l])
        ys = xattn(ys, g_xattn[l], wq, xq_norm_g[l], cache_mem_k, cache_mem_v, wo, layer=l, nb=8, rpb=dec_seq,
                   tiles_per_mem=1)
        yp, ys = _channel_mixer(yp, ys, l, *ffn_args)
        for lst, v in zip(outs[6:], (nk.reshape(dec_batch, dec_seq, A_KV_HEADS, A_HEAD_DIM),
                                     nv.reshape(dec_batch, dec_seq, A_KV_HEADS, A_HEAD_DIM), ncb, ncc)):
            lst.append(v)
    return (yp.reshape(batch, seq, d), ys.reshape(dec_batch, dec_seq, d)) + tuple(jnp.stack(o) for o in outs)
```

```python
import functools
import math

import numpy as np
import jax
import jax.numpy as jnp
from jax import lax
from jax.experimental import pallas as pl
from jax.experimental.pallas import tpu as pltpu

F32 = jnp.float32
BF16 = jnp.bfloat16

D_MODEL = 2048
CHUNK = 64
A_HEADS = 16
A_KV_HEADS = 4
A_HEAD_DIM = 64
A_WIDTH = A_HEADS * A_HEAD_DIM
A_KV_WIDTH = A_KV_HEADS * A_HEAD_DIM
WINDOW = 128
N_BUCKETS = 32
MAX_DISTANCE = 128
B_WIDTH = 512
B_CONV = 3
C_WIDTH = 512
C_CONV = 31
N_MEM = 256
X_HEADS = 4
X_HEAD_DIM = 128
X_WIDTH = X_HEADS * X_HEAD_DIM
D_FF = 5632
N_EXPERTS = 8
EPS = 1e-6

LANES = 128
KEY_TILE = 256
Q_TILE = 128
HALO = 32
NEG = -1e30
VMEM_LIMIT = 56 * 1024 * 1024
FFN_VMEM_LIMIT = 60 * 1024 * 1024

COL_Q = 0
COL_K, COL_V = 4, 5
COL_GB, COL_GC, COL_HB, COL_GA, COL_GG = 3, 4, 5, 6, 7
COL_GATE0 = 2
IN_COLS = 4096 + 3 * D_MODEL

HEAD_PERM = np.array([8 * n + (p % 2) * 4 + p // 2 for n in range(2) for p in range(8)])


def _cparams(sem):
    return pltpu.CompilerParams(dimension_semantics=sem, vmem_limit_bytes=VMEM_LIMIT)


def _rms(x, g):
    ms = jnp.mean(x * x, axis=-1, keepdims=True)
    return x * lax.rsqrt(ms + EPS) * g


def _resident(shape):
    nd = len(shape)
    return pl.BlockSpec(shape, lambda *_: (0,) * nd, pipeline_mode=pl.Buffered(1))


def _permuted_q_columns(acc):
    lo_lane = lax.broadcasted_iota(jnp.int32, (1, LANES), 1) < A_HEAD_DIM
    nat = [acc[:, c * LANES:(c + 1) * LANES] for c in range(A_WIDTH // LANES)]
    swapped = [pltpu.roll(x, A_HEAD_DIM, 1) for x in nat]
    cols = []
    for n in range(2):
        for m in range(4):
            ca, cb = 4 * n + m // 2, 4 * n + 2 + m // 2
            cols.append(jnp.where(lo_lane, nat[ca], swapped[cb]) if m % 2 == 0
                        else jnp.where(lo_lane, swapped[ca], nat[cb]))
    return cols


def _norm_matmul_kernel(x_ref, g_ref, w_ref, o_ref, hn_ref):
    j = pl.program_id(1)

    @pl.when(j == 0)
    def _():
        hn_ref[...] = _rms(x_ref[...], g_ref[...]).astype(BF16)
        acc = jnp.dot(hn_ref[...], w_ref[...], preferred_element_type=F32)
        for c, col in enumerate(_permuted_q_columns(acc)):
            o_ref[:, c * LANES:(c + 1) * LANES] = col.astype(o_ref.dtype)
        if o_ref.shape[1] > A_WIDTH:
            o_ref[:, A_WIDTH:] = acc[:, A_WIDTH:].astype(o_ref.dtype)

    @pl.when(j > 0)
    def _():
        o_ref[...] = jnp.dot(hn_ref[...], w_ref[...], preferred_element_type=F32).astype(o_ref.dtype)


def norm_matmul(x, g, w, layer, *, tm, tn, out_dtype):
    m, k = x.shape
    n = w.shape[2]
    return pl.pallas_call(
        _norm_matmul_kernel,
        grid=(m // tm, n // tn),
        in_specs=[pl.BlockSpec((tm, k), lambda i, j: (i, 0)),
                  pl.BlockSpec((1, k), lambda i, j: (0, 0)),
                  pl.BlockSpec((None, k, tn), lambda i, j: (layer, 0, j))],
        out_specs=pl.BlockSpec((tm, tn), lambda i, j: (i, j)),
        out_shape=jax.ShapeDtypeStruct((m, n), out_dtype),
        scratch_shapes=[pltpu.VMEM((tm, k), BF16)],
        compiler_params=_cparams(("parallel", "arbitrary")),
        name="norm_matmul",
    )(x, g.reshape(1, k), w)


def _mem_kv_kernel(x_ref, g_ref, w_ref, kg_ref, k_ref, v_ref):
    hn = _rms(x_ref[...], g_ref[...]).astype(BF16)
    kv = jnp.dot(hn, w_ref[...], preferred_element_type=F32)
    for h in range(X_HEADS):
        sl = slice(h * X_HEAD_DIM, (h + 1) * X_HEAD_DIM)
        k_ref[:, sl] = _rms(kv[:, sl], kg_ref[...])
    v_ref[...] = kv[:, X_WIDTH:]


def mem_kv(mem, g, w_bf, kg):
    m, k = mem.shape
    tm = 512
    return pl.pallas_call(
        _mem_kv_kernel,
        grid=(m // tm,),
        in_specs=[pl.BlockSpec((tm, k), lambda i: (i, 0)),
                  pl.BlockSpec((1, k), lambda i: (0, 0)),
                  _resident((k, 2 * X_WIDTH)),
                  pl.BlockSpec((1, X_HEAD_DIM), lambda i: (0, 0))],
        out_specs=[pl.BlockSpec((tm, X_WIDTH), lambda i: (i, 0)),
                   pl.BlockSpec((tm, X_WIDTH), lambda i: (i, 0))],
        out_shape=[jax.ShapeDtypeStruct((m, X_WIDTH), F32)] * 2,
        compiler_params=_cparams(("parallel",)),
        name="mem_kv",
    )(mem, g.reshape(1, k), w_bf, kg.reshape(1, X_HEAD_DIM))


def _half_norm(x, g):
    x2 = x * x
    lo_lane = lax.broadcasted_iota(jnp.int32, (1, LANES), 1) < A_HEAD_DIM
    s_lo = jnp.sum(jnp.where(lo_lane, x2, 0.0), axis=-1, keepdims=True)
    s_hi = jnp.sum(jnp.where(lo_lane, 0.0, x2), axis=-1, keepdims=True)
    ms = jnp.where(lo_lane, s_lo, s_hi) * (1.0 / A_HEAD_DIM)
    return x * lax.rsqrt(ms + EPS) * g


def _swa_heads(q_ref, qg, k2, v2, bias_ref, sink_ref, o_ref, *, stack):
    lo_lane = lax.broadcasted_iota(jnp.int32, (1, LANES), 1) < A_HEAD_DIM
    rows = q_ref.shape[0]
    cols = A_WIDTH // LANES // 2
    for n in range(2):
        k_half = (jnp.where(lo_lane, k2[n], 0.0).astype(BF16), jnp.where(lo_lane, 0.0, k2[n]).astype(BF16))
        for g in range(0, cols, stack):
            qs = [_half_norm(q_ref[:, c * LANES:(c + 1) * LANES].astype(F32), qg)
                  for c in range(cols * n + g, cols * n + g + stack)]
            qn = (qs[0] if stack == 1 else jnp.concatenate(qs, axis=0)).astype(BF16)
            at = slice(g * rows, (g + stack) * rows)
            halves = []
            for half in range(2):
                s = lax.dot_general(qn, k_half[half], (((1,), (1,)), ((), ())), preferred_element_type=F32)
                s = s + bias_ref[n, half, at, :]
                if sink_ref is None:
                    m = jnp.max(s, axis=-1, keepdims=True)
                    p = jnp.exp(s - m)
                    den = jnp.sum(p, axis=-1, keepdims=True)
                else:
                    sink = sink_ref[2 * (cols * n + g) + half]
                    m = jnp.maximum(jnp.max(s, axis=-1, keepdims=True), sink)
                    p = jnp.exp(s - m)
                    den = jnp.sum(p, axis=-1, keepdims=True) + jnp.exp(sink - m)
                o = jnp.dot(p.astype(BF16), v2[n], preferred_element_type=F32)
                halves.append(o / den)
            o = jnp.where(lo_lane, halves[0], halves[1]).astype(o_ref.dtype)
            for i in range(stack):
                c = cols * n + g + i
                o_ref[:, c * LANES:(c + 1) * LANES] = o[i * rows:(i + 1) * rows]


def _swa_prompt_kernel(q_ref, kc_ref, kp_ref, vc_ref, vp_ref, qg_ref, kg_ref,
                       bias_ref, sink_ref, o_ref, nk_ref, nv_ref):
    k2, v2 = [], []
    for n in range(2):
        sl = slice(n * LANES, (n + 1) * LANES)
        kcat = jnp.concatenate([kp_ref[:, sl], kc_ref[:, sl]], axis=0).astype(F32)
        kn = _half_norm(kcat, kg_ref[...])
        nk_ref[0, :, sl] = kn[Q_TILE:]
        k2.append(kn)
        v2.append(jnp.concatenate([vp_ref[:, sl], vc_ref[:, sl]], axis=0))
    nv_ref[0] = vc_ref[...].astype(F32)
    _swa_heads(q_ref, qg_ref[...], k2, v2, bias_ref, sink_ref, o_ref, stack=1)


def swa_prompt(z, batch, seq, qg2, kg2, bias, sinks):
    nt = seq // Q_TILE
    row = lambda b, t: b * nt + t
    prev = lambda b, t: jnp.maximum(b * nt + t - 1, 0)
    return pl.pallas_call(
        _swa_prompt_kernel,
        grid=(batch, nt),
        in_specs=[pl.BlockSpec((Q_TILE, A_WIDTH), lambda b, t: (row(b, t), COL_Q)),
                  pl.BlockSpec((Q_TILE, A_KV_WIDTH), lambda b, t: (row(b, t), COL_K)),
                  pl.BlockSpec((Q_TILE, A_KV_WIDTH), lambda b, t: (prev(b, t), COL_K)),
                  pl.BlockSpec((Q_TILE, A_KV_WIDTH), lambda b, t: (row(b, t), COL_V)),
                  pl.BlockSpec((Q_TILE, A_KV_WIDTH), lambda b, t: (prev(b, t), COL_V)),
                  pl.BlockSpec((1, LANES), lambda b, t: (0, 0)),
                  pl.BlockSpec((1, LANES), lambda b, t: (0, 0)),
                  pl.BlockSpec((None, 2, 2, 4 * Q_TILE, KEY_TILE),
                               lambda b, t: (jnp.minimum(t, 1), 0, 0, 0, 0)),
                  pl.BlockSpec(memory_space=pltpu.SMEM)],
        out_specs=[pl.BlockSpec((Q_TILE, A_WIDTH), lambda b, t: (row(b, t), 0)),
                   pl.BlockSpec((1, WINDOW, A_KV_WIDTH), lambda b, t: (b, 0, 0)),
                   pl.BlockSpec((1, WINDOW, A_KV_WIDTH), lambda b, t: (b, 0, 0))],
        out_shape=[jax.ShapeDtypeStruct((batch * seq, A_WIDTH), BF16),
                   jax.ShapeDtypeStruct((batch, WINDOW, A_KV_WIDTH), F32),
                   jax.ShapeDtypeStruct((batch, WINDOW, A_KV_WIDTH), F32)],
        compiler_params=_cparams(("parallel", "arbitrary")),
        name="swa_prompt",
    )(z, z, z, z, z, qg2, kg2, bias, sinks)


def _swa_sample_kernel(q_ref, kn_ref, vn_ref, ck_ref, cv_ref, qg_ref, kg_ref,
                       bias_ref, o_ref, nk_ref, nv_ref):
    rows = q_ref.shape[0]
    pad = KEY_TILE - WINDOW - rows
    k2, v2 = [], []
    for n in range(2):
        sl = slice(n * LANES, (n + 1) * LANES)
        kn = _half_norm(kn_ref[:, sl].astype(F32), kg_ref[...])
        nk_ref[0, :, sl] = kn
        k2.append(jnp.concatenate([ck_ref[0, :, sl], kn, jnp.zeros((pad, LANES), F32)], axis=0))
        v2.append(jnp.concatenate([cv_ref[0, :, sl].astype(BF16), vn_ref[:, sl],
                                   jnp.zeros((pad, LANES), BF16)], axis=0))
    nv_ref[0] = vn_ref[...].astype(F32)
    _swa_heads(q_ref, qg_ref[...], k2, v2, bias_ref, None, o_ref, stack=4)


def swa_sample(z, batch, rows, cache_k, cache_v, layer, qg2, kg2, bias):
    return pl.pallas_call(
        _swa_sample_kernel,
        grid=(batch,),
        in_specs=[pl.BlockSpec((rows, A_WIDTH), lambda b: (b, COL_Q)),
                  pl.BlockSpec((rows, A_KV_WIDTH), lambda b: (b, COL_K)),
                  pl.BlockSpec((rows, A_KV_WIDTH), lambda b: (b, COL_V)),
                  pl.BlockSpec((None, 1, WINDOW, A_KV_WIDTH), lambda b: (layer, b, 0, 0)),
                  pl.BlockSpec((None, 1, WINDOW, A_KV_WIDTH), lambda b: (layer, b, 0, 0)),
                  pl.BlockSpec((1, LANES), lambda b: (0, 0)),
                  pl.BlockSpec((1, LANES), lambda b: (0, 0)),
                  _resident((2, 2, 4 * rows, KEY_TILE))],
        out_specs=[pl.BlockSpec((rows, A_WIDTH), lambda b: (b, 0)),
                   pl.BlockSpec((1, rows, A_KV_WIDTH), lambda b: (b, 0, 0)),
                   pl.BlockSpec((1, rows, A_KV_WIDTH), lambda b: (b, 0, 0))],
        out_shape=[jax.ShapeDtypeStruct((batch * rows, A_WIDTH), BF16),
                   jax.ShapeDtypeStruct((batch, rows, A_KV_WIDTH), F32),
                   jax.ShapeDtypeStruct((batch, rows, A_KV_WIDTH), F32)],
        compiler_params=_cparams(("parallel",)),
        name="swa_sample",
    )(z, z, z, cache_k, cache_v, qg2, kg2, bias)


def _conv_body(gb_ref, ub_main, uc_main, ub_halo, uc_halo, wb_ref, wc_ref, bc_ref, lg_ref, lb_ref,
               ob_ref, oc_ref, nb_ref, nc_ref, sb_ref, sc_ref, ph_ref, write_state):
    rows = ub_main.shape[0]
    sb_ref[0:HALO] = ub_halo
    sb_ref[HALO:HALO + rows] = ub_main
    sc_ref[0:HALO] = uc_halo
    sc_ref[HALO:HALO + rows] = uc_main
    span = rows + HALO
    sc_ref[span:span + 8] = jnp.zeros((8, C_WIDTH), F32)
    for b in range(1, 8):
        ph_ref[b - 1] = sc_ref[pl.ds(b, span), :]
    sub = min(rows, 32)
    for r0 in range(0, rows, sub):
        yb = jnp.zeros((sub, B_WIDTH), F32)
        for k in range(B_CONV):
            yb = yb + wb_ref[k:k + 1, :] * sb_ref[pl.ds(r0 + HALO - (B_CONV - 1) + k, sub), :]
        ob_ref[r0:r0 + sub, :] = (gb_ref[r0:r0 + sub, :].astype(F32) * yb).astype(ob_ref.dtype)
        yc = jnp.zeros((sub, C_WIDTH), F32)
        for k in range(C_CONV):
            shift, phase = divmod(HALO - (C_CONV - 1) + k, 8)
            at = pl.ds(r0 + 8 * shift, sub)
            taps = sc_ref[at, :] if phase == 0 else ph_ref[phase - 1, at, :]
            yc = yc + wc_ref[k:k + 1, :] * taps
        yc = yc + bc_ref[...]
        mu = jnp.mean(yc, axis=-1, keepdims=True)
        xc = yc - mu
        y = xc * lax.rsqrt(jnp.mean(xc * xc, axis=-1, keepdims=True) + EPS)
        y = y * lg_ref[...] + lb_ref[...]
        oc_ref[r0:r0 + sub, :] = (y * jax.nn.sigmoid(y)).astype(oc_ref.dtype)

    def _state():
        nb_ref[0] = sb_ref[rows + HALO - 8:rows + HALO]
        nc_ref[0] = sc_ref[rows:rows + HALO]

    write_state(_state)


def _conv_prompt_kernel(gb_ref, gc_ref, hb_ref, ga_ref, gg_ref, gch_ref, hbh_ref, gah_ref, ggh_ref,
                        wb_ref, wc_ref, bc_ref, lg_ref, lb_ref,
                        ob_ref, oc_ref, nb_ref, nc_ref, sb_ref, sc_ref, ph_ref):
    t = pl.program_id(1)
    hist = (t > 0).astype(F32)
    ub_main = gc_ref[...].astype(F32) * hb_ref[...].astype(F32)
    uc_main = ga_ref[...].astype(F32) * jax.nn.sigmoid(gg_ref[...].astype(F32))
    ub_halo = gch_ref[...].astype(F32) * hbh_ref[...].astype(F32) * hist
    uc_halo = gah_ref[...].astype(F32) * jax.nn.sigmoid(ggh_ref[...].astype(F32)) * hist
    last = pl.num_programs(1) - 1
    _conv_body(gb_ref, ub_main, uc_main, ub_halo, uc_halo, wb_ref, wc_ref, bc_ref, lg_ref, lb_ref,
               ob_ref, oc_ref, nb_ref, nc_ref, sb_ref, sc_ref, ph_ref,
               lambda f: pl.when(t == last)(f))


def conv_prompt(z, batch, seq, wb, wc, bc, lg, lb, *, tr=512):
    nt = seq // tr
    hp = tr // HALO
    main = lambda c: pl.BlockSpec((tr, B_WIDTH), lambda b, t: (b * nt + t, c))
    halo = lambda c: pl.BlockSpec((HALO, B_WIDTH), lambda b, t: (jnp.maximum((b * nt + t) * hp - 1, 0), c))
    vec = lambda r: pl.BlockSpec((r, B_WIDTH), lambda b, t: (0, 0))
    return pl.pallas_call(
        _conv_prompt_kernel,
        grid=(batch, nt),
        in_specs=[main(COL_GB), main(COL_GC), main(COL_HB), main(COL_GA), main(COL_GG),
                  halo(COL_GC), halo(COL_HB), halo(COL_GA), halo(COL_GG),
                  vec(B_CONV), vec(C_CONV), vec(1), vec(1), vec(1)],
        out_specs=[pl.BlockSpec((tr, B_WIDTH), lambda b, t: (b * nt + t, 0)),
                   pl.BlockSpec((tr, C_WIDTH), lambda b, t: (b * nt + t, 0)),
                   pl.BlockSpec((1, 8, B_WIDTH), lambda b, t: (b, 0, 0)),
                   pl.BlockSpec((1, HALO, C_WIDTH), lambda b, t: (b, 0, 0))],
        out_shape=[jax.ShapeDtypeStruct((batch * seq, B_WIDTH), BF16),
                   jax.ShapeDtypeStruct((batch * seq, C_WIDTH), BF16),
                   jax.ShapeDtypeStruct((batch, 8, B_WIDTH), F32),
                   jax.ShapeDtypeStruct((batch, HALO, C_WIDTH), F32)],
        scratch_shapes=[pltpu.VMEM((tr + HALO, B_WIDTH), F32), pltpu.VMEM((tr + HALO + 8, C_WIDTH), F32),
                        pltpu.VMEM((7, tr + HALO, C_WIDTH), F32)],
        compiler_params=_cparams(("parallel", "arbitrary")),
        name="conv_prompt",
    )(z, z, z, z, z, z, z, z, z, wb, wc, bc.reshape(1, -1), lg.reshape(1, -1), lb.reshape(1, -1))


def _conv_sample_kernel(gb_ref, gc_ref, hb_ref, ga_ref, gg_ref, stb_ref, stc_ref,
                        wb_ref, wc_ref, bc_ref, lg_ref, lb_ref,
                        ob_ref, oc_ref, nb_ref, nc_ref, sb_ref, sc_ref, ph_ref):
    ub_main = gc_ref[...].astype(F32) * hb_ref[...].astype(F32)
    uc_main = ga_ref[...].astype(F32) * jax.nn.sigmoid(gg_ref[...].astype(F32))
    _conv_body(gb_ref, ub_main, uc_main, stb_ref[0], stc_ref[0], wb_ref, wc_ref, bc_ref, lg_ref, lb_ref,
               ob_ref, oc_ref, nb_ref, nc_ref, sb_ref, sc_ref, ph_ref, lambda f: f())


def conv_sample(z, batch, rows, stb, stc, wb, wc, bc, lg, lb):
    main = lambda c: pl.BlockSpec((rows, B_WIDTH), lambda b: (b, c))
    vec = lambda r: pl.BlockSpec((r, B_WIDTH), lambda b: (0, 0))
    return pl.pallas_call(
        _conv_sample_kernel,
        grid=(batch,),
        in_specs=[main(COL_GB), main(COL_GC), main(COL_HB), main(COL_GA), main(COL_GG),
                  pl.BlockSpec((1, HALO, B_WIDTH), lambda b: (b, 0, 0)),
                  pl.BlockSpec((1, HALO, C_WIDTH), lambda b: (b, 0, 0)),
                  vec(B_CONV), vec(C_CONV), vec(1), vec(1), vec(1)],
        out_specs=[pl.BlockSpec((rows, B_WIDTH), lambda b: (b, 0)),
                   pl.BlockSpec((rows, C_WIDTH), lambda b: (b, 0)),
                   pl.BlockSpec((1, 8, B_WIDTH), lambda b: (b, 0, 0)),
                   pl.BlockSpec((1, HALO, C_WIDTH), lambda b: (b, 0, 0))],
        out_shape=[jax.ShapeDtypeStruct((batch * rows, B_WIDTH), BF16),
                   jax.ShapeDtypeStruct((batch * rows, C_WIDTH), BF16),
                   jax.ShapeDtypeStruct((batch, 8, B_WIDTH), F32),
                   jax.ShapeDtypeStruct((batch, HALO, C_WIDTH), F32)],
        scratch_shapes=[pltpu.VMEM((rows + HALO, B_WIDTH), F32), pltpu.VMEM((rows + HALO + 8, C_WIDTH), F32),
                        pltpu.VMEM((7, rows + HALO, C_WIDTH), F32)],
        compiler_params=_cparams(("parallel",)),
        name="conv_sample",
    )(z, z, z, z, z, stb, stc, wb, wc, bc.reshape(1, -1), lg.reshape(1, -1), lb.reshape(1, -1))


def _merge_kernel(oa_ref, ob_ref, oc_ref, l0_ref, l1_ref, l2_ref, bg_ref,
                  wpa_ref, wpb_ref, wpc_ref, wo_ref, x_ref, o_ref):
    def gated(l_ref, i, o_r, w_r):
        gate = jax.nn.sigmoid(l_ref[...].astype(F32) + bg_ref[i:i + 1, :])
        return gate * jnp.dot(o_r[...], w_r[...], preferred_element_type=F32)

    merged = gated(l0_ref, 0, oa_ref, wpa_ref)
    merged = merged + gated(l1_ref, 1, ob_ref, wpb_ref)
    merged = merged + gated(l2_ref, 2, oc_ref, wpc_ref)
    o_ref[...] = x_ref[...] + jnp.dot(merged.astype(BF16), wo_ref[...], preferred_element_type=F32)


def merge(oa, ob, oc, z, bg, wpa, wpb, wpc, wo, x, *, tm=512):
    m = x.shape[0]
    tm = min(tm, m)
    rows = lambda w: pl.BlockSpec((tm, w), lambda i: (i, 0))
    gate = lambda c: pl.BlockSpec((tm, D_MODEL), lambda i: (i, COL_GATE0 + c))
    return pl.pallas_call(
        _merge_kernel,
        grid=(m // tm,),
        in_specs=[rows(A_WIDTH), rows(B_WIDTH), rows(C_WIDTH), gate(0), gate(1), gate(2),
                  pl.BlockSpec((3, D_MODEL), lambda i: (0, 0)),
                  _resident((A_WIDTH, D_MODEL)), _resident((B_WIDTH, D_MODEL)),
                  _resident((C_WIDTH, D_MODEL)), _resident((D_MODEL, D_MODEL)),
                  rows(D_MODEL)],
        out_specs=rows(D_MODEL),
        out_shape=jax.ShapeDtypeStruct((m, D_MODEL), F32),
        compiler_params=_cparams(("parallel",)),
        name="merge",
    )(oa, ob, oc, z, z, z, bg.reshape(3, D_MODEL), wpa, wpb, wpc, wo, x)


def _xattn_kernel(y_ref, g_ref, wq_ref, qg_ref, mk_ref, mv_ref, wo_ref, o_ref, *, nb, rpb):
    y = y_ref[...]
    hn = _rms(y, g_ref[...]).astype(BF16)
    q = jnp.dot(hn, wq_ref[...], preferred_element_type=F32)
    heads = []
    for h in range(X_HEADS):
        sl = slice(h * X_HEAD_DIM, (h + 1) * X_HEAD_DIM)
        qh = _rms(q[:, sl], qg_ref[...])
        per_batch = []
        for b in range(nb):
            qb = qh[b * rpb:(b + 1) * rpb].astype(BF16)
            if len(mk_ref.shape) == 4:
                kh = mk_ref[b, :, h, :].astype(BF16)
                vh = mv_ref[b, :, h, :].astype(BF16)
            else:
                kh = mk_ref[b, :, sl].astype(BF16)
                vh = mv_ref[b, :, sl].astype(BF16)
            s = lax.dot_general(qb, kh, (((1,), (1,)), ((), ())),
                                preferred_element_type=F32) * (X_HEAD_DIM ** -0.5)
            m = jnp.max(s, axis=-1, keepdims=True)
            p = jnp.exp(s - m)
            den = jnp.sum(p, axis=-1, keepdims=True)
            per_batch.append(jnp.dot(p.astype(BF16), vh, preferred_element_type=F32) / den)
        heads.append(per_batch[0] if nb == 1 else jnp.concatenate(per_batch, axis=0))
    o = jnp.concatenate(heads, axis=1).astype(BF16)
    o_ref[...] = y + jnp.dot(o, wo_ref[...], preferred_element_type=F32)


def xattn(y, g, wq, qg, mk, mv, wo, *, layer, nb, rpb, tiles_per_mem):
    m = y.shape[0]
    tm = nb * rpb
    tail = (0,) * (mk.ndim - 2)
    mem_idx = ((lambda i: (layer, i // tiles_per_mem) + tail) if nb == 1
               else (lambda i: (layer, i) + tail))
    mem_block = (None, nb) + mk.shape[2:]
    return pl.pallas_call(
        functools.partial(_xattn_kernel, nb=nb, rpb=rpb),
        grid=(m // tm,),
        in_specs=[pl.BlockSpec((tm, D_MODEL), lambda i: (i, 0)),
                  pl.BlockSpec((1, D_MODEL), lambda i: (0, 0)),
                  _resident((D_MODEL, X_WIDTH)),
                  pl.BlockSpec((1, X_HEAD_DIM), lambda i: (0, 0)),
                  pl.BlockSpec(mem_block, mem_idx),
                  pl.BlockSpec(mem_block, mem_idx),
                  _resident((X_WIDTH, D_MODEL))],
        out_specs=pl.BlockSpec((tm, D_MODEL), lambda i: (i, 0)),
        out_shape=jax.ShapeDtypeStruct((m, D_MODEL), F32),
        compiler_params=_cparams(("parallel",)),
        name="xattn",
    )(y, g.reshape(1, -1), wq, qg.reshape(1, -1), mk, mv, wo)


def _ffn_kernel(x_ref, g_ref, wg_ref, wu_ref, wd_ref, o_ref, hn_ref):
    @pl.when(pl.program_id(1) == 0)
    def _():
        x = x_ref[...]
        hn_ref[...] = _rms(x, g_ref[...]).astype(BF16)
        o_ref[...] = x

    hn = hn_ref[...]
    g = jnp.dot(hn, wg_ref[...].astype(BF16), preferred_element_type=F32)
    u = jnp.dot(hn, wu_ref[...].astype(BF16), preferred_element_type=F32)
    a = (g * jax.nn.sigmoid(g) * u).astype(BF16)
    o_ref[...] += jnp.dot(a, wd_ref[...].astype(BF16), preferred_element_type=F32)


def ffn(x, g, w_gu, w_d, *, tm=2048, tf=256):
    m = x.shape[0]
    if m < tm:
        tm, tf = m, 512
    nf = D_FF // tf
    return pl.pallas_call(
        _ffn_kernel,
        grid=(m // tm, nf),
        in_specs=[pl.BlockSpec((tm, D_MODEL), lambda i, f: (i, 0), pipeline_mode=pl.Buffered(1)),
                  pl.BlockSpec((1, D_MODEL), lambda i, f: (0, 0)),
                  pl.BlockSpec((D_MODEL, tf), lambda i, f: (0, f)),
                  pl.BlockSpec((D_MODEL, tf), lambda i, f: (0, nf + f)),
                  pl.BlockSpec((tf, D_MODEL), lambda i, f: (f, 0))],
        out_specs=pl.BlockSpec((tm, D_MODEL), lambda i, f: (i, 0), pipeline_mode=pl.Buffered(1)),
        out_shape=jax.ShapeDtypeStruct((m, D_MODEL), F32),
        scratch_shapes=[pltpu.VMEM((tm, D_MODEL), BF16)],
        compiler_params=pltpu.CompilerParams(dimension_semantics=("parallel", "arbitrary"),
                                             vmem_limit_bytes=FFN_VMEM_LIMIT),
        name="ffn",
    )(x, g.reshape(1, -1), w_gu, w_gu, w_d)


def _split3(x):
    hi = x.astype(BF16)
    lo = (x - hi.astype(F32)).astype(BF16)
    return hi, lo


def _router_kernel(x_ref, g_ref, w_ref, b_ref, tri_ref, base_ref, wts_ref, ids_ref, rank_ref, cnt_ref, run_ref):
    @pl.when(pl.program_id(0) == 0)
    def _():
        run_ref[...] = base_ref[...]

    hn = _rms(x_ref[...], g_ref[...])
    h_hi, h_lo = _split3(hn)
    w_hi, w_lo = _split3(w_ref[...])
    dot = functools.partial(jnp.dot, preferred_element_type=F32)
    logits = dot(h_hi, w_hi) + dot(h_hi, w_lo) + dot(h_lo, w_hi) + b_ref[...]
    lane = lax.broadcasted_iota(jnp.int32, logits.shape, 1).astype(F32)
    logits = jnp.where(lane < N_EXPERTS, logits, -jnp.inf)
    v1 = jnp.max(logits, axis=-1, keepdims=True)
    i1 = jnp.min(jnp.where(logits == v1, lane, float(LANES)), axis=-1, keepdims=True)
    rest = jnp.where(lane == i1, -jnp.inf, logits)
    v2 = jnp.max(rest, axis=-1, keepdims=True)
    i2 = jnp.min(jnp.where(rest == v2, lane, float(LANES)), axis=-1, keepdims=True)
    e2 = jnp.exp(v2 - v1)
    den = 1.0 + e2
    first, second = lane == 0.0, lane == 1.0
    wts_ref[...] = jnp.where(first, 1.0 / den, 0.0) + jnp.where(second, e2 / den, 0.0)
    ids_ref[...] = (jnp.where(first, i1, 0.0) + jnp.where(second, i2, 0.0)).astype(jnp.int32)
    hit1, hit2 = lane == i1, lane == i2
    hits = jnp.where(jnp.logical_or(hit1, hit2), 1.0, 0.0)
    before = dot(tri_ref[...], hits.astype(BF16)) + run_ref[...]
    r1 = jnp.sum(jnp.where(hit1, before, 0.0), axis=-1, keepdims=True)
    r2 = jnp.sum(jnp.where(hit2, before, 0.0), axis=-1, keepdims=True)
    rank_ref[...] = (jnp.where(first, r1, 0.0) + jnp.where(second, r2, 0.0)).astype(jnp.int32)
    run_ref[...] += jnp.sum(hits, axis=0, keepdims=True)
    cnt_ref[...] = run_ref[...]


def router(x, g, w_pad, b_pad, base, *, tm=1024):
    m = x.shape[0]
    tm = min(tm, m)
    tri = jnp.asarray(np.tril(np.ones((tm, tm), np.float32), -1), BF16)
    row = lambda w: pl.BlockSpec((tm, w), lambda i: (i, 0))
    fixed = lambda s: pl.BlockSpec(s, lambda i: (0, 0))
    return pl.pallas_call(
        _router_kernel,
        grid=(m // tm,),
        in_specs=[row(D_MODEL), fixed((1, D_MODEL)), fixed((D_MODEL, LANES)), fixed((1, LANES)),
                  fixed((tm, tm)), fixed((1, LANES))],
        out_specs=[row(LANES), row(LANES), row(LANES), fixed((1, LANES))],
        out_shape=[jax.ShapeDtypeStruct((m, LANES), F32),
                   jax.ShapeDtypeStruct((m, LANES), jnp.int32),
                   jax.ShapeDtypeStruct((m, LANES), jnp.int32),
                   jax.ShapeDtypeStruct((1, LANES), F32)],
        scratch_shapes=[pltpu.VMEM((1, LANES), F32)],
        compiler_params=_cparams(("arbitrary",)),
        name="router",
    )(x, g.reshape(1, -1), w_pad, b_pad, tri, base)


MOE_TM = 1024
MOE_TF = 512


def _row_copy(src, row, dst, r, sem):
    return pltpu.make_async_copy(src.at[pl.ds(row, 1)], dst.at[pl.ds(r, 1)], sem)


def _moe_dispatch_kernel(pos_ref, x_ref, g_ref, xs_in, xs_hbm, pk_ref, sem):
    del xs_in
    tm = x_ref.shape[0]
    half = D_MODEL // 2
    hn = _rms(x_ref[...], g_ref[...]).astype(BF16)
    lo_bits = lax.bitcast_convert_type(hn[:, :half].astype(F32), jnp.uint32)
    hi_bits = lax.bitcast_convert_type(hn[:, half:].astype(F32), jnp.uint32)
    pk_ref[...] = (lo_bits >> 16) | (hi_bits & jnp.uint32(0xFFFF0000))

    def issue(r, c):
        _row_copy(pk_ref, r, xs_hbm, pos_ref[0, 0, r], sem).start()
        _row_copy(pk_ref, r, xs_hbm, pos_ref[0, 0, tm + r], sem).start()
        return c

    lax.fori_loop(0, tm, issue, 0, unroll=8)

    def wait(r, c):
        _row_copy(pk_ref, r, xs_hbm, 0, sem).wait()
        _row_copy(pk_ref, r, xs_hbm, 0, sem).wait()
        return c

    lax.fori_loop(0, tm, wait, 0, unroll=8)


def moe_dispatch(pos, x, g, xs, *, tm):
    m = x.shape[0]
    return pl.pallas_call(
        _moe_dispatch_kernel,
        grid=(m // tm,),
        in_specs=[pl.BlockSpec((1, 1, 2 * tm), lambda i: (i, 0, 0), memory_space=pltpu.SMEM),
                  pl.BlockSpec((tm, D_MODEL), lambda i: (i, 0)),
                  pl.BlockSpec((1, D_MODEL), lambda i: (0, 0)),
                  pl.BlockSpec(memory_space=pl.ANY)],
        out_specs=pl.BlockSpec(memory_space=pl.ANY),
        out_shape=jax.ShapeDtypeStruct(xs.shape, xs.dtype),
        scratch_shapes=[pltpu.VMEM((tm, D_MODEL // 2), jnp.uint32), pltpu.SemaphoreType.DMA],
        input_output_aliases={3: 0},
        compiler_params=_cparams(("arbitrary",)),
        name="moe_dispatch",
    )(pos, x, g.reshape(1, -1), xs)


def _moe_ffn_kernel(te_ref, tv_ref, xs_ref, wg_ref, wu_ref, wd_ref, o_ref, hn_ref):
    t = pl.program_id(0)
    f = pl.program_id(1)
    rows = xs_ref.shape[0]
    half = D_MODEL // 2

    @pl.when(f == 0)
    def _():
        o_ref[...] = jnp.zeros_like(o_ref)
        xu = xs_ref[...]
        hn_ref[:, :half] = lax.bitcast_convert_type(xu << 16, F32).astype(BF16)
        hn_ref[:, half:] = lax.bitcast_convert_type(xu & jnp.uint32(0xFFFF0000), F32).astype(BF16)

    def swiglu(r):
        hn = hn_ref[0:r]
        g = jnp.dot(hn, wg_ref[...].astype(BF16), preferred_element_type=F32)
        u = jnp.dot(hn, wu_ref[...].astype(BF16), preferred_element_type=F32)
        a = (g * jax.nn.sigmoid(g) * u).astype(BF16)
        o_ref[0:r] += jnp.dot(a, wd_ref[...].astype(BF16), preferred_element_type=F32)

    nv = tv_ref[t]

    quarter = rows // 4
    for k in range(1, 5):
        @pl.when(jnp.logical_and(nv > (k - 1) * quarter, nv <= k * quarter))
        def _(k=k):
            swiglu(k * quarter)


def moe_ffn(tile_expert, tile_rows, xs, w_gu, w_d):
    tm = MOE_TM
    nt = xs.shape[0] // tm
    nf = D_FF // MOE_TF
    last = nf - 1
    col = lambda f, tv, t: jnp.where(tv[t] != 0, f, last)
    grid_spec = pltpu.PrefetchScalarGridSpec(
        num_scalar_prefetch=2,
        grid=(nt, nf),
        in_specs=[pl.BlockSpec((tm, D_MODEL // 2), lambda t, f, te, tv: (t, 0), pipeline_mode=pl.Buffered(1)),
                  pl.BlockSpec((None, D_MODEL, MOE_TF), lambda t, f, te, tv: (te[t], 0, col(f, tv, t))),
                  pl.BlockSpec((None, D_MODEL, MOE_TF), lambda t, f, te, tv: (te[t], 0, nf + col(f, tv, t))),
                  pl.BlockSpec((None, MOE_TF, D_MODEL), lambda t, f, te, tv: (te[t], col(f, tv, t), 0))],
        out_specs=pl.BlockSpec((tm, D_MODEL), lambda t, f, te, tv: (t, 0)),
        scratch_shapes=[pltpu.VMEM((tm, D_MODEL), BF16)],
    )
    return pl.pallas_call(
        _moe_ffn_kernel,
        grid_spec=grid_spec,
        out_shape=jax.ShapeDtypeStruct((nt * tm, D_MODEL), F32),
        compiler_params=_cparams(("arbitrary", "arbitrary")),
        name="moe_ffn",
    )(tile_expert, tile_rows, xs, w_gu, w_gu, w_d)


def _moe_combine_kernel(pos_ref, x_ref, w_ref, osort_hbm, o_ref, abuf, sem):
    tm = x_ref.shape[0]

    def issue(r, c):
        _row_copy(osort_hbm, pos_ref[0, 0, r], abuf, r, sem).start()
        return c

    lax.fori_loop(0, 2 * tm, issue, 0, unroll=8)

    def wait(r, c):
        _row_copy(osort_hbm, 0, abuf, r, sem).wait()
        return c

    lax.fori_loop(0, 2 * tm, wait, 0, unroll=8)
    w = w_ref[...]
    o_ref[...] = x_ref[...] + w[:, 0:1] * abuf[0:tm, :] + w[:, 1:2] * abuf[tm:2 * tm, :]


def moe_combine(pos, x, wts, osort, *, tm=256):
    m = x.shape[0]
    return pl.pallas_call(
        _moe_combine_kernel,
        grid=(m // tm,),
        in_specs=[pl.BlockSpec((1, 1, 2 * tm), lambda i: (i, 0, 0), memory_space=pltpu.SMEM),
                  pl.BlockSpec((tm, D_MODEL), lambda i: (i, 0)),
                  pl.BlockSpec((tm, LANES), lambda i: (i, 0)),
                  pl.BlockSpec(memory_space=pl.ANY)],
        out_specs=pl.BlockSpec((tm, D_MODEL), lambda i: (i, 0)),
        out_shape=jax.ShapeDtypeStruct((m, D_MODEL), F32),
        scratch_shapes=[pltpu.VMEM((2 * tm, D_MODEL), F32), pltpu.SemaphoreType.DMA],
        compiler_params=_cparams(("arbitrary",)),
        name="moe_combine",
    )(pos, x, wts, osort)


def _dispatch_plan(counts, ids, rank, tm, nt):
    padded = ((counts + tm - 1) // tm) * tm
    ends = jnp.cumsum(padded)
    offs = ends - padded
    off_of = jnp.zeros_like(ids)
    for e in range(N_EXPERTS):
        off_of = jnp.where(ids == e, offs[e], off_of)
    pos = off_of + rank
    starts = jnp.arange(nt, dtype=jnp.int32) * tm
    tile_expert = jnp.minimum(jnp.sum((starts[:, None] >= ends[None, :]).astype(jnp.int32), axis=1),
                              N_EXPERTS - 1)
    real_end = (offs + counts)[tile_expert]
    tile_rows = jnp.where(starts < ends[-1], jnp.clip(real_end - starts, 0, tm), 0).astype(jnp.int32)
    last_valid = jnp.maximum(jnp.sum((tile_rows != 0).astype(jnp.int32)) - 1, 0)
    tile_expert = jnp.where(tile_rows != 0, tile_expert, tile_expert[last_valid])
    return tile_expert, tile_rows, pos


def _combine_pos(pos, tm):
    m = pos.shape[0]
    return jnp.transpose(pos.reshape(m // tm, tm, 2), (0, 2, 1)).reshape(m // tm, 1, 2 * tm)


def _t5_bucket_np(rel):
    nb = N_BUCKETS // 2
    max_exact = nb // 2
    ret = np.where(rel > 0, nb, 0)
    n = np.abs(rel)
    nf = np.maximum(n, 1).astype(np.float32)
    large = max_exact + (np.log(nf / np.float32(max_exact)) / np.float32(math.log(MAX_DISTANCE / max_exact))
                         * np.float32(nb - max_exact)).astype(np.int32)
    large = np.minimum(large, nb - 1)
    return (ret + np.where(n < max_exact, n, large)).astype(np.int32)


def _bias_tensor(rel_table, n_q, valid):
    rel = np.arange(KEY_TILE, dtype=np.int32)[None, :] - WINDOW - np.arange(n_q, dtype=np.int32)[:, None]
    onehot = np.eye(N_BUCKETS, dtype=np.float32)[:, _t5_bucket_np(rel).reshape(-1)]
    bias = jnp.dot(rel_table.T[HEAD_PERM], jnp.asarray(onehot), precision=lax.Precision.HIGHEST)
    return jnp.where(jnp.asarray(valid)[None], bias.reshape(A_HEADS, n_q, KEY_TILE), NEG).astype(F32)


def _stack_heads(x):
    _, r, c = x.shape
    return jnp.transpose(x.reshape(2, 4, 2, r, c), (0, 2, 1, 3, 4)).reshape(2, 2, 4 * r, c)


def _with_sink_column(bias, sinks_perm):
    return _stack_heads(bias.at[:, :, KEY_TILE - 1].set(sinks_perm[:, None]))


def _prompt_valid():
    qc = np.arange(Q_TILE)[:, None] // CHUNK
    kc = np.arange(KEY_TILE)[None, :] // CHUNK
    return (kc >= qc) & (kc <= qc + WINDOW // CHUNK)


def _sample_valid(rows):
    return np.broadcast_to(np.arange(KEY_TILE)[None, :] < WINDOW + rows, (rows, KEY_TILE))


def _mixer_weights(l, g_mix, w_in_bf, b_gate, q_norm_g, k_norm_g, sinks, w_conv_b, w_conv_c, b_conv_c,
                   ln_c_g, ln_c_b, w_proj_a, w_proj_b, w_proj_c, w_out):
    wpa = w_proj_a[l].reshape(A_HEADS, A_HEAD_DIM, D_MODEL)[HEAD_PERM].reshape(A_WIDTH, D_MODEL)
    return dict(
        g_mix=g_mix[l], w_in=w_in_bf, layer=l, b_gate=b_gate[l],
        qg2=(jnp.tile(q_norm_g[l], 2) * (A_HEAD_DIM ** -0.5)).reshape(1, LANES), kg2=jnp.tile(k_norm_g[l], 2).reshape(1, LANES),
        sinks=sinks[l][HEAD_PERM],
        w_cb=w_conv_b[l], w_cc=w_conv_c[l], b_cc=b_conv_c[l], ln_g=ln_c_g[l], ln_b=ln_c_b[l],
        wpa=wpa.astype(BF16), wpb=w_proj_b[l].astype(BF16), wpc=w_proj_c[l].astype(BF16),
        wo=w_out[l].astype(BF16))


def _mixer_prompt(x, mw, bias, batch, seq):
    z = norm_matmul(x, mw["g_mix"], mw["w_in"], mw["layer"], tm=1024, tn=2048, out_dtype=BF16)
    oa, nk, nv = swa_prompt(z, batch, seq, mw["qg2"], mw["kg2"], bias, mw["sinks"])
    ob, oc, ncb, ncc = conv_prompt(z, batch, seq, mw["w_cb"], mw["w_cc"], mw["b_cc"], mw["ln_g"], mw["ln_b"])
    y = merge(oa, ob, oc, z, mw["b_gate"], mw["wpa"], mw["wpb"], mw["wpc"], mw["wo"], x)
    return y, (nk, nv, ncb[:, 8 - (B_CONV - 1):], ncc[:, HALO - (C_CONV - 1):])


def _mixer_sample(x, mw, bias, batch, rows, cache_k, cache_v, layer, st_b, st_c):
    z = norm_matmul(x, mw["g_mix"], mw["w_in"], mw["layer"], tm=x.shape[0], tn=2048, out_dtype=BF16)
    oa, nk, nv = swa_sample(z, batch, rows, cache_k, cache_v, layer, mw["qg2"], mw["kg2"],
                            _with_sink_column(bias, mw["sinks"]))
    stb = jnp.pad(st_b, ((0, 0), (HALO - (B_CONV - 1), 0), (0, 0)))
    stc = jnp.pad(st_c, ((0, 0), (HALO - (C_CONV - 1), 0), (0, 0)))
    ob, oc, ncb, ncc = conv_sample(z, batch, rows, stb, stc, mw["w_cb"], mw["w_cc"], mw["b_cc"],
                                   mw["ln_g"], mw["ln_b"])
    y = merge(oa, ob, oc, z, mw["b_gate"], mw["wpa"], mw["wpb"], mw["wpc"], mw["wo"], x)
    return y, (nk, nv, ncb[:, 8 - (B_CONV - 1):], ncc[:, HALO - (C_CONV - 1):])


def _channel_mixer(yp, ys, l, g_ffn, w_ffn_gu, w_ffn_d, w_router, b_router, w_moe_gu, w_moe_d):
    if l % 2 == 0:
        return (ffn(yp, g_ffn[l], w_ffn_gu[l // 2], w_ffn_d[l // 2]),
                ffn(ys, g_ffn[l], w_ffn_gu[l // 2], w_ffn_d[l // 2]))
    i = l // 2
    w_pad = jnp.pad(w_router[i], ((0, 0), (0, LANES - N_EXPERTS)))
    b_pad = jnp.pad(b_router[i], (0, LANES - N_EXPERTS)).reshape(1, LANES)
    wts_p, ids_p, rank_p, cnt_p = router(yp, g_ffn[l], w_pad, b_pad, jnp.zeros((1, LANES), F32))
    wts_s, ids_s, rank_s, cnt = router(ys, g_ffn[l], w_pad, b_pad, cnt_p)
    mp, ms = yp.shape[0], ys.shape[0]
    nt = (2 * (mp + ms)) // MOE_TM + N_EXPERTS
    ids = jnp.concatenate([ids_p[:, :2], ids_s[:, :2]], axis=0)
    rank = jnp.concatenate([rank_p[:, :2], rank_s[:, :2]], axis=0)
    tile_expert, tile_rows, pos = _dispatch_plan(cnt[0, :N_EXPERTS].astype(jnp.int32), ids, rank, MOE_TM, nt)
    tm_p, tm_s = 512, min(512, ms)
    pos_p, pos_s = _combine_pos(pos[:mp], tm_p), _combine_pos(pos[mp:], tm_s)
    xs = jnp.zeros((nt * MOE_TM, D_MODEL // 2), jnp.uint32)
    xs = moe_dispatch(pos_p, yp, g_ffn[l], xs, tm=tm_p)
    xs = moe_dispatch(pos_s, ys, g_ffn[l], xs, tm=tm_s)
    osort = moe_ffn(tile_expert, tile_rows, xs, w_moe_gu[i], w_moe_d[i])
    return (moe_combine(pos_p, yp, wts_p, osort, tm=tm_p), moe_combine(pos_s, ys, wts_s, osort, tm=tm_s))


def kernel(x_prompt, x_sample, mem_prompt, cache_mem_k, cache_mem_v, cache_swa_k, cache_swa_v, state_conv_b, state_conv_c, rel_table, g_mix, w_in, b_gate, q_norm_g, k_norm_g, sinks, w_conv_b, w_conv_c, b_conv_c, ln_c_g, ln_c_b, w_proj_a, w_proj_b, w_proj_c, w_out, g_xattn, g_mem, w_xq, w_xkv, xq_norm_g, xk_norm_g, w_xo, g_ffn, w_ffn_gu, w_ffn_d, w_router, b_router, w_moe_gu, w_moe_d):
    batch, seq, d = x_prompt.shape
    dec_batch, dec_seq, _ = x_sample.shape
    depth = g_mix.shape[0]
    yp = x_prompt.reshape(batch * seq, d)
    ys = x_sample.reshape(dec_batch * dec_seq, d)
    mem = mem_prompt.reshape(batch * N_MEM, d)
    first = np.arange(KEY_TILE)[None, :] >= Q_TILE
    bias_p = jnp.stack([_stack_heads(_bias_tensor(rel_table, Q_TILE, _prompt_valid() & first)),
                        _stack_heads(_bias_tensor(rel_table, Q_TILE, _prompt_valid()))])
    bias_s = _bias_tensor(rel_table, dec_seq, _sample_valid(dec_seq))
    outs = [[] for _ in range(10)]
    w_in_bf = w_in.astype(BF16)
    ck = cache_swa_k.reshape(depth, dec_batch, WINDOW, A_KV_WIDTH)
    cv = cache_swa_v.reshape(depth, dec_batch, WINDOW, A_KV_WIDTH)
    for l in range(depth):
        mw = _mixer_weights(l, g_mix, w_in_bf, b_gate, q_norm_g, k_norm_g, sinks, w_conv_b, w_conv_c,
                            b_conv_c, ln_c_g, ln_c_b, w_proj_a, w_proj_b, w_proj_c, w_out)
        wq = w_xq[l].astype(BF16)
        wo = w_xo[l].astype(BF16)
        ffn_args = (g_ffn, w_ffn_gu, w_ffn_d, w_router, b_router, w_moe_gu, w_moe_d)
        yp, (nk, nv, ncb, ncc) = _mixer_prompt(yp, mw, bias_p, batch, seq)
        mk, mv = mem_kv(mem, g_mem[l], w_xkv[l].astype(BF16), xk_norm_g[l])
        mk3 = mk.reshape(batch, N_MEM, X_WIDTH)
        mv3 = mv.reshape(batch, N_MEM, X_WIDTH)
        rpb = min(1024, seq)
        yp = xattn(yp, g_xattn[l], wq, xq_norm_g[l], mk3[None], mv3[None], wo, layer=0, nb=1, rpb=rpb,
                   tiles_per_mem=seq // rpb)
        for lst, v in zip(outs[:6], (mk3.reshape(batch, N_MEM, X_HEADS, X_HEAD_DIM),
                                     mv3.reshape(batch, N_MEM, X_HEADS, X_HEAD_DIM),
                                     nk.reshape(batch, WINDOW, A_KV_HEADS, A_HEAD_DIM),
                                     nv.reshape(batch, WINDOW, A_KV_HEADS, A_HEAD_DIM), ncb, ncc)):
            lst.append(v)
        ys, (nk, nv, ncb, ncc) = _mixer_sample(ys, mw, bias_s, dec_batch, dec_seq, ck, cv, l,
                                               state_conv_b[l], state_conv_c[l])
        ys = xattn(ys, g_xattn[l], wq, xq_norm_g[l], cache_mem_k, cache_mem_v, wo, layer=l, nb=8, rpb=dec_seq,
                   tiles_per_mem=1)
        yp, ys = _channel_mixer(yp, ys, l, *ffn_args)
        for lst, v in zip(outs[6:], (nk.reshape(dec_batch, dec_seq, A_KV_HEADS, A_HEAD_DIM),
                                     nv.reshape(dec_batch, dec_seq, A_KV_HEADS, A_HEAD_DIM), ncb, ncc)):
            lst.append(v)
    return (yp.reshape(batch, seq, d), ys.reshape(dec_batch, dec_seq, d)) + tuple(jnp.stack(o) for o in outs)
```

```python
import functools
import math

import numpy as np
import jax
import jax.numpy as jnp
from jax import lax
from jax.experimental import pallas as pl
from jax.experimental.pallas import tpu as pltpu

F32 = jnp.float32
BF16 = jnp.bfloat16

D_MODEL = 2048
CHUNK = 64
A_HEADS = 16
A_KV_HEADS = 4
A_HEAD_DIM = 64
A_WIDTH = A_HEADS * A_HEAD_DIM
A_KV_WIDTH = A_KV_HEADS * A_HEAD_DIM
WINDOW = 128
N_BUCKETS = 32
MAX_DISTANCE = 128
B_WIDTH = 512
B_CONV = 3
C_WIDTH = 512
C_CONV = 31
N_MEM = 256
X_HEADS = 4
X_HEAD_DIM = 128
X_WIDTH = X_HEADS * X_HEAD_DIM
D_FF = 5632
N_EXPERTS = 8
EPS = 1e-6

LANES = 128
KEY_TILE = 256
Q_TILE = 128
HALO = 32
NEG = -1e30
VMEM_LIMIT = 56 * 1024 * 1024
FFN_VMEM_LIMIT = 60 * 1024 * 1024

COL_Q = 0
COL_K, COL_V = 4, 5
COL_GB, COL_GC, COL_HB, COL_GA, COL_GG = 3, 4, 5, 6, 7
COL_GATE0 = 2
IN_COLS = 4096 + 3 * D_MODEL

HEAD_PERM = np.array([8 * n + (p % 2) * 4 + p // 2 for n in range(2) for p in range(8)])


def _cparams(sem):
    return pltpu.CompilerParams(dimension_semantics=sem, vmem_limit_bytes=VMEM_LIMIT)


def _rms(x, g):
    ms = jnp.mean(x * x, axis=-1, keepdims=True)
    return x * lax.rsqrt(ms + EPS) * g


def _resident(shape):
    nd = len(shape)
    return pl.BlockSpec(shape, lambda *_: (0,) * nd, pipeline_mode=pl.Buffered(1))


def _permuted_q_columns(acc):
    lo_lane = lax.broadcasted_iota(jnp.int32, (1, LANES), 1) < A_HEAD_DIM
    nat = [acc[:, c * LANES:(c + 1) * LANES] for c in range(A_WIDTH // LANES)]
    swapped = [pltpu.roll(x, A_HEAD_DIM, 1) for x in nat]
    cols = []
    for n in range(2):
        for m in range(4):
            ca, cb = 4 * n + m // 2, 4 * n + 2 + m // 2
            cols.append(jnp.where(lo_lane, nat[ca], swapped[cb]) if m % 2 == 0
                        else jnp.where(lo_lane, swapped[ca], nat[cb]))
    return cols


def _norm_matmul_kernel(x_ref, g_ref, w_ref, o_ref, hn_ref):
    j = pl.program_id(1)

    @pl.when(j == 0)
    def _():
        hn_ref[...] = _rms(x_ref[...], g_ref[...]).astype(BF16)
        acc = jnp.dot(hn_ref[...], w_ref[...], preferred_element_type=F32)
        for c, col in enumerate(_permuted_q_columns(acc)):
            o_ref[:, c * LANES:(c + 1) * LANES] = col.astype(o_ref.dtype)
        if o_ref.shape[1] > A_WIDTH:
            o_ref[:, A_WIDTH:] = acc[:, A_WIDTH:].astype(o_ref.dtype)

    @pl.when(j > 0)
    def _():
        o_ref[...] = jnp.dot(hn_ref[...], w_ref[...], preferred_element_type=F32).astype(o_ref.dtype)


def norm_matmul(x, g, w, layer, *, tm, tn, out_dtype):
    m, k = x.shape
    n = w.shape[2]
    return pl.pallas_call(
        _norm_matmul_kernel,
        grid=(m // tm, n // tn),
        in_specs=[pl.BlockSpec((tm, k), lambda i, j: (i, 0)),
                  pl.BlockSpec((1, k), lambda i, j: (0, 0)),
                  pl.BlockSpec((None, k, tn), lambda i, j: (layer, 0, j))],
        out_specs=pl.BlockSpec((tm, tn), lambda i, j: (i, j)),
        out_shape=jax.ShapeDtypeStruct((m, n), out_dtype),
        scratch_shapes=[pltpu.VMEM((tm, k), BF16)],
        compiler_params=_cparams(("parallel", "arbitrary")),
        name="norm_matmul",
    )(x, g.reshape(1, k), w)


def _mem_kv_kernel(x_ref, g_ref, w_ref, kg_ref, k_ref, v_ref):
    hn = _rms(x_ref[...], g_ref[...]).astype(BF16)
    kv = jnp.dot(hn, w_ref[...], preferred_element_type=F32)
    for h in range(X_HEADS):
        sl = slice(h * X_HEAD_DIM, (h + 1) * X_HEAD_DIM)
        k_ref[:, sl] = _rms(kv[:, sl], kg_ref[...])
    v_ref[...] = kv[:, X_WIDTH:]


def mem_kv(mem, g, w_bf, kg):
    m, k = mem.shape
    tm = 512
    return pl.pallas_call(
        _mem_kv_kernel,
        grid=(m // tm,),
        in_specs=[pl.BlockSpec((tm, k), lambda i: (i, 0)),
                  pl.BlockSpec((1, k), lambda i: (0, 0)),
                  _resident((k, 2 * X_WIDTH)),
                  pl.BlockSpec((1, X_HEAD_DIM), lambda i: (0, 0))],
        out_specs=[pl.BlockSpec((tm, X_WIDTH), lambda i: (i, 0)),
                   pl.BlockSpec((tm, X_WIDTH), lambda i: (i, 0))],
        out_shape=[jax.ShapeDtypeStruct((m, X_WIDTH), F32)] * 2,
        compiler_params=_cparams(("parallel",)),
        name="mem_kv",
    )(mem, g.reshape(1, k), w_bf, kg.reshape(1, X_HEAD_DIM))


def _half_norm(x, g):
    x2 = x * x
    lo_lane = lax.broadcasted_iota(jnp.int32, (1, LANES), 1) < A_HEAD_DIM
    s_lo = jnp.sum(jnp.where(lo_lane, x2, 0.0), axis=-1, keepdims=True)
    s_hi = jnp.sum(jnp.where(lo_lane, 0.0, x2), axis=-1, keepdims=True)
    ms = jnp.where(lo_lane, s_lo, s_hi) * (1.0 / A_HEAD_DIM)
    return x * lax.rsqrt(ms + EPS) * g


def _swa_heads(q_ref, qg, k2, v2, bias_ref, sink_ref, o_ref, *, stack):
    lo_lane = lax.broadcasted_iota(jnp.int32, (1, LANES), 1) < A_HEAD_DIM
    rows = q_ref.shape[0]
    cols = A_WIDTH // LANES // 2
    for n in range(2):
        k_half = (jnp.where(lo_lane, k2[n], 0.0).astype(BF16), jnp.where(lo_lane, 0.0, k2[n]).astype(BF16))
        for g in range(0, cols, stack):
            qs = [_half_norm(q_ref[:, c * LANES:(c + 1) * LANES].astype(F32), qg)
                  for c in range(cols * n + g, cols * n + g + stack)]
            qn = (qs[0] if stack == 1 else jnp.concatenate(qs, axis=0)).astype(BF16)
            at = slice(g * rows, (g + stack) * rows)
            halves = []
            for half in range(2):
                s = lax.dot_general(qn, k_half[half], (((1,), (1,)), ((), ())), preferred_element_type=F32)
                s = s + bias_ref[n, half, at, :]
                if sink_ref is None:
                    m = jnp.max(s, axis=-1, keepdims=True)
                    p = jnp.exp(s - m)
                    den = jnp.sum(p, axis=-1, keepdims=True)
                else:
                    sink = sink_ref[2 * (cols * n + g) + half]
                    m = jnp.maximum(jnp.max(s, axis=-1, keepdims=True), sink)
                    p = jnp.exp(s - m)
                    den = jnp.sum(p, axis=-1, keepdims=True) + jnp.exp(sink - m)
                o = jnp.dot(p.astype(BF16), v2[n], preferred_element_type=F32)
                halves.append(o / den)
            o = jnp.where(lo_lane, halves[0], halves[1]).astype(o_ref.dtype)
            for i in range(stack):
                c = cols * n + g + i
                o_ref[:, c * LANES:(c + 1) * LANES] = o[i * rows:(i + 1) * rows]


def _swa_prompt_kernel(q_ref, kc_ref, kp_ref, vc_ref, vp_ref, qg_ref, kg_ref,
                       bias_ref, sink_ref, o_ref, nk_ref, nv_ref):
    k2, v2 = [], []
    for n in range(2):
        sl = slice(n * LANES, (n + 1) * LANES)
        kcat = jnp.concatenate([kp_ref[:, sl], kc_ref[:, sl]], axis=0).astype(F32)
        kn = _half_norm(kcat, kg_ref[...])
        nk_ref[0, :, sl] = kn[Q_TILE:]
        k2.append(kn)
        v2.append(jnp.concatenate([vp_ref[:, sl], vc_ref[:, sl]], axis=0))
    nv_ref[0] = vc_ref[...].astype(F32)
    _swa_heads(q_ref, qg_ref[...], k2, v2, bias_ref, sink_ref, o_ref, stack=1)


def swa_prompt(z, batch, seq, qg2, kg2, bias, sinks):
    nt = seq // Q_TILE
    row = lambda b, t: b * nt + t
    prev = lambda b, t: jnp.maximum(b * nt + t - 1, 0)
    return pl.pallas_call(
        _swa_prompt_kernel,
        grid=(batch, nt),
        in_specs=[pl.BlockSpec((Q_TILE, A_WIDTH), lambda b, t: (row(b, t), COL_Q)),
                  pl.BlockSpec((Q_TILE, A_KV_WIDTH), lambda b, t: (row(b, t), COL_K)),
                  pl.BlockSpec((Q_TILE, A_KV_WIDTH), lambda b, t: (prev(b, t), COL_K)),
                  pl.BlockSpec((Q_TILE, A_KV_WIDTH), lambda b, t: (row(b, t), COL_V)),
                  pl.BlockSpec((Q_TILE, A_KV_WIDTH), lambda b, t: (prev(b, t), COL_V)),
                  pl.BlockSpec((1, LANES), lambda b, t: (0, 0)),
                  pl.BlockSpec((1, LANES), lambda b, t: (0, 0)),
                  pl.BlockSpec((None, 2, 2, 4 * Q_TILE, KEY_TILE),
                               lambda b, t: (jnp.minimum(t, 1), 0, 0, 0, 0)),
                  pl.BlockSpec(memory_space=pltpu.SMEM)],
        out_specs=[pl.BlockSpec((Q_TILE, A_WIDTH), lambda b, t: (row(b, t), 0)),
                   pl.BlockSpec((1, WINDOW, A_KV_WIDTH), lambda b, t: (b, 0, 0)),
                   pl.BlockSpec((1, WINDOW, A_KV_WIDTH), lambda b, t: (b, 0, 0))],
        out_shape=[jax.ShapeDtypeStruct((batch * seq, A_WIDTH), BF16),
                   jax.ShapeDtypeStruct((batch, WINDOW, A_KV_WIDTH), F32),
                   jax.ShapeDtypeStruct((batch, WINDOW, A_KV_WIDTH), F32)],
        compiler_params=_cparams(("parallel", "arbitrary")),
        name="swa_prompt",
    )(z, z, z, z, z, qg2, kg2, bias, sinks)


def _swa_sample_kernel(q_ref, kn_ref, vn_ref, ck_ref, cv_ref, qg_ref, kg_ref,
                       bias_ref, o_ref, nk_ref, nv_ref):
    rows = q_ref.shape[0]
    pad = KEY_TILE - WINDOW - rows
    k2, v2 = [], []
    for n in range(2):
        sl = slice(n * LANES, (n + 1) * LANES)
        kn = _half_norm(kn_ref[:, sl].astype(F32), kg_ref[...])
        nk_ref[0, :, sl] = kn
        k2.append(jnp.concatenate([ck_ref[0, :, sl], kn, jnp.zeros((pad, LANES), F32)], axis=0))
        v2.append(jnp.concatenate([cv_ref[0, :, sl].astype(BF16), vn_ref[:, sl],
                                   jnp.zeros((pad, LANES), BF16)], axis=0))
    nv_ref[0] = vn_ref[...].astype(F32)
    _swa_heads(q_ref, qg_ref[...], k2, v2, bias_ref, None, o_ref, stack=4)


def swa_sample(z, batch, rows, cache_k, cache_v, layer, qg2, kg2, bias):
    return pl.pallas_call(
        _swa_sample_kernel,
        grid=(batch,),
        in_specs=[pl.BlockSpec((rows, A_WIDTH), lambda b: (b, COL_Q)),
                  pl.BlockSpec((rows, A_KV_WIDTH), lambda b: (b, COL_K)),
                  pl.BlockSpec((rows, A_KV_WIDTH), lambda b: (b, COL_V)),
                  pl.BlockSpec((None, 1, WINDOW, A_KV_WIDTH), lambda b: (layer, b, 0, 0)),
                  pl.BlockSpec((None, 1, WINDOW, A_KV_WIDTH), lambda b: (layer, b, 0, 0)),
                  pl.BlockSpec((1, LANES), lambda b: (0, 0)),
                  pl.BlockSpec((1, LANES), lambda b: (0, 0)),
                  _resident((2, 2, 4 * rows, KEY_TILE))],
        out_specs=[pl.BlockSpec((rows, A_WIDTH), lambda b: (b, 0)),
                   pl.BlockSpec((1, rows, A_KV_WIDTH), lambda b: (b, 0, 0)),
                   pl.BlockSpec((1, rows, A_KV_WIDTH), lambda b: (b, 0, 0))],
        out_shape=[jax.ShapeDtypeStruct((batch * rows, A_WIDTH), BF16),
                   jax.ShapeDtypeStruct((batch, rows, A_KV_WIDTH), F32),
                   jax.ShapeDtypeStruct((batch, rows, A_KV_WIDTH), F32)],
        compiler_params=_cparams(("parallel",)),
        name="swa_sample",
    )(z, z, z, cache_k, cache_v, qg2, kg2, bias)


def _conv_body(gb_ref, ub_main, uc_main, ub_halo, uc_halo, wb_ref, wc_ref, bc_ref, lg_ref, lb_ref,
               ob_ref, oc_ref, nb_ref, nc_ref, sb_ref, sc_ref, ph_ref, write_state):
    rows = ub_main.shape[0]
    sb_ref[0:HALO] = ub_halo
    sb_ref[HALO:HALO + rows] = ub_main
    sc_ref[0:HALO] = uc_halo
    sc_ref[HALO:HALO + rows] = uc_main
    span = rows + HALO
    sc_ref[span:span + 8] = jnp.zeros((8, C_WIDTH), F32)
    for b in range(1, 8):
        ph_ref[b - 1] = sc_ref[pl.ds(b, span), :]
    sub = min(rows, 32)
    for r0 in range(0, rows, sub):
        yb = jnp.zeros((sub, B_WIDTH), F32)
        for k in range(B_CONV):
            yb = yb + wb_ref[k:k + 1, :] * sb_ref[pl.ds(r0 + HALO - (B_CONV - 1) + k, sub), :]
        ob_ref[r0:r0 + sub, :] = (gb_ref[r0:r0 + sub, :].astype(F32) * yb).astype(ob_ref.dtype)
        yc = jnp.zeros((sub, C_WIDTH), F32)
        for k in range(C_CONV):
            shift, phase = divmod(HALO - (C_CONV - 1) + k, 8)
            at = pl.ds(r0 + 8 * shift, sub)
            taps = sc_ref[at, :] if phase == 0 else ph_ref[phase - 1, at, :]
            yc = yc + wc_ref[k:k + 1, :] * taps
        yc = yc + bc_ref[...]
        mu = jnp.mean(yc, axis=-1, keepdims=True)
        xc = yc - mu
        y = xc * lax.rsqrt(jnp.mean(xc * xc, axis=-1, keepdims=True) + EPS)
        y = y * lg_ref[...] + lb_ref[...]
        oc_ref[r0:r0 + sub, :] = (y * jax.nn.sigmoid(y)).astype(oc_ref.dtype)

    def _state():
        nb_ref[0] = sb_ref[rows + HALO - 8:rows + HALO]
        nc_ref[0] = sc_ref[rows:rows + HALO]

    write_state(_state)


def _conv_prompt_kernel(gb_ref, gc_ref, hb_ref, ga_ref, gg_ref, gch_ref, hbh_ref, gah_ref, ggh_ref,
                        wb_ref, wc_ref, bc_ref, lg_ref, lb_ref,
                        ob_ref, oc_ref, nb_ref, nc_ref, sb_ref, sc_ref, ph_ref):
    t = pl.program_id(1)
    hist = (t > 0).astype(F32)
    ub_main = gc_ref[...].astype(F32) * hb_ref[...].astype(F32)
    uc_main = ga_ref[...].astype(F32) * jax.nn.sigmoid(gg_ref[...].astype(F32))
    ub_halo = gch_ref[...].astype(F32) * hbh_ref[...].astype(F32) * hist
    uc_halo = gah_ref[...].astype(F32) * jax.nn.sigmoid(ggh_ref[...].astype(F32)) * hist
    last = pl.num_programs(1) - 1
    _conv_body(gb_ref, ub_main, uc_main, ub_halo, uc_halo, wb_ref, wc_ref, bc_ref, lg_ref, lb_ref,
               ob_ref, oc_ref, nb_ref, nc_ref, sb_ref, sc_ref, ph_ref,
               lambda f: pl.when(t == last)(f))


def conv_prompt(z, batch, seq, wb, wc, bc, lg, lb, *, tr=512):
    nt = seq // tr
    hp = tr // HALO
    main = lambda c: pl.BlockSpec((tr, B_WIDTH), lambda b, t: (b * nt + t, c))
    halo = lambda c: pl.BlockSpec((HALO, B_WIDTH), lambda b, t: (jnp.maximum((b * nt + t) * hp - 1, 0), c))
    vec = lambda r: pl.BlockSpec((r, B_WIDTH), lambda b, t: (0, 0))
    return pl.pallas_call(
        _conv_prompt_kernel,
        grid=(batch, nt),
        in_specs=[main(COL_GB), main(COL_GC), main(COL_HB), main(COL_GA), main(COL_GG),
                  halo(COL_GC), halo(COL_HB), halo(COL_GA), halo(COL_GG),
                  vec(B_CONV), vec(C_CONV), vec(1), vec(1), vec(1)],
        out_specs=[pl.BlockSpec((tr, B_WIDTH), lambda b, t: (b * nt + t, 0)),
                   pl.BlockSpec((tr, C_WIDTH), lambda b, t: (b * nt + t, 0)),
                   pl.BlockSpec((1, 8, B_WIDTH), lambda b, t: (b, 0, 0)),
                   pl.BlockSpec((1, HALO, C_WIDTH), lambda b, t: (b, 0, 0))],
        out_shape=[jax.ShapeDtypeStruct((batch * seq, B_WIDTH), BF16),
                   jax.ShapeDtypeStruct((batch * seq, C_WIDTH), BF16),
                   jax.ShapeDtypeStruct((batch, 8, B_WIDTH), F32),
                   jax.ShapeDtypeStruct((batch, HALO, C_WIDTH), F32)],
        scratch_shapes=[pltpu.VMEM((tr + HALO, B_WIDTH), F32), pltpu.VMEM((tr + HALO + 8, C_WIDTH), F32),
                        pltpu.VMEM((7, tr + HALO, C_WIDTH), F32)],
        compiler_params=_cparams(("parallel", "arbitrary")),
        name="conv_prompt",
    )(z, z, z, z, z, z, z, z, z, wb, wc, bc.reshape(1, -1), lg.reshape(1, -1), lb.reshape(1, -1))


def _conv_sample_kernel(gb_ref, gc_ref, hb_ref, ga_ref, gg_ref, stb_ref, stc_ref,
                        wb_ref, wc_ref, bc_ref, lg_ref, lb_ref,
                        ob_ref, oc_ref, nb_ref, nc_ref, sb_ref, sc_ref, ph_ref):
    ub_main = gc_ref[...].astype(F32) * hb_ref[...].astype(F32)
    uc_main = ga_ref[...].astype(F32) * jax.nn.sigmoid(gg_ref[...].astype(F32))
    _conv_body(gb_ref, ub_main, uc_main, stb_ref[0], stc_ref[0], wb_ref, wc_ref, bc_ref, lg_ref, lb_ref,
               ob_ref, oc_ref, nb_ref, nc_ref, sb_ref, sc_ref, ph_ref, lambda f: f())


def conv_sample(z, batch, rows, stb, stc, wb, wc, bc, lg, lb):
    main = lambda c: pl.BlockSpec((rows, B_WIDTH), lambda b: (b, c))
    vec = lambda r: pl.BlockSpec((r, B_WIDTH), lambda b: (0, 0))
    return pl.pallas_call(
        _conv_sample_kernel,
        grid=(batch,),
        in_specs=[main(COL_GB), main(COL_GC), main(COL_HB), main(COL_GA), main(COL_GG),
                  pl.BlockSpec((1, HALO, B_WIDTH), lambda b: (b, 0, 0)),
                  pl.BlockSpec((1, HALO, C_WIDTH), lambda b: (b, 0, 0)),
                  vec(B_CONV), vec(C_CONV), vec(1), vec(1), vec(1)],
        out_specs=[pl.BlockSpec((rows, B_WIDTH), lambda b: (b, 0)),
                   pl.BlockSpec((rows, C_WIDTH), lambda b: (b, 0)),
                   pl.BlockSpec((1, 8, B_WIDTH), lambda b: (b, 0, 0)),
                   pl.BlockSpec((1, HALO, C_WIDTH), lambda b: (b, 0, 0))],
        out_shape=[jax.ShapeDtypeStruct((batch * rows, B_WIDTH), BF16),
                   jax.ShapeDtypeStruct((batch * rows, C_WIDTH), BF16),
                   jax.ShapeDtypeStruct((batch, 8, B_WIDTH), F32),
                   jax.ShapeDtypeStruct((batch, HALO, C_WIDTH), F32)],
        scratch_shapes=[pltpu.VMEM((rows + HALO, B_WIDTH), F32), pltpu.VMEM((rows + HALO + 8, C_WIDTH), F32),
                        pltpu.VMEM((7, rows + HALO, C_WIDTH), F32)],
        compiler_params=_cparams(("parallel",)),
        name="conv_sample",
    )(z, z, z, z, z, stb, stc, wb, wc, bc.reshape(1, -1), lg.reshape(1, -1), lb.reshape(1, -1))


def _merge_kernel(oa_ref, ob_ref, oc_ref, l0_ref, l1_ref, l2_ref, bg_ref,
                  wpa_ref, wpb_ref, wpc_ref, wo_ref, x_ref, o_ref):
    def gated(l_ref, i, o_r, w_r):
        gate = jax.nn.sigmoid(l_ref[...].astype(F32) + bg_ref[i:i + 1, :])
        return gate * jnp.dot(o_r[...], w_r[...], preferred_element_type=F32)

    merged = gated(l0_ref, 0, oa_ref, wpa_ref)
    merged = merged + gated(l1_ref, 1, ob_ref, wpb_ref)
    merged = merged + gated(l2_ref, 2, oc_ref, wpc_ref)
    o_ref[...] = x_ref[...] + jnp.dot(merged.astype(BF16), wo_ref[...], preferred_element_type=F32)


def merge(oa, ob, oc, z, bg, wpa, wpb, wpc, wo, x, *, tm=512):
    m = x.shape[0]
    tm = min(tm, m)
    rows = lambda w: pl.BlockSpec((tm, w), lambda i: (i, 0))
    gate = lambda c: pl.BlockSpec((tm, D_MODEL), lambda i: (i, COL_GATE0 + c))
    return pl.pallas_call(
        _merge_kernel,
        grid=(m // tm,),
        in_specs=[rows(A_WIDTH), rows(B_WIDTH), rows(C_WIDTH), gate(0), gate(1), gate(2),
                  pl.BlockSpec((3, D_MODEL), lambda i: (0, 0)),
                  _resident((A_WIDTH, D_MODEL)), _resident((B_WIDTH, D_MODEL)),
                  _resident((C_WIDTH, D_MODEL)), _resident((D_MODEL, D_MODEL)),
                  rows(D_MODEL)],
        out_specs=rows(D_MODEL),
        out_shape=jax.ShapeDtypeStruct((m, D_MODEL), F32),
        compiler_params=_cparams(("parallel",)),
        name="merge",
    )(oa, ob, oc, z, z, z, bg.reshape(3, D_MODEL), wpa, wpb, wpc, wo, x)


def _xattn_kernel(y_ref, g_ref, wq_ref, qg_ref, mk_ref, mv_ref, wo_ref, o_ref, *, nb, rpb):
    y = y_ref[...]
    hn = _rms(y, g_ref[...]).astype(BF16)
    q = jnp.dot(hn, wq_ref[...], preferred_element_type=F32)
    heads = []
    for h in range(X_HEADS):
        sl = slice(h * X_HEAD_DIM, (h + 1) * X_HEAD_DIM)
        qh = _rms(q[:, sl], qg_ref[...])
        per_batch = []
        for b in range(nb):
            qb = qh[b * rpb:(b + 1) * rpb].astype(BF16)
            if len(mk_ref.shape) == 4:
                kh = mk_ref[b, :, h, :].astype(BF16)
                vh = mv_ref[b, :, h, :].astype(BF16)
            else:
                kh = mk_ref[b, :, sl].astype(BF16)
                vh = mv_ref[b, :, sl].astype(BF16)
            s = lax.dot_general(qb, kh, (((1,), (1,)), ((), ())),
                                preferred_element_type=F32) * (X_HEAD_DIM ** -0.5)
            m = jnp.max(s, axis=-1, keepdims=True)
            p = jnp.exp(s - m)
            den = jnp.sum(p, axis=-1, keepdims=True)
            per_batch.append(jnp.dot(p.astype(BF16), vh, preferred_element_type=F32) / den)
        heads.append(per_batch[0] if nb == 1 else jnp.concatenate(per_batch, axis=0))
    o = jnp.concatenate(heads, axis=1).astype(BF16)
    o_ref[...] = y + jnp.dot(o, wo_ref[...], preferred_element_type=F32)


def xattn(y, g, wq, qg, mk, mv, wo, *, layer, nb, rpb, tiles_per_mem):
    m = y.shape[0]
    tm = nb * rpb
    tail = (0,) * (mk.ndim - 2)
    mem_idx = ((lambda i: (layer, i // tiles_per_mem) + tail) if nb == 1
               else (lambda i: (layer, i) + tail))
    mem_block = (None, nb) + mk.shape[2:]
    return pl.pallas_call(
        functools.partial(_xattn_kernel, nb=nb, rpb=rpb),
        grid=(m // tm,),
        in_specs=[pl.BlockSpec((tm, D_MODEL), lambda i: (i, 0)),
                  pl.BlockSpec((1, D_MODEL), lambda i: (0, 0)),
                  _resident((D_MODEL, X_WIDTH)),
                  pl.BlockSpec((1, X_HEAD_DIM), lambda i: (0, 0)),
                  pl.BlockSpec(mem_block, mem_idx),
                  pl.BlockSpec(mem_block, mem_idx),
                  _resident((X_WIDTH, D_MODEL))],
        out_specs=pl.BlockSpec((tm, D_MODEL), lambda i: (i, 0)),
        out_shape=jax.ShapeDtypeStruct((m, D_MODEL), F32),
        compiler_params=_cparams(("parallel",)),
        name="xattn",
    )(y, g.reshape(1, -1), wq, qg.reshape(1, -1), mk, mv, wo)


def _ffn_kernel(x_ref, g_ref, wg_ref, wu_ref, wd_ref, o_ref, hn_ref):
    @pl.when(pl.program_id(1) == 0)
    def _():
        x = x_ref[...]
        hn_ref[...] = _rms(x, g_ref[...]).astype(BF16)
        o_ref[...] = x

    hn = hn_ref[...]
    g = jnp.dot(hn, wg_ref[...].astype(BF16), preferred_element_type=F32)
    u = jnp.dot(hn, wu_ref[...].astype(BF16), preferred_element_type=F32)
    a = (g * jax.nn.sigmoid(g) * u).astype(BF16)
    o_ref[...] += jnp.dot(a, wd_ref[...].astype(BF16), preferred_element_type=F32)


def ffn(x, g, w_gu, w_d, *, tm=2048, tf=256):
    m = x.shape[0]
    if m < tm:
        tm, tf = m, 512
    nf = D_FF // tf
    return pl.pallas_call(
        _ffn_kernel,
        grid=(m // tm, nf),
        in_specs=[pl.BlockSpec((tm, D_MODEL), lambda i, f: (i, 0), pipeline_mode=pl.Buffered(1)),
                  pl.BlockSpec((1, D_MODEL), lambda i, f: (0, 0)),
                  pl.BlockSpec((D_MODEL, tf), lambda i, f: (0, f)),
                  pl.BlockSpec((D_MODEL, tf), lambda i, f: (0, nf + f)),
                  pl.BlockSpec((tf, D_MODEL), lambda i, f: (f, 0))],
        out_specs=pl.BlockSpec((tm, D_MODEL), lambda i, f: (i, 0), pipeline_mode=pl.Buffered(1)),
        out_shape=jax.ShapeDtypeStruct((m, D_MODEL), F32),
        scratch_shapes=[pltpu.VMEM((tm, D_MODEL), BF16)],
        compiler_params=pltpu.CompilerParams(dimension_semantics=("parallel", "arbitrary"),
                                             vmem_limit_bytes=FFN_VMEM_LIMIT),
        name="ffn",
    )(x, g.reshape(1, -1), w_gu, w_gu, w_d)


def _split3(x):
    hi = x.astype(BF16)
    lo = (x - hi.astype(F32)).astype(BF16)
    return hi, lo


def _router_kernel(x_ref, g_ref, w_ref, b_ref, tri_ref, base_ref, wts_ref, ids_ref, rank_ref, cnt_ref, run_ref):
    @pl.when(pl.program_id(0) == 0)
    def _():
        run_ref[...] = base_ref[...]

    hn = _rms(x_ref[...], g_ref[...])
    h_hi, h_lo = _split3(hn)
    w_hi, w_lo = _split3(w_ref[...])
    dot = functools.partial(jnp.dot, preferred_element_type=F32)
    logits = dot(h_hi, w_hi) + dot(h_hi, w_lo) + dot(h_lo, w_hi) + b_ref[...]
    lane = lax.broadcasted_iota(jnp.int32, logits.shape, 1).astype(F32)
    logits = jnp.where(lane < N_EXPERTS, logits, -jnp.inf)
    v1 = jnp.max(logits, axis=-1, keepdims=True)
    i1 = jnp.min(jnp.where(logits == v1, lane, float(LANES)), axis=-1, keepdims=True)
    rest = jnp.where(lane == i1, -jnp.inf, logits)
    v2 = jnp.max(rest, axis=-1, keepdims=True)
    i2 = jnp.min(jnp.where(rest == v2, lane, float(LANES)), axis=-1, keepdims=True)
    e2 = jnp.exp(v2 - v1)
    den = 1.0 + e2
    first, second = lane == 0.0, lane == 1.0
    wts_ref[...] = jnp.where(first, 1.0 / den, 0.0) + jnp.where(second, e2 / den, 0.0)
    ids_ref[...] = (jnp.where(first, i1, 0.0) + jnp.where(second, i2, 0.0)).astype(jnp.int32)
    hit1, hit2 = lane == i1, lane == i2
    hits = jnp.where(jnp.logical_or(hit1, hit2), 1.0, 0.0)
    before = dot(tri_ref[...], hits.astype(BF16)) + run_ref[...]
    r1 = jnp.sum(jnp.where(hit1, before, 0.0), axis=-1, keepdims=True)
    r2 = jnp.sum(jnp.where(hit2, before, 0.0), axis=-1, keepdims=True)
    rank_ref[...] = (jnp.where(first, r1, 0.0) + jnp.where(second, r2, 0.0)).astype(jnp.int32)
    run_ref[...] += jnp.sum(hits, axis=0, keepdims=True)
    cnt_ref[...] = run_ref[...]


def router(x, g, w_pad, b_pad, base, *, tm=1024):
    m = x.shape[0]
    tm = min(tm, m)
    tri = jnp.asarray(np.tril(np.ones((tm, tm), np.float32), -1), BF16)
    row = lambda w: pl.BlockSpec((tm, w), lambda i: (i, 0))
    fixed = lambda s: pl.BlockSpec(s, lambda i: (0, 0))
    return pl.pallas_call(
        _router_kernel,
        grid=(m // tm,),
        in_specs=[row(D_MODEL), fixed((1, D_MODEL)), fixed((D_MODEL, LANES)), fixed((1, LANES)),
                  fixed((tm, tm)), fixed((1, LANES))],
        out_specs=[row(LANES), row(LANES), row(LANES), fixed((1, LANES))],
        out_shape=[jax.ShapeDtypeStruct((m, LANES), F32),
                   jax.ShapeDtypeStruct((m, LANES), jnp.int32),
                   jax.ShapeDtypeStruct((m, LANES), jnp.int32),
                   jax.ShapeDtypeStruct((1, LANES), F32)],
        scratch_shapes=[pltpu.VMEM((1, LANES), F32)],
        compiler_params=_cparams(("arbitrary",)),
        name="router",
    )(x, g.reshape(1, -1), w_pad, b_pad, tri, base)


MOE_TM = 1024
MOE_TF = 512


def _row_copy(src, row, dst, r, sem):
    return pltpu.make_async_copy(src.at[pl.ds(row, 1)], dst.at[pl.ds(r, 1)], sem)


def _moe_dispatch_kernel(pos_ref, x_ref, g_ref, xs_in, xs_hbm, pk_ref, sem):
    del xs_in
    tm = x_ref.shape[0]
    half = D_MODEL // 2
    hn = _rms(x_ref[...], g_ref[...]).astype(BF16)
    lo_bits = lax.bitcast_convert_type(hn[:, :half].astype(F32), jnp.uint32)
    hi_bits = lax.bitcast_convert_type(hn[:, half:].astype(F32), jnp.uint32)
    pk_ref[...] = (lo_bits >> 16) | (hi_bits & jnp.uint32(0xFFFF0000))

    def issue(r, c):
        _row_copy(pk_ref, r, xs_hbm, pos_ref[0, 0, r], sem).start()
        _row_copy(pk_ref, r, xs_hbm, pos_ref[0, 0, tm + r], sem).start(priority=1)
        return c

    lax.fori_loop(0, tm, issue, 0, unroll=8)

    def wait(r, c):
        _row_copy(pk_ref, r, xs_hbm, 0, sem).wait()
        _row_copy(pk_ref, r, xs_hbm, 0, sem).wait()
        return c

    lax.fori_loop(0, tm, wait, 0, unroll=8)


def moe_dispatch(pos, x, g, xs, *, tm):
    m = x.shape[0]
    return pl.pallas_call(
        _moe_dispatch_kernel,
        grid=(m // tm,),
        in_specs=[pl.BlockSpec((1, 1, 2 * tm), lambda i: (i, 0, 0), memory_space=pltpu.SMEM),
                  pl.BlockSpec((tm, D_MODEL), lambda i: (i, 0)),
                  pl.BlockSpec((1, D_MODEL), lambda i: (0, 0)),
                  pl.BlockSpec(memory_space=pl.ANY)],
        out_specs=pl.BlockSpec(memory_space=pl.ANY),
        out_shape=jax.ShapeDtypeStruct(xs.shape, xs.dtype),
        scratch_shapes=[pltpu.VMEM((tm, D_MODEL // 2), jnp.uint32), pltpu.SemaphoreType.DMA],
        input_output_aliases={3: 0},
        compiler_params=_cparams(("arbitrary",)),
        name="moe_dispatch",
    )(pos, x, g.reshape(1, -1), xs)


def _moe_ffn_kernel(te_ref, tv_ref, xs_ref, wg_ref, wu_ref, wd_ref, o_ref, hn_ref):
    t = pl.program_id(0)
    f = pl.program_id(1)
    rows = xs_ref.shape[0]
    half = D_MODEL // 2

    @pl.when(f == 0)
    def _():
        o_ref[...] = jnp.zeros_like(o_ref)
        xu = xs_ref[...]
        hn_ref[:, :half] = lax.bitcast_convert_type(xu << 16, F32).astype(BF16)
        hn_ref[:, half:] = lax.bitcast_convert_type(xu & jnp.uint32(0xFFFF0000), F32).astype(BF16)

    def swiglu(r):
        hn = hn_ref[0:r]
        g = jnp.dot(hn, wg_ref[...].astype(BF16), preferred_element_type=F32)
        u = jnp.dot(hn, wu_ref[...].astype(BF16), preferred_element_type=F32)
        a = (g * jax.nn.sigmoid(g) * u).astype(BF16)
        o_ref[0:r] += jnp.dot(a, wd_ref[...].astype(BF16), preferred_element_type=F32)

    nv = tv_ref[t]

    quarter = rows // 4
    for k in range(1, 5):
        @pl.when(jnp.logical_and(nv > (k - 1) * quarter, nv <= k * quarter))
        def _(k=k):
            swiglu(k * quarter)


def moe_ffn(tile_expert, tile_rows, xs, w_gu, w_d):
    tm = MOE_TM
    nt = xs.shape[0] // tm
    nf = D_FF // MOE_TF
    last = nf - 1
    col = lambda f, tv, t: jnp.where(tv[t] != 0, f, last)
    grid_spec = pltpu.PrefetchScalarGridSpec(
        num_scalar_prefetch=2,
        grid=(nt, nf),
        in_specs=[pl.BlockSpec((tm, D_MODEL // 2), lambda t, f, te, tv: (t, 0), pipeline_mode=pl.Buffered(1)),
                  pl.BlockSpec((None, D_MODEL, MOE_TF), lambda t, f, te, tv: (te[t], 0, col(f, tv, t))),
                  pl.BlockSpec((None, D_MODEL, MOE_TF), lambda t, f, te, tv: (te[t], 0, nf + col(f, tv, t))),
                  pl.BlockSpec((None, MOE_TF, D_MODEL), lambda t, f, te, tv: (te[t], col(f, tv, t), 0))],
        out_specs=pl.BlockSpec((tm, D_MODEL), lambda t, f, te, tv: (t, 0)),
        scratch_shapes=[pltpu.VMEM((tm, D_MODEL), BF16)],
    )
    return pl.pallas_call(
        _moe_ffn_kernel,
        grid_spec=grid_spec,
        out_shape=jax.ShapeDtypeStruct((nt * tm, D_MODEL), F32),
        compiler_params=_cparams(("arbitrary", "arbitrary")),
        name="moe_ffn",
    )(tile_expert, tile_rows, xs, w_gu, w_gu, w_d)


def _moe_combine_kernel(pos_ref, x_ref, w_ref, osort_hbm, o_ref, abuf, sem):
    tm = x_ref.shape[0]

    def issue(r, c):
        _row_copy(osort_hbm, pos_ref[0, 0, r], abuf, r, sem).start()
        _row_copy(osort_hbm, pos_ref[0, 0, tm + r], abuf, tm + r, sem).start(priority=1)
        return c

    lax.fori_loop(0, tm, issue, 0, unroll=8)

    def wait(r, c):
        _row_copy(osort_hbm, 0, abuf, r, sem).wait()
        return c

    lax.fori_loop(0, 2 * tm, wait, 0, unroll=8)
    w = w_ref[...]
    o_ref[...] = x_ref[...] + w[:, 0:1] * abuf[0:tm, :] + w[:, 1:2] * abuf[tm:2 * tm, :]


def moe_combine(pos, x, wts, osort, *, tm=256):
    m = x.shape[0]
    return pl.pallas_call(
        _moe_combine_kernel,
        grid=(m // tm,),
        in_specs=[pl.BlockSpec((1, 1, 2 * tm), lambda i: (i, 0, 0), memory_space=pltpu.SMEM),
                  pl.BlockSpec((tm, D_MODEL), lambda i: (i, 0)),
                  pl.BlockSpec((tm, LANES), lambda i: (i, 0)),
                  pl.BlockSpec(memory_space=pl.ANY)],
        out_specs=pl.BlockSpec((tm, D_MODEL), lambda i: (i, 0)),
        out_shape=jax.ShapeDtypeStruct((m, D_MODEL), F32),
        scratch_shapes=[pltpu.VMEM((2 * tm, D_MODEL), F32), pltpu.SemaphoreType.DMA],
        compiler_params=_cparams(("arbitrary",)),
        name="moe_combine",
    )(pos, x, wts, osort)


def _dispatch_plan(counts, ids, rank, tm, nt):
    padded = ((counts + tm - 1) // tm) * tm
    ends = jnp.cumsum(padded)
    offs = ends - padded
    off_of = jnp.zeros_like(ids)
    for e in range(N_EXPERTS):
        off_of = jnp.where(ids == e, offs[e], off_of)
    pos = off_of + rank
    starts = jnp.arange(nt, dtype=jnp.int32) * tm
    tile_expert = jnp.minimum(jnp.sum((starts[:, None] >= ends[None, :]).astype(jnp.int32), axis=1),
                              N_EXPERTS - 1)
    real_end = (offs + counts)[tile_expert]
    tile_rows = jnp.where(starts < ends[-1], jnp.clip(real_end - starts, 0, tm), 0).astype(jnp.int32)
    last_valid = jnp.maximum(jnp.sum((tile_rows != 0).astype(jnp.int32)) - 1, 0)
    tile_expert = jnp.where(tile_rows != 0, tile_expert, tile_expert[last_valid])
    return tile_expert, tile_rows, pos


def _combine_pos(pos, tm):
    m = pos.shape[0]
    return jnp.transpose(pos.reshape(m // tm, tm, 2), (0, 2, 1)).reshape(m // tm, 1, 2 * tm)


def _t5_bucket_np(rel):
    nb = N_BUCKETS // 2
    max_exact = nb // 2
    ret = np.where(rel > 0, nb, 0)
    n = np.abs(rel)
    nf = np.maximum(n, 1).astype(np.float32)
    large = max_exact + (np.log(nf / np.float32(max_exact)) / np.float32(math.log(MAX_DISTANCE / max_exact))
                         * np.float32(nb - max_exact)).astype(np.int32)
    large = np.minimum(large, nb - 1)
    return (ret + np.where(n < max_exact, n, large)).astype(np.int32)


def _bias_tensor(rel_table, n_q, valid):
    rel = np.arange(KEY_TILE, dtype=np.int32)[None, :] - WINDOW - np.arange(n_q, dtype=np.int32)[:, None]
    onehot = np.eye(N_BUCKETS, dtype=np.float32)[:, _t5_bucket_np(rel).reshape(-1)]
    bias = jnp.dot(rel_table.T[HEAD_PERM], jnp.asarray(onehot), precision=lax.Precision.HIGHEST)
    return jnp.where(jnp.asarray(valid)[None], bias.reshape(A_HEADS, n_q, KEY_TILE), NEG).astype(F32)


def _stack_heads(x):
    _, r, c = x.shape
    return jnp.transpose(x.reshape(2, 4, 2, r, c), (0, 2, 1, 3, 4)).reshape(2, 2, 4 * r, c)


def _with_sink_column(bias, sinks_perm):
    return _stack_heads(bias.at[:, :, KEY_TILE - 1].set(sinks_perm[:, None]))


def _prompt_valid():
    qc = np.arange(Q_TILE)[:, None] // CHUNK
    kc = np.arange(KEY_TILE)[None, :] // CHUNK
    return (kc >= qc) & (kc <= qc + WINDOW // CHUNK)


def _sample_valid(rows):
    return np.broadcast_to(np.arange(KEY_TILE)[None, :] < WINDOW + rows, (rows, KEY_TILE))


def _mixer_weights(l, g_mix, w_in_bf, b_gate, q_norm_g, k_norm_g, sinks, w_conv_b, w_conv_c, b_conv_c,
                   ln_c_g, ln_c_b, w_proj_a, w_proj_b, w_proj_c, w_out):
    wpa = w_proj_a[l].reshape(A_HEADS, A_HEAD_DIM, D_MODEL)[HEAD_PERM].reshape(A_WIDTH, D_MODEL)
    return dict(
        g_mix=g_mix[l], w_in=w_in_bf, layer=l, b_gate=b_gate[l],
        qg2=(jnp.tile(q_norm_g[l], 2) * (A_HEAD_DIM ** -0.5)).reshape(1, LANES), kg2=jnp.tile(k_norm_g[l], 2).reshape(1, LANES),
        sinks=sinks[l][HEAD_PERM],
        w_cb=w_conv_b[l], w_cc=w_conv_c[l], b_cc=b_conv_c[l], ln_g=ln_c_g[l], ln_b=ln_c_b[l],
        wpa=wpa.astype(BF16), wpb=w_proj_b[l].astype(BF16), wpc=w_proj_c[l].astype(BF16),
        wo=w_out[l].astype(BF16))


def _mixer_prompt(x, mw, bias, batch, seq):
    z = norm_matmul(x, mw["g_mix"], mw["w_in"], mw["layer"], tm=1024, tn=2048, out_dtype=BF16)
    oa, nk, nv = swa_prompt(z, batch, seq, mw["qg2"], mw["kg2"], bias, mw["sinks"])
    ob, oc, ncb, ncc = conv_prompt(z, batch, seq, mw["w_cb"], mw["w_cc"], mw["b_cc"], mw["ln_g"], mw["ln_b"])
    y = merge(oa, ob, oc, z, mw["b_gate"], mw["wpa"], mw["wpb"], mw["wpc"], mw["wo"], x)
    return y, (nk, nv, ncb[:, 8 - (B_CONV - 1):], ncc[:, HALO - (C_CONV - 1):])


def _mixer_sample(x, mw, bias, batch, rows, cache_k, cache_v, layer, st_b, st_c):
    z = norm_matmul(x, mw["g_mix"], mw["w_in"], mw["layer"], tm=x.shape[0], tn=2048, out_dtype=BF16)
    oa, nk, nv = swa_sample(z, batch, rows, cache_k, cache_v, layer, mw["qg2"], mw["kg2"],
                            _with_sink_column(bias, mw["sinks"]))
    stb = jnp.pad(st_b, ((0, 0), (HALO - (B_CONV - 1), 0), (0, 0)))
    stc = jnp.pad(st_c, ((0, 0), (HALO - (C_CONV - 1), 0), (0, 0)))
    ob, oc, ncb, ncc = conv_sample(z, batch, rows, stb, stc, mw["w_cb"], mw["w_cc"], mw["b_cc"],
                                   mw["ln_g"], mw["ln_b"])
    y = merge(oa, ob, oc, z, mw["b_gate"], mw["wpa"], mw["wpb"], mw["wpc"], mw["wo"], x)
    return y, (nk, nv, ncb[:, 8 - (B_CONV - 1):], ncc[:, HALO - (C_CONV - 1):])


def _channel_mixer(yp, ys, l, g_ffn, w_ffn_gu, w_ffn_d, w_router, b_router, w_moe_gu, w_moe_d):
    if l % 2 == 0:
        return (ffn(yp, g_ffn[l], w_ffn_gu[l // 2], w_ffn_d[l // 2]),
                ffn(ys, g_ffn[l], w_ffn_gu[l // 2], w_ffn_d[l // 2]))
    i = l // 2
    w_pad = jnp.pad(w_router[i], ((0, 0), (0, LANES - N_EXPERTS)))
    b_pad = jnp.pad(b_router[i], (0, LANES - N_EXPERTS)).reshape(1, LANES)
    wts_p, ids_p, rank_p, cnt_p = router(yp, g_ffn[l], w_pad, b_pad, jnp.zeros((1, LANES), F32))
    wts_s, ids_s, rank_s, cnt = router(ys, g_ffn[l], w_pad, b_pad, cnt_p)
    mp, ms = yp.shape[0], ys.shape[0]
    nt = (2 * (mp + ms)) // MOE_TM + N_EXPERTS
    ids = jnp.concatenate([ids_p[:, :2], ids_s[:, :2]], axis=0)
    rank = jnp.concatenate([rank_p[:, :2], rank_s[:, :2]], axis=0)
    tile_expert, tile_rows, pos = _dispatch_plan(cnt[0, :N_EXPERTS].astype(jnp.int32), ids, rank, MOE_TM, nt)
    tm_p, tm_s = 512, min(512, ms)
    pos_p, pos_s = _combine_pos(pos[:mp], tm_p), _combine_pos(pos[mp:], tm_s)
    xs = jnp.zeros((nt * MOE_TM, D_MODEL // 2), jnp.uint32)
    xs = moe_dispatch(pos_p, yp, g_ffn[l], xs, tm=tm_p)
    xs = moe_dispatch(pos_s, ys, g_ffn[l], xs, tm=tm_s)
    osort = moe_ffn(tile_expert, tile_rows, xs, w_moe_gu[i], w_moe_d[i])
    return (moe_combine(pos_p, yp, wts_p, osort, tm=tm_p), moe_combine(pos_s, ys, wts_s, osort, tm=tm_s))


def kernel(x_prompt, x_sample, mem_prompt, cache_mem_k, cache_mem_v, cache_swa_k, cache_swa_v, state_conv_b, state_conv_c, rel_table, g_mix, w_in, b_gate, q_norm_g, k_norm_g, sinks, w_conv_b, w_conv_c, b_conv_c, ln_c_g, ln_c_b, w_proj_a, w_proj_b, w_proj_c, w_out, g_xattn, g_mem, w_xq, w_xkv, xq_norm_g, xk_norm_g, w_xo, g_ffn, w_ffn_gu, w_ffn_d, w_router, b_router, w_moe_gu, w_moe_d):
    batch, seq, d = x_prompt.shape
    dec_batch, dec_seq, _ = x_sample.shape
    depth = g_mix.shape[0]
    yp = x_prompt.reshape(batch * seq, d)
    ys = x_sample.reshape(dec_batch * dec_seq, d)
    mem = mem_prompt.reshape(batch * N_MEM, d)
    first = np.arange(KEY_TILE)[None, :] >= Q_TILE
    bias_p = jnp.stack([_stack_heads(_bias_tensor(rel_table, Q_TILE, _prompt_valid() & first)),
                        _stack_heads(_bias_tensor(rel_table, Q_TILE, _prompt_valid()))])
    bias_s = _bias_tensor(rel_table, dec_seq, _sample_valid(dec_seq))
    outs = [[] for _ in range(10)]
    w_in_bf = w_in.astype(BF16)
    ck = cache_swa_k.reshape(depth, dec_batch, WINDOW, A_KV_WIDTH)
    cv = cache_swa_v.reshape(depth, dec_batch, WINDOW, A_KV_WIDTH)
    for l in range(depth):
        mw = _mixer_weights(l, g_mix, w_in_bf, b_gate, q_norm_g, k_norm_g, sinks, w_conv_b, w_conv_c,
                            b_conv_c, ln_c_g, ln_c_b, w_proj_a, w_proj_b, w_proj_c, w_out)
        wq = w_xq[l].astype(BF16)
        wo = w_xo[l].astype(BF16)
        ffn_args = (g_ffn, w_ffn_gu, w_ffn_d, w_router, b_router, w_moe_gu, w_moe_d)
        yp, (nk, nv, ncb, ncc) = _mixer_prompt(yp, mw, bias_p, batch, seq)
        mk, mv = mem_kv(mem, g_mem[l], w_xkv[l].astype(BF16), xk_norm_g[l])
        mk3 = mk.reshape(batch, N_MEM, X_WIDTH)
        mv3 = mv.reshape(batch, N_MEM, X_WIDTH)
        rpb = min(1024, seq)
        yp = xattn(yp, g_xattn[l], wq, xq_norm_g[l], mk3[None], mv3[None], wo, layer=0, nb=1, rpb=rpb,
                   tiles_per_mem=seq // rpb)
        for lst, v in zip(outs[:6], (mk3.reshape(batch, N_MEM, X_HEADS, X_HEAD_DIM),
                                     mv3.reshape(batch, N_MEM, X_HEADS, X_HEAD_DIM),
                                     nk.reshape(batch, WINDOW, A_KV_HEADS, A_HEAD_DIM),
                                     nv.reshape(batch, WINDOW, A_KV_HEADS, A_HEAD_DIM), ncb, ncc)):
            lst.append(v)
        ys, (nk, nv, ncb, ncc) = _mixer_sample(ys, mw, bias_s, dec_batch, dec_seq, ck, cv, l,
                                               state_conv_b[l], state_conv_c[l])
        ys = xattn(ys, g_xattn[l], wq, xq_norm_g[l], cache_mem_k, cache_mem_v, wo, layer=l, nb=8, rpb=dec_seq,
                   tiles_per_mem=1)
        yp, ys = _channel_mixer(yp, ys, l, *ffn_args)
        for lst, v in zip(outs[6:], (nk.reshape(dec_batch, dec_seq, A_KV_HEADS, A_HEAD_DIM),
                                     nv.reshape(dec_batch, dec_seq, A_KV_HEADS, A_HEAD_DIM), ncb, ncc)):
            lst.append(v)
    return (yp.reshape(batch, seq, d), ys.reshape(dec_batch, dec_seq, d)) + tuple(jnp.stack(o) for o in outs)
```
